```python
import functools
import jax
import jax.numpy as jnp
from jax import lax
import numpy as np

D_MODEL = 2048
BATCH = 4
SEQ = 2048
DEPTH = 1
DEC_BATCH = 128
DEC_SEQ = 4
PAST_LEN = 16384
PAGE_SIZE = 128

HEAD_DIM = 64
N_RWKV_HEADS = D_MODEL // HEAD_DIM
D_RWKV = N_RWKV_HEADS * HEAD_DIM
D_DECAY_LORA = max(32, int(round(1.8 * D_MODEL ** 0.5 / 32)) * 32)
D_A_LORA = max(32, int(round(1.8 * D_MODEL ** 0.5 / 32)) * 32)
D_GATE_LORA = max(32, int(round(0.6 * D_MODEL ** 0.8 / 32)) * 32)
R_COLS = 3 * D_RWKV + D_DECAY_LORA + D_A_LORA + D_GATE_LORA
N_Q_HEADS = D_MODEL // HEAD_DIM
N_KV_HEADS = 8
GQA_GROUP = N_Q_HEADS // N_KV_HEADS
D_SWA = N_Q_HEADS * HEAD_DIM
D_KV = N_KV_HEADS * HEAD_DIM
WINDOW = 128
BLOCK = WINDOW
D_FF = 4 * D_MODEL
C_IN = R_COLS + D_SWA + 2 * D_KV + 2 * D_MODEL
RWKV_SPLITS = (D_RWKV, 2 * D_RWKV, 3 * D_RWKV, 3 * D_RWKV + D_DECAY_LORA, 3 * D_RWKV + D_DECAY_LORA + D_A_LORA)
IN_SPLITS = (R_COLS, R_COLS + D_SWA, R_COLS + D_SWA + D_KV, R_COLS + D_SWA + 2 * D_KV, R_COLS + D_SWA + 2 * D_KV + D_MODEL)
RMS_EPS = 1e-5
GN_EPS = 64e-5

kernel_name = 'rwkv7_swa_sink_hybrid_step'


def rmsnorm(x, w):
    xf = x.astype(jnp.float32)
    y = xf * lax.rsqrt(jnp.mean(xf * xf, axis=-1, keepdims=True) + RMS_EPS)
    return (y * w.astype(jnp.float32)).astype(x.dtype)


def alibi_slopes():
    h = jnp.arange(N_Q_HEADS, dtype=jnp.float32)
    return jnp.exp2(-8.0 * (h + 1.0) / N_Q_HEADS).reshape(N_KV_HEADS, GQA_GROUP)


def window_attend(q, k, v, dist, valid, sinks):
    f32 = jnp.float32
    s = jnp.einsum('...qhgd,...khd->...hgqk', q.astype(f32), k.astype(f32)) * (HEAD_DIM ** -0.5)
    s = s - alibi_slopes()[:, :, None, None] * dist.astype(f32)
    s = jnp.where(valid, s, -jnp.inf)
    sink = sinks.astype(f32).reshape(N_KV_HEADS, GQA_GROUP)[:, :, None, None]
    m = jnp.maximum(jnp.max(s, axis=-1, keepdims=True), sink)
    p = jnp.exp(s - m)
    den = jnp.sum(p, axis=-1, keepdims=True) + jnp.exp(sink - m)
    o = jnp.einsum('...hgqk,...khd->...qhgd', p / den, v.astype(f32))
    return o.astype(q.dtype)


def swa_prompt(q, k, v, sinks):
    b, t, _ = q.shape
    nb = t // BLOCK
    qb = q.reshape(b, nb, BLOCK, N_KV_HEADS, GQA_GROUP, HEAD_DIM)
    kb = k.reshape(b, nb, BLOCK, N_KV_HEADS, HEAD_DIM)
    vb = v.reshape(b, nb, BLOCK, N_KV_HEADS, HEAD_DIM)

    def with_prev(u):
        prev = jnp.concatenate([jnp.zeros_like(u[:, :1]), u[:, :-1]], axis=1)
        return jnp.concatenate([prev, u], axis=2)

    qi = jnp.arange(BLOCK)[:, None]
    kj = jnp.arange(2 * BLOCK)[None, :]
    dist = qi - kj + BLOCK
    blk = jnp.arange(nb)[:, None, None]
    valid = (dist >= 0) & (dist <= WINDOW) & (blk * BLOCK - BLOCK + kj >= 0)
    o = window_attend(qb, with_prev(kb), with_prev(vb), dist, valid[:, None, None], sinks)
    k_win = k[:, t - WINDOW:].reshape(b, WINDOW, N_KV_HEADS, HEAD_DIM)
    v_win = v[:, t - WINDOW:].reshape(b, WINDOW, N_KV_HEADS, HEAD_DIM)
    return o.reshape(b, t, D_SWA), k_win, v_win


def swa_sample(q, k, v, k_past, v_past, sinks):
    b, t, _ = q.shape
    qh = q.reshape(b, t, N_KV_HEADS, GQA_GROUP, HEAD_DIM)
    kcat = jnp.concatenate([k_past.astype(k.dtype), k.reshape(b, t, N_KV_HEADS, HEAD_DIM)], axis=1)
    vcat = jnp.concatenate([v_past.astype(v.dtype), v.reshape(b, t, N_KV_HEADS, HEAD_DIM)], axis=1)
    qpos = PAST_LEN + jnp.arange(t)
    kpos = jnp.concatenate([PAST_LEN - WINDOW + jnp.arange(WINDOW), PAST_LEN + jnp.arange(t)])
    dist = qpos[:, None] - kpos[None, :]
    valid = (dist >= 0) & (dist <= WINDOW)
    o = window_attend(qh, kcat, vcat, dist, valid, sinks)
    return o.reshape(b, t, D_SWA), kcat[:, -WINDOW:], vcat[:, -WINDOW:]


def wkv_scan(r, w, k, v, a, bvec, s0):
    def step(s, inp):
        r_t, w_t, k_t, v_t, a_t, b_t = inp
        sa = jnp.einsum('bhij,bhj->bhi', s, a_t)
        s = s * w_t[:, :, None, :] + sa[..., None] * b_t[:, :, None, :] + v_t[..., None] * k_t[:, :, None, :]
        return s, jnp.einsum('bhij,bhj->bhi', s, r_t)

    xs = tuple(jnp.swapaxes(u, 0, 1) for u in (r, w, k, v, a, bvec))
    s_fin, ys = lax.scan(step, s0.astype(jnp.float32), xs)
    return jnp.swapaxes(ys, 0, 1), s_fin


def rwkv_branch(p_rwkv, shift_prev, wkv0, tshift_mu, w0, w_lora, a0, a_lora, g_lora, k_k, k_a, r_k, ln_x_w, ln_x_b):
    f32 = jnp.float32
    b, t, _ = p_rwkv.shape
    p_prev = jnp.concatenate([shift_prev[:, None, :].astype(p_rwkv.dtype), p_rwkv[:, :-1]], axis=1)
    ps = p_rwkv + tshift_mu * (p_prev - p_rwkv)
    r, k, v, dw, da, dg = jnp.split(ps, RWKV_SPLITS, axis=-1)
    w_log = -jax.nn.softplus(-(w0 + jnp.tanh(dw) @ w_lora).astype(f32)) - 0.5
    decay = jnp.exp(-jnp.exp(w_log))
    a = jax.nn.sigmoid((a0 + da @ a_lora).astype(f32))
    g = jax.nn.sigmoid(dg) @ g_lora

    def heads(u):
        return u.astype(f32).reshape(b, t, N_RWKV_HEADS, HEAD_DIM)

    kk = heads(k * k_k)
    kk = kk / jnp.maximum(jnp.sqrt(jnp.sum(kk * kk, axis=-1, keepdims=True)), 1e-12)
    a_h = heads(a)
    k_h = heads(k.astype(f32) * (1.0 + (a - 1.0) * k_a.astype(f32)))
    r_h = heads(r)
    v_h = heads(v)
    y, s_fin = wkv_scan(r_h, heads(decay), k_h, v_h, -kk, kk * a_h, wkv0)
    mu = jnp.mean(y, axis=-1, keepdims=True)
    var = jnp.mean(jnp.square(y - mu), axis=-1, keepdims=True)
    yn = ((y - mu) * lax.rsqrt(var + GN_EPS)).reshape(b, t, D_RWKV) * ln_x_w.astype(f32) + ln_x_b.astype(f32)
    bonus = jnp.sum(r_h * k_h * r_k.astype(f32), axis=-1, keepdims=True) * v_h
    out = (yn + bonus.reshape(b, t, D_RWKV)) * g.astype(f32)
    return out.astype(p_rwkv.dtype), p_rwkv[:, -1], s_fin.astype(wkv0.dtype)


def trunk_layer(x, shift_prev, wkv0, swa_fn, norm_mix_w, w_in, tshift_mu, w0, w_lora, a0, a_lora, g_lora,
                k_k, k_a, r_k, ln_x_w, ln_x_b, w_out, norm_mlp_w, w_up, w_down):
    xn = rmsnorm(x, norm_mix_w)
    proj = jnp.einsum('btd,dc->btc', xn, w_in)
    p_rwkv, q, k, v, gate_a, gate_b = jnp.split(proj, IN_SPLITS, axis=-1)
    y_a, shift_new, wkv_new = rwkv_branch(p_rwkv, shift_prev, wkv0, tshift_mu, w0, w_lora, a0, a_lora, g_lora,
                                          k_k, k_a, r_k, ln_x_w, ln_x_b)
    y_b, k_win, v_win = swa_fn(q, k, v)
    mixed = jax.nn.sigmoid(gate_a) * y_a + jax.nn.sigmoid(gate_b) * y_b
    h = x + jnp.einsum('btc,cd->btd', mixed, w_out)
    hn = rmsnorm(h, norm_mlp_w)
    u = jnp.square(jax.nn.relu(jnp.einsum('btd,df->btf', hn, w_up)))
    h = h + jnp.einsum('btf,fd->btd', u, w_down)
    return h, shift_new, wkv_new, k_win, v_win


def setup_inputs(seed: int = 0) -> dict:
    key = jax.random.key(seed)
    ks = jax.random.split(key, 26)
    nrm = jax.random.normal
    f32 = jnp.float32
    L = DEPTH
    return {
        'x_prompt': nrm(ks[0], (BATCH, SEQ, D_MODEL), f32),
        'x_sample': nrm(ks[1], (DEC_BATCH, DEC_SEQ, D_MODEL), f32),
        'state_shift': nrm(ks[2], (L, DEC_BATCH, R_COLS), f32),
        'state_wkv': 0.5 * nrm(ks[3], (L, DEC_BATCH, N_RWKV_HEADS, HEAD_DIM, HEAD_DIM), f32),
        'cache_k_win': nrm(ks[4], (L, DEC_BATCH, WINDOW, N_KV_HEADS, HEAD_DIM), f32),
        'cache_v_win': nrm(ks[5], (L, DEC_BATCH, WINDOW, N_KV_HEADS, HEAD_DIM), f32),
        'norm_mix_w': 1.0 + 0.02 * nrm(ks[6], (L, D_MODEL), f32),
        'w_in': nrm(ks[7], (L, D_MODEL, C_IN), f32) * D_MODEL ** -0.5,
        'tshift_mu': jax.random.uniform(ks[8], (L, R_COLS), f32),
        'w0': jax.random.uniform(ks[9], (L, D_RWKV), f32, -6.0, 0.0),
        'w_lora': 0.5 * nrm(ks[10], (L, D_DECAY_LORA, D_RWKV), f32) * D_DECAY_LORA ** -0.5,
        'a0': 0.1 * nrm(ks[11], (L, D_RWKV), f32),
        'a_lora': 0.5 * nrm(ks[12], (L, D_A_LORA, D_RWKV), f32) * D_A_LORA ** -0.5,
        'g_lora': nrm(ks[13], (L, D_GATE_LORA, D_RWKV), f32) * D_GATE_LORA ** -0.5,
        'k_k': 0.85 + 0.05 * nrm(ks[14], (L, D_RWKV), f32),
        'k_a': 1.0 + 0.05 * nrm(ks[15], (L, D_RWKV), f32),
        'r_k': 0.1 * nrm(ks[16], (L, N_RWKV_HEADS, HEAD_DIM), f32),
        'ln_x_w': 1.0 + 0.02 * nrm(ks[17], (L, D_RWKV), f32),
        'ln_x_b': 0.02 * nrm(ks[18], (L, D_RWKV), f32),
        'attn_sinks': nrm(ks[19], (L, N_Q_HEADS), f32),
        'w_out': nrm(ks[20], (L, D_MODEL, D_MODEL), f32) * D_MODEL ** -0.5,
        'norm_mlp_w': 1.0 + 0.02 * nrm(ks[21], (L, D_MODEL), f32),
        'w_up': nrm(ks[22], (L, D_MODEL, D_FF), f32) * D_MODEL ** -0.5,
        'w_down': nrm(ks[23], (L, D_FF, D_MODEL), f32) * D_FF ** -0.5,
        'norm_final_w': 1.0 + 0.02 * nrm(ks[24], (D_MODEL,), f32),
    }


def reference(x_prompt, x_sample, state_shift, state_wkv, cache_k_win, cache_v_win, norm_mix_w, w_in, tshift_mu,
              w0, w_lora, a0, a_lora, g_lora, k_k, k_a, r_k, ln_x_w, ln_x_b, attn_sinks, w_out, norm_mlp_w,
              w_up, w_down, norm_final_w):
    hp, hs = x_prompt, x_sample
    bp = x_prompt.shape[0]
    sp_l, wp_l, kp_l, vp_l = [], [], [], []
    ss_l, ws_l, ksm_l, vsm_l = [], [], [], []
    for l in range(DEPTH):
        lw = (norm_mix_w[l], w_in[l], tshift_mu[l], w0[l], w_lora[l], a0[l], a_lora[l], g_lora[l], k_k[l], k_a[l],
              r_k[l], ln_x_w[l], ln_x_b[l], w_out[l], norm_mlp_w[l], w_up[l], w_down[l])
        hp, sp, wp, kp, vp = trunk_layer(
            hp, jnp.zeros((bp, R_COLS), hp.dtype), jnp.zeros((bp, N_RWKV_HEADS, HEAD_DIM, HEAD_DIM), hp.dtype),
            functools.partial(swa_prompt, sinks=attn_sinks[l]), *lw)
        hs, ss, ws, ksm, vsm = trunk_layer(
            hs, state_shift[l], state_wkv[l],
            functools.partial(swa_sample, k_past=cache_k_win[l], v_past=cache_v_win[l], sinks=attn_sinks[l]), *lw)
        sp_l.append(sp); wp_l.append(wp); kp_l.append(kp); vp_l.append(vp)
        ss_l.append(ss); ws_l.append(ws); ksm_l.append(ksm); vsm_l.append(vsm)
    y_prompt = rmsnorm(hp, norm_final_w)
    y_sample = rmsnorm(hs, norm_final_w)
    return (y_prompt, y_sample, jnp.stack(sp_l), jnp.stack(wp_l), jnp.stack(kp_l), jnp.stack(vp_l),
            jnp.stack(ss_l), jnp.stack(ws_l), jnp.stack(ksm_l), jnp.stack(vsm_l))
```

```python
import functools
import math

import jax
import jax.numpy as jnp
from jax import lax
from jax.experimental import pallas as pl
from jax.experimental.pallas import tpu as pltpu

F32 = jnp.float32
BF16 = jnp.bfloat16

D_MODEL = 2048
HEAD_DIM = 64
N_HEADS = D_MODEL // HEAD_DIM
N_KV_HEADS = 8
GQA_GROUP = N_HEADS // N_KV_HEADS
D_KV = N_KV_HEADS * HEAD_DIM
WINDOW = 128
D_FF = 4 * D_MODEL
D_DECAY_LORA = 96
D_A_LORA = 96
D_GATE_LORA = 256
D_RWKV = D_MODEL
R_COLS = 3 * D_RWKV + D_DECAY_LORA + D_A_LORA + D_GATE_LORA
C_IN = R_COLS + D_MODEL + 2 * D_KV + 2 * D_MODEL
RMS_EPS = 1e-5
GN_EPS = 64e-5

LANES = 128
LORA_SLOT = 128
LORA_W = 2 * LORA_SLOT + D_GATE_LORA
COL_RKV = 0
COL_Q = 3 * D_RWKV
COL_GA = COL_Q + D_MODEL
COL_GB = COL_GA + D_MODEL
COL_KS = COL_GB + D_MODEL
COL_VS = COL_KS + D_KV
COL_LORA = COL_VS + D_KV
C_PAD = COL_LORA + LORA_W

VMEM_LIMIT = 56 * 1024 * 1024


def _cparams(sem):
    return pltpu.CompilerParams(dimension_semantics=sem, vmem_limit_bytes=VMEM_LIMIT)


def _dot(a, b):
    return jnp.dot(a, b, preferred_element_type=F32)


def _dot_nt(a, b):
    return lax.dot_general(a, b, (((1,), (1,)), ((), ())), preferred_element_type=F32)


def _dot_tn(a, b):
    return lax.dot_general(a, b, (((0,), (0,)), ((), ())), preferred_element_type=F32)


def _split_r_cols(v):
    o = 3 * D_RWKV
    dw = v[..., o:o + D_DECAY_LORA]
    da = v[..., o + D_DECAY_LORA:o + D_DECAY_LORA + D_A_LORA]
    dg = v[..., o + D_DECAY_LORA + D_A_LORA:R_COLS]
    z = jnp.zeros(v.shape[:-1] + (LORA_SLOT - D_DECAY_LORA,), v.dtype)
    return v[..., :o], jnp.concatenate([dw, z, da, z, dg], axis=-1)


def _merge_r_cols(rkv, lora):
    return jnp.concatenate([rkv, lora[..., :D_DECAY_LORA], lora[..., LORA_SLOT:LORA_SLOT + D_A_LORA],
                            lora[..., 2 * LORA_SLOT:]], axis=-1)


def _reorder_in_cols(w):
    rkv, lora = _split_r_cols(w[..., :R_COLS])
    o = R_COLS
    q = w[..., o:o + D_MODEL]
    ks = w[..., o + D_MODEL:o + D_MODEL + D_KV]
    vs = w[..., o + D_MODEL + D_KV:o + D_MODEL + 2 * D_KV]
    ga = w[..., o + D_MODEL + 2 * D_KV:o + 2 * D_MODEL + 2 * D_KV]
    gb = w[..., o + 2 * D_MODEL + 2 * D_KV:]
    return jnp.concatenate([rkv, q, ga, gb, ks, vs, lora], axis=-1)


def _norm_matmul_kernel(x_ref, nw_ref, w_ref, o_ref, xn_ref):
    @pl.when(pl.program_id(1) == 0)
    def _():
        x = x_ref[...]
        ms = jnp.mean(x * x, axis=-1, keepdims=True)
        xn_ref[...] = (x * lax.rsqrt(ms + RMS_EPS) * nw_ref[...]).astype(BF16)

    o_ref[...] = _dot(xn_ref[...], w_ref[...])


def _norm_matmul(x, nw, w, tm, tn):
    m, d = x.shape
    n = w.shape[1]
    return pl.pallas_call(
        _norm_matmul_kernel,
        grid=(m // tm, n // tn),
        in_specs=[pl.BlockSpec((tm, d), lambda i, j: (i, 0)),
                  pl.BlockSpec((1, d), lambda i, j: (0, 0)),
                  pl.BlockSpec((d, tn), lambda i, j: (0, j))],
        out_specs=pl.BlockSpec((tm, tn), lambda i, j: (i, j)),
        out_shape=jax.ShapeDtypeStruct((m, n), F32),
        scratch_shapes=[pltpu.VMEM((tm, d), BF16)],
        compiler_params=_cparams(("parallel", "arbitrary")),
        name="norm_in_proj",
    )(x, nw, w)


def _softplus(x):
    return jnp.maximum(x, 0.0) + jnp.log(1.0 + jnp.exp(-jnp.abs(x)))


def _sigmoid(x):
    return 1.0 / (1.0 + jnp.exp(-x))


def _split2(x):
    hi = x.astype(BF16)
    lo = (x - hi.astype(F32)).astype(BF16)
    return hi, lo


def _split3(x):
    hi = x.astype(BF16)
    r1 = x - hi.astype(F32)
    mid = r1.astype(BF16)
    lo = (r1 - mid.astype(F32)).astype(BF16)
    return hi, mid, lo


def _rwkv_kernel(pr_ref, pk_ref, pv_ref, plo_ref, shr_ref, shk_ref, shv_ref, shl_ref, s0_ref,
                 mur_ref, muk_ref, muv_ref, mul_ref, w0_ref, a0_ref, kk_ref, ka_ref, rk_ref, lnw_ref, lnb_ref,
                 wl_ref, al_ref, gl_ref,
                 y_ref, so_ref,
                 sd_ref, br_ref, bk_ref, bv_ref, bl_ref, *, L, Lv, HPS, n_dbl):
    c = pl.program_id(2)
    nc = pl.num_programs(2)
    H = HEAD_DIM

    @pl.when(c == 0)
    def _init():
        br_ref[7:8, :] = shr_ref[0]
        bk_ref[7:8, :] = shk_ref[0]
        bv_ref[7:8, :] = shv_ref[0]
        bl_ref[7:8, :] = shl_ref[0]
        if Lv < L:
            for ref in (br_ref, bk_ref, bv_ref, bl_ref):
                ref[8 + Lv:8 + L, :] = jnp.zeros((L - Lv, ref.shape[1]), F32)
        z = jnp.zeros((H, H), F32)
        for i in range(HPS):
            top = jnp.concatenate([s0_ref[0, 2 * i], z], axis=1)
            bot = jnp.concatenate([z, s0_ref[0, 2 * i + 1]], axis=1)
            sd_ref[i] = jnp.concatenate([top, bot], axis=0)

    br_ref[8:8 + Lv, :] = pr_ref[0]
    bk_ref[8:8 + Lv, :] = pk_ref[0]
    bv_ref[8:8 + Lv, :] = pv_ref[0]
    bl_ref[8:8 + Lv, :] = plo_ref[0]

    def shifted(buf, mu, sl):
        p = buf[8:8 + L, sl]
        return p + mu * (buf[7:7 + L, sl] - p)

    ps_l = shifted(bl_ref, mul_ref[...], slice(None))
    td = jnp.tanh(ps_l[:, 0:LORA_SLOT]).astype(BF16)
    da = ps_l[:, LORA_SLOT:2 * LORA_SLOT].astype(BF16)
    sg = _sigmoid(ps_l[:, 2 * LORA_SLOT:]).astype(BF16)

    lane = lax.broadcasted_iota(jnp.int32, (L, LANES), 1)
    head1 = lane < H
    row_i = lax.broadcasted_iota(jnp.int32, (L, 2 * L), 0)
    col_i = lax.broadcasted_iota(jnp.int32, (L, 2 * L), 1)
    col_t = jnp.where(col_i >= L, col_i - L, col_i)
    strict = col_t < row_i
    incl = col_t <= row_i
    colh1 = col_i < L
    tri = (lax.broadcasted_iota(jnp.int32, (L, L), 1) <= lax.broadcasted_iota(jnp.int32, (L, L), 0)).astype(BF16)
    ji = lax.broadcasted_iota(jnp.int32, (LANES, LANES), 0)
    jj = lax.broadcasted_iota(jnp.int32, (LANES, LANES), 1)
    same_head = (ji < H) == (jj < H)
    seg = same_head.astype(BF16)
    if Lv < L:
        rowv = lax.broadcasted_iota(jnp.int32, (L, LANES), 0) < Lv

    def seg_sum(x):
        hi, lo = _split2(x)
        return _dot(hi, seg) + _dot(lo, seg)

    def stack_heads(x, m):
        zero = jnp.zeros_like(x)
        return jnp.concatenate([jnp.where(m, x, zero), jnp.where(m, zero, x)], axis=0)

    for i in range(HPS):
        sl = slice(i * LANES, (i + 1) * LANES)
        r = shifted(br_ref, mur_ref[:, sl], sl)
        k = shifted(bk_ref, muk_ref[:, sl], sl)
        v = shifted(bv_ref, muv_ref[:, sl], sl)

        zlog = w0_ref[:, sl] + _dot(td, wl_ref[:, sl])
        logw = -jnp.exp(-_softplus(-zlog) - 0.5)
        a_sig = _sigmoid(a0_ref[:, sl] + _dot(da, al_ref[:, sl]))
        kk = k * kk_ref[:, sl]
        nrm = jnp.sqrt(seg_sum(kk * kk))
        kk = kk / jnp.maximum(nrm, 1e-12)
        k_h = k * (1.0 + (a_sig - 1.0) * ka_ref[:, sl])
        if Lv < L:
            logw = jnp.where(rowv, logw, 0.0)
            kk = jnp.where(rowv, kk, 0.0)
            k_h = jnp.where(rowv, k_h, 0.0)
            v = jnp.where(rowv, v, 0.0)

        hi, mid, lo = _split3(logw)
        cum = _dot(tri, hi) + _dot(tri, mid) + _dot(tri, lo)
        cum_l = cum[L - 1:L, :]
        p_in = jnp.exp(cum)
        p_ex = jnp.exp(cum - logw)
        p_inv = jnp.exp(-cum)
        p_end = jnp.exp(cum_l - cum)
        bvec = kk * a_sig
        a_t = (-kk * p_ex).astype(BF16)
        r_t = (r * p_in).astype(BF16)
        b_t = (bvec * p_inv).astype(BF16)
        k_t = (k_h * p_inv).astype(BF16)
        b_e = (bvec * p_end).astype(BF16)
        k_e = (k_h * p_end).astype(BF16)
        v_b = v.astype(BF16)

        lhs = jnp.concatenate([a_t, r_t], axis=0)
        rhs = jnp.concatenate([stack_heads(b_t, head1), stack_heads(k_t, head1)], axis=0)
        aa = _dot_nt(lhs, rhs)
        a_ab = jnp.where(strict, aa[0:L, 0:2 * L], 0.0)
        a_ak = jnp.where(strict, aa[0:L, 2 * L:4 * L], 0.0)
        a_rb = jnp.where(incl, aa[L:2 * L, 0:2 * L], 0.0)
        a_rk = jnp.where(incl, aa[L:2 * L, 2 * L:4 * L], 0.0)

        sd = sd_ref[i]
        sa = _dot_nt(lhs, sd.astype(BF16))
        v_st = stack_heads(v_b, head1)
        x = sa[0:L] + _dot(a_ak.astype(BF16), v_st)
        apow = a_ab
        for d in range(n_dbl):
            ap = apow.astype(BF16)
            x = x + _dot(ap, stack_heads(x.astype(BF16), head1))
            if d + 1 < n_dbl:
                apow = _dot(ap, stack_heads(ap, colh1))
        u_b = x.astype(BF16)
        y = sa[L:2 * L] + _dot(a_rb.astype(BF16), stack_heads(u_b, head1)) + _dot(a_rk.astype(BF16), v_st)

        ds = _dot_tn(jnp.concatenate([u_b, v_b], axis=0), jnp.concatenate([b_e, k_e], axis=0))
        sd_ref[i] = sd * jnp.exp(cum_l) + jnp.where(same_head, ds, 0.0)

        mean = seg_sum(y) * (1.0 / H)
        dlt = y - mean
        var = seg_sum(dlt * dlt) * (1.0 / H)
        yn = dlt * lax.rsqrt(var + GN_EPS) * lnw_ref[:, sl] + lnb_ref[:, sl]
        bonus = seg_sum(r * k_h * rk_ref[:, sl]) * v
        g = _dot(sg, gl_ref[:, sl])
        out = (yn + bonus) * g
        y_ref[0, :, sl] = out[0:Lv]

    br_ref[7:8, :] = br_ref[7 + Lv:8 + Lv, :]
    bk_ref[7:8, :] = bk_ref[7 + Lv:8 + Lv, :]
    bv_ref[7:8, :] = bv_ref[7 + Lv:8 + Lv, :]
    bl_ref[7:8, :] = bl_ref[7 + Lv:8 + Lv, :]

    @pl.when(c == nc - 1)
    def _fin():
        for i in range(HPS):
            sd = sd_ref[i]
            so_ref[0, 2 * i] = sd[0:H, 0:H]
            so_ref[0, 2 * i + 1] = sd[H:2 * H, H:2 * H]


def _rwkv_mix(proj3, sh_rkv, sh_lora, s0, p, *, L, Lv, HPS):
    b, t, _ = proj3.shape
    assert t % Lv == 0
    nc = t // Lv
    w = HPS * LANES
    nhg = D_RWKV // w
    kb = D_RWKV // w
    n_dbl = max(1, math.ceil(math.log2(Lv)))
    col = lambda o: (lambda bi, g, c: (bi, c, o + g))
    sh = lambda o: (lambda bi, g, c: (bi, 0, o + g))
    par = lambda o: (lambda bi, g, c: (0, o + g))
    fixed2 = lambda bi, g, c: (0, 0)
    in_specs = [
        pl.BlockSpec((1, Lv, w), col(0)), pl.BlockSpec((1, Lv, w), col(kb)), pl.BlockSpec((1, Lv, w), col(2 * kb)),
        pl.BlockSpec((1, Lv, LORA_W), lambda bi, g, c: (bi, c, COL_LORA // LORA_W)),
        pl.BlockSpec((1, 1, w), sh(0)), pl.BlockSpec((1, 1, w), sh(kb)), pl.BlockSpec((1, 1, w), sh(2 * kb)),
        pl.BlockSpec((1, 1, LORA_W), lambda bi, g, c: (bi, 0, 0)),
        pl.BlockSpec((1, 2 * HPS, HEAD_DIM, HEAD_DIM), lambda bi, g, c: (bi, g, 0, 0)),
        pl.BlockSpec((1, w), par(0)), pl.BlockSpec((1, w), par(kb)), pl.BlockSpec((1, w), par(2 * kb)),
        pl.BlockSpec((1, LORA_W), fixed2),
    ] + [pl.BlockSpec((1, w), par(0))] * 7 + [
        pl.BlockSpec((LORA_SLOT, w), par(0)), pl.BlockSpec((LORA_SLOT, w), par(0)),
        pl.BlockSpec((D_GATE_LORA, w), par(0)),
    ]
    out_specs = [pl.BlockSpec((1, Lv, w), col(0)),
                 pl.BlockSpec((1, 2 * HPS, HEAD_DIM, HEAD_DIM), lambda bi, g, c: (bi, g, 0, 0))]
    kern = functools.partial(_rwkv_kernel, L=L, Lv=Lv, HPS=HPS, n_dbl=n_dbl)
    return pl.pallas_call(
        kern,
        grid=(b, nhg, nc),
        in_specs=in_specs,
        out_specs=out_specs,
        out_shape=[jax.ShapeDtypeStruct((b, t, D_RWKV), F32),
                   jax.ShapeDtypeStruct((b, N_HEADS, HEAD_DIM, HEAD_DIM), F32)],
        scratch_shapes=[pltpu.VMEM((HPS, LANES, LANES), F32),
                        pltpu.VMEM((L + 8, w), F32), pltpu.VMEM((L + 8, w), F32), pltpu.VMEM((L + 8, w), F32),
                        pltpu.VMEM((L + 8, LORA_W), F32)],
        compiler_params=_cparams(("parallel", "parallel", "arbitrary")),
        name="rwkv_mix",
    )(proj3, proj3, proj3, proj3, sh_rkv, sh_rkv, sh_rkv, sh_lora, s0,
      p["mu_rkv"], p["mu_rkv"], p["mu_rkv"], p["mu_lora"], p["w0"], p["a0"], p["k_k"], p["k_a"], p["r_k"],
      p["ln_w"], p["ln_b"], p["w_lora"], p["a_lora"], p["g_lora"])


def _softmax_sink_pv(s, valid, sink, v_b):
    s = jnp.where(valid, s, -jnp.inf)
    m = jnp.maximum(jnp.max(s, axis=-1, keepdims=True), sink)
    p = jnp.exp(s - m)
    den = jnp.sum(p, axis=-1, keepdims=True) + jnp.exp(sink - m)
    return _dot(p.astype(BF16), v_b) / den


def _swa_prompt_kernel(slope_ref, sink_ref, q_ref, kc_ref, kp_ref, vc_ref, vp_ref, o_ref):
    n = pl.program_id(1)
    blk = WINDOW
    t = lax.broadcasted_iota(jnp.int32, (blk, 2 * blk), 0)
    j = lax.broadcasted_iota(jnp.int32, (blk, 2 * blk), 1)
    dist = t - j + blk
    valid = (dist >= 0) & (dist <= WINDOW) & ((j >= blk) | (n > 0))
    distf = dist.astype(F32)
    scale = HEAD_DIM ** -0.5
    for hk in range(N_KV_HEADS):
        ksl = slice(hk * HEAD_DIM, (hk + 1) * HEAD_DIM)
        kcat = jnp.concatenate([kp_ref[0, :, ksl], kc_ref[0, :, ksl]], axis=0).astype(BF16)
        vcat = jnp.concatenate([vp_ref[0, :, ksl], vc_ref[0, :, ksl]], axis=0).astype(BF16)
        for pair in range(GQA_GROUP // 2):
            outs = []
            for g2 in range(2):
                h = hk * GQA_GROUP + pair * 2 + g2
                qh = q_ref[0, :, h * HEAD_DIM:(h + 1) * HEAD_DIM].astype(BF16)
                s = _dot_nt(qh, kcat) * scale - slope_ref[h] * distf
                outs.append(_softmax_sink_pv(s, valid, sink_ref[h], vcat))
            h0 = hk * GQA_GROUP + pair * 2
            o_ref[0, :, h0 * HEAD_DIM:(h0 + 2) * HEAD_DIM] = jnp.concatenate(outs, axis=1)


def _swa_prompt(proj3, slopes, sinks):
    b, t, _ = proj3.shape
    nb = t // WINDOW
    smem = pl.BlockSpec(memory_space=pltpu.SMEM)
    cq, ck, cv = COL_Q // D_MODEL, COL_KS // D_KV, COL_VS // D_KV
    return pl.pallas_call(
        _swa_prompt_kernel,
        grid=(b, nb),
        in_specs=[smem, smem,
                  pl.BlockSpec((1, WINDOW, D_MODEL), lambda bi, n: (bi, n, cq)),
                  pl.BlockSpec((1, WINDOW, D_KV), lambda bi, n: (bi, n, ck)),
                  pl.BlockSpec((1, WINDOW, D_KV), lambda bi, n: (bi, jnp.maximum(n - 1, 0), ck)),
                  pl.BlockSpec((1, WINDOW, D_KV), lambda bi, n: (bi, n, cv)),
                  pl.BlockSpec((1, WINDOW, D_KV), lambda bi, n: (bi, jnp.maximum(n - 1, 0), cv))],
        out_specs=pl.BlockSpec((1, WINDOW, D_MODEL), lambda bi, n: (bi, n, 0)),
        out_shape=jax.ShapeDtypeStruct((b, t, D_MODEL), F32),
        compiler_params=_cparams(("parallel", "arbitrary")),
        name="swa_prompt",
    )(slopes, sinks, proj3, proj3, proj3, proj3, proj3)


def _swa_sample_kernel(slope_ref, sink_ref, q_ref, kpast_ref, vpast_ref, knew_ref, vnew_ref,
                       o_ref, kwin_ref, vwin_ref, kc_ref, vc_ref, *, tq, kpad):
    nk = WINDOW + tq
    for cat, past, new, win in ((kc_ref, kpast_ref, knew_ref, kwin_ref), (vc_ref, vpast_ref, vnew_ref, vwin_ref)):
        cat[0:WINDOW, :] = past[0]
        cat[WINDOW:nk, :] = new[0]
        cat[nk:kpad, :] = jnp.zeros((kpad - nk, D_KV), F32)
        win[0] = cat[tq:nk, :]
    rows = GQA_GROUP * tq
    row = lax.broadcasted_iota(jnp.int32, (rows, kpad), 0)
    j = lax.broadcasted_iota(jnp.int32, (rows, kpad), 1)
    dist = lax.rem(row, tq) + WINDOW - j
    valid = (dist >= 0) & (dist <= WINDOW) & (j < nk)
    distf = dist.astype(F32)
    scale = HEAD_DIM ** -0.5
    for hk in range(N_KV_HEADS):
        ksl = slice(hk * HEAD_DIM, (hk + 1) * HEAD_DIM)
        kh = kc_ref[:, ksl].astype(BF16)
        vh = vc_ref[:, ksl].astype(BF16)
        s = _dot_nt(q_ref[0, hk].astype(BF16), kh) * scale - slope_ref[hk] * distf
        o_ref[0, hk] = _softmax_sink_pv(s, valid, sink_ref[hk], vh)


def _swa_sample(q_st, proj3, k_past, v_past, slope_rows, sink_rows):
    b, tq, _ = proj3.shape
    rows = GQA_GROUP * tq
    kpad = -(-(WINDOW + tq) // 16) * 16
    ck, cv = COL_KS // D_KV, COL_VS // D_KV
    kern = functools.partial(_swa_sample_kernel, tq=tq, kpad=kpad)
    full3 = lambda bi: (0, 0, 0)
    return pl.pallas_call(
        kern,
        grid=(b,),
        in_specs=[pl.BlockSpec((N_KV_HEADS, rows, 1), full3), pl.BlockSpec((N_KV_HEADS, rows, 1), full3),
                  pl.BlockSpec((1, N_KV_HEADS, rows, HEAD_DIM), lambda bi: (bi, 0, 0, 0)),
                  pl.BlockSpec((1, WINDOW, D_KV), lambda bi: (bi, 0, 0)),
                  pl.BlockSpec((1, WINDOW, D_KV), lambda bi: (bi, 0, 0)),
                  pl.BlockSpec((1, tq, D_KV), lambda bi: (bi, 0, ck)),
                  pl.BlockSpec((1, tq, D_KV), lambda bi: (bi, 0, cv))],
        out_specs=[pl.BlockSpec((1, N_KV_HEADS, rows, HEAD_DIM), lambda bi: (bi, 0, 0, 0)),
                   pl.BlockSpec((1, WINDOW, D_KV), lambda bi: (bi, 0, 0)),
                   pl.BlockSpec((1, WINDOW, D_KV), lambda bi: (bi, 0, 0))],
        out_shape=[jax.ShapeDtypeStruct((b, N_KV_HEADS, rows, HEAD_DIM), F32),
                   jax.ShapeDtypeStruct((b, WINDOW, D_KV), F32),
                   jax.ShapeDtypeStruct((b, WINDOW, D_KV), F32)],
        scratch_shapes=[pltpu.VMEM((kpad, D_KV), F32), pltpu.VMEM((kpad, D_KV), F32)],
        compiler_params=_cparams(("parallel",)),
        name="swa_sample",
    )(slope_rows, sink_rows, q_st, k_past, v_past, proj3, proj3)


def _merge_out_kernel(x_ref, ya_ref, yb_ref, ga_ref, gb_ref, wo_ref, nw_ref, h_ref, hn_ref):
    mixed = _sigmoid(ga_ref[...]) * ya_ref[...] + _sigmoid(gb_ref[...]) * yb_ref[...]
    h = x_ref[...] + _dot(mixed.astype(BF16), wo_ref[...])
    h_ref[...] = h
    ms = jnp.mean(h * h, axis=-1, keepdims=True)
    hn_ref[...] = (h * lax.rsqrt(ms + RMS_EPS) * nw_ref[...]).astype(BF16)


def _merge_out(x, ya, yb, proj, w_out, nw, tm):
    m, d = x.shape
    row = lambda o: (lambda i: (i, o))
    return pl.pallas_call(
        _merge_out_kernel,
        grid=(m // tm,),
        in_specs=[pl.BlockSpec((tm, d), row(0)), pl.BlockSpec((tm, d), row(0)), pl.BlockSpec((tm, d), row(0)),
                  pl.BlockSpec((tm, d), row(COL_GA // D_MODEL)), pl.BlockSpec((tm, d), row(COL_GB // D_MODEL)),
                  pl.BlockSpec((d, d), lambda i: (0, 0)), pl.BlockSpec((1, d), lambda i: (0, 0))],
        out_specs=[pl.BlockSpec((tm, d), row(0)), pl.BlockSpec((tm, d), row(0))],
        out_shape=[jax.ShapeDtypeStruct((m, d), F32), jax.ShapeDtypeStruct((m, d), BF16)],
        compiler_params=_cparams(("parallel",)),
        name="merge_out_proj",
    )(x, ya, yb, proj, proj, w_out, nw)


def _mlp_kernel(hn_ref, h_ref, wu_ref, wd_ref, nw_ref, o_ref, acc_ref):
    j = pl.program_id(1)

    @pl.when(j == 0)
    def _():
        acc_ref[...] = jnp.zeros_like(acc_ref)

    u = jnp.maximum(_dot(hn_ref[...], wu_ref[...]), 0.0)
    acc_ref[...] += _dot((u * u).astype(BF16), wd_ref[...])

    @pl.when(j == pl.num_programs(1) - 1)
    def _():
        h = h_ref[...] + acc_ref[...]
        ms = jnp.mean(h * h, axis=-1, keepdims=True)
        o_ref[...] = h * lax.rsqrt(ms + RMS_EPS) * nw_ref[...]


def _mlp(hn, h, w_up, w_down, nw, tm, tf):
    m, d = h.shape
    f = w_up.shape[1]
    return pl.pallas_call(
        _mlp_kernel,
        grid=(m // tm, f // tf),
        in_specs=[pl.BlockSpec((tm, d), lambda i, j: (i, 0)), pl.BlockSpec((tm, d), lambda i, j: (i, 0)),
                  pl.BlockSpec((d, tf), lambda i, j: (0, j)), pl.BlockSpec((tf, d), lambda i, j: (j, 0)),
                  pl.BlockSpec((1, d), lambda i, j: (0, 0))],
        out_specs=pl.BlockSpec((tm, d), lambda i, j: (i, 0)),
        out_shape=jax.ShapeDtypeStruct((m, d), F32),
        scratch_shapes=[pltpu.VMEM((tm, d), F32)],
        compiler_params=_cparams(("parallel", "arbitrary")),
        name="mlp_final_norm",
    )(hn, h, w_up, w_down, nw)


def _pick(m, prefs):
    for t in prefs:
        if m % t == 0:
            return t
    return m


def _pad_rows(w, rows):
    return jnp.concatenate([w, jnp.zeros((rows - w.shape[0], w.shape[1]), w.dtype)], axis=0)


def _layer(x, shift_prev, wkv0, lw, *, prompt, k_past=None, v_past=None):
    b, t, d = x.shape
    m = b * t
    x2 = x.reshape(m, d)
    proj = _norm_matmul(x2, lw["norm_mix_w"], lw["w_in"], _pick(m, (1024, 512, 256, 128, 8)), 512)
    proj3 = proj.reshape(b, t, C_PAD)

    sh_rkv, sh_lora = _split_r_cols(shift_prev)
    if prompt:
        ya, wkv_new = _rwkv_mix(proj3, sh_rkv[:, None], sh_lora[:, None], wkv0, lw, L=64, Lv=64, HPS=4)
        yb = _swa_prompt(proj3, lw["slopes"], lw["sinks"])
        k_win = proj3[:, t - WINDOW:, COL_KS:COL_KS + D_KV]
        v_win = proj3[:, t - WINDOW:, COL_VS:COL_VS + D_KV]
    else:
        ya, wkv_new = _rwkv_mix(proj3, sh_rkv[:, None], sh_lora[:, None], wkv0, lw, L=16, Lv=t, HPS=4)
        q = proj3[:, :, COL_Q:COL_Q + D_MODEL].reshape(b, t, N_KV_HEADS, GQA_GROUP, HEAD_DIM)
        q_st = q.transpose(0, 2, 3, 1, 4).reshape(b, N_KV_HEADS, GQA_GROUP * t, HEAD_DIM)
        slope_rows = jnp.repeat(lw["slopes"].reshape(N_KV_HEADS, GQA_GROUP), t, axis=1)[..., None]
        sink_rows = jnp.repeat(lw["sinks"].reshape(N_KV_HEADS, GQA_GROUP), t, axis=1)[..., None]
        o_st, k_win, v_win = _swa_sample(q_st, proj3, k_past, v_past, slope_rows, sink_rows)
        yb = o_st.reshape(b, N_KV_HEADS, GQA_GROUP, t, HEAD_DIM).transpose(0, 3, 1, 2, 4).reshape(b, t, D_MODEL)

    h, hn = _merge_out(x2, ya.reshape(m, d), yb.reshape(m, d), proj, lw["w_out"], lw["norm_mlp_w"],
                       _pick(m, (256, 128, 8)))
    y = _mlp(hn, h, lw["w_up"], lw["w_down"], lw["norm_final_w"], _pick(m, (512, 256, 128, 8)), 512)
    shift_new = _merge_r_cols(proj3[:, t - 1, :3 * D_RWKV], proj3[:, t - 1, COL_LORA:])
    return (y.reshape(b, t, d), shift_new, wkv_new,
            k_win.reshape(b, WINDOW, N_KV_HEADS, HEAD_DIM), v_win.reshape(b, WINDOW, N_KV_HEADS, HEAD_DIM))


def kernel(x_prompt, x_sample, state_shift, state_wkv, cache_k_win, cache_v_win, norm_mix_w, w_in, tshift_mu, w0, w_lora, a0, a_lora, g_lora, k_k, k_a, r_k, ln_x_w, ln_x_b, attn_sinks, w_out, norm_mlp_w, w_up, w_down, norm_final_w):
    depth = w_in.shape[0]
    assert depth == 1
    l = 0
    bp = x_prompt.shape[0]
    db = x_sample.shape[0]
    mu_rkv, mu_lora = _split_r_cols(tshift_mu[l][None])
    hh = jnp.arange(N_HEADS, dtype=F32)
    lw = dict(
        norm_mix_w=norm_mix_w[l][None], w_in=_reorder_in_cols(w_in[l]).astype(BF16),
        mu_rkv=mu_rkv, mu_lora=mu_lora, w0=w0[l][None], a0=a0[l][None], k_k=k_k[l][None], k_a=k_a[l][None],
        r_k=r_k[l].reshape(1, D_RWKV), ln_w=ln_x_w[l][None], ln_b=ln_x_b[l][None],
        w_lora=_pad_rows(w_lora[l], LORA_SLOT).astype(BF16), a_lora=_pad_rows(a_lora[l], LORA_SLOT).astype(BF16),
        g_lora=g_lora[l].astype(BF16),
        slopes=jnp.exp2(-8.0 * (hh + 1.0) / N_HEADS), sinks=attn_sinks[l].astype(F32),
        w_out=w_out[l].astype(BF16), norm_mlp_w=norm_mlp_w[l][None],
        w_up=w_up[l].astype(BF16), w_down=w_down[l].astype(BF16), norm_final_w=norm_final_w[None],
    )
    yp, sp, wp, kp, vp = _layer(
        x_prompt, jnp.zeros((bp, R_COLS), F32), jnp.zeros((bp, N_HEADS, HEAD_DIM, HEAD_DIM), F32), lw, prompt=True)
    ys, ss, ws, ksm, vsm = _layer(
        x_sample, state_shift[l], state_wkv[l], lw, prompt=False,
        k_past=cache_k_win[l].reshape(db, WINDOW, D_KV), v_past=cache_v_win[l].reshape(db, WINDOW, D_KV))
    return (yp, ys, sp[None], wp[None], kp[None], vp[None], ss[None], ws[None], ksm[None], vsm[None])
```

```python
import functools
import math

import jax
import jax.numpy as jnp
from jax import lax
from jax.experimental import pallas as pl
from jax.experimental.pallas import tpu as pltpu

F32 = jnp.float32
BF16 = jnp.bfloat16

D_MODEL = 2048
HEAD_DIM = 64
N_HEADS = D_MODEL // HEAD_DIM
N_KV_HEADS = 8
GQA_GROUP = N_HEADS // N_KV_HEADS
D_KV = N_KV_HEADS * HEAD_DIM
WINDOW = 128
D_FF = 4 * D_MODEL
D_DECAY_LORA = 96
D_A_LORA = 96
D_GATE_LORA = 256
D_RWKV = D_MODEL
R_COLS = 3 * D_RWKV + D_DECAY_LORA + D_A_LORA + D_GATE_LORA
C_IN = R_COLS + D_MODEL + 2 * D_KV + 2 * D_MODEL
RMS_EPS = 1e-5
GN_EPS = 64e-5

LANES = 128
LORA_SLOT = 128
LORA_W = 2 * LORA_SLOT + D_GATE_LORA
COL_RKV = 0
COL_Q = 3 * D_RWKV
COL_GA = COL_Q + D_MODEL
COL_GB = COL_GA + D_MODEL
COL_KS = COL_GB + D_MODEL
COL_VS = COL_KS + D_KV
COL_LORA = COL_VS + D_KV
C_PAD = COL_LORA + LORA_W

VMEM_LIMIT = 56 * 1024 * 1024


def _cparams(sem):
    return pltpu.CompilerParams(dimension_semantics=sem, vmem_limit_bytes=VMEM_LIMIT)


def _dot(a, b):
    return jnp.dot(a, b, preferred_element_type=F32)


def _dot_nt(a, b):
    return lax.dot_general(a, b, (((1,), (1,)), ((), ())), preferred_element_type=F32)


def _dot_tn(a, b):
    return lax.dot_general(a, b, (((0,), (0,)), ((), ())), preferred_element_type=F32)


def _split_r_cols(v):
    o = 3 * D_RWKV
    dw = v[..., o:o + D_DECAY_LORA]
    da = v[..., o + D_DECAY_LORA:o + D_DECAY_LORA + D_A_LORA]
    dg = v[..., o + D_DECAY_LORA + D_A_LORA:R_COLS]
    z = jnp.zeros(v.shape[:-1] + (LORA_SLOT - D_DECAY_LORA,), v.dtype)
    return v[..., :o], jnp.concatenate([dw, z, da, z, dg], axis=-1)


def _merge_r_cols(rkv, lora):
    return jnp.concatenate([rkv, lora[..., :D_DECAY_LORA], lora[..., LORA_SLOT:LORA_SLOT + D_A_LORA],
                            lora[..., 2 * LORA_SLOT:]], axis=-1)


def _reorder_in_cols(w):
    rkv, lora = _split_r_cols(w[..., :R_COLS])
    o = R_COLS
    q = w[..., o:o + D_MODEL]
    ks = w[..., o + D_MODEL:o + D_MODEL + D_KV]
    vs = w[..., o + D_MODEL + D_KV:o + D_MODEL + 2 * D_KV]
    ga = w[..., o + D_MODEL + 2 * D_KV:o + 2 * D_MODEL + 2 * D_KV]
    gb = w[..., o + 2 * D_MODEL + 2 * D_KV:]
    return jnp.concatenate([rkv, q, ga, gb, ks, vs, lora], axis=-1)


def _norm_matmul_kernel(x_ref, nw_ref, w_ref, o_ref, xn_ref):
    @pl.when(pl.program_id(1) == 0)
    def _():
        x = x_ref[...]
        ms = jnp.mean(x * x, axis=-1, keepdims=True)
        xn_ref[...] = (x * lax.rsqrt(ms + RMS_EPS) * nw_ref[...]).astype(BF16)

    o_ref[...] = _dot(xn_ref[...], w_ref[...])


def _norm_matmul(x, nw, w, tm, tn):
    m, d = x.shape
    n = w.shape[1]
    return pl.pallas_call(
        _norm_matmul_kernel,
        grid=(m // tm, n // tn),
        in_specs=[pl.BlockSpec((tm, d), lambda i, j: (i, 0)),
                  pl.BlockSpec((1, d), lambda i, j: (0, 0)),
                  pl.BlockSpec((d, tn), lambda i, j: (0, j))],
        out_specs=pl.BlockSpec((tm, tn), lambda i, j: (i, j)),
        out_shape=jax.ShapeDtypeStruct((m, n), F32),
        scratch_shapes=[pltpu.VMEM((tm, d), BF16)],
        compiler_params=_cparams(("parallel", "arbitrary")),
        name="norm_in_proj",
    )(x, nw, w)


def _softplus(x):
    return jnp.maximum(x, 0.0) + jnp.log(1.0 + jnp.exp(-jnp.abs(x)))


def _sigmoid(x):
    return 1.0 / (1.0 + jnp.exp(-x))


def _split2(x):
    hi = x.astype(BF16)
    lo = (x - hi.astype(F32)).astype(BF16)
    return hi, lo


def _split3(x):
    hi = x.astype(BF16)
    r1 = x - hi.astype(F32)
    mid = r1.astype(BF16)
    lo = (r1 - mid.astype(F32)).astype(BF16)
    return hi, mid, lo


def _rwkv_kernel(pr_ref, pk_ref, pv_ref, plo_ref, shr_ref, shk_ref, shv_ref, shl_ref, s0_ref,
                 mur_ref, muk_ref, muv_ref, mul_ref, w0_ref, a0_ref, kk_ref, ka_ref, rk_ref, lnw_ref, lnb_ref,
                 wl_ref, al_ref, gl_ref,
                 y_ref, so_ref,
                 sd_ref, br_ref, bk_ref, bv_ref, bl_ref, *, L, Lv, HPS, BB, NC, n_dbl):
    c = pl.program_id(2)
    H = HEAD_DIM
    W = HPS * LANES
    sls = [slice(i * LANES, (i + 1) * LANES) for i in range(HPS)]

    lane = lax.broadcasted_iota(jnp.int32, (L, LANES), 1)
    head1 = lane < H
    row_i = lax.broadcasted_iota(jnp.int32, (L, 2 * L), 0)
    col_i = lax.broadcasted_iota(jnp.int32, (L, 2 * L), 1)
    col_t = jnp.where(col_i >= L, col_i - L, col_i)
    strict = col_t < row_i
    incl = col_t <= row_i
    colh1 = col_i < L
    tri = (lax.broadcasted_iota(jnp.int32, (L, L), 1) <= lax.broadcasted_iota(jnp.int32, (L, L), 0)).astype(BF16)
    ji = lax.broadcasted_iota(jnp.int32, (LANES, LANES), 0)
    jj = lax.broadcasted_iota(jnp.int32, (LANES, LANES), 1)
    same_head = (ji < H) == (jj < H)
    seg = same_head.astype(BF16)
    rowv = lax.broadcasted_iota(jnp.int32, (L, W), 0) < Lv

    def seg_sum(x):
        xs = jnp.concatenate([x[:, s] for s in sls], axis=0)
        hi, lo = _split2(xs)
        ys = _dot(hi, seg) + _dot(lo, seg)
        return jnp.concatenate([ys[i * L:(i + 1) * L] for i in range(HPS)], axis=1)

    def stack_heads(x, m):
        zero = jnp.zeros_like(x)
        return jnp.concatenate([jnp.where(m, x, zero), jnp.where(m, zero, x)], axis=0)

    def init(bi):
        br_ref[bi, 7:8, :] = shr_ref[bi]
        bk_ref[bi, 7:8, :] = shk_ref[bi]
        bv_ref[bi, 7:8, :] = shv_ref[bi]
        bl_ref[bi, 7:8, :] = shl_ref[bi]
        if Lv < L:
            for ref in (br_ref, bk_ref, bv_ref, bl_ref):
                ref[bi, 8 + Lv:8 + L, :] = jnp.zeros((L - Lv, ref.shape[2]), F32)
        z = jnp.zeros((H, H), F32)
        for i in range(HPS):
            top = jnp.concatenate([s0_ref[bi, 2 * i], z], axis=1)
            bot = jnp.concatenate([z, s0_ref[bi, 2 * i + 1]], axis=1)
            sd_ref[bi, i] = jnp.concatenate([top, bot], axis=0)

    def finish(bi):
        for i in range(HPS):
            sd = sd_ref[bi, i]
            so_ref[bi, 2 * i] = sd[0:H, 0:H]
            so_ref[bi, 2 * i + 1] = sd[H:2 * H, H:2 * H]

    def one_batch(bi):
        if NC == 1:
            init(bi)
        else:
            pl.when(c == 0)(lambda: init(bi))

        br_ref[bi, 8:8 + Lv, :] = pr_ref[bi]
        bk_ref[bi, 8:8 + Lv, :] = pk_ref[bi]
        bv_ref[bi, 8:8 + Lv, :] = pv_ref[bi]
        bl_ref[bi, 8:8 + Lv, :] = plo_ref[bi]

        def shifted(buf, mu):
            p = buf[bi, 8:8 + L, :]
            return p + mu * (buf[bi, 7:7 + L, :] - p)

        ps_l = shifted(bl_ref, mul_ref[...])
        td = jnp.tanh(ps_l[:, 0:LORA_SLOT]).astype(BF16)
        da = ps_l[:, LORA_SLOT:2 * LORA_SLOT].astype(BF16)
        sg = _sigmoid(ps_l[:, 2 * LORA_SLOT:]).astype(BF16)
        r = shifted(br_ref, mur_ref[...])
        k = shifted(bk_ref, muk_ref[...])
        v = shifted(bv_ref, muv_ref[...])

        zlog = w0_ref[...] + _dot(td, wl_ref[...])
        logw = -jnp.exp(-_softplus(-zlog) - 0.5)
        a_sig = _sigmoid(a0_ref[...] + _dot(da, al_ref[...]))
        kk = k * kk_ref[...]
        nrm = jnp.sqrt(seg_sum(kk * kk))
        kk = kk / jnp.maximum(nrm, 1e-12)
        k_h = k * (1.0 + (a_sig - 1.0) * ka_ref[...])
        if Lv < L:
            logw = jnp.where(rowv, logw, 0.0)
            kk = jnp.where(rowv, kk, 0.0)
            k_h = jnp.where(rowv, k_h, 0.0)
            v = jnp.where(rowv, v, 0.0)

        hi, mid, lo = _split3(logw)
        cum = _dot(tri, hi) + _dot(tri, mid) + _dot(tri, lo)
        cum_l = cum[L - 1:L, :]
        p_inv = jnp.exp(-cum)
        p_end = jnp.exp(cum_l - cum)
        bvec = kk * a_sig
        a_t = (-kk * jnp.exp(cum - logw)).astype(BF16)
        r_t = (r * jnp.exp(cum)).astype(BF16)
        b_t = (bvec * p_inv).astype(BF16)
        k_t = (k_h * p_inv).astype(BF16)
        b_e = (bvec * p_end).astype(BF16)
        k_e = (k_h * p_end).astype(BF16)
        v_b = v.astype(BF16)
        p_l = jnp.exp(cum_l)

        P = range(HPS)
        lhs = [jnp.concatenate([a_t[:, s], r_t[:, s]], axis=0) for s in sls]
        rhs = [jnp.concatenate([stack_heads(b_t[:, s], head1), stack_heads(k_t[:, s], head1)], axis=0) for s in sls]
        sd = [sd_ref[bi, i] for i in P]
        aa = [_dot_nt(lhs[i], rhs[i]) for i in P]
        sa = [_dot_nt(lhs[i], sd[i].astype(BF16)) for i in P]
        v_st = [stack_heads(v_b[:, s], head1) for s in sls]
        a_ak = [jnp.where(strict, aa[i][0:L, 2 * L:4 * L], 0.0).astype(BF16) for i in P]
        x = [sa[i][0:L] + _dot(a_ak[i], v_st[i]) for i in P]
        ap = [jnp.where(strict, aa[i][0:L, 0:2 * L], 0.0).astype(BF16) for i in P]
        for d in range(n_dbl):
            x = [x[i] + _dot(ap[i], stack_heads(x[i].astype(BF16), head1)) for i in P]
            if d + 1 < n_dbl:
                ap = [_dot(ap[i], stack_heads(ap[i], colh1)).astype(BF16) for i in P]
        u_b = [x[i].astype(BF16) for i in P]
        a_rb = [jnp.where(incl, aa[i][L:2 * L, 0:2 * L], 0.0).astype(BF16) for i in P]
        a_rk = [jnp.where(incl, aa[i][L:2 * L, 2 * L:4 * L], 0.0).astype(BF16) for i in P]
        y = [sa[i][L:2 * L] + _dot(a_rb[i], stack_heads(u_b[i], head1)) + _dot(a_rk[i], v_st[i]) for i in P]
        ds = [_dot_tn(jnp.concatenate([u_b[i], v_b[:, sls[i]]], axis=0),
                      jnp.concatenate([b_e[:, sls[i]], k_e[:, sls[i]]], axis=0)) for i in P]
        for i in P:
            sd_ref[bi, i] = sd[i] * p_l[:, sls[i]] + jnp.where(same_head, ds[i], 0.0)

        y = jnp.concatenate(y, axis=1)
        mean = seg_sum(y) * (1.0 / H)
        dlt = y - mean
        var = seg_sum(dlt * dlt) * (1.0 / H)
        yn = dlt * lax.rsqrt(var + GN_EPS) * lnw_ref[...] + lnb_ref[...]
        bonus = seg_sum(r * k_h * rk_ref[...]) * v
        g = _dot(sg, gl_ref[...])
        out = (yn + bonus) * g
        y_ref[bi] = out[0:Lv]

        br_ref[bi, 7:8, :] = br_ref[bi, 7 + Lv:8 + Lv, :]
        bk_ref[bi, 7:8, :] = bk_ref[bi, 7 + Lv:8 + Lv, :]
        bv_ref[bi, 7:8, :] = bv_ref[bi, 7 + Lv:8 + Lv, :]
        bl_ref[bi, 7:8, :] = bl_ref[bi, 7 + Lv:8 + Lv, :]

        if NC == 1:
            finish(bi)
        else:
            pl.when(c == NC - 1)(lambda: finish(bi))

    if BB == 1:
        one_batch(0)
    else:
        def loop_body(bi, carry):
            one_batch(bi)
            return carry
        lax.fori_loop(0, BB, loop_body, 0)


def _rwkv_mix(proj3, sh_rkv, sh_lora, s0, p, *, L, Lv, HPS, BB):
    b, t, _ = proj3.shape
    assert t % Lv == 0 and b % BB == 0
    nc = t // Lv
    w = HPS * LANES
    nhg = D_RWKV // w
    kb = D_RWKV // w
    n_dbl = max(1, math.ceil(math.log2(Lv)))
    col = lambda o: (lambda bi, g, c: (bi, c, o + g))
    sh = lambda o: (lambda bi, g, c: (bi, 0, o + g))
    par = lambda o: (lambda bi, g, c: (0, o + g))
    fixed2 = lambda bi, g, c: (0, 0)
    in_specs = [
        pl.BlockSpec((BB, Lv, w), col(0)), pl.BlockSpec((BB, Lv, w), col(kb)), pl.BlockSpec((BB, Lv, w), col(2 * kb)),
        pl.BlockSpec((BB, Lv, LORA_W), lambda bi, g, c: (bi, c, COL_LORA // LORA_W)),
        pl.BlockSpec((BB, 1, w), sh(0)), pl.BlockSpec((BB, 1, w), sh(kb)), pl.BlockSpec((BB, 1, w), sh(2 * kb)),
        pl.BlockSpec((BB, 1, LORA_W), lambda bi, g, c: (bi, 0, 0)),
        pl.BlockSpec((BB, 2 * HPS, HEAD_DIM, HEAD_DIM), lambda bi, g, c: (bi, g, 0, 0)),
        pl.BlockSpec((1, w), par(0)), pl.BlockSpec((1, w), par(kb)), pl.BlockSpec((1, w), par(2 * kb)),
        pl.BlockSpec((1, LORA_W), fixed2),
    ] + [pl.BlockSpec((1, w), par(0))] * 7 + [
        pl.BlockSpec((LORA_SLOT, w), par(0)), pl.BlockSpec((LORA_SLOT, w), par(0)),
        pl.BlockSpec((D_GATE_LORA, w), par(0)),
    ]
    out_specs = [pl.BlockSpec((BB, Lv, w), col(0)),
                 pl.BlockSpec((BB, 2 * HPS, HEAD_DIM, HEAD_DIM), lambda bi, g, c: (bi, g, 0, 0))]
    kern = functools.partial(_rwkv_kernel, L=L, Lv=Lv, HPS=HPS, BB=BB, NC=nc, n_dbl=n_dbl)
    return pl.pallas_call(
        kern,
        grid=(b // BB, nhg, nc),
        in_specs=in_specs,
        out_specs=out_specs,
        out_shape=[jax.ShapeDtypeStruct((b, t, D_RWKV), F32),
                   jax.ShapeDtypeStruct((b, N_HEADS, HEAD_DIM, HEAD_DIM), F32)],
        scratch_shapes=[pltpu.VMEM((BB, HPS, LANES, LANES), F32),
                        pltpu.VMEM((BB, L + 8, w), F32), pltpu.VMEM((BB, L + 8, w), F32),
                        pltpu.VMEM((BB, L + 8, w), F32), pltpu.VMEM((BB, L + 8, LORA_W), F32)],
        compiler_params=_cparams(("parallel", "parallel", "arbitrary")),
        name="rwkv_mix",
    )(proj3, proj3, proj3, proj3, sh_rkv, sh_rkv, sh_rkv, sh_lora, s0,
      p["mu_rkv"], p["mu_rkv"], p["mu_rkv"], p["mu_lora"], p["w0"], p["a0"], p["k_k"], p["k_a"], p["r_k"],
      p["ln_w"], p["ln_b"], p["w_lora"], p["a_lora"], p["g_lora"])


def _softmax_sink_pv(s, valid, sink, v_b):
    s = jnp.where(valid, s, -jnp.inf)
    m = jnp.maximum(jnp.max(s, axis=-1, keepdims=True), sink)
    p = jnp.exp(s - m)
    den = jnp.sum(p, axis=-1, keepdims=True) + jnp.exp(sink - m)
    return _dot(p.astype(BF16), v_b) / den


def _swa_prompt_kernel(slope_ref, sink_ref, q_ref, kc_ref, kp_ref, vc_ref, vp_ref, o_ref):
    n = pl.program_id(1)
    blk = WINDOW
    t = lax.broadcasted_iota(jnp.int32, (blk, 2 * blk), 0)
    j = lax.broadcasted_iota(jnp.int32, (blk, 2 * blk), 1)
    dist = t - j + blk
    valid = (dist >= 0) & (dist <= WINDOW) & ((j >= blk) | (n > 0))
    distf = dist.astype(F32)
    scale = HEAD_DIM ** -0.5
    for hk in range(N_KV_HEADS):
        ksl = slice(hk * HEAD_DIM, (hk + 1) * HEAD_DIM)
        kcat = jnp.concatenate([kp_ref[0, :, ksl], kc_ref[0, :, ksl]], axis=0).astype(BF16)
        vcat = jnp.concatenate([vp_ref[0, :, ksl], vc_ref[0, :, ksl]], axis=0).astype(BF16)
        for pair in range(GQA_GROUP // 2):
            outs = []
            for g2 in range(2):
                h = hk * GQA_GROUP + pair * 2 + g2
                qh = q_ref[0, :, h * HEAD_DIM:(h + 1) * HEAD_DIM].astype(BF16)
                s = _dot_nt(qh, kcat) * scale - slope_ref[h] * distf
                outs.append(_softmax_sink_pv(s, valid, sink_ref[h], vcat))
            h0 = hk * GQA_GROUP + pair * 2
            o_ref[0, :, h0 * HEAD_DIM:(h0 + 2) * HEAD_DIM] = jnp.concatenate(outs, axis=1)


def _swa_prompt(proj3, slopes, sinks):
    b, t, _ = proj3.shape
    nb = t // WINDOW
    smem = pl.BlockSpec(memory_space=pltpu.SMEM)
    cq, ck, cv = COL_Q // D_MODEL, COL_KS // D_KV, COL_VS // D_KV
    return pl.pallas_call(
        _swa_prompt_kernel,
        grid=(b, nb),
        in_specs=[smem, smem,
                  pl.BlockSpec((1, WINDOW, D_MODEL), lambda bi, n: (bi, n, cq)),
                  pl.BlockSpec((1, WINDOW, D_KV), lambda bi, n: (bi, n, ck)),
                  pl.BlockSpec((1, WINDOW, D_KV), lambda bi, n: (bi, jnp.maximum(n - 1, 0), ck)),
                  pl.BlockSpec((1, WINDOW, D_KV), lambda bi, n: (bi, n, cv)),
                  pl.BlockSpec((1, WINDOW, D_KV), lambda bi, n: (bi, jnp.maximum(n - 1, 0), cv))],
        out_specs=pl.BlockSpec((1, WINDOW, D_MODEL), lambda bi, n: (bi, n, 0)),
        out_shape=jax.ShapeDtypeStruct((b, t, D_MODEL), F32),
        compiler_params=_cparams(("parallel", "arbitrary")),
        name="swa_prompt",
    )(slopes, sinks, proj3, proj3, proj3, proj3, proj3)


def _swa_sample_kernel(slope_ref, sink_ref, q_ref, kpast_ref, vpast_ref, knew_ref, vnew_ref,
                       o_ref, kwin_ref, vwin_ref, kc_ref, vc_ref, *, tq, kpad):
    nk = WINDOW + tq
    for cat, past, new, win in ((kc_ref, kpast_ref, knew_ref, kwin_ref), (vc_ref, vpast_ref, vnew_ref, vwin_ref)):
        cat[0:WINDOW, :] = past[0]
        cat[WINDOW:nk, :] = new[0]
        cat[nk:kpad, :] = jnp.zeros((kpad - nk, D_KV), F32)
        win[0] = cat[tq:nk, :]
    rows = GQA_GROUP * tq
    row = lax.broadcasted_iota(jnp.int32, (rows, kpad), 0)
    j = lax.broadcasted_iota(jnp.int32, (rows, kpad), 1)
    dist = lax.rem(row, tq) + WINDOW - j
    valid = (dist >= 0) & (dist <= WINDOW) & (j < nk)
    distf = dist.astype(F32)
    scale = HEAD_DIM ** -0.5
    for hk in range(N_KV_HEADS):
        ksl = slice(hk * HEAD_DIM, (hk + 1) * HEAD_DIM)
        kh = kc_ref[:, ksl].astype(BF16)
        vh = vc_ref[:, ksl].astype(BF16)
        s = _dot_nt(q_ref[0, hk].astype(BF16), kh) * scale - slope_ref[hk] * distf
        o_ref[0, hk] = _softmax_sink_pv(s, valid, sink_ref[hk], vh)


def _swa_sample(q_st, proj3, k_past, v_past, slope_rows, sink_rows):
    b, tq, _ = proj3.shape
    rows = GQA_GROUP * tq
    kpad = -(-(WINDOW + tq) // 16) * 16
    ck, cv = COL_KS // D_KV, COL_VS // D_KV
    kern = functools.partial(_swa_sample_kernel, tq=tq, kpad=kpad)
    full3 = lambda bi: (0, 0, 0)
    return pl.pallas_call(
        kern,
        grid=(b,),
        in_specs=[pl.BlockSpec((N_KV_HEADS, rows, 1), full3), pl.BlockSpec((N_KV_HEADS, rows, 1), full3),
                  pl.BlockSpec((1, N_KV_HEADS, rows, HEAD_DIM), lambda bi: (bi, 0, 0, 0)),
                  pl.BlockSpec((1, WINDOW, D_KV), lambda bi: (bi, 0, 0)),
                  pl.BlockSpec((1, WINDOW, D_KV), lambda bi: (bi, 0, 0)),
                  pl.BlockSpec((1, tq, D_KV), lambda bi: (bi, 0, ck)),
                  pl.BlockSpec((1, tq, D_KV), lambda bi: (bi, 0, cv))],
        out_specs=[pl.BlockSpec((1, N_KV_HEADS, rows, HEAD_DIM), lambda bi: (bi, 0, 0, 0)),
                   pl.BlockSpec((1, WINDOW, D_KV), lambda bi: (bi, 0, 0)),
                   pl.BlockSpec((1, WINDOW, D_KV), lambda bi: (bi, 0, 0))],
        out_shape=[jax.ShapeDtypeStruct((b, N_KV_HEADS, rows, HEAD_DIM), F32),
                   jax.ShapeDtypeStruct((b, WINDOW, D_KV), F32),
                   jax.ShapeDtypeStruct((b, WINDOW, D_KV), F32)],
        scratch_shapes=[pltpu.VMEM((kpad, D_KV), F32), pltpu.VMEM((kpad, D_KV), F32)],
        compiler_params=_cparams(("parallel",)),
        name="swa_sample",
    )(slope_rows, sink_rows, q_st, k_past, v_past, proj3, proj3)


def _merge_out_kernel(x_ref, ya_ref, yb_ref, ga_ref, gb_ref, wo_ref, nw_ref, h_ref, hn_ref):
    mixed = _sigmoid(ga_ref[...]) * ya_ref[...] + _sigmoid(gb_ref[...]) * yb_ref[...]
    h = x_ref[...] + _dot(mixed.astype(BF16), wo_ref[...])
    h_ref[...] = h
    ms = jnp.mean(h * h, axis=-1, keepdims=True)
    hn_ref[...] = (h * lax.rsqrt(ms + RMS_EPS) * nw_ref[...]).astype(BF16)


def _merge_out(x, ya, yb, proj, w_out, nw, tm):
    m, d = x.shape
    row = lambda o: (lambda i: (i, o))
    return pl.pallas_call(
        _merge_out_kernel,
        grid=(m // tm,),
        in_specs=[pl.BlockSpec((tm, d), row(0)), pl.BlockSpec((tm, d), row(0)), pl.BlockSpec((tm, d), row(0)),
                  pl.BlockSpec((tm, d), row(COL_GA // D_MODEL)), pl.BlockSpec((tm, d), row(COL_GB // D_MODEL)),
                  pl.BlockSpec((d, d), lambda i: (0, 0)), pl.BlockSpec((1, d), lambda i: (0, 0))],
        out_specs=[pl.BlockSpec((tm, d), row(0)), pl.BlockSpec((tm, d), row(0))],
        out_shape=[jax.ShapeDtypeStruct((m, d), F32), jax.ShapeDtypeStruct((m, d), BF16)],
        compiler_params=_cparams(("parallel",)),
        name="merge_out_proj",
    )(x, ya, yb, proj, proj, w_out, nw)


def _mlp_kernel(hn_ref, h_ref, wu_ref, wd_ref, nw_ref, o_ref, acc_ref):
    j = pl.program_id(1)

    @pl.when(j == 0)
    def _():
        acc_ref[...] = jnp.zeros_like(acc_ref)

    u = jnp.maximum(_dot(hn_ref[...], wu_ref[...]), 0.0)
    acc_ref[...] += _dot((u * u).astype(BF16), wd_ref[...])

    @pl.when(j == pl.num_programs(1) - 1)
    def _():
        h = h_ref[...] + acc_ref[...]
        ms = jnp.mean(h * h, axis=-1, keepdims=True)
        o_ref[...] = h * lax.rsqrt(ms + RMS_EPS) * nw_ref[...]


def _mlp(hn, h, w_up, w_down, nw, tm, tf):
    m, d = h.shape
    f = w_up.shape[1]
    return pl.pallas_call(
        _mlp_kernel,
        grid=(m // tm, f // tf),
        in_specs=[pl.BlockSpec((tm, d), lambda i, j: (i, 0)), pl.BlockSpec((tm, d), lambda i, j: (i, 0)),
                  pl.BlockSpec((d, tf), lambda i, j: (0, j)), pl.BlockSpec((tf, d), lambda i, j: (j, 0)),
                  pl.BlockSpec((1, d), lambda i, j: (0, 0))],
        out_specs=pl.BlockSpec((tm, d), lambda i, j: (i, 0)),
        out_shape=jax.ShapeDtypeStruct((m, d), F32),
        scratch_shapes=[pltpu.VMEM((tm, d), F32)],
        compiler_params=_cparams(("parallel", "arbitrary")),
        name="mlp_final_norm",
    )(hn, h, w_up, w_down, nw)


def _pick(m, prefs):
    for t in prefs:
        if m % t == 0:
            return t
    return m


def _pad_rows(w, rows):
    return jnp.concatenate([w, jnp.zeros((rows - w.shape[0], w.shape[1]), w.dtype)], axis=0)


def _layer(x, shift_prev, wkv0, lw, *, prompt, k_past=None, v_past=None):
    b, t, d = x.shape
    m = b * t
    x2 = x.reshape(m, d)
    proj = _norm_matmul(x2, lw["norm_mix_w"], lw["w_in"], _pick(m, (1024, 512, 256, 128, 8)), 512)
    proj3 = proj.reshape(b, t, C_PAD)

    sh_rkv, sh_lora = _split_r_cols(shift_prev)
    if prompt:
        ya, wkv_new = _rwkv_mix(proj3, sh_rkv[:, None], sh_lora[:, None], wkv0, lw, L=64, Lv=64, HPS=16, BB=1)
        yb = _swa_prompt(proj3, lw["slopes"], lw["sinks"])
        k_win = proj3[:, t - WINDOW:, COL_KS:COL_KS + D_KV]
        v_win = proj3[:, t - WINDOW:, COL_VS:COL_VS + D_KV]
    else:
        ya, wkv_new = _rwkv_mix(proj3, sh_rkv[:, None], sh_lora[:, None], wkv0, lw, L=16, Lv=t, HPS=16,
                                BB=_pick(b, (4, 2, 1)))
        q = proj3[:, :, COL_Q:COL_Q + D_MODEL].reshape(b, t, N_KV_HEADS, GQA_GROUP, HEAD_DIM)
        q_st = q.transpose(0, 2, 3, 1, 4).reshape(b, N_KV_HEADS, GQA_GROUP * t, HEAD_DIM)
        slope_rows = jnp.repeat(lw["slopes"].reshape(N_KV_HEADS, GQA_GROUP), t, axis=1)[..., None]
        sink_rows = jnp.repeat(lw["sinks"].reshape(N_KV_HEADS, GQA_GROUP), t, axis=1)[..., None]
        o_st, k_win, v_win = _swa_sample(q_st, proj3, k_past, v_past, slope_rows, sink_rows)
        yb = o_st.reshape(b, N_KV_HEADS, GQA_GROUP, t, HEAD_DIM).transpose(0, 3, 1, 2, 4).reshape(b, t, D_MODEL)

    h, hn = _merge_out(x2, ya.reshape(m, d), yb.reshape(m, d), proj, lw["w_out"], lw["norm_mlp_w"],
                       _pick(m, (256, 128, 8)))
    y = _mlp(hn, h, lw["w_up"], lw["w_down"], lw["norm_final_w"], _pick(m, (512, 256, 128, 8)), 512)
    shift_new = _merge_r_cols(proj3[:, t - 1, :3 * D_RWKV], proj3[:, t - 1, COL_LORA:])
    return (y.reshape(b, t, d), shift_new, wkv_new,
            k_win.reshape(b, WINDOW, N_KV_HEADS, HEAD_DIM), v_win.reshape(b, WINDOW, N_KV_HEADS, HEAD_DIM))


def kernel(x_prompt, x_sample, state_shift, state_wkv, cache_k_win, cache_v_win, norm_mix_w, w_in, tshift_mu, w0, w_lora, a0, a_lora, g_lora, k_k, k_a, r_k, ln_x_w, ln_x_b, attn_sinks, w_out, norm_mlp_w, w_up, w_down, norm_final_w):
    depth = w_in.shape[0]
    assert depth == 1
    l = 0
    bp = x_prompt.shape[0]
    db = x_sample.shape[0]
    mu_rkv, mu_lora = _split_r_cols(tshift_mu[l][None])
    hh = jnp.arange(N_HEADS, dtype=F32)
    lw = dict(
        norm_mix_w=norm_mix_w[l][None], w_in=_reorder_in_cols(w_in[l]).astype(BF16),
        mu_rkv=mu_rkv, mu_lora=mu_lora, w0=w0[l][None], a0=a0[l][None], k_k=k_k[l][None], k_a=k_a[l][None],
        r_k=r_k[l].reshape(1, D_RWKV), ln_w=ln_x_w[l][None], ln_b=ln_x_b[l][None],
        w_lora=_pad_rows(w_lora[l], LORA_SLOT).astype(BF16), a_lora=_pad_rows(a_lora[l], LORA_SLOT).astype(BF16),
        g_lora=g_lora[l].astype(BF16),
        slopes=jnp.exp2(-8.0 * (hh + 1.0) / N_HEADS), sinks=attn_sinks[l].astype(F32),
        w_out=w_out[l].astype(BF16), norm_mlp_w=norm_mlp_w[l][None],
        w_up=w_up[l].astype(BF16), w_down=w_down[l].astype(BF16), norm_final_w=norm_final_w[None],
    )
    yp, sp, wp, kp, vp = _layer(
        x_prompt, jnp.zeros((bp, R_COLS), F32), jnp.zeros((bp, N_HEADS, HEAD_DIM, HEAD_DIM), F32), lw, prompt=True)
    ys, ss, ws, ksm, vsm = _layer(
        x_sample, state_shift[l], state_wkv[l], lw, prompt=False,
        k_past=cache_k_win[l].reshape(db, WINDOW, D_KV), v_past=cache_v_win[l].reshape(db, WINDOW, D_KV))
    return (yp, ys, sp[None], wp[None], kp[None], vp[None], ss[None], ws[None], ksm[None], vsm[None])
```

```python
import functools
import math

import jax
import jax.numpy as jnp
from jax import lax
from jax.experimental import pallas as pl
from jax.experimental.pallas import tpu as pltpu

F32 = jnp.float32
BF16 = jnp.bfloat16

D_MODEL = 2048
HEAD_DIM = 64
N_HEADS = D_MODEL // HEAD_DIM
N_KV_HEADS = 8
GQA_GROUP = N_HEADS // N_KV_HEADS
D_KV = N_KV_HEADS * HEAD_DIM
WINDOW = 128
D_FF = 4 * D_MODEL
D_DECAY_LORA = 96
D_A_LORA = 96
D_GATE_LORA = 256
D_RWKV = D_MODEL
R_COLS = 3 * D_RWKV + D_DECAY_LORA + D_A_LORA + D_GATE_LORA
C_IN = R_COLS + D_MODEL + 2 * D_KV + 2 * D_MODEL
RMS_EPS = 1e-5
GN_EPS = 64e-5

LANES = 128
LORA_SLOT = 128
LORA_W = 2 * LORA_SLOT + D_GATE_LORA
COL_RKV = 0
COL_Q = 3 * D_RWKV
COL_GA = COL_Q + D_MODEL
COL_GB = COL_GA + D_MODEL
COL_KS = COL_GB + D_MODEL
COL_VS = COL_KS + D_KV
COL_LORA = COL_VS + D_KV
C_PAD = COL_LORA + LORA_W

VMEM_LIMIT = 56 * 1024 * 1024


def _cparams(sem):
    return pltpu.CompilerParams(dimension_semantics=sem, vmem_limit_bytes=VMEM_LIMIT)


def _dot(a, b):
    return jnp.dot(a, b, preferred_element_type=F32)


def _dot_nt(a, b):
    return lax.dot_general(a, b, (((1,), (1,)), ((), ())), preferred_element_type=F32)


def _dot_tn(a, b):
    return lax.dot_general(a, b, (((0,), (0,)), ((), ())), preferred_element_type=F32)


def _split_r_cols(v):
    o = 3 * D_RWKV
    dw = v[..., o:o + D_DECAY_LORA]
    da = v[..., o + D_DECAY_LORA:o + D_DECAY_LORA + D_A_LORA]
    dg = v[..., o + D_DECAY_LORA + D_A_LORA:R_COLS]
    z = jnp.zeros(v.shape[:-1] + (LORA_SLOT - D_DECAY_LORA,), v.dtype)
    return v[..., :o], jnp.concatenate([dw, z, da, z, dg], axis=-1)


def _merge_r_cols(rkv, lora):
    return jnp.concatenate([rkv, lora[..., :D_DECAY_LORA], lora[..., LORA_SLOT:LORA_SLOT + D_A_LORA],
                            lora[..., 2 * LORA_SLOT:]], axis=-1)


def _reorder_in_cols(w):
    rkv, lora = _split_r_cols(w[..., :R_COLS])
    o = R_COLS
    q = w[..., o:o + D_MODEL]
    ks = w[..., o + D_MODEL:o + D_MODEL + D_KV]
    vs = w[..., o + D_MODEL + D_KV:o + D_MODEL + 2 * D_KV]
    ga = w[..., o + D_MODEL + 2 * D_KV:o + 2 * D_MODEL + 2 * D_KV]
    gb = w[..., o + 2 * D_MODEL + 2 * D_KV:]
    return jnp.concatenate([rkv, q, ga, gb, ks, vs, lora], axis=-1)


def _norm_matmul_kernel(x_ref, nw_ref, w_ref, o_ref, xn_ref):
    @pl.when(pl.program_id(1) == 0)
    def _():
        x = x_ref[...]
        ms = jnp.mean(x * x, axis=-1, keepdims=True)
        xn_ref[...] = (x * lax.rsqrt(ms + RMS_EPS) * nw_ref[...]).astype(BF16)

    o_ref[...] = _dot(xn_ref[...], w_ref[...])


def _norm_matmul(x, nw, w, tm, tn):
    m, d = x.shape
    n = w.shape[1]
    return pl.pallas_call(
        _norm_matmul_kernel,
        grid=(m // tm, n // tn),
        in_specs=[pl.BlockSpec((tm, d), lambda i, j: (i, 0)),
                  pl.BlockSpec((1, d), lambda i, j: (0, 0)),
                  pl.BlockSpec((d, tn), lambda i, j: (0, j))],
        out_specs=pl.BlockSpec((tm, tn), lambda i, j: (i, j)),
        out_shape=jax.ShapeDtypeStruct((m, n), F32),
        scratch_shapes=[pltpu.VMEM((tm, d), BF16)],
        compiler_params=_cparams(("parallel", "arbitrary")),
        name="norm_in_proj",
    )(x, nw, w)


def _softplus(x):
    return jnp.maximum(x, 0.0) + jnp.log(1.0 + jnp.exp(-jnp.abs(x)))


def _sigmoid(x):
    return 1.0 / (1.0 + jnp.exp(-x))


def _split2(x):
    hi = x.astype(BF16)
    lo = (x - hi.astype(F32)).astype(BF16)
    return hi, lo


def _split3(x):
    hi = x.astype(BF16)
    r1 = x - hi.astype(F32)
    mid = r1.astype(BF16)
    lo = (r1 - mid.astype(F32)).astype(BF16)
    return hi, mid, lo


def _rwkv_kernel(pr_ref, pk_ref, pv_ref, plo_ref, shr_ref, shk_ref, shv_ref, shl_ref, s0_ref,
                 mur_ref, muk_ref, muv_ref, mul_ref, w0_ref, a0_ref, kk_ref, ka_ref, rk_ref, lnw_ref, lnb_ref,
                 wl_ref, al_ref, gl_ref,
                 y_ref, so_ref,
                 sd_ref, br_ref, bk_ref, bv_ref, bl_ref, *, L, Lv, HPS, BB, NC, n_dbl):
    c = pl.program_id(2)
    H = HEAD_DIM
    W = HPS * LANES
    sls = [slice(i * LANES, (i + 1) * LANES) for i in range(HPS)]

    lane = lax.broadcasted_iota(jnp.int32, (L, LANES), 1)
    head1 = lane < H
    row_i = lax.broadcasted_iota(jnp.int32, (L, 2 * L), 0)
    col_i = lax.broadcasted_iota(jnp.int32, (L, 2 * L), 1)
    col_t = jnp.where(col_i >= L, col_i - L, col_i)
    strict = col_t < row_i
    incl = col_t <= row_i
    colh1 = col_i < L
    tri = (lax.broadcasted_iota(jnp.int32, (L, L), 1) <= lax.broadcasted_iota(jnp.int32, (L, L), 0)).astype(BF16)
    ji = lax.broadcasted_iota(jnp.int32, (LANES, LANES), 0)
    jj = lax.broadcasted_iota(jnp.int32, (LANES, LANES), 1)
    same_head = (ji < H) == (jj < H)
    seg = same_head.astype(BF16)
    rowv = lax.broadcasted_iota(jnp.int32, (L, W), 0) < Lv

    def seg_sum(x):
        xs = jnp.concatenate([x[:, s] for s in sls], axis=0)
        hi, lo = _split2(xs)
        ys = _dot(hi, seg) + _dot(lo, seg)
        return jnp.concatenate([ys[i * L:(i + 1) * L] for i in range(HPS)], axis=1)

    def stack_heads(x, m):
        zero = jnp.zeros_like(x)
        return jnp.concatenate([jnp.where(m, x, zero), jnp.where(m, zero, x)], axis=0)

    def init(bi):
        br_ref[bi, 7:8, :] = shr_ref[bi]
        bk_ref[bi, 7:8, :] = shk_ref[bi]
        bv_ref[bi, 7:8, :] = shv_ref[bi]
        bl_ref[bi, 7:8, :] = shl_ref[bi]
        if Lv < L:
            for ref in (br_ref, bk_ref, bv_ref, bl_ref):
                ref[bi, 8 + Lv:8 + L, :] = jnp.zeros((L - Lv, ref.shape[2]), F32)
        z = jnp.zeros((H, H), F32)
        for i in range(HPS):
            top = jnp.concatenate([s0_ref[bi, 2 * i], z], axis=1)
            bot = jnp.concatenate([z, s0_ref[bi, 2 * i + 1]], axis=1)
            sd_ref[bi, i] = jnp.concatenate([top, bot], axis=0)

    def finish(bi):
        for i in range(HPS):
            sd = sd_ref[bi, i]
            so_ref[bi, 2 * i] = sd[0:H, 0:H]
            so_ref[bi, 2 * i + 1] = sd[H:2 * H, H:2 * H]

    def one_batch(bi):
        if NC == 1:
            init(bi)
        else:
            pl.when(c == 0)(lambda: init(bi))

        br_ref[bi, 8:8 + Lv, :] = pr_ref[bi]
        bk_ref[bi, 8:8 + Lv, :] = pk_ref[bi]
        bv_ref[bi, 8:8 + Lv, :] = pv_ref[bi]
        bl_ref[bi, 8:8 + Lv, :] = plo_ref[bi]

        def shifted(buf, mu):
            p = buf[bi, 8:8 + L, :]
            return p + mu * (buf[bi, 7:7 + L, :] - p)

        ps_l = shifted(bl_ref, mul_ref[...])
        td = jnp.tanh(ps_l[:, 0:LORA_SLOT]).astype(BF16)
        da = ps_l[:, LORA_SLOT:2 * LORA_SLOT].astype(BF16)
        sg = _sigmoid(ps_l[:, 2 * LORA_SLOT:]).astype(BF16)
        r = shifted(br_ref, mur_ref[...])
        k = shifted(bk_ref, muk_ref[...])
        v = shifted(bv_ref, muv_ref[...])

        zlog = w0_ref[...] + _dot(td, wl_ref[...])
        logw = -jnp.exp(-_softplus(-zlog) - 0.5)
        a_sig = _sigmoid(a0_ref[...] + _dot(da, al_ref[...]))
        kk = k * kk_ref[...]
        nrm = jnp.sqrt(seg_sum(kk * kk))
        kk = kk / jnp.maximum(nrm, 1e-12)
        k_h = k * (1.0 + (a_sig - 1.0) * ka_ref[...])
        if Lv < L:
            logw = jnp.where(rowv, logw, 0.0)
            kk = jnp.where(rowv, kk, 0.0)
            k_h = jnp.where(rowv, k_h, 0.0)
            v = jnp.where(rowv, v, 0.0)

        hi, mid, lo = _split3(logw)
        cum = _dot(tri, hi) + _dot(tri, mid) + _dot(tri, lo)
        cum_l = cum[L - 1:L, :]
        p_inv = jnp.exp(-cum)
        p_end = jnp.exp(cum_l - cum)
        bvec = kk * a_sig
        a_t = (-kk * jnp.exp(cum - logw)).astype(BF16)
        r_t = (r * jnp.exp(cum)).astype(BF16)
        b_t = (bvec * p_inv).astype(BF16)
        k_t = (k_h * p_inv).astype(BF16)
        b_e = (bvec * p_end).astype(BF16)
        k_e = (k_h * p_end).astype(BF16)
        v_b = v.astype(BF16)
        p_l = jnp.exp(cum_l)

        P = range(HPS)
        lhs = [jnp.concatenate([a_t[:, s], r_t[:, s]], axis=0) for s in sls]
        rhs = [jnp.concatenate([stack_heads(b_t[:, s], head1), stack_heads(k_t[:, s], head1)], axis=0) for s in sls]
        sd = [sd_ref[bi, i] for i in P]
        aa = [_dot_nt(lhs[i], rhs[i]) for i in P]
        sa = [_dot_nt(lhs[i], sd[i].astype(BF16)) for i in P]
        v_st = [stack_heads(v_b[:, s], head1) for s in sls]
        a_ak = [jnp.where(strict, aa[i][0:L, 2 * L:4 * L], 0.0).astype(BF16) for i in P]
        x = [sa[i][0:L] + _dot(a_ak[i], v_st[i]) for i in P]
        ap = [jnp.where(strict, aa[i][0:L, 0:2 * L], 0.0).astype(BF16) for i in P]
        for d in range(n_dbl):
            x = [x[i] + _dot(ap[i], stack_heads(x[i].astype(BF16), head1)) for i in P]
            if d + 1 < n_dbl:
                ap = [_dot(ap[i], stack_heads(ap[i], colh1)).astype(BF16) for i in P]
        u_b = [x[i].astype(BF16) for i in P]
        a_rb = [jnp.where(incl, aa[i][L:2 * L, 0:2 * L], 0.0).astype(BF16) for i in P]
        a_rk = [jnp.where(incl, aa[i][L:2 * L, 2 * L:4 * L], 0.0).astype(BF16) for i in P]
        y = [sa[i][L:2 * L] + _dot(a_rb[i], stack_heads(u_b[i], head1)) + _dot(a_rk[i], v_st[i]) for i in P]
        ds = [_dot_tn(jnp.concatenate([u_b[i], v_b[:, sls[i]]], axis=0),
                      jnp.concatenate([b_e[:, sls[i]], k_e[:, sls[i]]], axis=0)) for i in P]
        for i in P:
            sd_ref[bi, i] = sd[i] * p_l[:, sls[i]] + jnp.where(same_head, ds[i], 0.0)

        y = jnp.concatenate(y, axis=1)
        mean = seg_sum(y) * (1.0 / H)
        dlt = y - mean
        var = seg_sum(dlt * dlt) * (1.0 / H)
        yn = dlt * lax.rsqrt(var + GN_EPS) * lnw_ref[...] + lnb_ref[...]
        bonus = seg_sum(r * k_h * rk_ref[...]) * v
        g = _dot(sg, gl_ref[...])
        out = (yn + bonus) * g
        y_ref[bi] = out[0:Lv].astype(y_ref.dtype)

        br_ref[bi, 7:8, :] = br_ref[bi, 7 + Lv:8 + Lv, :]
        bk_ref[bi, 7:8, :] = bk_ref[bi, 7 + Lv:8 + Lv, :]
        bv_ref[bi, 7:8, :] = bv_ref[bi, 7 + Lv:8 + Lv, :]
        bl_ref[bi, 7:8, :] = bl_ref[bi, 7 + Lv:8 + Lv, :]

        if NC == 1:
            finish(bi)
        else:
            pl.when(c == NC - 1)(lambda: finish(bi))

    if BB == 1:
        one_batch(0)
    else:
        def loop_body(bi, carry):
            one_batch(bi)
            return carry
        lax.fori_loop(0, BB, loop_body, 0)


def _rwkv_mix(proj3, sh_rkv, sh_lora, s0, p, *, L, Lv, HPS, BB, y_dtype):
    b, t, _ = proj3.shape
    assert t % Lv == 0 and b % BB == 0
    nc = t // Lv
    w = HPS * LANES
    nhg = D_RWKV // w
    kb = D_RWKV // w
    n_dbl = max(1, math.ceil(math.log2(Lv)))
    col = lambda o: (lambda bi, g, c: (bi, c, o + g))
    sh = lambda o: (lambda bi, g, c: (bi, 0, o + g))
    par = lambda o: (lambda bi, g, c: (0, o + g))
    fixed2 = lambda bi, g, c: (0, 0)
    in_specs = [
        pl.BlockSpec((BB, Lv, w), col(0)), pl.BlockSpec((BB, Lv, w), col(kb)), pl.BlockSpec((BB, Lv, w), col(2 * kb)),
        pl.BlockSpec((BB, Lv, LORA_W), lambda bi, g, c: (bi, c, COL_LORA // LORA_W)),
        pl.BlockSpec((BB, 1, w), sh(0)), pl.BlockSpec((BB, 1, w), sh(kb)), pl.BlockSpec((BB, 1, w), sh(2 * kb)),
        pl.BlockSpec((BB, 1, LORA_W), lambda bi, g, c: (bi, 0, 0)),
        pl.BlockSpec((BB, 2 * HPS, HEAD_DIM, HEAD_DIM), lambda bi, g, c: (bi, g, 0, 0)),
        pl.BlockSpec((1, w), par(0)), pl.BlockSpec((1, w), par(kb)), pl.BlockSpec((1, w), par(2 * kb)),
        pl.BlockSpec((1, LORA_W), fixed2),
    ] + [pl.BlockSpec((1, w), par(0))] * 7 + [
        pl.BlockSpec((LORA_SLOT, w), par(0)), pl.BlockSpec((LORA_SLOT, w), par(0)),
        pl.BlockSpec((D_GATE_LORA, w), par(0)),
    ]
    out_specs = [pl.BlockSpec((BB, Lv, w), col(0)),
                 pl.BlockSpec((BB, 2 * HPS, HEAD_DIM, HEAD_DIM), lambda bi, g, c: (bi, g, 0, 0))]
    kern = functools.partial(_rwkv_kernel, L=L, Lv=Lv, HPS=HPS, BB=BB, NC=nc, n_dbl=n_dbl)
    return pl.pallas_call(
        kern,
        grid=(b // BB, nhg, nc),
        in_specs=in_specs,
        out_specs=out_specs,
        out_shape=[jax.ShapeDtypeStruct((b, t, D_RWKV), y_dtype),
                   jax.ShapeDtypeStruct((b, N_HEADS, HEAD_DIM, HEAD_DIM), F32)],
        scratch_shapes=[pltpu.VMEM((BB, HPS, LANES, LANES), F32),
                        pltpu.VMEM((BB, L + 8, w), F32), pltpu.VMEM((BB, L + 8, w), F32),
                        pltpu.VMEM((BB, L + 8, w), F32), pltpu.VMEM((BB, L + 8, LORA_W), F32)],
        compiler_params=_cparams(("parallel", "parallel", "arbitrary")),
        name="rwkv_mix",
    )(proj3, proj3, proj3, proj3, sh_rkv, sh_rkv, sh_rkv, sh_lora, s0,
      p["mu_rkv"], p["mu_rkv"], p["mu_rkv"], p["mu_lora"], p["w0"], p["a0"], p["k_k"], p["k_a"], p["r_k"],
      p["ln_w"], p["ln_b"], p["w_lora"], p["a_lora"], p["g_lora"])


def _softmax_sink_pv(s, valid, sink, v_b):
    s = jnp.where(valid, s, -jnp.inf)
    m = jnp.maximum(jnp.max(s, axis=-1, keepdims=True), sink)
    p = jnp.exp(s - m)
    den = jnp.sum(p, axis=-1, keepdims=True) + jnp.exp(sink - m)
    return _dot(p.astype(BF16), v_b) / den


def _swa_prompt_kernel(slope_ref, sink_ref, q_ref, kc_ref, kp_ref, vc_ref, vp_ref, o_ref, bias_ref):
    n = pl.program_id(1)
    blk = WINDOW
    H = HEAD_DIM

    @pl.when((pl.program_id(0) == 0) & (n == 0))
    def _():
        t = lax.broadcasted_iota(jnp.int32, (blk, 2 * blk), 0)
        j = lax.broadcasted_iota(jnp.int32, (blk, 2 * blk), 1)
        dist = t - j + blk
        band = (dist >= 0) & (dist <= WINDOW)
        first = band & (j >= blk)
        distf = dist.astype(F32)
        for h in range(N_HEADS):
            ab = -slope_ref[h] * distf
            bias_ref[0, h] = jnp.where(first, ab, -jnp.inf)
            bias_ref[1, h] = jnp.where(band, ab, -jnp.inf)

    sel = jnp.where(n == 0, 0, 1)
    low = lax.broadcasted_iota(jnp.int32, (blk, LANES), 1) < H
    scale = H ** -0.5
    KV = range(N_KV_HEADS)
    tile = lambda i: slice(i * LANES, (i + 1) * LANES)

    kslab = [jnp.concatenate([kp_ref[0, :, tile(hk // 2)], kc_ref[0, :, tile(hk // 2)]], axis=0).astype(BF16)
             for hk in KV]
    vslab = [jnp.concatenate([vp_ref[0, :, tile(hk // 2)], vc_ref[0, :, tile(hk // 2)]], axis=0).astype(BF16)
             for hk in KV]
    lhs = []
    for hk in KV:
        parts = []
        for s2 in range(2):
            xs = q_ref[0, :, tile(2 * hk + s2)] * scale
            xr = pltpu.roll(xs, H, axis=1)
            if hk % 2 == 0:
                parts += [jnp.where(low, xs, 0.0), jnp.where(low, xr, 0.0)]
            else:
                parts += [jnp.where(low, 0.0, xr), jnp.where(low, 0.0, xs)]
        lhs.append(jnp.concatenate(parts, axis=0).astype(BF16))
    s = [_dot_nt(lhs[hk], kslab[hk]) for hk in KV]
    p, rden = [], []
    for hk in KV:
        ps, rs = [], []
        for g in range(GQA_GROUP):
            h = hk * GQA_GROUP + g
            sg = s[hk][g * blk:(g + 1) * blk] + bias_ref[sel, h]
            m = jnp.maximum(jnp.max(sg, axis=-1, keepdims=True), sink_ref[h])
            e = jnp.exp(sg - m)
            rs.append(1.0 / (jnp.sum(e, axis=-1, keepdims=True) + jnp.exp(sink_ref[h] - m)))
            ps.append(e.astype(BF16))
        p.append(jnp.concatenate(ps, axis=0))
        rden.append(rs)
    o = [_dot(p[hk], vslab[hk]) for hk in KV]
    for hk in KV:
        for s2 in range(2):
            ga, gb = 2 * s2, 2 * s2 + 1
            oa = o[hk][ga * blk:(ga + 1) * blk] * rden[hk][ga]
            ob = o[hk][gb * blk:(gb + 1) * blk] * rden[hk][gb]
            if hk % 2 == 0:
                out = jnp.where(low, oa, pltpu.roll(ob, H, axis=1))
            else:
                out = jnp.where(low, pltpu.roll(oa, H, axis=1), ob)
            o_ref[0, :, tile(2 * hk + s2)] = out.astype(o_ref.dtype)


def _swa_prompt(proj3, slopes, sinks):
    b, t, _ = proj3.shape
    nb = t // WINDOW
    smem = pl.BlockSpec(memory_space=pltpu.SMEM)
    cq, ck, cv = COL_Q // D_MODEL, COL_KS // D_KV, COL_VS // D_KV
    return pl.pallas_call(
        _swa_prompt_kernel,
        grid=(b, nb),
        in_specs=[smem, smem,
                  pl.BlockSpec((1, WINDOW, D_MODEL), lambda bi, n: (bi, n, cq)),
                  pl.BlockSpec((1, WINDOW, D_KV), lambda bi, n: (bi, n, ck)),
                  pl.BlockSpec((1, WINDOW, D_KV), lambda bi, n: (bi, jnp.maximum(n - 1, 0), ck)),
                  pl.BlockSpec((1, WINDOW, D_KV), lambda bi, n: (bi, n, cv)),
                  pl.BlockSpec((1, WINDOW, D_KV), lambda bi, n: (bi, jnp.maximum(n - 1, 0), cv))],
        out_specs=pl.BlockSpec((1, WINDOW, D_MODEL), lambda bi, n: (bi, n, 0)),
        out_shape=jax.ShapeDtypeStruct((b, t, D_MODEL), BF16),
        scratch_shapes=[pltpu.VMEM((2, N_HEADS, WINDOW, 2 * WINDOW), F32)],
        compiler_params=_cparams(("arbitrary", "arbitrary")),
        name="swa_prompt",
    )(slopes, sinks, proj3, proj3, proj3, proj3, proj3)


def _swa_sample_kernel(slope_ref, sink_ref, q_ref, kpast_ref, vpast_ref, knew_ref, vnew_ref,
                       o_ref, kwin_ref, vwin_ref, kc_ref, vc_ref, *, tq, kpad):
    nk = WINDOW + tq
    for cat, past, new, win in ((kc_ref, kpast_ref, knew_ref, kwin_ref), (vc_ref, vpast_ref, vnew_ref, vwin_ref)):
        cat[0:WINDOW, :] = past[0]
        cat[WINDOW:nk, :] = new[0]
        cat[nk:kpad, :] = jnp.zeros((kpad - nk, D_KV), F32)
        win[0] = cat[tq:nk, :]
    rows = GQA_GROUP * tq
    row = lax.broadcasted_iota(jnp.int32, (rows, kpad), 0)
    j = lax.broadcasted_iota(jnp.int32, (rows, kpad), 1)
    dist = lax.rem(row, tq) + WINDOW - j
    valid = (dist >= 0) & (dist <= WINDOW) & (j < nk)
    distf = dist.astype(F32)
    scale = HEAD_DIM ** -0.5
    for hk in range(N_KV_HEADS):
        ksl = slice(hk * HEAD_DIM, (hk + 1) * HEAD_DIM)
        kh = kc_ref[:, ksl].astype(BF16)
        vh = vc_ref[:, ksl].astype(BF16)
        s = _dot_nt(q_ref[0, hk].astype(BF16), kh) * scale - slope_ref[hk] * distf
        o_ref[0, hk] = _softmax_sink_pv(s, valid, sink_ref[hk], vh)


def _swa_sample(q_st, proj3, k_past, v_past, slope_rows, sink_rows):
    b, tq, _ = proj3.shape
    rows = GQA_GROUP * tq
    kpad = -(-(WINDOW + tq) // 16) * 16
    ck, cv = COL_KS // D_KV, COL_VS // D_KV
    kern = functools.partial(_swa_sample_kernel, tq=tq, kpad=kpad)
    full3 = lambda bi: (0, 0, 0)
    return pl.pallas_call(
        kern,
        grid=(b,),
        in_specs=[pl.BlockSpec((N_KV_HEADS, rows, 1), full3), pl.BlockSpec((N_KV_HEADS, rows, 1), full3),
                  pl.BlockSpec((1, N_KV_HEADS, rows, HEAD_DIM), lambda bi: (bi, 0, 0, 0)),
                  pl.BlockSpec((1, WINDOW, D_KV), lambda bi: (bi, 0, 0)),
                  pl.BlockSpec((1, WINDOW, D_KV), lambda bi: (bi, 0, 0)),
                  pl.BlockSpec((1, tq, D_KV), lambda bi: (bi, 0, ck)),
                  pl.BlockSpec((1, tq, D_KV), lambda bi: (bi, 0, cv))],
        out_specs=[pl.BlockSpec((1, N_KV_HEADS, rows, HEAD_DIM), lambda bi: (bi, 0, 0, 0)),
                   pl.BlockSpec((1, WINDOW, D_KV), lambda bi: (bi, 0, 0)),
                   pl.BlockSpec((1, WINDOW, D_KV), lambda bi: (bi, 0, 0))],
        out_shape=[jax.ShapeDtypeStruct((b, N_KV_HEADS, rows, HEAD_DIM), F32),
                   jax.ShapeDtypeStruct((b, WINDOW, D_KV), F32),
                   jax.ShapeDtypeStruct((b, WINDOW, D_KV), F32)],
        scratch_shapes=[pltpu.VMEM((kpad, D_KV), F32), pltpu.VMEM((kpad, D_KV), F32)],
        compiler_params=_cparams(("parallel",)),
        name="swa_sample",
    )(slope_rows, sink_rows, q_st, k_past, v_past, proj3, proj3)


def _merge_out_kernel(x_ref, ya_ref, yb_ref, ga_ref, gb_ref, wo_ref, nw_ref, h_ref, hn_ref):
    mixed = _sigmoid(ga_ref[...]) * ya_ref[...] + _sigmoid(gb_ref[...]) * yb_ref[...]
    h = x_ref[...] + _dot(mixed.astype(BF16), wo_ref[...])
    h_ref[...] = h
    ms = jnp.mean(h * h, axis=-1, keepdims=True)
    hn_ref[...] = (h * lax.rsqrt(ms + RMS_EPS) * nw_ref[...]).astype(BF16)


def _merge_out(x, ya, yb, proj, w_out, nw, tm):
    m, d = x.shape
    row = lambda o: (lambda i: (i, o))
    return pl.pallas_call(
        _merge_out_kernel,
        grid=(m // tm,),
        in_specs=[pl.BlockSpec((tm, d), row(0)), pl.BlockSpec((tm, d), row(0)), pl.BlockSpec((tm, d), row(0)),
                  pl.BlockSpec((tm, d), row(COL_GA // D_MODEL)), pl.BlockSpec((tm, d), row(COL_GB // D_MODEL)),
                  pl.BlockSpec((d, d), lambda i: (0, 0)), pl.BlockSpec((1, d), lambda i: (0, 0))],
        out_specs=[pl.BlockSpec((tm, d), row(0)), pl.BlockSpec((tm, d), row(0))],
        out_shape=[jax.ShapeDtypeStruct((m, d), F32), jax.ShapeDtypeStruct((m, d), BF16)],
        compiler_params=_cparams(("parallel",)),
        name="merge_out_proj",
    )(x, ya, yb, proj, proj, w_out, nw)


def _mlp_kernel(hn_ref, h_ref, wu_ref, wd_ref, nw_ref, o_ref, acc_ref):
    j = pl.program_id(1)

    @pl.when(j == 0)
    def _():
        acc_ref[...] = jnp.zeros_like(acc_ref)

    u = jnp.maximum(_dot(hn_ref[...], wu_ref[...]), 0.0)
    acc_ref[...] += _dot((u * u).astype(BF16), wd_ref[...])

    @pl.when(j == pl.num_programs(1) - 1)
    def _():
        h = h_ref[...] + acc_ref[...]
        ms = jnp.mean(h * h, axis=-1, keepdims=True)
        o_ref[...] = h * lax.rsqrt(ms + RMS_EPS) * nw_ref[...]


def _mlp(hn, h, w_up, w_down, nw, tm, tf):
    m, d = h.shape
    f = w_up.shape[1]
    return pl.pallas_call(
        _mlp_kernel,
        grid=(m // tm, f // tf),
        in_specs=[pl.BlockSpec((tm, d), lambda i, j: (i, 0)), pl.BlockSpec((tm, d), lambda i, j: (i, 0)),
                  pl.BlockSpec((d, tf), lambda i, j: (0, j)), pl.BlockSpec((tf, d), lambda i, j: (j, 0)),
                  pl.BlockSpec((1, d), lambda i, j: (0, 0))],
        out_specs=pl.BlockSpec((tm, d), lambda i, j: (i, 0)),
        out_shape=jax.ShapeDtypeStruct((m, d), F32),
        scratch_shapes=[pltpu.VMEM((tm, d), F32)],
        compiler_params=_cparams(("parallel", "arbitrary")),
        name="mlp_final_norm",
    )(hn, h, w_up, w_down, nw)


def _pick(m, prefs):
    for t in prefs:
        if m % t == 0:
            return t
    return m


def _pad_rows(w, rows):
    return jnp.concatenate([w, jnp.zeros((rows - w.shape[0], w.shape[1]), w.dtype)], axis=0)


def _layer(x, shift_prev, wkv0, lw, *, prompt, k_past=None, v_past=None):
    b, t, d = x.shape
    m = b * t
    x2 = x.reshape(m, d)
    proj = _norm_matmul(x2, lw["norm_mix_w"], lw["w_in"], _pick(m, (1024, 512, 256, 128, 8)), 512)
    proj3 = proj.reshape(b, t, C_PAD)

    sh_rkv, sh_lora = _split_r_cols(shift_prev)
    if prompt:
        ya, wkv_new = _rwkv_mix(proj3, sh_rkv[:, None], sh_lora[:, None], wkv0, lw, L=64, Lv=64, HPS=16, BB=1,
                                y_dtype=BF16)
        yb = _swa_prompt(proj3, lw["slopes"], lw["sinks"])
        k_win = proj3[:, t - WINDOW:, COL_KS:COL_KS + D_KV]
        v_win = proj3[:, t - WINDOW:, COL_VS:COL_VS + D_KV]
    else:
        ya, wkv_new = _rwkv_mix(proj3, sh_rkv[:, None], sh_lora[:, None], wkv0, lw, L=16, Lv=t, HPS=16,
                                BB=_pick(b, (4, 2, 1)), y_dtype=F32)
        q = proj3[:, :, COL_Q:COL_Q + D_MODEL].reshape(b, t, N_KV_HEADS, GQA_GROUP, HEAD_DIM)
        q_st = q.transpose(0, 2, 3, 1, 4).reshape(b, N_KV_HEADS, GQA_GROUP * t, HEAD_DIM)
        slope_rows = jnp.repeat(lw["slopes"].reshape(N_KV_HEADS, GQA_GROUP), t, axis=1)[..., None]
        sink_rows = jnp.repeat(lw["sinks"].reshape(N_KV_HEADS, GQA_GROUP), t, axis=1)[..., None]
        o_st, k_win, v_win = _swa_sample(q_st, proj3, k_past, v_past, slope_rows, sink_rows)
        yb = o_st.reshape(b, N_KV_HEADS, GQA_GROUP, t, HEAD_DIM).transpose(0, 3, 1, 2, 4).reshape(b, t, D_MODEL)

    h, hn = _merge_out(x2, ya.reshape(m, d), yb.reshape(m, d), proj, lw["w_out"], lw["norm_mlp_w"],
                       _pick(m, (256, 128, 8)))
    y = _mlp(hn, h, lw["w_up"], lw["w_down"], lw["norm_final_w"], _pick(m, (512, 256, 128, 8)), 512)
    shift_new = _merge_r_cols(proj3[:, t - 1, :3 * D_RWKV], proj3[:, t - 1, COL_LORA:])
    return (y.reshape(b, t, d), shift_new, wkv_new,
            k_win.reshape(b, WINDOW, N_KV_HEADS, HEAD_DIM), v_win.reshape(b, WINDOW, N_KV_HEADS, HEAD_DIM))


def kernel(x_prompt, x_sample, state_shift, state_wkv, cache_k_win, cache_v_win, norm_mix_w, w_in, tshift_mu, w0, w_lora, a0, a_lora, g_lora, k_k, k_a, r_k, ln_x_w, ln_x_b, attn_sinks, w_out, norm_mlp_w, w_up, w_down, norm_final_w):
    depth = w_in.shape[0]
    assert depth == 1
    l = 0
    bp = x_prompt.shape[0]
    db = x_sample.shape[0]
    mu_rkv, mu_lora = _split_r_cols(tshift_mu[l][None])
    hh = jnp.arange(N_HEADS, dtype=F32)
    lw = dict(
        norm_mix_w=norm_mix_w[l][None], w_in=_reorder_in_cols(w_in[l]).astype(BF16),
        mu_rkv=mu_rkv, mu_lora=mu_lora, w0=w0[l][None], a0=a0[l][None], k_k=k_k[l][None], k_a=k_a[l][None],
        r_k=r_k[l].reshape(1, D_RWKV), ln_w=ln_x_w[l][None], ln_b=ln_x_b[l][None],
        w_lora=_pad_rows(w_lora[l], LORA_SLOT).astype(BF16), a_lora=_pad_rows(a_lora[l], LORA_SLOT).astype(BF16),
        g_lora=g_lora[l].astype(BF16),
        slopes=jnp.exp2(-8.0 * (hh + 1.0) / N_HEADS), sinks=attn_sinks[l].astype(F32),
        w_out=w_out[l].astype(BF16), norm_mlp_w=norm_mlp_w[l][None],
        w_up=w_up[l].astype(BF16), w_down=w_down[l].astype(BF16), norm_final_w=norm_final_w[None],
    )
    yp, sp, wp, kp, vp = _layer(
        x_prompt, jnp.zeros((bp, R_COLS), F32), jnp.zeros((bp, N_HEADS, HEAD_DIM, HEAD_DIM), F32), lw, prompt=True)
    ys, ss, ws, ksm, vsm = _layer(
        x_sample, state_shift[l], state_wkv[l], lw, prompt=False,
        k_past=cache_k_win[l].reshape(db, WINDOW, D_KV), v_past=cache_v_win[l].reshape(db, WINDOW, D_KV))
    return (yp, ys, sp[None], wp[None], kp[None], vp[None], ss[None], ws[None], ksm[None], vsm[None])
```

```python
import functools
import math

import jax
import jax.numpy as jnp
from jax import lax
from jax.experimental import pallas as pl
from jax.experimental.pallas import tpu as pltpu

F32 = jnp.float32
BF16 = jnp.bfloat16

D_MODEL = 2048
HEAD_DIM = 64
N_HEADS = D_MODEL // HEAD_DIM
N_KV_HEADS = 8
GQA_GROUP = N_HEADS // N_KV_HEADS
D_KV = N_KV_HEADS * HEAD_DIM
WINDOW = 128
D_FF = 4 * D_MODEL
D_DECAY_LORA = 96
D_A_LORA = 96
D_GATE_LORA = 256
D_RWKV = D_MODEL
R_COLS = 3 * D_RWKV + D_DECAY_LORA + D_A_LORA + D_GATE_LORA
C_IN = R_COLS + D_MODEL + 2 * D_KV + 2 * D_MODEL
RMS_EPS = 1e-5
GN_EPS = 64e-5

LANES = 128
PROJ_TILE = 512
R_PAD = -(-R_COLS // PROJ_TILE) * PROJ_TILE
LORA_W = R_PAD - 3 * D_RWKV
LORA_DECAY_END = D_DECAY_LORA
LORA_A_END = D_DECAY_LORA + D_A_LORA
LORA_G_END = LORA_A_END + D_GATE_LORA
LORA_A_WIN = -(-LORA_A_END // LANES) * LANES
LORA_G_START = (LORA_A_END // LANES) * LANES

VMEM_LIMIT = 56 * 1024 * 1024
KV_PER_GROUP = 2


def _cparams(sem):
    return pltpu.CompilerParams(dimension_semantics=sem, vmem_limit_bytes=VMEM_LIMIT)


def _dot(a, b):
    return jnp.dot(a, b, preferred_element_type=F32)


def _dot_nt(a, b):
    return lax.dot_general(a, b, (((1,), (1,)), ((), ())), preferred_element_type=F32)


def _dot_tn(a, b):
    return lax.dot_general(a, b, (((0,), (0,)), ((), ())), preferred_element_type=F32)


def _softplus(x):
    return jnp.maximum(x, 0.0) + jnp.log(1.0 + jnp.exp(-jnp.abs(x)))


def _sigmoid(x):
    return 1.0 / (1.0 + jnp.exp(-x))


def _split2(x):
    hi = x.astype(BF16)
    lo = (x - hi.astype(F32)).astype(BF16)
    return hi, lo


def _split3(x):
    hi = x.astype(BF16)
    r1 = x - hi.astype(F32)
    mid = r1.astype(BF16)
    lo = (r1 - mid.astype(F32)).astype(BF16)
    return hi, mid, lo


def _in_proj_kernel(x_ref, nw_ref, *refs, bounds):
    n = len(bounds)
    w_refs, o_refs, xn_ref = refs[:n], refs[n:2 * n], refs[2 * n]
    j = pl.program_id(1)

    @pl.when(j == 0)
    def _():
        x = x_ref[...]
        ms = jnp.mean(x * x, axis=-1, keepdims=True)
        xn_ref[...] = (x * lax.rsqrt(ms + RMS_EPS) * nw_ref[...]).astype(BF16)

    for w_ref, o_ref, (lo, hi) in zip(w_refs, o_refs, bounds):
        @pl.when((j >= lo) & (j < hi))
        def _(w_ref=w_ref, o_ref=o_ref):
            o_ref[...] = _dot_nt(xn_ref[...], w_ref[...]).astype(o_ref.dtype)


def _in_proj(x, nw, weights, out_dtypes, tm):
    m, d = x.shape
    tn = PROJ_TILE
    bounds, lo = [], 0
    for w in weights:
        bounds.append((lo, lo + w.shape[0] // tn))
        lo = bounds[-1][1]

    def clamp(lo_, hi_):
        return lambda j: jnp.clip(j - lo_, 0, hi_ - lo_ - 1)

    in_specs = [pl.BlockSpec((tm, d), lambda i, j: (i, 0)), pl.BlockSpec((1, d), lambda i, j: (0, 0))]
    out_specs, out_shape = [], []
    for w, dt, (lo_, hi_) in zip(weights, out_dtypes, bounds):
        c = clamp(lo_, hi_)
        in_specs.append(pl.BlockSpec((tn, d), lambda i, j, c=c: (c(j), 0)))
        out_specs.append(pl.BlockSpec((tm, tn), lambda i, j, c=c: (i, c(j))))
        out_shape.append(jax.ShapeDtypeStruct((m, w.shape[0]), dt))
    return pl.pallas_call(
        functools.partial(_in_proj_kernel, bounds=tuple(bounds)),
        grid=(m // tm, lo),
        in_specs=in_specs,
        out_specs=out_specs,
        out_shape=out_shape,
        scratch_shapes=[pltpu.VMEM((tm, d), BF16)],
        compiler_params=_cparams(("parallel", "arbitrary")),
        name="norm_in_proj",
    )(x, nw, *weights)


def _rwkv_kernel(pr_ref, pk_ref, pv_ref, plo_ref, shr_ref, shk_ref, shv_ref, shl_ref, s0_ref,
                 mur_ref, muk_ref, muv_ref, mul_ref, w0_ref, a0_ref, kk_ref, ka_ref, rk_ref, lnw_ref, lnb_ref,
                 wl_ref, al_ref, gl_ref,
                 y_ref, so_ref,
                 sd_ref, br_ref, bk_ref, bv_ref, bl_ref, *, L, Lv, HPS, BB, NC, n_dbl):
    c = pl.program_id(2)
    H = HEAD_DIM
    W = HPS * LANES
    sls = [slice(i * LANES, (i + 1) * LANES) for i in range(HPS)]

    lane = lax.broadcasted_iota(jnp.int32, (L, LANES), 1)
    head1 = lane < H
    row_i = lax.broadcasted_iota(jnp.int32, (L, 2 * L), 0)
    col_i = lax.broadcasted_iota(jnp.int32, (L, 2 * L), 1)
    col_t = jnp.where(col_i >= L, col_i - L, col_i)
    strict = col_t < row_i
    incl = col_t <= row_i
    colh1 = col_i < L
    tri = (lax.broadcasted_iota(jnp.int32, (L, L), 1) <= lax.broadcasted_iota(jnp.int32, (L, L), 0)).astype(BF16)
    ji = lax.broadcasted_iota(jnp.int32, (LANES, LANES), 0)
    jj = lax.broadcasted_iota(jnp.int32, (LANES, LANES), 1)
    same_head = (ji < H) == (jj < H)
    seg = same_head.astype(BF16)
    rowv = lax.broadcasted_iota(jnp.int32, (L, W), 0) < Lv

    def seg_sum(x):
        xs = jnp.concatenate([x[:, s] for s in sls], axis=0)
        hi, lo = _split2(xs)
        ys = _dot(hi, seg) + _dot(lo, seg)
        return jnp.concatenate([ys[i * L:(i + 1) * L] for i in range(HPS)], axis=1)

    def stack_heads(x, m):
        zero = jnp.zeros_like(x)
        return jnp.concatenate([jnp.where(m, x, zero), jnp.where(m, zero, x)], axis=0)

    def init(bi):
        br_ref[bi, 7:8, :] = shr_ref[bi]
        bk_ref[bi, 7:8, :] = shk_ref[bi]
        bv_ref[bi, 7:8, :] = shv_ref[bi]
        bl_ref[bi, 7:8, :] = shl_ref[bi]
        if Lv < L:
            for ref in (br_ref, bk_ref, bv_ref, bl_ref):
                ref[bi, 8 + Lv:8 + L, :] = jnp.zeros((L - Lv, ref.shape[2]), F32)
        z = jnp.zeros((H, H), F32)
        for i in range(HPS):
            top = jnp.concatenate([s0_ref[bi, 2 * i], z], axis=1)
            bot = jnp.concatenate([z, s0_ref[bi, 2 * i + 1]], axis=1)
            sd_ref[bi, i] = jnp.concatenate([top, bot], axis=0)

    def finish(bi):
        for i in range(HPS):
            sd = sd_ref[bi, i]
            so_ref[bi, 2 * i] = sd[0:H, 0:H]
            so_ref[bi, 2 * i + 1] = sd[H:2 * H, H:2 * H]

    def one_batch(bi):
        if NC == 1:
            init(bi)
        else:
            pl.when(c == 0)(lambda: init(bi))

        br_ref[bi, 8:8 + Lv, :] = pr_ref[bi]
        bk_ref[bi, 8:8 + Lv, :] = pk_ref[bi]
        bv_ref[bi, 8:8 + Lv, :] = pv_ref[bi]
        bl_ref[bi, 8:8 + Lv, :] = plo_ref[bi]

        def shifted(buf, mu):
            p = buf[bi, 8:8 + L, :]
            return p + mu * (buf[bi, 7:7 + L, :] - p)

        ps_l = shifted(bl_ref, mul_ref[...])
        td = jnp.tanh(ps_l[:, 0:LANES]).astype(BF16)
        da = ps_l[:, 0:LORA_A_WIN].astype(BF16)
        sg = _sigmoid(ps_l[:, LORA_G_START:]).astype(BF16)
        r = shifted(br_ref, mur_ref[...])
        k = shifted(bk_ref, muk_ref[...])
        v = shifted(bv_ref, muv_ref[...])

        zlog = w0_ref[...] + _dot(td, wl_ref[...])
        logw = -jnp.exp(-_softplus(-zlog) - 0.5)
        a_sig = _sigmoid(a0_ref[...] + _dot(da, al_ref[...]))
        kk = k * kk_ref[...]
        nrm = jnp.sqrt(seg_sum(kk * kk))
        kk = kk / jnp.maximum(nrm, 1e-12)
        k_h = k * (1.0 + (a_sig - 1.0) * ka_ref[...])
        if Lv < L:
            logw = jnp.where(rowv, logw, 0.0)
            kk = jnp.where(rowv, kk, 0.0)
            k_h = jnp.where(rowv, k_h, 0.0)
            v = jnp.where(rowv, v, 0.0)

        hi, mid, lo = _split3(logw)
        cum = _dot(tri, hi) + _dot(tri, mid) + _dot(tri, lo)
        cum_l = cum[L - 1:L, :]
        p_inv = jnp.exp(-cum)
        p_end = jnp.exp(cum_l - cum)
        bvec = kk * a_sig
        a_t = (-kk * jnp.exp(cum - logw)).astype(BF16)
        r_t = (r * jnp.exp(cum)).astype(BF16)
        b_t = (bvec * p_inv).astype(BF16)
        k_t = (k_h * p_inv).astype(BF16)
        b_e = (bvec * p_end).astype(BF16)
        k_e = (k_h * p_end).astype(BF16)
        v_b = v.astype(BF16)
        p_l = jnp.exp(cum_l)

        P = range(HPS)
        lhs = [jnp.concatenate([a_t[:, s], r_t[:, s]], axis=0) for s in sls]
        rhs = [jnp.concatenate([stack_heads(b_t[:, s], head1), stack_heads(k_t[:, s], head1)], axis=0) for s in sls]
        sd = [sd_ref[bi, i] for i in P]
        aa = [_dot_nt(lhs[i], rhs[i]) for i in P]
        sa = [_dot_nt(lhs[i], sd[i].astype(BF16)) for i in P]
        v_st = [stack_heads(v_b[:, s], head1) for s in sls]
        a_ak = [jnp.where(strict, aa[i][0:L, 2 * L:4 * L], 0.0).astype(BF16) for i in P]
        x = [sa[i][0:L] + _dot(a_ak[i], v_st[i]) for i in P]
        ap = [jnp.where(strict, aa[i][0:L, 0:2 * L], 0.0).astype(BF16) for i in P]
        for d in range(n_dbl):
            x = [x[i] + _dot(ap[i], stack_heads(x[i].astype(BF16), head1)) for i in P]
            if d + 1 < n_dbl:
                ap = [_dot(ap[i], stack_heads(ap[i], colh1)).astype(BF16) for i in P]
        u_b = [x[i].astype(BF16) for i in P]
        a_rb = [jnp.where(incl, aa[i][L:2 * L, 0:2 * L], 0.0).astype(BF16) for i in P]
        a_rk = [jnp.where(incl, aa[i][L:2 * L, 2 * L:4 * L], 0.0).astype(BF16) for i in P]
        y = [sa[i][L:2 * L] + _dot(a_rb[i], stack_heads(u_b[i], head1)) + _dot(a_rk[i], v_st[i]) for i in P]
        ds = [_dot_tn(jnp.concatenate([u_b[i], v_b[:, sls[i]]], axis=0),
                      jnp.concatenate([b_e[:, sls[i]], k_e[:, sls[i]]], axis=0)) for i in P]
        for i in P:
            sd_ref[bi, i] = sd[i] * p_l[:, sls[i]] + jnp.where(same_head, ds[i], 0.0)

        y = jnp.concatenate(y, axis=1)
        mean = seg_sum(y) * (1.0 / H)
        dlt = y - mean
        var = seg_sum(dlt * dlt) * (1.0 / H)
        yn = dlt * lax.rsqrt(var + GN_EPS) * lnw_ref[...] + lnb_ref[...]
        bonus = seg_sum(r * k_h * rk_ref[...]) * v
        g = _dot(sg, gl_ref[...])
        out = (yn + bonus) * g
        y_ref[bi] = out[0:Lv].astype(y_ref.dtype)

        br_ref[bi, 7:8, :] = br_ref[bi, 7 + Lv:8 + Lv, :]
        bk_ref[bi, 7:8, :] = bk_ref[bi, 7 + Lv:8 + Lv, :]
        bv_ref[bi, 7:8, :] = bv_ref[bi, 7 + Lv:8 + Lv, :]
        bl_ref[bi, 7:8, :] = bl_ref[bi, 7 + Lv:8 + Lv, :]

        if NC == 1:
            finish(bi)
        else:
            pl.when(c == NC - 1)(lambda: finish(bi))

    if BB == 1:
        one_batch(0)
    else:
        def loop_body(bi, carry):
            one_batch(bi)
            return carry
        lax.fori_loop(0, BB, loop_body, 0)


def _rwkv_mix(proj3, shift3, s0, p, *, L, Lv, HPS, BB, y_dtype):
    b, t, _ = proj3.shape
    assert t % Lv == 0 and b % BB == 0
    nc = t // Lv
    w = HPS * LANES
    nhg = D_RWKV // w
    kb = D_RWKV // w
    lora_blk = 3 * D_RWKV // LORA_W
    n_dbl = max(1, math.ceil(math.log2(Lv)))
    col = lambda o: (lambda bi, g, c: (bi, c, o + g))
    sh = lambda o: (lambda bi, g, c: (bi, 0, o + g))
    par = lambda o: (lambda bi, g, c: (0, o + g))
    in_specs = [
        pl.BlockSpec((BB, Lv, w), col(0)), pl.BlockSpec((BB, Lv, w), col(kb)), pl.BlockSpec((BB, Lv, w), col(2 * kb)),
        pl.BlockSpec((BB, Lv, LORA_W), lambda bi, g, c: (bi, c, lora_blk)),
        pl.BlockSpec((BB, 1, w), sh(0)), pl.BlockSpec((BB, 1, w), sh(kb)), pl.BlockSpec((BB, 1, w), sh(2 * kb)),
        pl.BlockSpec((BB, 1, LORA_W), lambda bi, g, c: (bi, 0, lora_blk)),
        pl.BlockSpec((BB, 2 * HPS, HEAD_DIM, HEAD_DIM), lambda bi, g, c: (bi, g, 0, 0)),
        pl.BlockSpec((1, w), par(0)), pl.BlockSpec((1, w), par(kb)), pl.BlockSpec((1, w), par(2 * kb)),
        pl.BlockSpec((1, LORA_W), lambda bi, g, c: (0, lora_blk)),
    ] + [pl.BlockSpec((1, w), par(0))] * 7 + [
        pl.BlockSpec((LANES, w), par(0)), pl.BlockSpec((LORA_A_WIN, w), par(0)),
        pl.BlockSpec((LORA_W - LORA_G_START, w), par(0)),
    ]
    out_specs = [pl.BlockSpec((BB, Lv, w), col(0)),
                 pl.BlockSpec((BB, 2 * HPS, HEAD_DIM, HEAD_DIM), lambda bi, g, c: (bi, g, 0, 0))]
    kern = functools.partial(_rwkv_kernel, L=L, Lv=Lv, HPS=HPS, BB=BB, NC=nc, n_dbl=n_dbl)
    return pl.pallas_call(
        kern,
        grid=(b // BB, nhg, nc),
        in_specs=in_specs,
        out_specs=out_specs,
        out_shape=[jax.ShapeDtypeStruct((b, t, D_RWKV), y_dtype),
                   jax.ShapeDtypeStruct((b, N_HEADS, HEAD_DIM, HEAD_DIM), F32)],
        scratch_shapes=[pltpu.VMEM((BB, HPS, LANES, LANES), F32),
                        pltpu.VMEM((BB, L + 8, w), F32), pltpu.VMEM((BB, L + 8, w), F32),
                        pltpu.VMEM((BB, L + 8, w), F32), pltpu.VMEM((BB, L + 8, LORA_W), F32)],
        compiler_params=_cparams(("parallel", "parallel", "arbitrary")),
        name="rwkv_mix",
    )(proj3, proj3, proj3, proj3, shift3, shift3, shift3, shift3, s0,
      p["mu"], p["mu"], p["mu"], p["mu"], p["w0"], p["a0"], p["k_k"], p["k_a"], p["r_k"],
      p["ln_w"], p["ln_b"], p["w_lora"], p["a_lora"], p["g_lora"])


def _swa_prompt_kernel(slope_ref, sink_ref, q_ref, kc_ref, kp_ref, vc_ref, vp_ref, o_ref, bias_ref):
    n = pl.program_id(1)
    blk = WINDOW
    H = HEAD_DIM

    @pl.when((pl.program_id(0) == 0) & (n == 0))
    def _():
        t = lax.broadcasted_iota(jnp.int32, (blk, 2 * blk), 0)
        j = lax.broadcasted_iota(jnp.int32, (blk, 2 * blk), 1)
        dist = t - j + blk
        band = (dist >= 0) & (dist <= WINDOW)
        first = band & (j >= blk)
        distf = dist.astype(F32)
        for h in range(N_HEADS):
            ab = -slope_ref[h] * distf
            bias_ref[0, h] = jnp.where(first, ab, -jnp.inf)
            bias_ref[1, h] = jnp.where(band, ab, -jnp.inf)

    sel = jnp.where(n == 0, 0, 1)
    low = lax.broadcasted_iota(jnp.int32, (blk, LANES), 1) < H
    scale = H ** -0.5
    tile = lambda i: slice(i * LANES, (i + 1) * LANES)

    def kv_group(KV):
        kslab = {hk: jnp.concatenate([kp_ref[0, :, tile(hk // 2)], kc_ref[0, :, tile(hk // 2)]],
                                     axis=0).astype(BF16) for hk in KV}
        vslab = {hk: jnp.concatenate([vp_ref[0, :, tile(hk // 2)], vc_ref[0, :, tile(hk // 2)]],
                                     axis=0).astype(BF16) for hk in KV}
        lhs = {}
        for hk in KV:
            parts = []
            for s2 in range(2):
                xs = q_ref[0, :, tile(2 * hk + s2)].astype(F32) * scale
                xr = pltpu.roll(xs, H, axis=1)
                if hk % 2 == 0:
                    parts += [jnp.where(low, xs, 0.0), jnp.where(low, xr, 0.0)]
                else:
                    parts += [jnp.where(low, 0.0, xr), jnp.where(low, 0.0, xs)]
            lhs[hk] = jnp.concatenate(parts, axis=0).astype(BF16)
        s = {hk: _dot_nt(lhs[hk], kslab[hk]) for hk in KV}
        p, rden = {}, {}
        for hk in KV:
            ps, rs = [], []
            for g in range(GQA_GROUP):
                h = hk * GQA_GROUP + g
                sg = s[hk][g * blk:(g + 1) * blk] + bias_ref[sel, h]
                m = jnp.maximum(jnp.max(sg, axis=-1, keepdims=True), sink_ref[h])
                e = jnp.exp(sg - m)
                rs.append(1.0 / (jnp.sum(e, axis=-1, keepdims=True) + jnp.exp(sink_ref[h] - m)))
                ps.append(e.astype(BF16))
            p[hk] = jnp.concatenate(ps, axis=0)
            rden[hk] = rs
        o = {hk: _dot(p[hk], vslab[hk]) for hk in KV}
        for hk in KV:
            for s2 in range(2):
                ga, gb = 2 * s2, 2 * s2 + 1
                oa = o[hk][ga * blk:(ga + 1) * blk] * rden[hk][ga]
                ob = o[hk][gb * blk:(gb + 1) * blk] * rden[hk][gb]
                if hk % 2 == 0:
                    out = jnp.where(low, oa, pltpu.roll(ob, H, axis=1))
                else:
                    out = jnp.where(low, pltpu.roll(oa, H, axis=1), ob)
                o_ref[0, :, tile(2 * hk + s2)] = out.astype(o_ref.dtype)

    for g0 in range(0, N_KV_HEADS, KV_PER_GROUP):
        kv_group(range(g0, g0 + KV_PER_GROUP))


def _swa_prompt(q3, kv3, slopes, sinks):
    b, t, _ = q3.shape
    nb = t // WINDOW
    smem = pl.BlockSpec(memory_space=pltpu.SMEM)
    prev = lambda n: jnp.maximum(n - 1, 0)
    return pl.pallas_call(
        _swa_prompt_kernel,
        grid=(b, nb),
        in_specs=[smem, smem,
                  pl.BlockSpec((1, WINDOW, D_MODEL), lambda bi, n: (bi, n, 0)),
                  pl.BlockSpec((1, WINDOW, D_KV), lambda bi, n: (bi, n, 0)),
                  pl.BlockSpec((1, WINDOW, D_KV), lambda bi, n: (bi, prev(n), 0)),
                  pl.BlockSpec((1, WINDOW, D_KV), lambda bi, n: (bi, n, 1)),
                  pl.BlockSpec((1, WINDOW, D_KV), lambda bi, n: (bi, prev(n), 1))],
        out_specs=pl.BlockSpec((1, WINDOW, D_MODEL), lambda bi, n: (bi, n, 0)),
        out_shape=jax.ShapeDtypeStruct((b, t, D_MODEL), BF16),
        scratch_shapes=[pltpu.VMEM((2, N_HEADS, WINDOW, 2 * WINDOW), F32)],
        compiler_params=_cparams(("arbitrary", "arbitrary")),
        name="swa_prompt",
    )(slopes, sinks, q3, kv3, kv3, kv3, kv3)


def _swa_sample_kernel(slope_ref, sink_ref, q_ref, kc_ref, vc_ref, knew_ref, vnew_ref,
                       o_ref, kwin_ref, vwin_ref, nbuf_ref, *, tq, BB, UNR):
    GT = GQA_GROUP * tq
    R = N_KV_HEADS * GT
    NP = 16
    C = D_KV
    row = lax.broadcasted_iota(jnp.int32, (R, WINDOW), 0)
    wcol = lax.broadcasted_iota(jnp.int32, (R, WINDOW), 1)
    t = lax.rem(row, tq)
    slope = slope_ref[...]
    sink = sink_ref[...]
    dist_o = WINDOW + t - wcol
    bias_old = jnp.where(dist_o <= WINDOW, -slope * dist_o.astype(F32), -jnp.inf)
    s_idx = wcol - (WINDOW - tq)
    dist_n = t - s_idx
    bias_new = jnp.where((s_idx >= 0) & (dist_n >= 0), -slope * dist_n.astype(F32), -jnp.inf)
    hkmask = (lax.broadcasted_iota(jnp.int32, (R, C), 0) // GT) == (lax.broadcasted_iota(jnp.int32, (R, C), 1) // HEAD_DIM)
    srow = lax.broadcasted_iota(jnp.int32, (NP, WINDOW), 0)
    scol = lax.broadcasted_iota(jnp.int32, (NP, WINDOW), 1)
    selw = ((scol == srow + (WINDOW - tq)) & (srow < tq)).astype(BF16)
    lane_new = lax.broadcasted_iota(jnp.int32, (C, WINDOW), 1) >= WINDOW - tq
    scale = HEAD_DIM ** -0.5
    for u in range(UNR):
        nbuf_ref[u, :, tq:NP, :] = jnp.zeros((2, NP - tq, C), F32)

    def transposed_new(x):
        return sum(_dot_tn(part, selw) for part in _split3(x))

    def body(i, carry):
        bs = [i * UNR + u for u in range(UNR)]
        U = range(UNR)
        for u in U:
            nbuf_ref[u, 0, 0:tq, :] = knew_ref[bs[u]]
            nbuf_ref[u, 1, 0:tq, :] = vnew_ref[bs[u]]
        kt = [kc_ref[b] for b in bs]
        vt = [vc_ref[b] for b in bs]
        knt = [transposed_new(nbuf_ref[u, 0]) for u in U]
        vnt = [transposed_new(nbuf_ref[u, 1]) for u in U]
        qbd = [jnp.where(hkmask, jnp.concatenate([q_ref[b] * scale] * N_KV_HEADS, axis=0), 0.0).astype(BF16)
               for b in bs]
        s_o = [_dot(qbd[u], kt[u].astype(BF16)) + bias_old for u in U]
        s_n = [_dot(qbd[u], knt[u].astype(BF16)) + bias_new for u in U]
        outs = []
        for u in U:
            m = jnp.maximum(jnp.maximum(jnp.max(s_o[u], axis=-1, keepdims=True),
                                        jnp.max(s_n[u], axis=-1, keepdims=True)), sink)
            p_o = jnp.exp(s_o[u] - m)
            p_n = jnp.exp(s_n[u] - m)
            rden = 1.0 / (jnp.sum(p_o, axis=-1, keepdims=True) + jnp.sum(p_n, axis=-1, keepdims=True)
                          + jnp.exp(sink - m))
            o = _dot_nt(p_o.astype(BF16), vt[u].astype(BF16)) + _dot_nt(p_n.astype(BF16), vnt[u].astype(BF16))
            o = jnp.where(hkmask, o * rden, 0.0)
            acc = o[0:GT]
            for hk in range(1, N_KV_HEADS):
                acc = acc + o[hk * GT:(hk + 1) * GT]
            outs.append(acc)
        for u in U:
            kwin_ref[bs[u]] = jnp.where(lane_new, knt[u], pltpu.roll(kt[u], WINDOW - tq, axis=1))
            vwin_ref[bs[u]] = jnp.where(lane_new, vnt[u], pltpu.roll(vt[u], WINDOW - tq, axis=1))
            o_ref[bs[u]] = outs[u]
        return carry

    lax.fori_loop(0, BB // UNR, body, 0)


def _swa_sample(q16, knew3, vnew3, kct, vct, slope_rows, sink_rows):
    b, gt, c = q16.shape
    tq = gt // GQA_GROUP
    bb = 8 if b % 8 == 0 else 1
    unr = 2 if bb % 2 == 0 else 1
    rows = N_KV_HEADS * gt
    blk3 = lambda shape: pl.BlockSpec(shape, lambda i: (i, 0, 0))
    full2 = pl.BlockSpec((rows, 1), lambda i: (0, 0))
    kern = functools.partial(_swa_sample_kernel, tq=tq, BB=bb, UNR=unr)
    return pl.pallas_call(
        kern,
        grid=(b // bb,),
        in_specs=[full2, full2, blk3((bb, gt, c)), blk3((bb, c, WINDOW)), blk3((bb, c, WINDOW)),
                  blk3((bb, tq, c)), blk3((bb, tq, c))],
        out_specs=[blk3((bb, gt, c)), blk3((bb, c, WINDOW)), blk3((bb, c, WINDOW))],
        out_shape=[jax.ShapeDtypeStruct((b, gt, c), F32),
                   jax.ShapeDtypeStruct((b, c, WINDOW), F32),
                   jax.ShapeDtypeStruct((b, c, WINDOW), F32)],
        scratch_shapes=[pltpu.VMEM((unr, 2, 16, c), F32)],
        compiler_params=_cparams(("parallel",)),
        name="swa_sample",
    )(slope_rows, sink_rows, q16, kct, vct, knew3, vnew3)


def _merge_out_kernel(x_ref, ya_ref, yb_ref, ga_ref, gb_ref, wo_ref, nw_ref, h_ref, hn_ref):
    mixed = _sigmoid(ga_ref[...]) * ya_ref[...] + _sigmoid(gb_ref[...]) * yb_ref[...]
    h = x_ref[...] + _dot(mixed.astype(BF16), wo_ref[...])
    h_ref[...] = h
    ms = jnp.mean(h * h, axis=-1, keepdims=True)
    hn_ref[...] = (h * lax.rsqrt(ms + RMS_EPS) * nw_ref[...]).astype(BF16)


def _merge_out(x, ya, yb, gates, w_out, nw, tm):
    m, d = x.shape
    row = lambda o: (lambda i: (i, o))
    return pl.pallas_call(
        _merge_out_kernel,
        grid=(m // tm,),
        in_specs=[pl.BlockSpec((tm, d), row(0)), pl.BlockSpec((tm, d), row(0)), pl.BlockSpec((tm, d), row(0)),
                  pl.BlockSpec((tm, d), row(0)), pl.BlockSpec((tm, d), row(1)),
                  pl.BlockSpec((d, d), lambda i: (0, 0)), pl.BlockSpec((1, d), lambda i: (0, 0))],
        out_specs=[pl.BlockSpec((tm, d), row(0)), pl.BlockSpec((tm, d), row(0))],
        out_shape=[jax.ShapeDtypeStruct((m, d), F32), jax.ShapeDtypeStruct((m, d), BF16)],
        compiler_params=_cparams(("parallel",)),
        name="merge_out_proj",
    )(x, ya, yb, gates, gates, w_out, nw)


def _mlp_kernel(hn_ref, h_ref, wu_ref, wd_ref, nw_ref, o_ref, acc_ref):
    j = pl.program_id(1)

    @pl.when(j == 0)
    def _():
        acc_ref[...] = jnp.zeros_like(acc_ref)

    u = jnp.maximum(_dot(hn_ref[...], wu_ref[...]), 0.0)
    acc_ref[...] += _dot((u * u).astype(BF16), wd_ref[...])

    @pl.when(j == pl.num_programs(1) - 1)
    def _():
        h = h_ref[...] + acc_ref[...]
        ms = jnp.mean(h * h, axis=-1, keepdims=True)
        o_ref[...] = h * lax.rsqrt(ms + RMS_EPS) * nw_ref[...]


def _mlp(hn, h, w_up, w_down, nw, tm, tf):
    m, d = h.shape
    f = w_up.shape[1]
    return pl.pallas_call(
        _mlp_kernel,
        grid=(m // tm, f // tf),
        in_specs=[pl.BlockSpec((tm, d), lambda i, j: (i, 0)), pl.BlockSpec((tm, d), lambda i, j: (i, 0)),
                  pl.BlockSpec((d, tf), lambda i, j: (0, j)), pl.BlockSpec((tf, d), lambda i, j: (j, 0)),
                  pl.BlockSpec((1, d), lambda i, j: (0, 0))],
        out_specs=pl.BlockSpec((tm, d), lambda i, j: (i, 0)),
        out_shape=jax.ShapeDtypeStruct((m, d), F32),
        scratch_shapes=[pltpu.VMEM((tm, d), F32)],
        compiler_params=_cparams(("parallel", "arbitrary")),
        name="mlp_final_norm",
    )(hn, h, w_up, w_down, nw)


def _pick(m, prefs):
    for t in prefs:
        if m % t == 0:
            return t
    return m


def _pad_cols(v, n):
    return jnp.concatenate([v, jnp.zeros(v.shape[:-1] + (n - v.shape[-1],), v.dtype)], axis=-1)


def _place_rows(w, start, rows):
    n, d = w.shape
    return jnp.concatenate([jnp.zeros((start, d), w.dtype), w, jnp.zeros((rows - start - n, d), w.dtype)], axis=0)


def _layer(x, shift_prev, wkv0, lw, *, prompt, k_cache_t=None, v_cache_t=None):
    b, t, d = x.shape
    m = b * t
    x2 = x.reshape(m, d)
    proj_r, proj_q, proj_kv, proj_g = _in_proj(
        x2, lw["norm_mix_w"], [lw["w_r"], lw["w_q"], lw["w_kv"], lw["w_g"]], [F32, BF16, F32, F32],
        _pick(m, (1024, 512, 256, 128, 8)))
    proj_r3 = proj_r.reshape(b, t, R_PAD)
    kv3 = proj_kv.reshape(b, t, 2 * D_KV)
    shift3 = _pad_cols(shift_prev, R_PAD)[:, None]

    if prompt:
        ya, wkv_new = _rwkv_mix(proj_r3, shift3, wkv0, lw, L=64, Lv=64, HPS=16, BB=1, y_dtype=BF16)
        yb = _swa_prompt(proj_q.reshape(b, t, D_MODEL), kv3, lw["slopes"], lw["sinks"])
        k_win = kv3[:, t - WINDOW:, :D_KV].reshape(b, WINDOW, N_KV_HEADS, HEAD_DIM)
        v_win = kv3[:, t - WINDOW:, D_KV:].reshape(b, WINDOW, N_KV_HEADS, HEAD_DIM)
    else:
        ya, wkv_new = _rwkv_mix(proj_r3, shift3, wkv0, lw, L=16, Lv=t, HPS=16, BB=_pick(b, (4, 2, 1)), y_dtype=F32)
        q16 = proj_q.reshape(b, t, N_KV_HEADS, GQA_GROUP, HEAD_DIM).transpose(0, 3, 1, 2, 4)
        q16 = q16.reshape(b, GQA_GROUP * t, D_KV)
        gt_head = (jnp.arange(N_KV_HEADS)[:, None] * GQA_GROUP + jnp.arange(GQA_GROUP)[None, :])
        row_head = jnp.repeat(gt_head, t, axis=1).reshape(-1)
        o16, kwt, vwt = _swa_sample(q16, kv3[:, :, :D_KV], kv3[:, :, D_KV:], k_cache_t, v_cache_t,
                                    lw["slopes"][row_head][:, None], lw["sinks"][row_head][:, None])
        yb = o16.reshape(b, GQA_GROUP, t, N_KV_HEADS, HEAD_DIM).transpose(0, 2, 3, 1, 4).reshape(b, t, D_MODEL)
        k_win = kwt.reshape(b, N_KV_HEADS, HEAD_DIM, WINDOW).transpose(0, 3, 1, 2)
        v_win = vwt.reshape(b, N_KV_HEADS, HEAD_DIM, WINDOW).transpose(0, 3, 1, 2)

    h, hn = _merge_out(x2, ya.reshape(m, d), yb.reshape(m, d), proj_g, lw["w_out"], lw["norm_mlp_w"],
                       _pick(m, (256, 128, 8)))
    y = _mlp(hn, h, lw["w_up"], lw["w_down"], lw["norm_final_w"], _pick(m, (512, 256, 128, 8)), 512)
    shift_new = proj_r3[:, t - 1, :R_COLS]
    return y.reshape(b, t, d), shift_new, wkv_new, k_win, v_win


def kernel(x_prompt, x_sample, state_shift, state_wkv, cache_k_win, cache_v_win, norm_mix_w, w_in, tshift_mu, w0, w_lora, a0, a_lora, g_lora, k_k, k_a, r_k, ln_x_w, ln_x_b, attn_sinks, w_out, norm_mlp_w, w_up, w_down, norm_final_w):
    depth = w_in.shape[0]
    assert depth == 1
    l = 0
    bp = x_prompt.shape[0]
    db = x_sample.shape[0]
    hh = jnp.arange(N_HEADS, dtype=F32)
    w_t = jnp.swapaxes(w_in[l], 0, 1)
    o_q, o_kv, o_g = R_COLS, R_COLS + D_MODEL, R_COLS + D_MODEL + 2 * D_KV
    lw = dict(
        norm_mix_w=norm_mix_w[l][None],
        w_r=_place_rows(w_t[:R_COLS], 0, R_PAD).astype(BF16), w_q=w_t[o_q:o_kv].astype(BF16),
        w_kv=w_t[o_kv:o_g].astype(BF16), w_g=w_t[o_g:].astype(BF16),
        mu=_pad_cols(tshift_mu[l][None], R_PAD), w0=w0[l][None], a0=a0[l][None], k_k=k_k[l][None],
        k_a=k_a[l][None], r_k=r_k[l].reshape(1, D_RWKV), ln_w=ln_x_w[l][None], ln_b=ln_x_b[l][None],
        w_lora=_place_rows(w_lora[l], 0, LANES).astype(BF16),
        a_lora=_place_rows(a_lora[l], LORA_DECAY_END, LORA_A_WIN).astype(BF16),
        g_lora=_place_rows(g_lora[l], LORA_A_END - LORA_G_START, LORA_W - LORA_G_START).astype(BF16),
        slopes=jnp.exp2(-8.0 * (hh + 1.0) / N_HEADS), sinks=attn_sinks[l].astype(F32),
        w_out=w_out[l].astype(BF16), norm_mlp_w=norm_mlp_w[l][None],
        w_up=w_up[l].astype(BF16), w_down=w_down[l].astype(BF16), norm_final_w=norm_final_w[None],
    )
    yp, sp, wp, kp, vp = _layer(
        x_prompt, jnp.zeros((bp, R_COLS), F32), jnp.zeros((bp, N_HEADS, HEAD_DIM, HEAD_DIM), F32), lw, prompt=True)
    kct = cache_k_win[l].transpose(0, 2, 3, 1).reshape(db, D_KV, WINDOW)
    vct = cache_v_win[l].transpose(0, 2, 3, 1).reshape(db, D_KV, WINDOW)
    ys, ss, ws, ksm, vsm = _layer(x_sample, state_shift[l], state_wkv[l], lw, prompt=False,
                                  k_cache_t=kct, v_cache_t=vct)
    return (yp, ys, sp[None], wp[None], kp[None], vp[None], ss[None], ws[None], ksm[None], vsm[None])
```

```python
import functools
import math

import jax
import jax.numpy as jnp
from jax import lax
from jax.experimental import pallas as pl
from jax.experimental.pallas import tpu as pltpu

F32 = jnp.float32
BF16 = jnp.bfloat16

D_MODEL = 2048
HEAD_DIM = 64
N_HEADS = D_MODEL // HEAD_DIM
N_KV_HEADS = 8
GQA_GROUP = N_HEADS // N_KV_HEADS
D_KV = N_KV_HEADS * HEAD_DIM
WINDOW = 128
D_FF = 4 * D_MODEL
D_DECAY_LORA = 96
D_A_LORA = 96
D_GATE_LORA = 256
D_RWKV = D_MODEL
R_COLS = 3 * D_RWKV + D_DECAY_LORA + D_A_LORA + D_GATE_LORA
C_IN = R_COLS + D_MODEL + 2 * D_KV + 2 * D_MODEL
RMS_EPS = 1e-5
GN_EPS = 64e-5

LANES = 128
PROJ_TILE = 512
R_PAD = -(-R_COLS // PROJ_TILE) * PROJ_TILE
LORA_W = R_PAD - 3 * D_RWKV
LORA_DECAY_END = D_DECAY_LORA
LORA_A_END = D_DECAY_LORA + D_A_LORA
LORA_G_END = LORA_A_END + D_GATE_LORA
LORA_A_WIN = -(-LORA_A_END // LANES) * LANES
LORA_G_START = (LORA_A_END // LANES) * LANES

VMEM_LIMIT = 56 * 1024 * 1024
KV_PER_GROUP = 2


def _cparams(sem):
    return pltpu.CompilerParams(dimension_semantics=sem, vmem_limit_bytes=VMEM_LIMIT)


def _dot(a, b):
    return jnp.dot(a, b, preferred_element_type=F32)


def _dot_nt(a, b):
    return lax.dot_general(a, b, (((1,), (1,)), ((), ())), preferred_element_type=F32)


def _dot_tn(a, b):
    return lax.dot_general(a, b, (((0,), (0,)), ((), ())), preferred_element_type=F32)


def _softplus(x):
    return jnp.maximum(x, 0.0) + jnp.log(1.0 + jnp.exp(-jnp.abs(x)))


def _sigmoid(x):
    return 1.0 / (1.0 + jnp.exp(-x))


def _split2(x):
    hi = x.astype(BF16)
    lo = (x - hi.astype(F32)).astype(BF16)
    return hi, lo


def _split3(x):
    hi = x.astype(BF16)
    r1 = x - hi.astype(F32)
    mid = r1.astype(BF16)
    lo = (r1 - mid.astype(F32)).astype(BF16)
    return hi, mid, lo


def _in_proj_kernel(x_ref, nw_ref, *refs, bounds):
    n = len(bounds)
    w_refs, o_refs, xn_ref = refs[:n], refs[n:2 * n], refs[2 * n]
    j = pl.program_id(1)

    @pl.when(j == 0)
    def _():
        x = x_ref[...]
        ms = jnp.mean(x * x, axis=-1, keepdims=True)
        xn_ref[...] = (x * lax.rsqrt(ms + RMS_EPS) * nw_ref[...]).astype(BF16)

    for w_ref, o_ref, (lo, hi) in zip(w_refs, o_refs, bounds):
        @pl.when((j >= lo) & (j < hi))
        def _(w_ref=w_ref, o_ref=o_ref):
            o_ref[...] = _dot_nt(xn_ref[...], w_ref[...]).astype(o_ref.dtype)


def _in_proj(x, nw, weights, out_dtypes, tm):
    m, d = x.shape
    tn = PROJ_TILE
    bounds, lo = [], 0
    for w in weights:
        bounds.append((lo, lo + w.shape[0] // tn))
        lo = bounds[-1][1]

    def clamp(lo_, hi_):
        return lambda j: jnp.clip(j - lo_, 0, hi_ - lo_ - 1)

    in_specs = [pl.BlockSpec((tm, d), lambda i, j: (i, 0)), pl.BlockSpec((1, d), lambda i, j: (0, 0))]
    out_specs, out_shape = [], []
    for w, dt, (lo_, hi_) in zip(weights, out_dtypes, bounds):
        c = clamp(lo_, hi_)
        in_specs.append(pl.BlockSpec((tn, d), lambda i, j, c=c: (c(j), 0)))
        out_specs.append(pl.BlockSpec((tm, tn), lambda i, j, c=c: (i, c(j))))
        out_shape.append(jax.ShapeDtypeStruct((m, w.shape[0]), dt))
    return pl.pallas_call(
        functools.partial(_in_proj_kernel, bounds=tuple(bounds)),
        grid=(m // tm, lo),
        in_specs=in_specs,
        out_specs=out_specs,
        out_shape=out_shape,
        scratch_shapes=[pltpu.VMEM((tm, d), BF16)],
        compiler_params=_cparams(("parallel", "arbitrary")),
        name="norm_in_proj",
    )(x, nw, *weights)


def _rwkv_kernel_unpipelined(pr_ref, pk_ref, pv_ref, plo_ref, shr_ref, shk_ref, shv_ref, shl_ref, s0_ref,
                 mur_ref, muk_ref, muv_ref, mul_ref, w0_ref, a0_ref, kk_ref, ka_ref, rk_ref, lnw_ref, lnb_ref,
                 wl_ref, al_ref, gl_ref,
                 y_ref, so_ref,
                 sd_ref, br_ref, bk_ref, bv_ref, bl_ref, *, L, Lv, HPS, BB, NC, n_dbl):
    c = pl.program_id(2)
    H = HEAD_DIM
    W = HPS * LANES
    sls = [slice(i * LANES, (i + 1) * LANES) for i in range(HPS)]

    lane = lax.broadcasted_iota(jnp.int32, (L, LANES), 1)
    head1 = lane < H
    row_i = lax.broadcasted_iota(jnp.int32, (L, 2 * L), 0)
    col_i = lax.broadcasted_iota(jnp.int32, (L, 2 * L), 1)
    col_t = jnp.where(col_i >= L, col_i - L, col_i)
    strict = col_t < row_i
    incl = col_t <= row_i
    colh1 = col_i < L
    tri = (lax.broadcasted_iota(jnp.int32, (L, L), 1) <= lax.broadcasted_iota(jnp.int32, (L, L), 0)).astype(BF16)
    ji = lax.broadcasted_iota(jnp.int32, (LANES, LANES), 0)
    jj = lax.broadcasted_iota(jnp.int32, (LANES, LANES), 1)
    same_head = (ji < H) == (jj < H)
    seg = same_head.astype(BF16)
    rowv = lax.broadcasted_iota(jnp.int32, (L, W), 0) < Lv

    def seg_sum(x):
        xs = jnp.concatenate([x[:, s] for s in sls], axis=0)
        hi, lo = _split2(xs)
        ys = _dot(hi, seg) + _dot(lo, seg)
        return jnp.concatenate([ys[i * L:(i + 1) * L] for i in range(HPS)], axis=1)

    def stack_heads(x, m):
        zero = jnp.zeros_like(x)
        return jnp.concatenate([jnp.where(m, x, zero), jnp.where(m, zero, x)], axis=0)

    def init(bi):
        br_ref[bi, 7:8, :] = shr_ref[bi]
        bk_ref[bi, 7:8, :] = shk_ref[bi]
        bv_ref[bi, 7:8, :] = shv_ref[bi]
        bl_ref[bi, 7:8, :] = shl_ref[bi]
        if Lv < L:
            for ref in (br_ref, bk_ref, bv_ref, bl_ref):
                ref[bi, 8 + Lv:8 + L, :] = jnp.zeros((L - Lv, ref.shape[2]), F32)
        z = jnp.zeros((H, H), F32)
        for i in range(HPS):
            top = jnp.concatenate([s0_ref[bi, 2 * i], z], axis=1)
            bot = jnp.concatenate([z, s0_ref[bi, 2 * i + 1]], axis=1)
            sd_ref[bi, i] = jnp.concatenate([top, bot], axis=0)

    def finish(bi):
        for i in range(HPS):
            sd = sd_ref[bi, i]
            so_ref[bi, 2 * i] = sd[0:H, 0:H]
            so_ref[bi, 2 * i + 1] = sd[H:2 * H, H:2 * H]

    def one_batch(bi):
        if NC == 1:
            init(bi)
        else:
            pl.when(c == 0)(lambda: init(bi))

        br_ref[bi, 8:8 + Lv, :] = pr_ref[bi]
        bk_ref[bi, 8:8 + Lv, :] = pk_ref[bi]
        bv_ref[bi, 8:8 + Lv, :] = pv_ref[bi]
        bl_ref[bi, 8:8 + Lv, :] = plo_ref[bi]

        def shifted(buf, mu):
            p = buf[bi, 8:8 + L, :]
            return p + mu * (buf[bi, 7:7 + L, :] - p)

        ps_l = shifted(bl_ref, mul_ref[...])
        td = jnp.tanh(ps_l[:, 0:LANES]).astype(BF16)
        da = ps_l[:, 0:LORA_A_WIN].astype(BF16)
        sg = _sigmoid(ps_l[:, LORA_G_START:]).astype(BF16)
        r = shifted(br_ref, mur_ref[...])
        k = shifted(bk_ref, muk_ref[...])
        v = shifted(bv_ref, muv_ref[...])

        zlog = w0_ref[...] + _dot(td, wl_ref[...])
        logw = -jnp.exp(-_softplus(-zlog) - 0.5)
        a_sig = _sigmoid(a0_ref[...] + _dot(da, al_ref[...]))
        kk = k * kk_ref[...]
        nrm = jnp.sqrt(seg_sum(kk * kk))
        kk = kk / jnp.maximum(nrm, 1e-12)
        k_h = k * (1.0 + (a_sig - 1.0) * ka_ref[...])
        if Lv < L:
            logw = jnp.where(rowv, logw, 0.0)
            kk = jnp.where(rowv, kk, 0.0)
            k_h = jnp.where(rowv, k_h, 0.0)
            v = jnp.where(rowv, v, 0.0)

        hi, mid, lo = _split3(logw)
        cum = _dot(tri, hi) + _dot(tri, mid) + _dot(tri, lo)
        cum_l = cum[L - 1:L, :]
        p_inv = jnp.exp(-cum)
        p_end = jnp.exp(cum_l - cum)
        bvec = kk * a_sig
        a_t = (-kk * jnp.exp(cum - logw)).astype(BF16)
        r_t = (r * jnp.exp(cum)).astype(BF16)
        b_t = (bvec * p_inv).astype(BF16)
        k_t = (k_h * p_inv).astype(BF16)
        b_e = (bvec * p_end).astype(BF16)
        k_e = (k_h * p_end).astype(BF16)
        v_b = v.astype(BF16)
        p_l = jnp.exp(cum_l)

        P = range(HPS)
        lhs = [jnp.concatenate([a_t[:, s], r_t[:, s]], axis=0) for s in sls]
        rhs = [jnp.concatenate([stack_heads(b_t[:, s], head1), stack_heads(k_t[:, s], head1)], axis=0) for s in sls]
        sd = [sd_ref[bi, i] for i in P]
        aa = [_dot_nt(lhs[i], rhs[i]) for i in P]
        sa = [_dot_nt(lhs[i], sd[i].astype(BF16)) for i in P]
        v_st = [stack_heads(v_b[:, s], head1) for s in sls]
        a_ak = [jnp.where(strict, aa[i][0:L, 2 * L:4 * L], 0.0).astype(BF16) for i in P]
        x = [sa[i][0:L] + _dot(a_ak[i], v_st[i]) for i in P]
        ap = [jnp.where(strict, aa[i][0:L, 0:2 * L], 0.0).astype(BF16) for i in P]
        for d in range(n_dbl):
            x = [x[i] + _dot(ap[i], stack_heads(x[i].astype(BF16), head1)) for i in P]
            if d + 1 < n_dbl:
                ap = [_dot(ap[i], stack_heads(ap[i], colh1)).astype(BF16) for i in P]
        u_b = [x[i].astype(BF16) for i in P]
        a_rb = [jnp.where(incl, aa[i][L:2 * L, 0:2 * L], 0.0).astype(BF16) for i in P]
        a_rk = [jnp.where(incl, aa[i][L:2 * L, 2 * L:4 * L], 0.0).astype(BF16) for i in P]
        y = [sa[i][L:2 * L] + _dot(a_rb[i], stack_heads(u_b[i], head1)) + _dot(a_rk[i], v_st[i]) for i in P]
        ds = [_dot_tn(jnp.concatenate([u_b[i], v_b[:, sls[i]]], axis=0),
                      jnp.concatenate([b_e[:, sls[i]], k_e[:, sls[i]]], axis=0)) for i in P]
        for i in P:
            sd_ref[bi, i] = sd[i] * p_l[:, sls[i]] + jnp.where(same_head, ds[i], 0.0)

        y = jnp.concatenate(y, axis=1)
        mean = seg_sum(y) * (1.0 / H)
        dlt = y - mean
        var = seg_sum(dlt * dlt) * (1.0 / H)
        yn = dlt * lax.rsqrt(var + GN_EPS) * lnw_ref[...] + lnb_ref[...]
        bonus = seg_sum(r * k_h * rk_ref[...]) * v
        g = _dot(sg, gl_ref[...])
        out = (yn + bonus) * g
        y_ref[bi] = out[0:Lv].astype(y_ref.dtype)

        br_ref[bi, 7:8, :] = br_ref[bi, 7 + Lv:8 + Lv, :]
        bk_ref[bi, 7:8, :] = bk_ref[bi, 7 + Lv:8 + Lv, :]
        bv_ref[bi, 7:8, :] = bv_ref[bi, 7 + Lv:8 + Lv, :]
        bl_ref[bi, 7:8, :] = bl_ref[bi, 7 + Lv:8 + Lv, :]

        if NC == 1:
            finish(bi)
        else:
            pl.when(c == NC - 1)(lambda: finish(bi))

    if BB == 1:
        one_batch(0)
    else:
        def loop_body(bi, carry):
            one_batch(bi)
            return carry
        lax.fori_loop(0, BB, loop_body, 0)


def _rwkv_mix_unpipelined(proj3, shift3, s0, p, *, L, Lv, HPS, BB, y_dtype):
    b, t, _ = proj3.shape
    assert t % Lv == 0 and b % BB == 0
    nc = t // Lv
    w = HPS * LANES
    nhg = D_RWKV // w
    kb = D_RWKV // w
    lora_blk = 3 * D_RWKV // LORA_W
    n_dbl = max(1, math.ceil(math.log2(Lv)))
    col = lambda o: (lambda bi, g, c: (bi, c, o + g))
    sh = lambda o: (lambda bi, g, c: (bi, 0, o + g))
    par = lambda o: (lambda bi, g, c: (0, o + g))
    in_specs = [
        pl.BlockSpec((BB, Lv, w), col(0)), pl.BlockSpec((BB, Lv, w), col(kb)), pl.BlockSpec((BB, Lv, w), col(2 * kb)),
        pl.BlockSpec((BB, Lv, LORA_W), lambda bi, g, c: (bi, c, lora_blk)),
        pl.BlockSpec((BB, 1, w), sh(0)), pl.BlockSpec((BB, 1, w), sh(kb)), pl.BlockSpec((BB, 1, w), sh(2 * kb)),
        pl.BlockSpec((BB, 1, LORA_W), lambda bi, g, c: (bi, 0, lora_blk)),
        pl.BlockSpec((BB, 2 * HPS, HEAD_DIM, HEAD_DIM), lambda bi, g, c: (bi, g, 0, 0)),
        pl.BlockSpec((1, w), par(0)), pl.BlockSpec((1, w), par(kb)), pl.BlockSpec((1, w), par(2 * kb)),
        pl.BlockSpec((1, LORA_W), lambda bi, g, c: (0, lora_blk)),
    ] + [pl.BlockSpec((1, w), par(0))] * 7 + [
        pl.BlockSpec((LANES, w), par(0)), pl.BlockSpec((LORA_A_WIN, w), par(0)),
        pl.BlockSpec((LORA_W - LORA_G_START, w), par(0)),
    ]
    out_specs = [pl.BlockSpec((BB, Lv, w), col(0)),
                 pl.BlockSpec((BB, 2 * HPS, HEAD_DIM, HEAD_DIM), lambda bi, g, c: (bi, g, 0, 0))]
    kern = functools.partial(_rwkv_kernel, L=L, Lv=Lv, HPS=HPS, BB=BB, NC=nc, n_dbl=n_dbl)
    return pl.pallas_call(
        kern,
        grid=(b // BB, nhg, nc),
        in_specs=in_specs,
        out_specs=out_specs,
        out_shape=[jax.ShapeDtypeStruct((b, t, D_RWKV), y_dtype),
                   jax.ShapeDtypeStruct((b, N_HEADS, HEAD_DIM, HEAD_DIM), F32)],
        scratch_shapes=[pltpu.VMEM((BB, HPS, LANES, LANES), F32),
                        pltpu.VMEM((BB, L + 8, w), F32), pltpu.VMEM((BB, L + 8, w), F32),
                        pltpu.VMEM((BB, L + 8, w), F32), pltpu.VMEM((BB, L + 8, LORA_W), F32)],
        compiler_params=_cparams(("parallel", "parallel", "arbitrary")),
        name="rwkv_mix",
    )(proj3, proj3, proj3, proj3, shift3, shift3, shift3, shift3, s0,
      p["mu"], p["mu"], p["mu"], p["mu"], p["w0"], p["a0"], p["k_k"], p["k_a"], p["r_k"],
      p["ln_w"], p["ln_b"], p["w_lora"], p["a_lora"], p["g_lora"])


def _rwkv_kernel(pr_ref, pk_ref, pv_ref, plo_ref, shr_ref, shk_ref, shv_ref, shl_ref, s0_ref,
                 mur_ref, muk_ref, muv_ref, mul_ref, w0_ref, a0_ref, kk_ref, ka_ref, rk_ref, lnw_ref, lnb_ref,
                 wl_ref, al_ref, gl_ref,
                 y_ref, so_ref,
                 sd_ref, br_ref, bk_ref, bv_ref, bl_ref, *stage_refs, L, Lv, HPS, BB, NC, n_dbl, pipelined):
    c = pl.program_id(2)
    H = HEAD_DIM
    W = HPS * LANES
    sls = [slice(i * LANES, (i + 1) * LANES) for i in range(HPS)]
    P = range(HPS)
    STAGED = ("a_t", "r_t", "b_t", "k_t", "b_e", "k_e", "v_b", "g", "bg", "p_l")

    lane = lax.broadcasted_iota(jnp.int32, (L, LANES), 1)
    head1 = lane < H
    row_i = lax.broadcasted_iota(jnp.int32, (L, 2 * L), 0)
    col_i = lax.broadcasted_iota(jnp.int32, (L, 2 * L), 1)
    col_t = jnp.where(col_i >= L, col_i - L, col_i)
    strict = col_t < row_i
    incl = col_t <= row_i
    colh1 = col_i < L
    tri = (lax.broadcasted_iota(jnp.int32, (L, L), 1) <= lax.broadcasted_iota(jnp.int32, (L, L), 0)).astype(BF16)
    ji = lax.broadcasted_iota(jnp.int32, (LANES, LANES), 0)
    jj = lax.broadcasted_iota(jnp.int32, (LANES, LANES), 1)
    same_head = (ji < H) == (jj < H)
    seg = same_head.astype(BF16)
    rowv = lax.broadcasted_iota(jnp.int32, (L, W), 0) < Lv

    def seg_sum(x):
        n = x.shape[1] // LANES
        xs = jnp.concatenate([x[:, s] for s in sls[:n]], axis=0)
        hi, lo = _split2(xs)
        ys = _dot(hi, seg) + _dot(lo, seg)
        return jnp.concatenate([ys[i * L:(i + 1) * L] for i in range(n)], axis=1)

    def stack_heads(x, m):
        zero = jnp.zeros_like(x)
        return jnp.concatenate([jnp.where(m, x, zero), jnp.where(m, zero, x)], axis=0)

    def init(bi):
        br_ref[bi, 7:8, :] = shr_ref[bi]
        bk_ref[bi, 7:8, :] = shk_ref[bi]
        bv_ref[bi, 7:8, :] = shv_ref[bi]
        bl_ref[bi, 7:8, :] = shl_ref[bi]
        if Lv < L:
            for ref in (br_ref, bk_ref, bv_ref, bl_ref):
                ref[bi, 8 + Lv:8 + L, :] = jnp.zeros((L - Lv, ref.shape[2]), F32)
        z = jnp.zeros((H, H), F32)
        for i in P:
            top = jnp.concatenate([s0_ref[bi, 2 * i], z], axis=1)
            bot = jnp.concatenate([z, s0_ref[bi, 2 * i + 1]], axis=1)
            sd_ref[bi, i] = jnp.concatenate([top, bot], axis=0)

    def finish(bi):
        for i in P:
            sd = sd_ref[bi, i]
            so_ref[bi, 2 * i] = sd[0:H, 0:H]
            so_ref[bi, 2 * i + 1] = sd[H:2 * H, H:2 * H]

    def stage_inputs(bi):
        br_ref[bi, 8:8 + Lv, :] = pr_ref[bi]
        bk_ref[bi, 8:8 + Lv, :] = pk_ref[bi]
        bv_ref[bi, 8:8 + Lv, :] = pv_ref[bi]
        bl_ref[bi, 8:8 + Lv, :] = plo_ref[bi]

    def carry_rows(bi):
        br_ref[bi, 7:8, :] = br_ref[bi, 7 + Lv:8 + Lv, :]
        bk_ref[bi, 7:8, :] = bk_ref[bi, 7 + Lv:8 + Lv, :]
        bv_ref[bi, 7:8, :] = bv_ref[bi, 7 + Lv:8 + Lv, :]
        bl_ref[bi, 7:8, :] = bl_ref[bi, 7 + Lv:8 + Lv, :]

    def lora_inputs(bi):
        p = bl_ref[bi, 8:8 + L, :]
        ps_l = p + mul_ref[...] * (bl_ref[bi, 7:7 + L, :] - p)
        td = (1.0 - 2.0 / (1.0 + jnp.exp(2.0 * ps_l[:, 0:LANES]))).astype(BF16)
        da = ps_l[:, 0:LORA_A_WIN].astype(BF16)
        sg = _sigmoid(ps_l[:, LORA_G_START:]).astype(BF16)
        return td, da, sg

    def prologue(bi, lora, cols):
        td, da, sg = lora

        def shifted(buf, mu_ref):
            p = buf[bi, 8:8 + L, cols]
            return p + mu_ref[:, cols] * (buf[bi, 7:7 + L, cols] - p)

        r = shifted(br_ref, mur_ref)
        k = shifted(bk_ref, muk_ref)
        v = shifted(bv_ref, muv_ref)
        zlog = w0_ref[:, cols] + _dot(td, wl_ref[:, cols])
        logw = -math.exp(-0.5) / (1.0 + jnp.exp(-zlog))
        a_sig = _sigmoid(a0_ref[:, cols] + _dot(da, al_ref[:, cols]))
        kk = k * kk_ref[:, cols]
        kk = kk * jnp.minimum(lax.rsqrt(jnp.maximum(seg_sum(kk * kk), 0.0)), 1e12)
        k_h = k * (1.0 + (a_sig - 1.0) * ka_ref[:, cols])
        if Lv < L:
            valid = rowv[:, 0:r.shape[1]]
            logw = jnp.where(valid, logw, 0.0)
            kk = jnp.where(valid, kk, 0.0)
            k_h = jnp.where(valid, k_h, 0.0)
            v = jnp.where(valid, v, 0.0)

        hi, mid, lo = _split3(logw)
        cum = _dot(tri, hi) + _dot(tri, mid) + _dot(tri, lo)
        cum_l = cum[L - 1:L, :]
        p_inv = jnp.exp(-cum)
        p_end = jnp.exp(cum_l - cum)
        bvec = kk * a_sig
        g = _dot(sg, gl_ref[:, cols])
        return dict(
            a_t=(-kk * jnp.exp(cum - logw)).astype(BF16), r_t=(r * jnp.exp(cum)).astype(BF16),
            b_t=(bvec * p_inv).astype(BF16), k_t=(k_h * p_inv).astype(BF16),
            b_e=(bvec * p_end).astype(BF16), k_e=(k_h * p_end).astype(BF16), v_b=v.astype(BF16),
            g=g, bg=seg_sum(r * k_h * rk_ref[:, cols]) * v * g, p_l=jnp.exp(cum_l))

    def chain_stages(t, sd, lnw, lnb, res):
        a_t, r_t, b_t, k_t, b_e, k_e, v_b = (t[n] for n in STAGED[:7])
        lhs = [jnp.concatenate([a_t[:, s], r_t[:, s]], axis=0) for s in sls]
        rhs = [jnp.concatenate([stack_heads(b_t[:, s], head1), stack_heads(k_t[:, s], head1)], axis=0) for s in sls]
        aa = [_dot_nt(lhs[i], rhs[i]) for i in P]
        yield
        sa = [_dot_nt(lhs[i], sd[i].astype(BF16)) for i in P]
        yield
        v_st = [stack_heads(v_b[:, s], head1) for s in sls]
        a_ak = [jnp.where(strict, aa[i][0:L, 2 * L:4 * L], 0.0).astype(BF16) for i in P]
        x = [sa[i][0:L] + _dot(a_ak[i], v_st[i]) for i in P]
        ap = [jnp.where(strict, aa[i][0:L, 0:2 * L], 0.0).astype(BF16) for i in P]
        yield
        for d in range(n_dbl):
            x = [x[i] + _dot(ap[i], stack_heads(x[i].astype(BF16), head1)) for i in P]
            yield
            if d + 1 < n_dbl:
                ap = [_dot(ap[i], stack_heads(ap[i], colh1)).astype(BF16) for i in P]
                yield
        u_b = [x[i].astype(BF16) for i in P]
        a_rb = [jnp.where(incl, aa[i][L:2 * L, 0:2 * L], 0.0).astype(BF16) for i in P]
        a_rk = [jnp.where(incl, aa[i][L:2 * L, 2 * L:4 * L], 0.0).astype(BF16) for i in P]
        y = [sa[i][L:2 * L] + _dot(a_rb[i], stack_heads(u_b[i], head1)) + _dot(a_rk[i], v_st[i]) for i in P]
        yield
        ds = [_dot_tn(jnp.concatenate([u_b[i], v_b[:, sls[i]]], axis=0),
                      jnp.concatenate([b_e[:, sls[i]], k_e[:, sls[i]]], axis=0)) for i in P]
        res["sd"] = [sd[i] * t["p_l"][:, sls[i]] + jnp.where(same_head, ds[i], 0.0) for i in P]
        yield
        y = jnp.concatenate(y, axis=1)
        mean = seg_sum(y) * (1.0 / H)
        yield
        dlt = y - mean
        var = seg_sum(dlt * dlt) * (1.0 / H)
        yield
        yn = dlt * lax.rsqrt(var + GN_EPS) * lnw + lnb
        res["out"] = yn * t["g"] + t["bg"]

    def one_batch(bi):
        if NC == 1:
            init(bi)
        else:
            pl.when(c == 0)(lambda: init(bi))
        stage_inputs(bi)
        t = prologue(bi, lora_inputs(bi), slice(None))
        res = {}
        for _ in chain_stages(t, [sd_ref[bi, i] for i in P], lnw_ref[...], lnb_ref[...], res):
            pass
        for i in P:
            sd_ref[bi, i] = res["sd"][i]
        y_ref[bi] = res["out"][0:Lv].astype(y_ref.dtype)
        carry_rows(bi)
        if NC == 1:
            finish(bi)
        else:
            pl.when(c == NC - 1)(lambda: finish(bi))

    def one_batch_pipelined():
        st = dict(zip(STAGED, stage_refs))

        @pl.when(c == 0)
        def _():
            init(0)
            for ref in stage_refs:
                ref[...] = jnp.zeros(ref.shape, ref.dtype)

        prev = {n: st[n][...] for n in STAGED}
        sd = [sd_ref[0, i] for i in P]
        lnw, lnb = lnw_ref[...], lnb_ref[...]
        stage_inputs(0)
        lora = lora_inputs(0)
        res, tiles = {}, []
        for _ in chain_stages(prev, sd, lnw, lnb, res):
            if len(tiles) < HPS:
                tiles.append(prologue(0, lora, sls[len(tiles)]))
        while len(tiles) < HPS:
            tiles.append(prologue(0, lora, sls[len(tiles)]))
        for n in STAGED:
            for i in P:
                st[n][:, sls[i]] = tiles[i][n]
        live = c > 0
        for i in P:
            sd_ref[0, i] = jnp.where(live, res["sd"][i], sd[i])
        y_ref[0] = res["out"].astype(y_ref.dtype)
        carry_rows(0)
        pl.when(c == NC)(lambda: finish(0))

    if pipelined:
        one_batch_pipelined()
    elif BB == 1:
        one_batch(0)
    else:
        def loop_body(bi, carry):
            one_batch(bi)
            return carry
        lax.fori_loop(0, BB, loop_body, 0)


def _rwkv_mix(proj3, shift3, s0, p, *, L, Lv, HPS, BB, y_dtype, pipelined):
    b, t, _ = proj3.shape
    assert t % Lv == 0 and b % BB == 0
    assert not pipelined or (BB == 1 and Lv == L)
    nc = t // Lv
    w = HPS * LANES
    nhg = D_RWKV // w
    kb = D_RWKV // w
    lora_blk = 3 * D_RWKV // LORA_W
    n_dbl = max(1, math.ceil(math.log2(Lv)))
    if pipelined:
        cin = lambda c: jnp.minimum(c, nc - 1)
        cout = lambda c: jnp.maximum(c - 1, 0)
    else:
        cin = cout = lambda c: c
    col = lambda o: (lambda bi, g, c: (bi, cin(c), o + g))
    sh = lambda o: (lambda bi, g, c: (bi, 0, o + g))
    par = lambda o: (lambda bi, g, c: (0, o + g))
    in_specs = [
        pl.BlockSpec((BB, Lv, w), col(0)), pl.BlockSpec((BB, Lv, w), col(kb)), pl.BlockSpec((BB, Lv, w), col(2 * kb)),
        pl.BlockSpec((BB, Lv, LORA_W), lambda bi, g, c: (bi, cin(c), lora_blk)),
        pl.BlockSpec((BB, 1, w), sh(0)), pl.BlockSpec((BB, 1, w), sh(kb)), pl.BlockSpec((BB, 1, w), sh(2 * kb)),
        pl.BlockSpec((BB, 1, LORA_W), lambda bi, g, c: (bi, 0, lora_blk)),
        pl.BlockSpec((BB, 2 * HPS, HEAD_DIM, HEAD_DIM), lambda bi, g, c: (bi, g, 0, 0)),
        pl.BlockSpec((1, w), par(0)), pl.BlockSpec((1, w), par(kb)), pl.BlockSpec((1, w), par(2 * kb)),
        pl.BlockSpec((1, LORA_W), lambda bi, g, c: (0, lora_blk)),
    ] + [pl.BlockSpec((1, w), par(0))] * 7 + [
        pl.BlockSpec((LANES, w), par(0)), pl.BlockSpec((LORA_A_WIN, w), par(0)),
        pl.BlockSpec((LORA_W - LORA_G_START, w), par(0)),
    ]
    out_specs = [pl.BlockSpec((BB, Lv, w), lambda bi, g, c: (bi, cout(c), g)),
                 pl.BlockSpec((BB, 2 * HPS, HEAD_DIM, HEAD_DIM), lambda bi, g, c: (bi, g, 0, 0))]
    scratch = [pltpu.VMEM((BB, HPS, LANES, LANES), F32),
               pltpu.VMEM((BB, L + 8, w), F32), pltpu.VMEM((BB, L + 8, w), F32),
               pltpu.VMEM((BB, L + 8, w), F32), pltpu.VMEM((BB, L + 8, LORA_W), F32)]
    if pipelined:
        scratch += [pltpu.VMEM((L, w), BF16)] * 7 + [pltpu.VMEM((L, w), F32)] * 2 + [pltpu.VMEM((1, w), F32)]
    kern = functools.partial(_rwkv_kernel, L=L, Lv=Lv, HPS=HPS, BB=BB, NC=nc, n_dbl=n_dbl, pipelined=pipelined)
    return pl.pallas_call(
        kern,
        grid=(b // BB, nhg, nc + 1 if pipelined else nc),
        in_specs=in_specs,
        out_specs=out_specs,
        out_shape=[jax.ShapeDtypeStruct((b, t, D_RWKV), y_dtype),
                   jax.ShapeDtypeStruct((b, N_HEADS, HEAD_DIM, HEAD_DIM), F32)],
        scratch_shapes=scratch,
        compiler_params=_cparams(("parallel", "parallel", "arbitrary")),
        name="rwkv_mix",
    )(proj3, proj3, proj3, proj3, shift3, shift3, shift3, shift3, s0,
      p["mu"], p["mu"], p["mu"], p["mu"], p["w0"], p["a0"], p["k_k"], p["k_a"], p["r_k"],
      p["ln_w"], p["ln_b"], p["w_lora"], p["a_lora"], p["g_lora"])


def _swa_prompt_kernel(slope_ref, sink_ref, q_ref, kc_ref, kp_ref, vc_ref, vp_ref, o_ref, bias_ref):
    n = pl.program_id(1)
    blk = WINDOW
    H = HEAD_DIM

    @pl.when((pl.program_id(0) == 0) & (n == 0))
    def _():
        t = lax.broadcasted_iota(jnp.int32, (blk, 2 * blk), 0)
        j = lax.broadcasted_iota(jnp.int32, (blk, 2 * blk), 1)
        dist = t - j + blk
        band = (dist >= 0) & (dist <= WINDOW)
        first = band & (j >= blk)
        distf = dist.astype(F32)
        for h in range(N_HEADS):
            ab = -slope_ref[h] * distf
            bias_ref[0, h] = jnp.where(first, ab, -jnp.inf)
            bias_ref[1, h] = jnp.where(band, ab, -jnp.inf)

    sel = jnp.where(n == 0, 0, 1)
    low = lax.broadcasted_iota(jnp.int32, (blk, LANES), 1) < H
    scale = H ** -0.5
    tile = lambda i: slice(i * LANES, (i + 1) * LANES)

    def kv_group(KV):
        kslab = {hk: jnp.concatenate([kp_ref[0, :, tile(hk // 2)], kc_ref[0, :, tile(hk // 2)]],
                                     axis=0).astype(BF16) for hk in KV}
        vslab = {hk: jnp.concatenate([vp_ref[0, :, tile(hk // 2)], vc_ref[0, :, tile(hk // 2)]],
                                     axis=0).astype(BF16) for hk in KV}
        lhs = {}
        for hk in KV:
            parts = []
            for s2 in range(2):
                xs = q_ref[0, :, tile(2 * hk + s2)].astype(F32) * scale
                xr = pltpu.roll(xs, H, axis=1)
                if hk % 2 == 0:
                    parts += [jnp.where(low, xs, 0.0), jnp.where(low, xr, 0.0)]
                else:
                    parts += [jnp.where(low, 0.0, xr), jnp.where(low, 0.0, xs)]
            lhs[hk] = jnp.concatenate(parts, axis=0).astype(BF16)
        s = {hk: _dot_nt(lhs[hk], kslab[hk]) for hk in KV}
        p, rden = {}, {}
        for hk in KV:
            ps, rs = [], []
            for g in range(GQA_GROUP):
                h = hk * GQA_GROUP + g
                sg = s[hk][g * blk:(g + 1) * blk] + bias_ref[sel, h]
                m = jnp.maximum(jnp.max(sg, axis=-1, keepdims=True), sink_ref[h])
                e = jnp.exp(sg - m)
                rs.append(1.0 / (jnp.sum(e, axis=-1, keepdims=True) + jnp.exp(sink_ref[h] - m)))
                ps.append(e.astype(BF16))
            p[hk] = jnp.concatenate(ps, axis=0)
            rden[hk] = rs
        o = {hk: _dot(p[hk], vslab[hk]) for hk in KV}
        for hk in KV:
            for s2 in range(2):
                ga, gb = 2 * s2, 2 * s2 + 1
                oa = o[hk][ga * blk:(ga + 1) * blk] * rden[hk][ga]
                ob = o[hk][gb * blk:(gb + 1) * blk] * rden[hk][gb]
                if hk % 2 == 0:
                    out = jnp.where(low, oa, pltpu.roll(ob, H, axis=1))
                else:
                    out = jnp.where(low, pltpu.roll(oa, H, axis=1), ob)
                o_ref[0, :, tile(2 * hk + s2)] = out.astype(o_ref.dtype)

    for g0 in range(0, N_KV_HEADS, KV_PER_GROUP):
        kv_group(range(g0, g0 + KV_PER_GROUP))


def _swa_prompt(q3, kv3, slopes, sinks):
    b, t, _ = q3.shape
    nb = t // WINDOW
    smem = pl.BlockSpec(memory_space=pltpu.SMEM)
    prev = lambda n: jnp.maximum(n - 1, 0)
    return pl.pallas_call(
        _swa_prompt_kernel,
        grid=(b, nb),
        in_specs=[smem, smem,
                  pl.BlockSpec((1, WINDOW, D_MODEL), lambda bi, n: (bi, n, 0)),
                  pl.BlockSpec((1, WINDOW, D_KV), lambda bi, n: (bi, n, 0)),
                  pl.BlockSpec((1, WINDOW, D_KV), lambda bi, n: (bi, prev(n), 0)),
                  pl.BlockSpec((1, WINDOW, D_KV), lambda bi, n: (bi, n, 1)),
                  pl.BlockSpec((1, WINDOW, D_KV), lambda bi, n: (bi, prev(n), 1))],
        out_specs=pl.BlockSpec((1, WINDOW, D_MODEL), lambda bi, n: (bi, n, 0)),
        out_shape=jax.ShapeDtypeStruct((b, t, D_MODEL), BF16),
        scratch_shapes=[pltpu.VMEM((2, N_HEADS, WINDOW, 2 * WINDOW), F32)],
        compiler_params=_cparams(("arbitrary", "arbitrary")),
        name="swa_prompt",
    )(slopes, sinks, q3, kv3, kv3, kv3, kv3)


def _swa_sample_kernel(slope_ref, sink_ref, q_ref, kc_ref, vc_ref, knew_ref, vnew_ref,
                       o_ref, kwin_ref, vwin_ref, nbuf_ref, *, tq, BB, UNR):
    GT = GQA_GROUP * tq
    R = N_KV_HEADS * GT
    NP = 16
    C = D_KV
    row = lax.broadcasted_iota(jnp.int32, (R, WINDOW), 0)
    wcol = lax.broadcasted_iota(jnp.int32, (R, WINDOW), 1)
    t = lax.rem(row, tq)
    slope = slope_ref[...]
    sink = sink_ref[...]
    dist_o = WINDOW + t - wcol
    bias_old = jnp.where(dist_o <= WINDOW, -slope * dist_o.astype(F32), -jnp.inf)
    s_idx = wcol - (WINDOW - tq)
    dist_n = t - s_idx
    bias_new = jnp.where((s_idx >= 0) & (dist_n >= 0), -slope * dist_n.astype(F32), -jnp.inf)
    hkmask = (lax.broadcasted_iota(jnp.int32, (R, C), 0) // GT) == (lax.broadcasted_iota(jnp.int32, (R, C), 1) // HEAD_DIM)
    srow = lax.broadcasted_iota(jnp.int32, (NP, WINDOW), 0)
    scol = lax.broadcasted_iota(jnp.int32, (NP, WINDOW), 1)
    selw = ((scol == srow + (WINDOW - tq)) & (srow < tq)).astype(BF16)
    lane_new = lax.broadcasted_iota(jnp.int32, (C, WINDOW), 1) >= WINDOW - tq
    scale = HEAD_DIM ** -0.5
    for u in range(UNR):
        nbuf_ref[u, :, tq:NP, :] = jnp.zeros((2, NP - tq, C), F32)

    def transposed_new(x):
        return sum(_dot_tn(part, selw) for part in _split3(x))

    def body(i, carry):
        bs = [i * UNR + u for u in range(UNR)]
        U = range(UNR)
        for u in U:
            nbuf_ref[u, 0, 0:tq, :] = knew_ref[bs[u]]
            nbuf_ref[u, 1, 0:tq, :] = vnew_ref[bs[u]]
        kt = [kc_ref[b] for b in bs]
        vt = [vc_ref[b] for b in bs]
        knt = [transposed_new(nbuf_ref[u, 0]) for u in U]
        vnt = [transposed_new(nbuf_ref[u, 1]) for u in U]
        qbd = [jnp.where(hkmask, jnp.concatenate([q_ref[b] * scale] * N_KV_HEADS, axis=0), 0.0).astype(BF16)
               for b in bs]
        s_o = [_dot(qbd[u], kt[u].astype(BF16)) + bias_old for u in U]
        s_n = [_dot(qbd[u], knt[u].astype(BF16)) + bias_new for u in U]
        outs = []
        for u in U:
            m = jnp.maximum(jnp.maximum(jnp.max(s_o[u], axis=-1, keepdims=True),
                                        jnp.max(s_n[u], axis=-1, keepdims=True)), sink)
            p_o = jnp.exp(s_o[u] - m)
            p_n = jnp.exp(s_n[u] - m)
            rden = 1.0 / (jnp.sum(p_o, axis=-1, keepdims=True) + jnp.sum(p_n, axis=-1, keepdims=True)
                          + jnp.exp(sink - m))
            o = _dot_nt(p_o.astype(BF16), vt[u].astype(BF16)) + _dot_nt(p_n.astype(BF16), vnt[u].astype(BF16))
            o = jnp.where(hkmask, o * rden, 0.0)
            acc = o[0:GT]
            for hk in range(1, N_KV_HEADS):
                acc = acc + o[hk * GT:(hk + 1) * GT]
            outs.append(acc)
        for u in U:
            kwin_ref[bs[u]] = jnp.where(lane_new, knt[u], pltpu.roll(kt[u], WINDOW - tq, axis=1))
            vwin_ref[bs[u]] = jnp.where(lane_new, vnt[u], pltpu.roll(vt[u], WINDOW - tq, axis=1))
            o_ref[bs[u]] = outs[u]
        return carry

    lax.fori_loop(0, BB // UNR, body, 0)


def _swa_sample(q16, knew3, vnew3, kct, vct, slope_rows, sink_rows):
    b, gt, c = q16.shape
    tq = gt // GQA_GROUP
    bb = 8 if b % 8 == 0 else 1
    unr = 2 if bb % 2 == 0 else 1
    rows = N_KV_HEADS * gt
    blk3 = lambda shape: pl.BlockSpec(shape, lambda i: (i, 0, 0))
    full2 = pl.BlockSpec((rows, 1), lambda i: (0, 0))
    kern = functools.partial(_swa_sample_kernel, tq=tq, BB=bb, UNR=unr)
    return pl.pallas_call(
        kern,
        grid=(b // bb,),
        in_specs=[full2, full2, blk3((bb, gt, c)), blk3((bb, c, WINDOW)), blk3((bb, c, WINDOW)),
                  blk3((bb, tq, c)), blk3((bb, tq, c))],
        out_specs=[blk3((bb, gt, c)), blk3((bb, c, WINDOW)), blk3((bb, c, WINDOW))],
        out_shape=[jax.ShapeDtypeStruct((b, gt, c), F32),
                   jax.ShapeDtypeStruct((b, c, WINDOW), F32),
                   jax.ShapeDtypeStruct((b, c, WINDOW), F32)],
        scratch_shapes=[pltpu.VMEM((unr, 2, 16, c), F32)],
        compiler_params=_cparams(("parallel",)),
        name="swa_sample",
    )(slope_rows, sink_rows, q16, kct, vct, knew3, vnew3)


def _merge_out_kernel(x_ref, ya_ref, yb_ref, ga_ref, gb_ref, wo_ref, nw_ref, h_ref, hn_ref):
    mixed = _sigmoid(ga_ref[...]) * ya_ref[...] + _sigmoid(gb_ref[...]) * yb_ref[...]
    h = x_ref[...] + _dot(mixed.astype(BF16), wo_ref[...])
    h_ref[...] = h
    ms = jnp.mean(h * h, axis=-1, keepdims=True)
    hn_ref[...] = (h * lax.rsqrt(ms + RMS_EPS) * nw_ref[...]).astype(BF16)


def _merge_out(x, ya, yb, gates, w_out, nw, tm):
    m, d = x.shape
    row = lambda o: (lambda i: (i, o))
    return pl.pallas_call(
        _merge_out_kernel,
        grid=(m // tm,),
        in_specs=[pl.BlockSpec((tm, d), row(0)), pl.BlockSpec((tm, d), row(0)), pl.BlockSpec((tm, d), row(0)),
                  pl.BlockSpec((tm, d), row(0)), pl.BlockSpec((tm, d), row(1)),
                  pl.BlockSpec((d, d), lambda i: (0, 0)), pl.BlockSpec((1, d), lambda i: (0, 0))],
        out_specs=[pl.BlockSpec((tm, d), row(0)), pl.BlockSpec((tm, d), row(0))],
        out_shape=[jax.ShapeDtypeStruct((m, d), F32), jax.ShapeDtypeStruct((m, d), BF16)],
        compiler_params=_cparams(("parallel",)),
        name="merge_out_proj",
    )(x, ya, yb, gates, gates, w_out, nw)


def _mlp_kernel(hn_ref, h_ref, wu_ref, wd_ref, nw_ref, o_ref, acc_ref):
    j = pl.program_id(1)

    @pl.when(j == 0)
    def _():
        acc_ref[...] = jnp.zeros_like(acc_ref)

    u = jnp.maximum(_dot(hn_ref[...], wu_ref[...]), 0.0)
    acc_ref[...] += _dot((u * u).astype(BF16), wd_ref[...])

    @pl.when(j == pl.num_programs(1) - 1)
    def _():
        h = h_ref[...] + acc_ref[...]
        ms = jnp.mean(h * h, axis=-1, keepdims=True)
        o_ref[...] = h * lax.rsqrt(ms + RMS_EPS) * nw_ref[...]


def _mlp(hn, h, w_up, w_down, nw, tm, tf):
    m, d = h.shape
    f = w_up.shape[1]
    return pl.pallas_call(
        _mlp_kernel,
        grid=(m // tm, f // tf),
        in_specs=[pl.BlockSpec((tm, d), lambda i, j: (i, 0)), pl.BlockSpec((tm, d), lambda i, j: (i, 0)),
                  pl.BlockSpec((d, tf), lambda i, j: (0, j)), pl.BlockSpec((tf, d), lambda i, j: (j, 0)),
                  pl.BlockSpec((1, d), lambda i, j: (0, 0))],
        out_specs=pl.BlockSpec((tm, d), lambda i, j: (i, 0)),
        out_shape=jax.ShapeDtypeStruct((m, d), F32),
        scratch_shapes=[pltpu.VMEM((tm, d), F32)],
        compiler_params=_cparams(("parallel", "arbitrary")),
        name="mlp_final_norm",
    )(hn, h, w_up, w_down, nw)


def _pick(m, prefs):
    for t in prefs:
        if m % t == 0:
            return t
    return m


def _pad_cols(v, n):
    return jnp.concatenate([v, jnp.zeros(v.shape[:-1] + (n - v.shape[-1],), v.dtype)], axis=-1)


def _place_rows(w, start, rows):
    n, d = w.shape
    return jnp.concatenate([jnp.zeros((start, d), w.dtype), w, jnp.zeros((rows - start - n, d), w.dtype)], axis=0)


def _layer(x, shift_prev, wkv0, lw, *, prompt, k_cache_t=None, v_cache_t=None):
    b, t, d = x.shape
    m = b * t
    x2 = x.reshape(m, d)
    proj_r, proj_q, proj_kv, proj_g = _in_proj(
        x2, lw["norm_mix_w"], [lw["w_r"], lw["w_q"], lw["w_kv"], lw["w_g"]], [F32, BF16, F32, F32],
        _pick(m, (1024, 512, 256, 128, 8)))
    proj_r3 = proj_r.reshape(b, t, R_PAD)
    kv3 = proj_kv.reshape(b, t, 2 * D_KV)
    shift3 = _pad_cols(shift_prev, R_PAD)[:, None]

    if prompt:
        ya, wkv_new = _rwkv_mix(proj_r3, shift3, wkv0, lw, L=64, Lv=64, HPS=16, BB=1, y_dtype=BF16,
                                pipelined=False)
        yb = _swa_prompt(proj_q.reshape(b, t, D_MODEL), kv3, lw["slopes"], lw["sinks"])
        k_win = kv3[:, t - WINDOW:, :D_KV].reshape(b, WINDOW, N_KV_HEADS, HEAD_DIM)
        v_win = kv3[:, t - WINDOW:, D_KV:].reshape(b, WINDOW, N_KV_HEADS, HEAD_DIM)
    else:
        ya, wkv_new = _rwkv_mix(proj_r3, shift3, wkv0, lw, L=16, Lv=t, HPS=16, BB=_pick(b, (4, 2, 1)), y_dtype=F32,
                                pipelined=False)
        q16 = proj_q.reshape(b, t, N_KV_HEADS, GQA_GROUP, HEAD_DIM).transpose(0, 3, 1, 2, 4)
        q16 = q16.reshape(b, GQA_GROUP * t, D_KV)
        gt_head = (jnp.arange(N_KV_HEADS)[:, None] * GQA_GROUP + jnp.arange(GQA_GROUP)[None, :])
        row_head = jnp.repeat(gt_head, t, axis=1).reshape(-1)
        o16, kwt, vwt = _swa_sample(q16, kv3[:, :, :D_KV], kv3[:, :, D_KV:], k_cache_t, v_cache_t,
                                    lw["slopes"][row_head][:, None], lw["sinks"][row_head][:, None])
        yb = o16.reshape(b, GQA_GROUP, t, N_KV_HEADS, HEAD_DIM).transpose(0, 2, 3, 1, 4).reshape(b, t, D_MODEL)
        k_win = kwt.reshape(b, N_KV_HEADS, HEAD_DIM, WINDOW).transpose(0, 3, 1, 2)
        v_win = vwt.reshape(b, N_KV_HEADS, HEAD_DIM, WINDOW).transpose(0, 3, 1, 2)

    h, hn = _merge_out(x2, ya.reshape(m, d), yb.reshape(m, d), proj_g, lw["w_out"], lw["norm_mlp_w"],
                       _pick(m, (256, 128, 8)))
    y = _mlp(hn, h, lw["w_up"], lw["w_down"], lw["norm_final_w"], _pick(m, (512, 256, 128, 8)), 1024)
    shift_new = proj_r3[:, t - 1, :R_COLS]
    return y.reshape(b, t, d), shift_new, wkv_new, k_win, v_win


def kernel(x_prompt, x_sample, state_shift, state_wkv, cache_k_win, cache_v_win, norm_mix_w, w_in, tshift_mu, w0, w_lora, a0, a_lora, g_lora, k_k, k_a, r_k, ln_x_w, ln_x_b, attn_sinks, w_out, norm_mlp_w, w_up, w_down, norm_final_w):
    depth = w_in.shape[0]
    assert depth == 1
    l = 0
    bp = x_prompt.shape[0]
    db = x_sample.shape[0]
    hh = jnp.arange(N_HEADS, dtype=F32)
    w_t = jnp.swapaxes(w_in[l], 0, 1)
    o_q, o_kv, o_g = R_COLS, R_COLS + D_MODEL, R_COLS + D_MODEL + 2 * D_KV
    lw = dict(
        norm_mix_w=norm_mix_w[l][None],
        w_r=_place_rows(w_t[:R_COLS], 0, R_PAD).astype(BF16), w_q=w_t[o_q:o_kv].astype(BF16),
        w_kv=w_t[o_kv:o_g].astype(BF16), w_g=w_t[o_g:].astype(BF16),
        mu=_pad_cols(tshift_mu[l][None], R_PAD), w0=w0[l][None], a0=a0[l][None], k_k=k_k[l][None],
        k_a=k_a[l][None], r_k=r_k[l].reshape(1, D_RWKV), ln_w=ln_x_w[l][None], ln_b=ln_x_b[l][None],
        w_lora=_place_rows(w_lora[l], 0, LANES).astype(BF16),
        a_lora=_place_rows(a_lora[l], LORA_DECAY_END, LORA_A_WIN).astype(BF16),
        g_lora=_place_rows(g_lora[l], LORA_A_END - LORA_G_START, LORA_W - LORA_G_START).astype(BF16),
        slopes=jnp.exp2(-8.0 * (hh + 1.0) / N_HEADS), sinks=attn_sinks[l].astype(F32),
        w_out=w_out[l].astype(BF16), norm_mlp_w=norm_mlp_w[l][None],
        w_up=w_up[l].astype(BF16), w_down=w_down[l].astype(BF16), norm_final_w=norm_final_w[None],
    )
    yp, sp, wp, kp, vp = _layer(
        x_prompt, jnp.zeros((bp, R_COLS), F32), jnp.zeros((bp, N_HEADS, HEAD_DIM, HEAD_DIM), F32), lw, prompt=True)
    kct = cache_k_win[l].transpose(0, 2, 3, 1).reshape(db, D_KV, WINDOW)
    vct = cache_v_win[l].transpose(0, 2, 3, 1).reshape(db, D_KV, WINDOW)
    ys, ss, ws, ksm, vsm = _layer(x_sample, state_shift[l], state_wkv[l], lw, prompt=False,
                                  k_cache_t=kct, v_cache_t=vct)
    return (yp, ys, sp[None], wp[None], kp[None], vp[None], ss[None], ws[None], ksm[None], vsm[None])
```

```python
import functools
import math

import jax
import jax.numpy as jnp
from jax import lax
from jax.experimental import pallas as pl
from jax.experimental.pallas import tpu as pltpu

F32 = jnp.float32
BF16 = jnp.bfloat16

D_MODEL = 2048
HEAD_DIM = 64
N_HEADS = D_MODEL // HEAD_DIM
N_KV_HEADS = 8
GQA_GROUP = N_HEADS // N_KV_HEADS
D_KV = N_KV_HEADS * HEAD_DIM
WINDOW = 128
D_FF = 4 * D_MODEL
D_DECAY_LORA = 96
D_A_LORA = 96
D_GATE_LORA = 256
D_RWKV = D_MODEL
R_COLS = 3 * D_RWKV + D_DECAY_LORA + D_A_LORA + D_GATE_LORA
C_IN = R_COLS + D_MODEL + 2 * D_KV + 2 * D_MODEL
RMS_EPS = 1e-5
GN_EPS = 64e-5

LANES = 128
PROJ_TILE = 512
R_PAD = -(-R_COLS // PROJ_TILE) * PROJ_TILE
LORA_W = R_PAD - 3 * D_RWKV
LORA_DECAY_END = D_DECAY_LORA
LORA_A_END = D_DECAY_LORA + D_A_LORA
LORA_G_END = LORA_A_END + D_GATE_LORA
LORA_A_WIN = -(-LORA_A_END // LANES) * LANES
LORA_G_START = (LORA_A_END // LANES) * LANES

VMEM_LIMIT = 56 * 1024 * 1024
KV_PER_GROUP = 2


def _cparams(sem):
    return pltpu.CompilerParams(dimension_semantics=sem, vmem_limit_bytes=VMEM_LIMIT)


def _dot(a, b):
    return jnp.dot(a, b, preferred_element_type=F32)


def _dot_nt(a, b):
    return lax.dot_general(a, b, (((1,), (1,)), ((), ())), preferred_element_type=F32)


def _dot_tn(a, b):
    return lax.dot_general(a, b, (((0,), (0,)), ((), ())), preferred_element_type=F32)


def _softplus(x):
    return jnp.maximum(x, 0.0) + jnp.log(1.0 + jnp.exp(-jnp.abs(x)))


def _sigmoid(x):
    return 1.0 / (1.0 + jnp.exp(-x))


def _split2(x):
    hi = x.astype(BF16)
    lo = (x - hi.astype(F32)).astype(BF16)
    return hi, lo


def _split3(x):
    hi = x.astype(BF16)
    r1 = x - hi.astype(F32)
    mid = r1.astype(BF16)
    lo = (r1 - mid.astype(F32)).astype(BF16)
    return hi, mid, lo


def _in_proj_kernel(x_ref, nw_ref, *refs, bounds):
    n = len(bounds)
    w_refs, o_refs, xn_ref = refs[:n], refs[n:2 * n], refs[2 * n]
    j = pl.program_id(1)

    @pl.when(j == 0)
    def _():
        x = x_ref[...]
        ms = jnp.mean(x * x, axis=-1, keepdims=True)
        xn_ref[...] = (x * lax.rsqrt(ms + RMS_EPS) * nw_ref[...]).astype(BF16)

    for w_ref, o_ref, (lo, hi) in zip(w_refs, o_refs, bounds):
        @pl.when((j >= lo) & (j < hi))
        def _(w_ref=w_ref, o_ref=o_ref):
            o_ref[...] = _dot_nt(xn_ref[...], w_ref[...]).astype(o_ref.dtype)


def _in_proj(x, nw, weights, out_dtypes, tm):
    m, d = x.shape
    tn = PROJ_TILE
    bounds, lo = [], 0
    for w in weights:
        bounds.append((lo, lo + w.shape[0] // tn))
        lo = bounds[-1][1]

    def clamp(lo_, hi_):
        return lambda j: jnp.clip(j - lo_, 0, hi_ - lo_ - 1)

    in_specs = [pl.BlockSpec((tm, d), lambda i, j: (i, 0)), pl.BlockSpec((1, d), lambda i, j: (0, 0))]
    out_specs, out_shape = [], []
    for w, dt, (lo_, hi_) in zip(weights, out_dtypes, bounds):
        c = clamp(lo_, hi_)
        in_specs.append(pl.BlockSpec((tn, d), lambda i, j, c=c: (c(j), 0)))
        out_specs.append(pl.BlockSpec((tm, tn), lambda i, j, c=c: (i, c(j))))
        out_shape.append(jax.ShapeDtypeStruct((m, w.shape[0]), dt))
    return pl.pallas_call(
        functools.partial(_in_proj_kernel, bounds=tuple(bounds)),
        grid=(m // tm, lo),
        in_specs=in_specs,
        out_specs=out_specs,
        out_shape=out_shape,
        scratch_shapes=[pltpu.VMEM((tm, d), BF16)],
        compiler_params=_cparams(("parallel", "arbitrary")),
        name="norm_in_proj",
    )(x, nw, *weights)


def _rwkv_kernel_unpipelined(pr_ref, pk_ref, pv_ref, plo_ref, shr_ref, shk_ref, shv_ref, shl_ref, s0_ref,
                 mur_ref, muk_ref, muv_ref, mul_ref, w0_ref, a0_ref, kk_ref, ka_ref, rk_ref, lnw_ref, lnb_ref,
                 wl_ref, al_ref, gl_ref,
                 y_ref, so_ref,
                 sd_ref, br_ref, bk_ref, bv_ref, bl_ref, *, L, Lv, HPS, BB, NC, n_dbl):
    c = pl.program_id(2)
    H = HEAD_DIM
    W = HPS * LANES
    sls = [slice(i * LANES, (i + 1) * LANES) for i in range(HPS)]

    lane = lax.broadcasted_iota(jnp.int32, (L, LANES), 1)
    head1 = lane < H
    row_i = lax.broadcasted_iota(jnp.int32, (L, 2 * L), 0)
    col_i = lax.broadcasted_iota(jnp.int32, (L, 2 * L), 1)
    col_t = jnp.where(col_i >= L, col_i - L, col_i)
    strict = col_t < row_i
    incl = col_t <= row_i
    colh1 = col_i < L
    tri = (lax.broadcasted_iota(jnp.int32, (L, L), 1) <= lax.broadcasted_iota(jnp.int32, (L, L), 0)).astype(BF16)
    ji = lax.broadcasted_iota(jnp.int32, (LANES, LANES), 0)
    jj = lax.broadcasted_iota(jnp.int32, (LANES, LANES), 1)
    same_head = (ji < H) == (jj < H)
    seg = same_head.astype(BF16)
    rowv = lax.broadcasted_iota(jnp.int32, (L, W), 0) < Lv

    def seg_sum(x):
        xs = jnp.concatenate([x[:, s] for s in sls], axis=0)
        hi, lo = _split2(xs)
        ys = _dot(hi, seg) + _dot(lo, seg)
        return jnp.concatenate([ys[i * L:(i + 1) * L] for i in range(HPS)], axis=1)

    def stack_heads(x, m):
        zero = jnp.zeros_like(x)
        return jnp.concatenate([jnp.where(m, x, zero), jnp.where(m, zero, x)], axis=0)

    def init(bi):
        br_ref[bi, 7:8, :] = shr_ref[bi]
        bk_ref[bi, 7:8, :] = shk_ref[bi]
        bv_ref[bi, 7:8, :] = shv_ref[bi]
        bl_ref[bi, 7:8, :] = shl_ref[bi]
        if Lv < L:
            for ref in (br_ref, bk_ref, bv_ref, bl_ref):
                ref[bi, 8 + Lv:8 + L, :] = jnp.zeros((L - Lv, ref.shape[2]), F32)
        z = jnp.zeros((H, H), F32)
        for i in range(HPS):
            top = jnp.concatenate([s0_ref[bi, 2 * i], z], axis=1)
            bot = jnp.concatenate([z, s0_ref[bi, 2 * i + 1]], axis=1)
            sd_ref[bi, i] = jnp.concatenate([top, bot], axis=0)

    def finish(bi):
        for i in range(HPS):
            sd = sd_ref[bi, i]
            so_ref[bi, 2 * i] = sd[0:H, 0:H]
            so_ref[bi, 2 * i + 1] = sd[H:2 * H, H:2 * H]

    def one_batch(bi):
        if NC == 1:
            init(bi)
        else:
            pl.when(c == 0)(lambda: init(bi))

        br_ref[bi, 8:8 + Lv, :] = pr_ref[bi]
        bk_ref[bi, 8:8 + Lv, :] = pk_ref[bi]
        bv_ref[bi, 8:8 + Lv, :] = pv_ref[bi]
        bl_ref[bi, 8:8 + Lv, :] = plo_ref[bi]

        def shifted(buf, mu):
            p = buf[bi, 8:8 + L, :]
            return p + mu * (buf[bi, 7:7 + L, :] - p)

        ps_l = shifted(bl_ref, mul_ref[...])
        td = jnp.tanh(ps_l[:, 0:LANES]).astype(BF16)
        da = ps_l[:, 0:LORA_A_WIN].astype(BF16)
        sg = _sigmoid(ps_l[:, LORA_G_START:]).astype(BF16)
        r = shifted(br_ref, mur_ref[...])
        k = shifted(bk_ref, muk_ref[...])
        v = shifted(bv_ref, muv_ref[...])

        zlog = w0_ref[...] + _dot(td, wl_ref[...])
        logw = -jnp.exp(-_softplus(-zlog) - 0.5)
        a_sig = _sigmoid(a0_ref[...] + _dot(da, al_ref[...]))
        kk = k * kk_ref[...]
        nrm = jnp.sqrt(seg_sum(kk * kk))
        kk = kk / jnp.maximum(nrm, 1e-12)
        k_h = k * (1.0 + (a_sig - 1.0) * ka_ref[...])
        if Lv < L:
            logw = jnp.where(rowv, logw, 0.0)
            kk = jnp.where(rowv, kk, 0.0)
            k_h = jnp.where(rowv, k_h, 0.0)
            v = jnp.where(rowv, v, 0.0)

        hi, mid, lo = _split3(logw)
        cum = _dot(tri, hi) + _dot(tri, mid) + _dot(tri, lo)
        cum_l = cum[L - 1:L, :]
        p_inv = jnp.exp(-cum)
        p_end = jnp.exp(cum_l - cum)
        bvec = kk * a_sig
        a_t = (-kk * jnp.exp(cum - logw)).astype(BF16)
        r_t = (r * jnp.exp(cum)).astype(BF16)
        b_t = (bvec * p_inv).astype(BF16)
        k_t = (k_h * p_inv).astype(BF16)
        b_e = (bvec * p_end).astype(BF16)
        k_e = (k_h * p_end).astype(BF16)
        v_b = v.astype(BF16)
        p_l = jnp.exp(cum_l)

        P = range(HPS)
        lhs = [jnp.concatenate([a_t[:, s], r_t[:, s]], axis=0) for s in sls]
        rhs = [jnp.concatenate([stack_heads(b_t[:, s], head1), stack_heads(k_t[:, s], head1)], axis=0) for s in sls]
        sd = [sd_ref[bi, i] for i in P]
        aa = [_dot_nt(lhs[i], rhs[i]) for i in P]
        sa = [_dot_nt(lhs[i], sd[i].astype(BF16)) for i in P]
        v_st = [stack_heads(v_b[:, s], head1) for s in sls]
        a_ak = [jnp.where(strict, aa[i][0:L, 2 * L:4 * L], 0.0).astype(BF16) for i in P]
        x = [sa[i][0:L] + _dot(a_ak[i], v_st[i]) for i in P]
        ap = [jnp.where(strict, aa[i][0:L, 0:2 * L], 0.0).astype(BF16) for i in P]
        for d in range(n_dbl):
            x = [x[i] + _dot(ap[i], stack_heads(x[i].astype(BF16), head1)) for i in P]
            if d + 1 < n_dbl:
                ap = [_dot(ap[i], stack_heads(ap[i], colh1)).astype(BF16) for i in P]
        u_b = [x[i].astype(BF16) for i in P]
        a_rb = [jnp.where(incl, aa[i][L:2 * L, 0:2 * L], 0.0).astype(BF16) for i in P]
        a_rk = [jnp.where(incl, aa[i][L:2 * L, 2 * L:4 * L], 0.0).astype(BF16) for i in P]
        y = [sa[i][L:2 * L] + _dot(a_rb[i], stack_heads(u_b[i], head1)) + _dot(a_rk[i], v_st[i]) for i in P]
        ds = [_dot_tn(jnp.concatenate([u_b[i], v_b[:, sls[i]]], axis=0),
                      jnp.concatenate([b_e[:, sls[i]], k_e[:, sls[i]]], axis=0)) for i in P]
        for i in P:
            sd_ref[bi, i] = sd[i] * p_l[:, sls[i]] + jnp.where(same_head, ds[i], 0.0)

        y = jnp.concatenate(y, axis=1)
        mean = seg_sum(y) * (1.0 / H)
        dlt = y - mean
        var = seg_sum(dlt * dlt) * (1.0 / H)
        yn = dlt * lax.rsqrt(var + GN_EPS) * lnw_ref[...] + lnb_ref[...]
        bonus = seg_sum(r * k_h * rk_ref[...]) * v
        g = _dot(sg, gl_ref[...])
        out = (yn + bonus) * g
        y_ref[bi] = out[0:Lv].astype(y_ref.dtype)

        br_ref[bi, 7:8, :] = br_ref[bi, 7 + Lv:8 + Lv, :]
        bk_ref[bi, 7:8, :] = bk_ref[bi, 7 + Lv:8 + Lv, :]
        bv_ref[bi, 7:8, :] = bv_ref[bi, 7 + Lv:8 + Lv, :]
        bl_ref[bi, 7:8, :] = bl_ref[bi, 7 + Lv:8 + Lv, :]

        if NC == 1:
            finish(bi)
        else:
            pl.when(c == NC - 1)(lambda: finish(bi))

    if BB == 1:
        one_batch(0)
    else:
        def loop_body(bi, carry):
            one_batch(bi)
            return carry
        lax.fori_loop(0, BB, loop_body, 0)


def _rwkv_mix_unpipelined(proj3, shift3, s0, p, *, L, Lv, HPS, BB, y_dtype):
    b, t, _ = proj3.shape
    assert t % Lv == 0 and b % BB == 0
    nc = t // Lv
    w = HPS * LANES
    nhg = D_RWKV // w
    kb = D_RWKV // w
    lora_blk = 3 * D_RWKV // LORA_W
    n_dbl = max(1, math.ceil(math.log2(Lv)))
    col = lambda o: (lambda bi, g, c: (bi, c, o + g))
    sh = lambda o: (lambda bi, g, c: (bi, 0, o + g))
    par = lambda o: (lambda bi, g, c: (0, o + g))
    in_specs = [
        pl.BlockSpec((BB, Lv, w), col(0)), pl.BlockSpec((BB, Lv, w), col(kb)), pl.BlockSpec((BB, Lv, w), col(2 * kb)),
        pl.BlockSpec((BB, Lv, LORA_W), lambda bi, g, c: (bi, c, lora_blk)),
        pl.BlockSpec((BB, 1, w), sh(0)), pl.BlockSpec((BB, 1, w), sh(kb)), pl.BlockSpec((BB, 1, w), sh(2 * kb)),
        pl.BlockSpec((BB, 1, LORA_W), lambda bi, g, c: (bi, 0, lora_blk)),
        pl.BlockSpec((BB, 2 * HPS, HEAD_DIM, HEAD_DIM), lambda bi, g, c: (bi, g, 0, 0)),
        pl.BlockSpec((1, w), par(0)), pl.BlockSpec((1, w), par(kb)), pl.BlockSpec((1, w), par(2 * kb)),
        pl.BlockSpec((1, LORA_W), lambda bi, g, c: (0, lora_blk)),
    ] + [pl.BlockSpec((1, w), par(0))] * 7 + [
        pl.BlockSpec((LANES, w), par(0)), pl.BlockSpec((LORA_A_WIN, w), par(0)),
        pl.BlockSpec((LORA_W - LORA_G_START, w), par(0)),
    ]
    out_specs = [pl.BlockSpec((BB, Lv, w), col(0)),
                 pl.BlockSpec((BB, 2 * HPS, HEAD_DIM, HEAD_DIM), lambda bi, g, c: (bi, g, 0, 0))]
    kern = functools.partial(_rwkv_kernel, L=L, Lv=Lv, HPS=HPS, BB=BB, NC=nc, n_dbl=n_dbl)
    return pl.pallas_call(
        kern,
        grid=(b // BB, nhg, nc),
        in_specs=in_specs,
        out_specs=out_specs,
        out_shape=[jax.ShapeDtypeStruct((b, t, D_RWKV), y_dtype),
                   jax.ShapeDtypeStruct((b, N_HEADS, HEAD_DIM, HEAD_DIM), F32)],
        scratch_shapes=[pltpu.VMEM((BB, HPS, LANES, LANES), F32),
                        pltpu.VMEM((BB, L + 8, w), F32), pltpu.VMEM((BB, L + 8, w), F32),
                        pltpu.VMEM((BB, L + 8, w), F32), pltpu.VMEM((BB, L + 8, LORA_W), F32)],
        compiler_params=_cparams(("parallel", "parallel", "arbitrary")),
        name="rwkv_mix",
    )(proj3, proj3, proj3, proj3, shift3, shift3, shift3, shift3, s0,
      p["mu"], p["mu"], p["mu"], p["mu"], p["w0"], p["a0"], p["k_k"], p["k_a"], p["r_k"],
      p["ln_w"], p["ln_b"], p["w_lora"], p["a_lora"], p["g_lora"])


def _rwkv_kernel(pr_ref, pk_ref, pv_ref, plo_ref, shr_ref, shk_ref, shv_ref, shl_ref, s0_ref,
                 mur_ref, muk_ref, muv_ref, mul_ref, w0_ref, a0_ref, kk_ref, ka_ref, rk_ref, lnw_ref, lnb_ref,
                 wl_ref, al_ref, gl_ref,
                 y_ref, so_ref,
                 sd_ref, br_ref, bk_ref, bv_ref, bl_ref, *stage_refs, L, Lv, HPS, BB, NC, n_dbl, pipelined):
    c = pl.program_id(2)
    H = HEAD_DIM
    W = HPS * LANES
    sls = [slice(i * LANES, (i + 1) * LANES) for i in range(HPS)]
    P = range(HPS)
    STAGED = ("a_t", "r_t", "b_t", "k_t", "b_e", "k_e", "v_b", "g", "bg", "p_l")

    lane = lax.broadcasted_iota(jnp.int32, (L, LANES), 1)
    head1 = lane < H
    row_i = lax.broadcasted_iota(jnp.int32, (L, 2 * L), 0)
    col_i = lax.broadcasted_iota(jnp.int32, (L, 2 * L), 1)
    col_t = jnp.where(col_i >= L, col_i - L, col_i)
    strict = col_t < row_i
    incl = col_t <= row_i
    colh1 = col_i < L
    tri = (lax.broadcasted_iota(jnp.int32, (L, L), 1) <= lax.broadcasted_iota(jnp.int32, (L, L), 0)).astype(BF16)
    ji = lax.broadcasted_iota(jnp.int32, (LANES, LANES), 0)
    jj = lax.broadcasted_iota(jnp.int32, (LANES, LANES), 1)
    same_head = (ji < H) == (jj < H)
    seg = same_head.astype(BF16)
    rowv = lax.broadcasted_iota(jnp.int32, (L, W), 0) < Lv

    def seg_sum(x):
        n = x.shape[1] // LANES
        xs = jnp.concatenate([x[:, s] for s in sls[:n]], axis=0)
        ys = _dot(xs.astype(BF16), seg)
        return jnp.concatenate([ys[i * L:(i + 1) * L] for i in range(n)], axis=1)

    def stack_heads(x, m):
        zero = jnp.zeros_like(x)
        return jnp.concatenate([jnp.where(m, x, zero), jnp.where(m, zero, x)], axis=0)

    def init(bi):
        br_ref[bi, 7:8, :] = shr_ref[bi]
        bk_ref[bi, 7:8, :] = shk_ref[bi]
        bv_ref[bi, 7:8, :] = shv_ref[bi]
        bl_ref[bi, 7:8, :] = shl_ref[bi]
        if Lv < L:
            for ref in (br_ref, bk_ref, bv_ref, bl_ref):
                ref[bi, 8 + Lv:8 + L, :] = jnp.zeros((L - Lv, ref.shape[2]), F32)
        z = jnp.zeros((H, H), F32)
        for i in P:
            top = jnp.concatenate([s0_ref[bi, 2 * i], z], axis=1)
            bot = jnp.concatenate([z, s0_ref[bi, 2 * i + 1]], axis=1)
            sd_ref[bi, i] = jnp.concatenate([top, bot], axis=0)

    def finish(bi):
        for i in P:
            sd = sd_ref[bi, i]
            so_ref[bi, 2 * i] = sd[0:H, 0:H]
            so_ref[bi, 2 * i + 1] = sd[H:2 * H, H:2 * H]

    def stage_inputs(bi):
        br_ref[bi, 8:8 + Lv, :] = pr_ref[bi]
        bk_ref[bi, 8:8 + Lv, :] = pk_ref[bi]
        bv_ref[bi, 8:8 + Lv, :] = pv_ref[bi]
        bl_ref[bi, 8:8 + Lv, :] = plo_ref[bi]

    def carry_rows(bi):
        br_ref[bi, 7:8, :] = br_ref[bi, 7 + Lv:8 + Lv, :]
        bk_ref[bi, 7:8, :] = bk_ref[bi, 7 + Lv:8 + Lv, :]
        bv_ref[bi, 7:8, :] = bv_ref[bi, 7 + Lv:8 + Lv, :]
        bl_ref[bi, 7:8, :] = bl_ref[bi, 7 + Lv:8 + Lv, :]

    def lora_inputs(bi):
        p = bl_ref[bi, 8:8 + L, :]
        ps_l = p + mul_ref[...] * (bl_ref[bi, 7:7 + L, :] - p)
        td = (1.0 - 2.0 / (1.0 + jnp.exp(2.0 * ps_l[:, 0:LANES]))).astype(BF16)
        da = ps_l[:, 0:LORA_A_WIN].astype(BF16)
        sg = _sigmoid(ps_l[:, LORA_G_START:]).astype(BF16)
        return td, da, sg

    def prologue(bi, lora, cols):
        td, da, sg = lora

        def shifted(buf, mu_ref):
            p = buf[bi, 8:8 + L, cols]
            return p + mu_ref[:, cols] * (buf[bi, 7:7 + L, cols] - p)

        r = shifted(br_ref, mur_ref)
        k = shifted(bk_ref, muk_ref)
        v = shifted(bv_ref, muv_ref)
        zlog = w0_ref[:, cols] + _dot(td, wl_ref[:, cols])
        logw = -math.exp(-0.5) / (1.0 + jnp.exp(-zlog))
        a_sig = _sigmoid(a0_ref[:, cols] + _dot(da, al_ref[:, cols]))
        kk = k * kk_ref[:, cols]
        kk = kk * jnp.minimum(lax.rsqrt(jnp.maximum(seg_sum(kk * kk), 0.0)), 1e12)
        k_h = k * (1.0 + (a_sig - 1.0) * ka_ref[:, cols])
        if Lv < L:
            valid = rowv[:, 0:r.shape[1]]
            logw = jnp.where(valid, logw, 0.0)
            kk = jnp.where(valid, kk, 0.0)
            k_h = jnp.where(valid, k_h, 0.0)
            v = jnp.where(valid, v, 0.0)

        hi, lo = _split2(logw)
        cum = _dot(tri, hi) + _dot(tri, lo)
        cum_l = cum[L - 1:L, :]
        p_inv = jnp.exp(-cum)
        p_end = jnp.exp(cum_l - cum)
        bvec = kk * a_sig
        g = _dot(sg, gl_ref[:, cols])
        return dict(
            a_t=(-kk * jnp.exp(cum - logw)).astype(BF16), r_t=(r * jnp.exp(cum)).astype(BF16),
            b_t=(bvec * p_inv).astype(BF16), k_t=(k_h * p_inv).astype(BF16),
            b_e=(bvec * p_end).astype(BF16), k_e=(k_h * p_end).astype(BF16), v_b=v.astype(BF16),
            g=g, bg=seg_sum(r * k_h * rk_ref[:, cols]) * v * g, p_l=jnp.exp(cum_l))

    def prologue_items(bi, tiles):
        S = {}

        def load_shift():
            S["lora"] = lora_inputs(bi)
            for name, buf, mu_ref in (("r", br_ref, mur_ref), ("k", bk_ref, muk_ref), ("v", bv_ref, muv_ref)):
                p = buf[bi, 8:8 + L, :]
                S[name] = p + mu_ref[...] * (buf[bi, 7:7 + L, :] - p)
            S["kk"] = S["k"] * kk_ref[...]

        def matmuls_1():
            td, da, sg = S["lora"]
            S["zlog"] = w0_ref[...] + _dot(td, wl_ref[...])
            S["apre"] = a0_ref[...] + _dot(da, al_ref[...])
            S["g"] = _dot(sg, gl_ref[...])
            S["n2"] = seg_sum(S["kk"] * S["kk"])

        def vector_1(i):
            s = sls[i]
            logw = -math.exp(-0.5) / (1.0 + jnp.exp(-S["zlog"][:, s]))
            a_sig = _sigmoid(S["apre"][:, s])
            kk = S["kk"][:, s] * jnp.minimum(lax.rsqrt(jnp.maximum(S["n2"][:, s], 0.0)), 1e12)
            k_h = S["k"][:, s] * (1.0 + (a_sig - 1.0) * ka_ref[:, s])
            hi, lo = _split2(logw)
            S[("v1", i)] = dict(logw=logw, bvec=kk * a_sig, kk=kk, k_h=k_h, hi=hi, lo=lo,
                                prod=(S["r"][:, s] * k_h * rk_ref[:, s]).astype(BF16))

        def matmuls_2():
            cat = lambda n: jnp.concatenate([S[("v1", i)][n] for i in P], axis=1)
            S["cum"] = _dot(tri, cat("hi")) + _dot(tri, cat("lo"))
            xs = jnp.concatenate([S[("v1", i)]["prod"] for i in P], axis=0)
            S["bsum"] = _dot(xs, seg)

        def vector_2(i):
            s = sls[i]
            t1 = S[("v1", i)]
            cum = S["cum"][:, s]
            cum_l = cum[L - 1:L, :]
            p_inv = jnp.exp(-cum)
            p_end = jnp.exp(cum_l - cum)
            v = S["v"][:, s]
            g = S["g"][:, s]
            tiles[i] = dict(
                a_t=(-t1["kk"] * jnp.exp(cum - t1["logw"])).astype(BF16),
                r_t=(S["r"][:, s] * jnp.exp(cum)).astype(BF16),
                b_t=(t1["bvec"] * p_inv).astype(BF16), k_t=(t1["k_h"] * p_inv).astype(BF16),
                b_e=(t1["bvec"] * p_end).astype(BF16), k_e=(t1["k_h"] * p_end).astype(BF16),
                v_b=v.astype(BF16), g=g, bg=S["bsum"][i * L:(i + 1) * L] * v * g, p_l=jnp.exp(cum_l))

        return ([load_shift, matmuls_1] + [functools.partial(vector_1, i) for i in P] + [matmuls_2]
                + [functools.partial(vector_2, i) for i in P])

    def chain_stages(t, sd, lnw, lnb, res):
        a_t, r_t, b_t, k_t, b_e, k_e, v_b = (t[n] for n in STAGED[:7])
        lhs = [jnp.concatenate([a_t[:, s], r_t[:, s]], axis=0) for s in sls]
        rhs = [jnp.concatenate([stack_heads(b_t[:, s], head1), stack_heads(k_t[:, s], head1)], axis=0) for s in sls]
        aa = [_dot_nt(lhs[i], rhs[i]) for i in P]
        yield
        sa = [_dot_nt(lhs[i], sd[i].astype(BF16)) for i in P]
        yield
        v_st = [stack_heads(v_b[:, s], head1) for s in sls]
        a_ak = [jnp.where(strict, aa[i][0:L, 2 * L:4 * L], 0.0).astype(BF16) for i in P]
        x = [sa[i][0:L] + _dot(a_ak[i], v_st[i]) for i in P]
        ap = [jnp.where(strict, aa[i][0:L, 0:2 * L], 0.0).astype(BF16) for i in P]
        yield
        for d in range(n_dbl):
            x = [x[i] + _dot(ap[i], stack_heads(x[i].astype(BF16), head1)) for i in P]
            yield
            if d + 1 < n_dbl:
                ap = [_dot(ap[i], stack_heads(ap[i], colh1)).astype(BF16) for i in P]
                yield
        u_b = [x[i].astype(BF16) for i in P]
        a_rb = [jnp.where(incl, aa[i][L:2 * L, 0:2 * L], 0.0).astype(BF16) for i in P]
        a_rk = [jnp.where(incl, aa[i][L:2 * L, 2 * L:4 * L], 0.0).astype(BF16) for i in P]
        y = [sa[i][L:2 * L] + _dot(a_rb[i], stack_heads(u_b[i], head1)) + _dot(a_rk[i], v_st[i]) for i in P]
        yield
        ds = [_dot_tn(jnp.concatenate([u_b[i], v_b[:, sls[i]]], axis=0),
                      jnp.concatenate([b_e[:, sls[i]], k_e[:, sls[i]]], axis=0)) for i in P]
        res["sd"] = [sd[i] * t["p_l"][:, sls[i]] + jnp.where(same_head, ds[i], 0.0) for i in P]
        yield
        y = jnp.concatenate(y, axis=1)
        mean = seg_sum(y) * (1.0 / H)
        yield
        dlt = y - mean
        var = seg_sum(dlt * dlt) * (1.0 / H)
        yield
        yn = dlt * lax.rsqrt(var + GN_EPS) * lnw + lnb
        res["out"] = yn * t["g"] + t["bg"]

    def one_batch(bi):
        if NC == 1:
            init(bi)
        else:
            pl.when(c == 0)(lambda: init(bi))
        stage_inputs(bi)
        t = prologue(bi, lora_inputs(bi), slice(None))
        res = {}
        for _ in chain_stages(t, [sd_ref[bi, i] for i in P], lnw_ref[...], lnb_ref[...], res):
            pass
        for i in P:
            sd_ref[bi, i] = res["sd"][i]
        y_ref[bi] = res["out"][0:Lv].astype(y_ref.dtype)
        carry_rows(bi)
        if NC == 1:
            finish(bi)
        else:
            pl.when(c == NC - 1)(lambda: finish(bi))

    def one_batch_pipelined():
        st = dict(zip(STAGED, stage_refs))

        @pl.when(c == 0)
        def _():
            init(0)
            for ref in stage_refs:
                ref[...] = jnp.zeros(ref.shape, ref.dtype)

        prev = {n: st[n][...] for n in STAGED}
        sd = [sd_ref[0, i] for i in P]
        lnw, lnb = lnw_ref[...], lnb_ref[...]
        stage_inputs(0)
        res, tiles = {}, [None] * HPS
        items = prologue_items(0, tiles)
        quota = iter([2] + [HPS // 2] * 2 + [1] + [HPS // 8] * 8)
        for _ in chain_stages(prev, sd, lnw, lnb, res):
            for _ in range(next(quota, 0)):
                if items:
                    items.pop(0)()
        while items:
            items.pop(0)()
        for n in STAGED:
            for i in P:
                st[n][:, sls[i]] = tiles[i][n]
        live = c > 0
        for i in P:
            sd_ref[0, i] = jnp.where(live, res["sd"][i], sd[i])
        y_ref[0] = res["out"].astype(y_ref.dtype)
        carry_rows(0)
        pl.when(c == NC)(lambda: finish(0))

    if pipelined:
        one_batch_pipelined()
    elif BB == 1:
        one_batch(0)
    else:
        def loop_body(bi, carry):
            one_batch(bi)
            return carry
        lax.fori_loop(0, BB, loop_body, 0)


def _rwkv_mix(proj3, shift3, s0, p, *, L, Lv, HPS, BB, y_dtype, pipelined):
    b, t, _ = proj3.shape
    assert t % Lv == 0 and b % BB == 0
    assert not pipelined or (BB == 1 and Lv == L)
    nc = t // Lv
    w = HPS * LANES
    nhg = D_RWKV // w
    kb = D_RWKV // w
    lora_blk = 3 * D_RWKV // LORA_W
    n_dbl = max(1, math.ceil(math.log2(Lv)))
    if pipelined:
        cin = lambda c: jnp.minimum(c, nc - 1)
        cout = lambda c: jnp.maximum(c - 1, 0)
    else:
        cin = cout = lambda c: c
    col = lambda o: (lambda bi, g, c: (bi, cin(c), o + g))
    sh = lambda o: (lambda bi, g, c: (bi, 0, o + g))
    par = lambda o: (lambda bi, g, c: (0, o + g))
    in_specs = [
        pl.BlockSpec((BB, Lv, w), col(0)), pl.BlockSpec((BB, Lv, w), col(kb)), pl.BlockSpec((BB, Lv, w), col(2 * kb)),
        pl.BlockSpec((BB, Lv, LORA_W), lambda bi, g, c: (bi, cin(c), lora_blk)),
        pl.BlockSpec((BB, 1, w), sh(0)), pl.BlockSpec((BB, 1, w), sh(kb)), pl.BlockSpec((BB, 1, w), sh(2 * kb)),
        pl.BlockSpec((BB, 1, LORA_W), lambda bi, g, c: (bi, 0, lora_blk)),
        pl.BlockSpec((BB, 2 * HPS, HEAD_DIM, HEAD_DIM), lambda bi, g, c: (bi, g, 0, 0)),
        pl.BlockSpec((1, w), par(0)), pl.BlockSpec((1, w), par(kb)), pl.BlockSpec((1, w), par(2 * kb)),
        pl.BlockSpec((1, LORA_W), lambda bi, g, c: (0, lora_blk)),
    ] + [pl.BlockSpec((1, w), par(0))] * 7 + [
        pl.BlockSpec((LANES, w), par(0)), pl.BlockSpec((LORA_A_WIN, w), par(0)),
        pl.BlockSpec((LORA_W - LORA_G_START, w), par(0)),
    ]
    out_specs = [pl.BlockSpec((BB, Lv, w), lambda bi, g, c: (bi, cout(c), g)),
                 pl.BlockSpec((BB, 2 * HPS, HEAD_DIM, HEAD_DIM), lambda bi, g, c: (bi, g, 0, 0))]
    scratch = [pltpu.VMEM((BB, HPS, LANES, LANES), F32),
               pltpu.VMEM((BB, L + 8, w), F32), pltpu.VMEM((BB, L + 8, w), F32),
               pltpu.VMEM((BB, L + 8, w), F32), pltpu.VMEM((BB, L + 8, LORA_W), F32)]
    if pipelined:
        scratch += [pltpu.VMEM((L, w), BF16)] * 7 + [pltpu.VMEM((L, w), F32)] * 2 + [pltpu.VMEM((1, w), F32)]
    kern = functools.partial(_rwkv_kernel, L=L, Lv=Lv, HPS=HPS, BB=BB, NC=nc, n_dbl=n_dbl, pipelined=pipelined)
    return pl.pallas_call(
        kern,
        grid=(b // BB, nhg, nc + 1 if pipelined else nc),
        in_specs=in_specs,
        out_specs=out_specs,
        out_shape=[jax.ShapeDtypeStruct((b, t, D_RWKV), y_dtype),
                   jax.ShapeDtypeStruct((b, N_HEADS, HEAD_DIM, HEAD_DIM), F32)],
        scratch_shapes=scratch,
        compiler_params=_cparams(("parallel", "parallel", "arbitrary")),
        name="rwkv_mix",
    )(proj3, proj3, proj3, proj3, shift3, shift3, shift3, shift3, s0,
      p["mu"], p["mu"], p["mu"], p["mu"], p["w0"], p["a0"], p["k_k"], p["k_a"], p["r_k"],
      p["ln_w"], p["ln_b"], p["w_lora"], p["a_lora"], p["g_lora"])


def _swa_prompt_kernel(slope_ref, sink_ref, q_ref, kc_ref, kp_ref, vc_ref, vp_ref, o_ref, bias_ref):
    n = pl.program_id(1)
    blk = WINDOW
    H = HEAD_DIM

    @pl.when((pl.program_id(0) == 0) & (n == 0))
    def _():
        t = lax.broadcasted_iota(jnp.int32, (blk, 2 * blk), 0)
        j = lax.broadcasted_iota(jnp.int32, (blk, 2 * blk), 1)
        dist = t - j + blk
        band = (dist >= 0) & (dist <= WINDOW)
        first = band & (j >= blk)
        distf = dist.astype(F32)
        for h in range(N_HEADS):
            ab = -slope_ref[h] * distf
            bias_ref[0, h] = jnp.where(first, ab, -jnp.inf)
            bias_ref[1, h] = jnp.where(band, ab, -jnp.inf)

    sel = jnp.where(n == 0, 0, 1)
    low = lax.broadcasted_iota(jnp.int32, (blk, LANES), 1) < H
    scale = H ** -0.5
    tile = lambda i: slice(i * LANES, (i + 1) * LANES)

    def kv_group(KV):
        kslab = {hk: jnp.concatenate([kp_ref[0, :, tile(hk // 2)], kc_ref[0, :, tile(hk // 2)]],
                                     axis=0).astype(BF16) for hk in KV}
        vslab = {hk: jnp.concatenate([vp_ref[0, :, tile(hk // 2)], vc_ref[0, :, tile(hk // 2)]],
                                     axis=0).astype(BF16) for hk in KV}
        lhs = {}
        for hk in KV:
            parts = []
            for s2 in range(2):
                xs = q_ref[0, :, tile(2 * hk + s2)].astype(F32) * scale
                xr = pltpu.roll(xs, H, axis=1)
                if hk % 2 == 0:
                    parts += [jnp.where(low, xs, 0.0), jnp.where(low, xr, 0.0)]
                else:
                    parts += [jnp.where(low, 0.0, xr), jnp.where(low, 0.0, xs)]
            lhs[hk] = jnp.concatenate(parts, axis=0).astype(BF16)
        s = {hk: _dot_nt(lhs[hk], kslab[hk]) for hk in KV}
        p, rden = {}, {}
        for hk in KV:
            ps, rs = [], []
            for g in range(GQA_GROUP):
                h = hk * GQA_GROUP + g
                sg = s[hk][g * blk:(g + 1) * blk] + bias_ref[sel, h]
                m = jnp.maximum(jnp.max(sg, axis=-1, keepdims=True), sink_ref[h])
                e = jnp.exp(sg - m)
                rs.append(1.0 / (jnp.sum(e, axis=-1, keepdims=True) + jnp.exp(sink_ref[h] - m)))
                ps.append(e.astype(BF16))
            p[hk] = jnp.concatenate(ps, axis=0)
            rden[hk] = rs
        o = {hk: _dot(p[hk], vslab[hk]) for hk in KV}
        for hk in KV:
            for s2 in range(2):
                ga, gb = 2 * s2, 2 * s2 + 1
                oa = o[hk][ga * blk:(ga + 1) * blk] * rden[hk][ga]
                ob = o[hk][gb * blk:(gb + 1) * blk] * rden[hk][gb]
                if hk % 2 == 0:
                    out = jnp.where(low, oa, pltpu.roll(ob, H, axis=1))
                else:
                    out = jnp.where(low, pltpu.roll(oa, H, axis=1), ob)
                o_ref[0, :, tile(2 * hk + s2)] = out.astype(o_ref.dtype)

    for g0 in range(0, N_KV_HEADS, KV_PER_GROUP):
        kv_group(range(g0, g0 + KV_PER_GROUP))


def _swa_prompt(q3, kv3, slopes, sinks):
    b, t, _ = q3.shape
    nb = t // WINDOW
    smem = pl.BlockSpec(memory_space=pltpu.SMEM)
    prev = lambda n: jnp.maximum(n - 1, 0)
    return pl.pallas_call(
        _swa_prompt_kernel,
        grid=(b, nb),
        in_specs=[smem, smem,
                  pl.BlockSpec((1, WINDOW, D_MODEL), lambda bi, n: (bi, n, 0)),
                  pl.BlockSpec((1, WINDOW, D_KV), lambda bi, n: (bi, n, 0)),
                  pl.BlockSpec((1, WINDOW, D_KV), lambda bi, n: (bi, prev(n), 0)),
                  pl.BlockSpec((1, WINDOW, D_KV), lambda bi, n: (bi, n, 1)),
                  pl.BlockSpec((1, WINDOW, D_KV), lambda bi, n: (bi, prev(n), 1))],
        out_specs=pl.BlockSpec((1, WINDOW, D_MODEL), lambda bi, n: (bi, n, 0)),
        out_shape=jax.ShapeDtypeStruct((b, t, D_MODEL), BF16),
        scratch_shapes=[pltpu.VMEM((2, N_HEADS, WINDOW, 2 * WINDOW), F32)],
        compiler_params=_cparams(("arbitrary", "arbitrary")),
        name="swa_prompt",
    )(slopes, sinks, q3, kv3, kv3, kv3, kv3)


def _swa_sample_kernel(slope_ref, sink_ref, q_ref, kc_ref, vc_ref, knew_ref, vnew_ref,
                       o_ref, kwin_ref, vwin_ref, nbuf_ref, *, tq, BB, UNR):
    GT = GQA_GROUP * tq
    R = N_KV_HEADS * GT
    NP = 16
    C = D_KV
    row = lax.broadcasted_iota(jnp.int32, (R, WINDOW), 0)
    wcol = lax.broadcasted_iota(jnp.int32, (R, WINDOW), 1)
    t = lax.rem(row, tq)
    slope = slope_ref[...]
    sink = sink_ref[...]
    dist_o = WINDOW + t - wcol
    bias_old = jnp.where(dist_o <= WINDOW, -slope * dist_o.astype(F32), -jnp.inf)
    s_idx = wcol - (WINDOW - tq)
    dist_n = t - s_idx
    bias_new = jnp.where((s_idx >= 0) & (dist_n >= 0), -slope * dist_n.astype(F32), -jnp.inf)
    hkmask = (lax.broadcasted_iota(jnp.int32, (R, C), 0) // GT) == (lax.broadcasted_iota(jnp.int32, (R, C), 1) // HEAD_DIM)
    srow = lax.broadcasted_iota(jnp.int32, (NP, WINDOW), 0)
    scol = lax.broadcasted_iota(jnp.int32, (NP, WINDOW), 1)
    selw = ((scol == srow + (WINDOW - tq)) & (srow < tq)).astype(BF16)
    lane_new = lax.broadcasted_iota(jnp.int32, (C, WINDOW), 1) >= WINDOW - tq
    scale = HEAD_DIM ** -0.5
    for u in range(UNR):
        nbuf_ref[u, :, tq:NP, :] = jnp.zeros((2, NP - tq, C), F32)

    def transposed_new(x):
        return sum(_dot_tn(part, selw) for part in _split3(x))

    def body(i, carry):
        bs = [i * UNR + u for u in range(UNR)]
        U = range(UNR)
        for u in U:
            nbuf_ref[u, 0, 0:tq, :] = knew_ref[bs[u]]
            nbuf_ref[u, 1, 0:tq, :] = vnew_ref[bs[u]]
        kt = [kc_ref[b] for b in bs]
        vt = [vc_ref[b] for b in bs]
        knt = [transposed_new(nbuf_ref[u, 0]) for u in U]
        vnt = [transposed_new(nbuf_ref[u, 1]) for u in U]
        qbd = [jnp.where(hkmask, jnp.concatenate([q_ref[b] * scale] * N_KV_HEADS, axis=0), 0.0).astype(BF16)
               for b in bs]
        s_o = [_dot(qbd[u], kt[u].astype(BF16)) + bias_old for u in U]
        s_n = [_dot(qbd[u], knt[u].astype(BF16)) + bias_new for u in U]
        outs = []
        for u in U:
            m = jnp.maximum(jnp.maximum(jnp.max(s_o[u], axis=-1, keepdims=True),
                                        jnp.max(s_n[u], axis=-1, keepdims=True)), sink)
            p_o = jnp.exp(s_o[u] - m)
            p_n = jnp.exp(s_n[u] - m)
            rden = 1.0 / (jnp.sum(p_o, axis=-1, keepdims=True) + jnp.sum(p_n, axis=-1, keepdims=True)
                          + jnp.exp(sink - m))
            o = _dot_nt(p_o.astype(BF16), vt[u].astype(BF16)) + _dot_nt(p_n.astype(BF16), vnt[u].astype(BF16))
            o = jnp.where(hkmask, o * rden, 0.0)
            acc = o[0:GT]
            for hk in range(1, N_KV_HEADS):
                acc = acc + o[hk * GT:(hk + 1) * GT]
            outs.append(acc)
        for u in U:
            kwin_ref[bs[u]] = jnp.where(lane_new, knt[u], pltpu.roll(kt[u], WINDOW - tq, axis=1))
            vwin_ref[bs[u]] = jnp.where(lane_new, vnt[u], pltpu.roll(vt[u], WINDOW - tq, axis=1))
            o_ref[bs[u]] = outs[u]
        return carry

    lax.fori_loop(0, BB // UNR, body, 0)


def _swa_sample(q16, knew3, vnew3, kct, vct, slope_rows, sink_rows):
    b, gt, c = q16.shape
    tq = gt // GQA_GROUP
    bb = 8 if b % 8 == 0 else 1
    unr = 2 if bb % 2 == 0 else 1
    rows = N_KV_HEADS * gt
    blk3 = lambda shape: pl.BlockSpec(shape, lambda i: (i, 0, 0))
    full2 = pl.BlockSpec((rows, 1), lambda i: (0, 0))
    kern = functools.partial(_swa_sample_kernel, tq=tq, BB=bb, UNR=unr)
    return pl.pallas_call(
        kern,
        grid=(b // bb,),
        in_specs=[full2, full2, blk3((bb, gt, c)), blk3((bb, c, WINDOW)), blk3((bb, c, WINDOW)),
                  blk3((bb, tq, c)), blk3((bb, tq, c))],
        out_specs=[blk3((bb, gt, c)), blk3((bb, c, WINDOW)), blk3((bb, c, WINDOW))],
        out_shape=[jax.ShapeDtypeStruct((b, gt, c), F32),
                   jax.ShapeDtypeStruct((b, c, WINDOW), F32),
                   jax.ShapeDtypeStruct((b, c, WINDOW), F32)],
        scratch_shapes=[pltpu.VMEM((unr, 2, 16, c), F32)],
        compiler_params=_cparams(("parallel",)),
        name="swa_sample",
    )(slope_rows, sink_rows, q16, kct, vct, knew3, vnew3)


def _merge_out_kernel(x_ref, ya_ref, yb_ref, ga_ref, gb_ref, wo_ref, nw_ref, h_ref, hn_ref):
    f32 = lambda ref: ref[...].astype(F32)
    mixed = _sigmoid(f32(ga_ref)) * f32(ya_ref) + _sigmoid(f32(gb_ref)) * f32(yb_ref)
    h = x_ref[...] + _dot(mixed.astype(BF16), wo_ref[...])
    h_ref[...] = h
    ms = jnp.mean(h * h, axis=-1, keepdims=True)
    hn_ref[...] = (h * lax.rsqrt(ms + RMS_EPS) * nw_ref[...]).astype(BF16)


def _merge_out(x, ya, yb, gates, w_out, nw, tm):
    m, d = x.shape
    row = lambda o: (lambda i: (i, o))
    return pl.pallas_call(
        _merge_out_kernel,
        grid=(m // tm,),
        in_specs=[pl.BlockSpec((tm, d), row(0)), pl.BlockSpec((tm, d), row(0)), pl.BlockSpec((tm, d), row(0)),
                  pl.BlockSpec((tm, d), row(0)), pl.BlockSpec((tm, d), row(1)),
                  pl.BlockSpec((d, d), lambda i: (0, 0)), pl.BlockSpec((1, d), lambda i: (0, 0))],
        out_specs=[pl.BlockSpec((tm, d), row(0)), pl.BlockSpec((tm, d), row(0))],
        out_shape=[jax.ShapeDtypeStruct((m, d), F32), jax.ShapeDtypeStruct((m, d), BF16)],
        compiler_params=_cparams(("parallel",)),
        name="merge_out_proj",
    )(x, ya, yb, gates, gates, w_out, nw)


def _mlp_kernel(hn_ref, h_ref, wu_ref, wd_ref, nw_ref, o_ref, acc_ref):
    j = pl.program_id(1)

    @pl.when(j == 0)
    def _():
        acc_ref[...] = jnp.zeros_like(acc_ref)

    u = jnp.maximum(_dot(hn_ref[...], wu_ref[...]), 0.0)
    acc_ref[...] += _dot((u * u).astype(BF16), wd_ref[...])

    @pl.when(j == pl.num_programs(1) - 1)
    def _():
        h = h_ref[...] + acc_ref[...]
        ms = jnp.mean(h * h, axis=-1, keepdims=True)
        o_ref[...] = h * lax.rsqrt(ms + RMS_EPS) * nw_ref[...]


def _mlp(hn, h, w_up, w_down, nw, tm, tf):
    m, d = h.shape
    f = w_up.shape[1]
    return pl.pallas_call(
        _mlp_kernel,
        grid=(m // tm, f // tf),
        in_specs=[pl.BlockSpec((tm, d), lambda i, j: (i, 0)), pl.BlockSpec((tm, d), lambda i, j: (i, 0)),
                  pl.BlockSpec((d, tf), lambda i, j: (0, j)), pl.BlockSpec((tf, d), lambda i, j: (j, 0)),
                  pl.BlockSpec((1, d), lambda i, j: (0, 0))],
        out_specs=pl.BlockSpec((tm, d), lambda i, j: (i, 0)),
        out_shape=jax.ShapeDtypeStruct((m, d), F32),
        scratch_shapes=[pltpu.VMEM((tm, d), F32)],
        compiler_params=_cparams(("parallel", "arbitrary")),
        name="mlp_final_norm",
    )(hn, h, w_up, w_down, nw)


def _pick(m, prefs):
    for t in prefs:
        if m % t == 0:
            return t
    return m


def _pad_cols(v, n):
    return jnp.concatenate([v, jnp.zeros(v.shape[:-1] + (n - v.shape[-1],), v.dtype)], axis=-1)


def _place_rows(w, start, rows):
    n, d = w.shape
    return jnp.concatenate([jnp.zeros((start, d), w.dtype), w, jnp.zeros((rows - start - n, d), w.dtype)], axis=0)


def _layer(x, shift_prev, wkv0, lw, *, prompt, k_cache_t=None, v_cache_t=None):
    b, t, d = x.shape
    m = b * t
    x2 = x.reshape(m, d)
    proj_r, proj_q, proj_kv, proj_g = _in_proj(
        x2, lw["norm_mix_w"], [lw["w_r"], lw["w_q"], lw["w_kv"], lw["w_g"]], [F32, BF16, F32, BF16],
        _pick(m, (1024, 512, 256, 128, 8)))
    proj_r3 = proj_r.reshape(b, t, R_PAD)
    kv3 = proj_kv.reshape(b, t, 2 * D_KV)
    shift3 = _pad_cols(shift_prev, R_PAD)[:, None]

    if prompt:
        ya, wkv_new = _rwkv_mix(proj_r3, shift3, wkv0, lw, L=64, Lv=64, HPS=16, BB=1, y_dtype=BF16,
                                pipelined=False)
        yb = _swa_prompt(proj_q.reshape(b, t, D_MODEL), kv3, lw["slopes"], lw["sinks"])
        k_win = kv3[:, t - WINDOW:, :D_KV].reshape(b, WINDOW, N_KV_HEADS, HEAD_DIM)
        v_win = kv3[:, t - WINDOW:, D_KV:].reshape(b, WINDOW, N_KV_HEADS, HEAD_DIM)
    else:
        ya, wkv_new = _rwkv_mix(proj_r3, shift3, wkv0, lw, L=16, Lv=t, HPS=16, BB=_pick(b, (4, 2, 1)), y_dtype=F32,
                                pipelined=False)
        q16 = proj_q.reshape(b, t, N_KV_HEADS, GQA_GROUP, HEAD_DIM).transpose(0, 3, 1, 2, 4)
        q16 = q16.reshape(b, GQA_GROUP * t, D_KV)
        gt_head = (jnp.arange(N_KV_HEADS)[:, None] * GQA_GROUP + jnp.arange(GQA_GROUP)[None, :])
        row_head = jnp.repeat(gt_head, t, axis=1).reshape(-1)
        o16, kwt, vwt = _swa_sample(q16, kv3[:, :, :D_KV], kv3[:, :, D_KV:], k_cache_t, v_cache_t,
                                    lw["slopes"][row_head][:, None], lw["sinks"][row_head][:, None])
        yb = o16.reshape(b, GQA_GROUP, t, N_KV_HEADS, HEAD_DIM).transpose(0, 2, 3, 1, 4).reshape(b, t, D_MODEL)
        k_win = kwt.reshape(b, N_KV_HEADS, HEAD_DIM, WINDOW).transpose(0, 3, 1, 2)
        v_win = vwt.reshape(b, N_KV_HEADS, HEAD_DIM, WINDOW).transpose(0, 3, 1, 2)

    h, hn = _merge_out(x2, ya.reshape(m, d), yb.reshape(m, d), proj_g, lw["w_out"], lw["norm_mlp_w"],
                       _pick(m, (256, 128, 8)))
    y = _mlp(hn, h, lw["w_up"], lw["w_down"], lw["norm_final_w"], _pick(m, (512, 256, 128, 8)), 1024)
    shift_new = proj_r3[:, t - 1, :R_COLS]
    return y.reshape(b, t, d), shift_new, wkv_new, k_win, v_win


def kernel(x_prompt, x_sample, state_shift, state_wkv, cache_k_win, cache_v_win, norm_mix_w, w_in, tshift_mu, w0, w_lora, a0, a_lora, g_lora, k_k, k_a, r_k, ln_x_w, ln_x_b, attn_sinks, w_out, norm_mlp_w, w_up, w_down, norm_final_w):
    depth = w_in.shape[0]
    assert depth == 1
    l = 0
    bp = x_prompt.shape[0]
    db = x_sample.shape[0]
    hh = jnp.arange(N_HEADS, dtype=F32)
    w_t = jnp.swapaxes(w_in[l], 0, 1)
    o_q, o_kv, o_g = R_COLS, R_COLS + D_MODEL, R_COLS + D_MODEL + 2 * D_KV
    lw = dict(
        norm_mix_w=norm_mix_w[l][None],
        w_r=_place_rows(w_t[:R_COLS], 0, R_PAD).astype(BF16), w_q=w_t[o_q:o_kv].astype(BF16),
        w_kv=w_t[o_kv:o_g].astype(BF16), w_g=w_t[o_g:].astype(BF16),
        mu=_pad_cols(tshift_mu[l][None], R_PAD), w0=w0[l][None], a0=a0[l][None], k_k=k_k[l][None],
        k_a=k_a[l][None], r_k=r_k[l].reshape(1, D_RWKV), ln_w=ln_x_w[l][None], ln_b=ln_x_b[l][None],
        w_lora=_place_rows(w_lora[l], 0, LANES).astype(BF16),
        a_lora=_place_rows(a_lora[l], LORA_DECAY_END, LORA_A_WIN).astype(BF16),
        g_lora=_place_rows(g_lora[l], LORA_A_END - LORA_G_START, LORA_W - LORA_G_START).astype(BF16),
        slopes=jnp.exp2(-8.0 * (hh + 1.0) / N_HEADS), sinks=attn_sinks[l].astype(F32),
        w_out=w_out[l].astype(BF16), norm_mlp_w=norm_mlp_w[l][None],
        w_up=w_up[l].astype(BF16), w_down=w_down[l].astype(BF16), norm_final_w=norm_final_w[None],
    )
    yp, sp, wp, kp, vp = _layer(
        x_prompt, jnp.zeros((bp, R_COLS), F32), jnp.zeros((bp, N_HEADS, HEAD_DIM, HEAD_DIM), F32), lw, prompt=True)
    kct = cache_k_win[l].transpose(0, 2, 3, 1).reshape(db, D_KV, WINDOW)
    vct = cache_v_win[l].transpose(0, 2, 3, 1).reshape(db, D_KV, WINDOW)
    ys, ss, ws, ksm, vsm = _layer(x_sample, state_shift[l], state_wkv[l], lw, prompt=False,
                                  k_cache_t=kct, v_cache_t=vct)
    return (yp, ys, sp[None], wp[None], kp[None], vp[None], ss[None], ws[None], ksm[None], vsm[None])
```

```python
import functools
import math

import jax
import jax.numpy as jnp
from jax import lax
from jax.experimental import pallas as pl
from jax.experimental.pallas import tpu as pltpu

F32 = jnp.float32
BF16 = jnp.bfloat16

D_MODEL = 2048
HEAD_DIM = 64
N_HEADS = D_MODEL // HEAD_DIM
N_KV_HEADS = 8
GQA_GROUP = N_HEADS // N_KV_HEADS
D_KV = N_KV_HEADS * HEAD_DIM
WINDOW = 128
D_FF = 4 * D_MODEL
D_DECAY_LORA = 96
D_A_LORA = 96
D_GATE_LORA = 256
D_RWKV = D_MODEL
R_COLS = 3 * D_RWKV + D_DECAY_LORA + D_A_LORA + D_GATE_LORA
C_IN = R_COLS + D_MODEL + 2 * D_KV + 2 * D_MODEL
RMS_EPS = 1e-5
GN_EPS = 64e-5

LANES = 128
PROJ_TILE = 512
R_PAD = -(-R_COLS // PROJ_TILE) * PROJ_TILE
LORA_W = R_PAD - 3 * D_RWKV
LORA_DECAY_END = D_DECAY_LORA
LORA_A_END = D_DECAY_LORA + D_A_LORA
LORA_G_END = LORA_A_END + D_GATE_LORA
LORA_A_WIN = -(-LORA_A_END // LANES) * LANES
LORA_G_START = (LORA_A_END // LANES) * LANES

VMEM_LIMIT = 56 * 1024 * 1024
KV_PER_GROUP = 2


def _cparams(sem):
    return pltpu.CompilerParams(dimension_semantics=sem, vmem_limit_bytes=VMEM_LIMIT)


def _dot(a, b):
    return jnp.dot(a, b, preferred_element_type=F32)


def _dot_nt(a, b):
    return lax.dot_general(a, b, (((1,), (1,)), ((), ())), preferred_element_type=F32)


def _dot_tn(a, b):
    return lax.dot_general(a, b, (((0,), (0,)), ((), ())), preferred_element_type=F32)


def _softplus(x):
    return jnp.maximum(x, 0.0) + jnp.log(1.0 + jnp.exp(-jnp.abs(x)))


def _sigmoid(x):
    return 1.0 / (1.0 + jnp.exp(-x))


def _split2(x):
    hi = x.astype(BF16)
    lo = (x - hi.astype(F32)).astype(BF16)
    return hi, lo


def _split3(x):
    hi = x.astype(BF16)
    r1 = x - hi.astype(F32)
    mid = r1.astype(BF16)
    lo = (r1 - mid.astype(F32)).astype(BF16)
    return hi, mid, lo


def _in_proj_kernel(x_ref, nw_ref, *refs, bounds):
    n = len(bounds)
    w_refs, o_refs, xn_ref = refs[:n], refs[n:2 * n], refs[2 * n]
    j = pl.program_id(1)

    @pl.when(j == 0)
    def _():
        x = x_ref[...]
        ms = jnp.mean(x * x, axis=-1, keepdims=True)
        xn_ref[...] = (x * lax.rsqrt(ms + RMS_EPS) * nw_ref[...]).astype(BF16)

    for w_ref, o_ref, (lo, hi) in zip(w_refs, o_refs, bounds):
        @pl.when((j >= lo) & (j < hi))
        def _(w_ref=w_ref, o_ref=o_ref):
            o_ref[...] = _dot_nt(xn_ref[...], w_ref[...]).astype(o_ref.dtype)


def _in_proj(x, nw, weights, out_dtypes, tm):
    m, d = x.shape
    tn = PROJ_TILE
    bounds, lo = [], 0
    for w in weights:
        bounds.append((lo, lo + w.shape[0] // tn))
        lo = bounds[-1][1]

    def clamp(lo_, hi_):
        return lambda j: jnp.clip(j - lo_, 0, hi_ - lo_ - 1)

    in_specs = [pl.BlockSpec((tm, d), lambda i, j: (i, 0)), pl.BlockSpec((1, d), lambda i, j: (0, 0))]
    out_specs, out_shape = [], []
    for w, dt, (lo_, hi_) in zip(weights, out_dtypes, bounds):
        c = clamp(lo_, hi_)
        in_specs.append(pl.BlockSpec((tn, d), lambda i, j, c=c: (c(j), 0)))
        out_specs.append(pl.BlockSpec((tm, tn), lambda i, j, c=c: (i, c(j))))
        out_shape.append(jax.ShapeDtypeStruct((m, w.shape[0]), dt))
    return pl.pallas_call(
        functools.partial(_in_proj_kernel, bounds=tuple(bounds)),
        grid=(m // tm, lo),
        in_specs=in_specs,
        out_specs=out_specs,
        out_shape=out_shape,
        scratch_shapes=[pltpu.VMEM((tm, d), BF16)],
        compiler_params=_cparams(("parallel", "arbitrary")),
        name="norm_in_proj",
    )(x, nw, *weights)


def _rwkv_kernel_unpipelined(pr_ref, pk_ref, pv_ref, plo_ref, shr_ref, shk_ref, shv_ref, shl_ref, s0_ref,
                 mur_ref, muk_ref, muv_ref, mul_ref, w0_ref, a0_ref, kk_ref, ka_ref, rk_ref, lnw_ref, lnb_ref,
                 wl_ref, al_ref, gl_ref,
                 y_ref, so_ref,
                 sd_ref, br_ref, bk_ref, bv_ref, bl_ref, *, L, Lv, HPS, BB, NC, n_dbl):
    c = pl.program_id(2)
    H = HEAD_DIM
    W = HPS * LANES
    sls = [slice(i * LANES, (i + 1) * LANES) for i in range(HPS)]

    lane = lax.broadcasted_iota(jnp.int32, (L, LANES), 1)
    head1 = lane < H
    row_i = lax.broadcasted_iota(jnp.int32, (L, 2 * L), 0)
    col_i = lax.broadcasted_iota(jnp.int32, (L, 2 * L), 1)
    col_t = jnp.where(col_i >= L, col_i - L, col_i)
    strict = col_t < row_i
    incl = col_t <= row_i
    colh1 = col_i < L
    tri = (lax.broadcasted_iota(jnp.int32, (L, L), 1) <= lax.broadcasted_iota(jnp.int32, (L, L), 0)).astype(BF16)
    ji = lax.broadcasted_iota(jnp.int32, (LANES, LANES), 0)
    jj = lax.broadcasted_iota(jnp.int32, (LANES, LANES), 1)
    same_head = (ji < H) == (jj < H)
    seg = same_head.astype(BF16)
    rowv = lax.broadcasted_iota(jnp.int32, (L, W), 0) < Lv

    def seg_sum(x):
        xs = jnp.concatenate([x[:, s] for s in sls], axis=0)
        hi, lo = _split2(xs)
        ys = _dot(hi, seg) + _dot(lo, seg)
        return jnp.concatenate([ys[i * L:(i + 1) * L] for i in range(HPS)], axis=1)

    def stack_heads(x, m):
        zero = jnp.zeros_like(x)
        return jnp.concatenate([jnp.where(m, x, zero), jnp.where(m, zero, x)], axis=0)

    def init(bi):
        br_ref[bi, 7:8, :] = shr_ref[bi]
        bk_ref[bi, 7:8, :] = shk_ref[bi]
        bv_ref[bi, 7:8, :] = shv_ref[bi]
        bl_ref[bi, 7:8, :] = shl_ref[bi]
        if Lv < L:
            for ref in (br_ref, bk_ref, bv_ref, bl_ref):
                ref[bi, 8 + Lv:8 + L, :] = jnp.zeros((L - Lv, ref.shape[2]), F32)
        z = jnp.zeros((H, H), F32)
        for i in range(HPS):
            top = jnp.concatenate([s0_ref[bi, 2 * i], z], axis=1)
            bot = jnp.concatenate([z, s0_ref[bi, 2 * i + 1]], axis=1)
            sd_ref[bi, i] = jnp.concatenate([top, bot], axis=0)

    def finish(bi):
        for i in range(HPS):
            sd = sd_ref[bi, i]
            so_ref[bi, 2 * i] = sd[0:H, 0:H]
            so_ref[bi, 2 * i + 1] = sd[H:2 * H, H:2 * H]

    def one_batch(bi):
        if NC == 1:
            init(bi)
        else:
            pl.when(c == 0)(lambda: init(bi))

        br_ref[bi, 8:8 + Lv, :] = pr_ref[bi]
        bk_ref[bi, 8:8 + Lv, :] = pk_ref[bi]
        bv_ref[bi, 8:8 + Lv, :] = pv_ref[bi]
        bl_ref[bi, 8:8 + Lv, :] = plo_ref[bi]

        def shifted(buf, mu):
            p = buf[bi, 8:8 + L, :]
            return p + mu * (buf[bi, 7:7 + L, :] - p)

        ps_l = shifted(bl_ref, mul_ref[...])
        td = jnp.tanh(ps_l[:, 0:LANES]).astype(BF16)
        da = ps_l[:, 0:LORA_A_WIN].astype(BF16)
        sg = _sigmoid(ps_l[:, LORA_G_START:]).astype(BF16)
        r = shifted(br_ref, mur_ref[...])
        k = shifted(bk_ref, muk_ref[...])
        v = shifted(bv_ref, muv_ref[...])

        zlog = w0_ref[...] + _dot(td, wl_ref[...])
        logw = -jnp.exp(-_softplus(-zlog) - 0.5)
        a_sig = _sigmoid(a0_ref[...] + _dot(da, al_ref[...]))
        kk = k * kk_ref[...]
        nrm = jnp.sqrt(seg_sum(kk * kk))
        kk = kk / jnp.maximum(nrm, 1e-12)
        k_h = k * (1.0 + (a_sig - 1.0) * ka_ref[...])
        if Lv < L:
            logw = jnp.where(rowv, logw, 0.0)
            kk = jnp.where(rowv, kk, 0.0)
            k_h = jnp.where(rowv, k_h, 0.0)
            v = jnp.where(rowv, v, 0.0)

        hi, mid, lo = _split3(logw)
        cum = _dot(tri, hi) + _dot(tri, mid) + _dot(tri, lo)
        cum_l = cum[L - 1:L, :]
        p_inv = jnp.exp(-cum)
        p_end = jnp.exp(cum_l - cum)
        bvec = kk * a_sig
        a_t = (-kk * jnp.exp(cum - logw)).astype(BF16)
        r_t = (r * jnp.exp(cum)).astype(BF16)
        b_t = (bvec * p_inv).astype(BF16)
        k_t = (k_h * p_inv).astype(BF16)
        b_e = (bvec * p_end).astype(BF16)
        k_e = (k_h * p_end).astype(BF16)
        v_b = v.astype(BF16)
        p_l = jnp.exp(cum_l)

        P = range(HPS)
        lhs = [jnp.concatenate([a_t[:, s], r_t[:, s]], axis=0) for s in sls]
        rhs = [jnp.concatenate([stack_heads(b_t[:, s], head1), stack_heads(k_t[:, s], head1)], axis=0) for s in sls]
        sd = [sd_ref[bi, i] for i in P]
        aa = [_dot_nt(lhs[i], rhs[i]) for i in P]
        sa = [_dot_nt(lhs[i], sd[i].astype(BF16)) for i in P]
        v_st = [stack_heads(v_b[:, s], head1) for s in sls]
        a_ak = [jnp.where(strict, aa[i][0:L, 2 * L:4 * L], 0.0).astype(BF16) for i in P]
        x = [sa[i][0:L] + _dot(a_ak[i], v_st[i]) for i in P]
        ap = [jnp.where(strict, aa[i][0:L, 0:2 * L], 0.0).astype(BF16) for i in P]
        for d in range(n_dbl):
            x = [x[i] + _dot(ap[i], stack_heads(x[i].astype(BF16), head1)) for i in P]
            if d + 1 < n_dbl:
                ap = [_dot(ap[i], stack_heads(ap[i], colh1)).astype(BF16) for i in P]
        u_b = [x[i].astype(BF16) for i in P]
        a_rb = [jnp.where(incl, aa[i][L:2 * L, 0:2 * L], 0.0).astype(BF16) for i in P]
        a_rk = [jnp.where(incl, aa[i][L:2 * L, 2 * L:4 * L], 0.0).astype(BF16) for i in P]
        y = [sa[i][L:2 * L] + _dot(a_rb[i], stack_heads(u_b[i], head1)) + _dot(a_rk[i], v_st[i]) for i in P]
        ds = [_dot_tn(jnp.concatenate([u_b[i], v_b[:, sls[i]]], axis=0),
                      jnp.concatenate([b_e[:, sls[i]], k_e[:, sls[i]]], axis=0)) for i in P]
        for i in P:
            sd_ref[bi, i] = sd[i] * p_l[:, sls[i]] + jnp.where(same_head, ds[i], 0.0)

        y = jnp.concatenate(y, axis=1)
        mean = seg_sum(y) * (1.0 / H)
        dlt = y - mean
        var = seg_sum(dlt * dlt) * (1.0 / H)
        yn = dlt * lax.rsqrt(var + GN_EPS) * lnw_ref[...] + lnb_ref[...]
        bonus = seg_sum(r * k_h * rk_ref[...]) * v
        g = _dot(sg, gl_ref[...])
        out = (yn + bonus) * g
        y_ref[bi] = out[0:Lv].astype(y_ref.dtype)

        br_ref[bi, 7:8, :] = br_ref[bi, 7 + Lv:8 + Lv, :]
        bk_ref[bi, 7:8, :] = bk_ref[bi, 7 + Lv:8 + Lv, :]
        bv_ref[bi, 7:8, :] = bv_ref[bi, 7 + Lv:8 + Lv, :]
        bl_ref[bi, 7:8, :] = bl_ref[bi, 7 + Lv:8 + Lv, :]

        if NC == 1:
            finish(bi)
        else:
            pl.when(c == NC - 1)(lambda: finish(bi))

    if BB == 1:
        one_batch(0)
    else:
        def loop_body(bi, carry):
            one_batch(bi)
            return carry
        lax.fori_loop(0, BB, loop_body, 0)


def _rwkv_mix_unpipelined(proj3, shift3, s0, p, *, L, Lv, HPS, BB, y_dtype):
    b, t, _ = proj3.shape
    assert t % Lv == 0 and b % BB == 0
    nc = t // Lv
    w = HPS * LANES
    nhg = D_RWKV // w
    kb = D_RWKV // w
    lora_blk = 3 * D_RWKV // LORA_W
    n_dbl = max(1, math.ceil(math.log2(Lv)))
    col = lambda o: (lambda bi, g, c: (bi, c, o + g))
    sh = lambda o: (lambda bi, g, c: (bi, 0, o + g))
    par = lambda o: (lambda bi, g, c: (0, o + g))
    in_specs = [
        pl.BlockSpec((BB, Lv, w), col(0)), pl.BlockSpec((BB, Lv, w), col(kb)), pl.BlockSpec((BB, Lv, w), col(2 * kb)),
        pl.BlockSpec((BB, Lv, LORA_W), lambda bi, g, c: (bi, c, lora_blk)),
        pl.BlockSpec((BB, 1, w), sh(0)), pl.BlockSpec((BB, 1, w), sh(kb)), pl.BlockSpec((BB, 1, w), sh(2 * kb)),
        pl.BlockSpec((BB, 1, LORA_W), lambda bi, g, c: (bi, 0, lora_blk)),
        pl.BlockSpec((BB, 2 * HPS, HEAD_DIM, HEAD_DIM), lambda bi, g, c: (bi, g, 0, 0)),
        pl.BlockSpec((1, w), par(0)), pl.BlockSpec((1, w), par(kb)), pl.BlockSpec((1, w), par(2 * kb)),
        pl.BlockSpec((1, LORA_W), lambda bi, g, c: (0, lora_blk)),
    ] + [pl.BlockSpec((1, w), par(0))] * 7 + [
        pl.BlockSpec((LANES, w), par(0)), pl.BlockSpec((LORA_A_WIN, w), par(0)),
        pl.BlockSpec((LORA_W - LORA_G_START, w), par(0)),
    ]
    out_specs = [pl.BlockSpec((BB, Lv, w), col(0)),
                 pl.BlockSpec((BB, 2 * HPS, HEAD_DIM, HEAD_DIM), lambda bi, g, c: (bi, g, 0, 0))]
    kern = functools.partial(_rwkv_kernel, L=L, Lv=Lv, HPS=HPS, BB=BB, NC=nc, n_dbl=n_dbl)
    return pl.pallas_call(
        kern,
        grid=(b // BB, nhg, nc),
        in_specs=in_specs,
        out_specs=out_specs,
        out_shape=[jax.ShapeDtypeStruct((b, t, D_RWKV), y_dtype),
                   jax.ShapeDtypeStruct((b, N_HEADS, HEAD_DIM, HEAD_DIM), F32)],
        scratch_shapes=[pltpu.VMEM((BB, HPS, LANES, LANES), F32),
                        pltpu.VMEM((BB, L + 8, w), F32), pltpu.VMEM((BB, L + 8, w), F32),
                        pltpu.VMEM((BB, L + 8, w), F32), pltpu.VMEM((BB, L + 8, LORA_W), F32)],
        compiler_params=_cparams(("parallel", "parallel", "arbitrary")),
        name="rwkv_mix",
    )(proj3, proj3, proj3, proj3, shift3, shift3, shift3, shift3, s0,
      p["mu"], p["mu"], p["mu"], p["mu"], p["w0"], p["a0"], p["k_k"], p["k_a"], p["r_k"],
      p["ln_w"], p["ln_b"], p["w_lora"], p["a_lora"], p["g_lora"])


def _rwkv_kernel(pr_ref, pk_ref, pv_ref, plo_ref, shr_ref, shk_ref, shv_ref, shl_ref, s0_ref,
                 mur_ref, muk_ref, muv_ref, mul_ref, w0_ref, a0_ref, kk_ref, ka_ref, rk_ref, lnw_ref, lnb_ref,
                 wl_ref, al_ref, gl_ref,
                 y_ref, so_ref,
                 sd_ref, br_ref, bk_ref, bv_ref, bl_ref, *stage_refs, L, Lv, HPS, BB, NC, n_dbl, pipelined):
    c = pl.program_id(2)
    H = HEAD_DIM
    W = HPS * LANES
    sls = [slice(i * LANES, (i + 1) * LANES) for i in range(HPS)]
    P = range(HPS)
    STAGED = ("a_t", "r_t", "b_t", "k_t", "b_e", "k_e", "v_b", "g", "bg", "p_l")

    lane = lax.broadcasted_iota(jnp.int32, (L, LANES), 1)
    head1 = lane < H
    row_i = lax.broadcasted_iota(jnp.int32, (L, 2 * L), 0)
    col_i = lax.broadcasted_iota(jnp.int32, (L, 2 * L), 1)
    col_t = jnp.where(col_i >= L, col_i - L, col_i)
    strict = col_t < row_i
    incl = col_t <= row_i
    colh1 = col_i < L
    tri = (lax.broadcasted_iota(jnp.int32, (L, L), 1) <= lax.broadcasted_iota(jnp.int32, (L, L), 0)).astype(BF16)
    ji = lax.broadcasted_iota(jnp.int32, (LANES, LANES), 0)
    jj = lax.broadcasted_iota(jnp.int32, (LANES, LANES), 1)
    same_head = (ji < H) == (jj < H)
    seg = same_head.astype(BF16)
    rowv = lax.broadcasted_iota(jnp.int32, (L, W), 0) < Lv

    def seg_sum(x):
        n = x.shape[1] // LANES
        xs = jnp.concatenate([x[:, s] for s in sls[:n]], axis=0)
        ys = _dot(xs.astype(BF16), seg)
        return jnp.concatenate([ys[i * L:(i + 1) * L] for i in range(n)], axis=1)

    def stack_heads(x, m):
        zero = jnp.zeros_like(x)
        return jnp.concatenate([jnp.where(m, x, zero), jnp.where(m, zero, x)], axis=0)

    def init(bi):
        br_ref[bi, 7:8, :] = shr_ref[bi]
        bk_ref[bi, 7:8, :] = shk_ref[bi]
        bv_ref[bi, 7:8, :] = shv_ref[bi]
        bl_ref[bi, 7:8, :] = shl_ref[bi]
        if Lv < L:
            for ref in (br_ref, bk_ref, bv_ref, bl_ref):
                ref[bi, 8 + Lv:8 + L, :] = jnp.zeros((L - Lv, ref.shape[2]), F32)
        z = jnp.zeros((H, H), F32)
        for i in P:
            top = jnp.concatenate([s0_ref[bi, 2 * i], z], axis=1)
            bot = jnp.concatenate([z, s0_ref[bi, 2 * i + 1]], axis=1)
            sd_ref[bi, i] = jnp.concatenate([top, bot], axis=0)

    def finish(bi):
        for i in P:
            sd = sd_ref[bi, i]
            so_ref[bi, 2 * i] = sd[0:H, 0:H]
            so_ref[bi, 2 * i + 1] = sd[H:2 * H, H:2 * H]

    def stage_inputs(bi):
        br_ref[bi, 8:8 + Lv, :] = pr_ref[bi]
        bk_ref[bi, 8:8 + Lv, :] = pk_ref[bi]
        bv_ref[bi, 8:8 + Lv, :] = pv_ref[bi]
        bl_ref[bi, 8:8 + Lv, :] = plo_ref[bi]

    def carry_rows(bi):
        br_ref[bi, 7:8, :] = br_ref[bi, 7 + Lv:8 + Lv, :]
        bk_ref[bi, 7:8, :] = bk_ref[bi, 7 + Lv:8 + Lv, :]
        bv_ref[bi, 7:8, :] = bv_ref[bi, 7 + Lv:8 + Lv, :]
        bl_ref[bi, 7:8, :] = bl_ref[bi, 7 + Lv:8 + Lv, :]

    def lora_inputs(bi):
        p = bl_ref[bi, 8:8 + L, :]
        ps_l = p + mul_ref[...] * (bl_ref[bi, 7:7 + L, :] - p)
        td = (1.0 - 2.0 / (1.0 + jnp.exp(2.0 * ps_l[:, 0:LANES]))).astype(BF16)
        da = ps_l[:, 0:LORA_A_WIN].astype(BF16)
        sg = _sigmoid(ps_l[:, LORA_G_START:]).astype(BF16)
        return td, da, sg

    def prologue(bi, lora, cols):
        td, da, sg = lora

        def shifted(buf, mu_ref):
            p = buf[bi, 8:8 + L, cols]
            return p + mu_ref[:, cols] * (buf[bi, 7:7 + L, cols] - p)

        r = shifted(br_ref, mur_ref)
        k = shifted(bk_ref, muk_ref)
        v = shifted(bv_ref, muv_ref)
        zlog = w0_ref[:, cols] + _dot(td, wl_ref[:, cols])
        logw = -math.exp(-0.5) / (1.0 + jnp.exp(-zlog))
        a_sig = _sigmoid(a0_ref[:, cols] + _dot(da, al_ref[:, cols]))
        kk = k * kk_ref[:, cols]
        kk = kk * jnp.minimum(lax.rsqrt(jnp.maximum(seg_sum(kk * kk), 0.0)), 1e12)
        k_h = k * (1.0 + (a_sig - 1.0) * ka_ref[:, cols])
        if Lv < L:
            valid = rowv[:, 0:r.shape[1]]
            logw = jnp.where(valid, logw, 0.0)
            kk = jnp.where(valid, kk, 0.0)
            k_h = jnp.where(valid, k_h, 0.0)
            v = jnp.where(valid, v, 0.0)

        hi, lo = _split2(logw)
        cum = _dot(tri, hi) + _dot(tri, lo)
        cum_l = cum[L - 1:L, :]
        p_inv = jnp.exp(-cum)
        p_end = jnp.exp(cum_l - cum)
        bvec = kk * a_sig
        g = _dot(sg, gl_ref[:, cols])
        return dict(
            a_t=(-kk * jnp.exp(cum - logw)).astype(BF16), r_t=(r * jnp.exp(cum)).astype(BF16),
            b_t=(bvec * p_inv).astype(BF16), k_t=(k_h * p_inv).astype(BF16),
            b_e=(bvec * p_end).astype(BF16), k_e=(k_h * p_end).astype(BF16), v_b=v.astype(BF16),
            g=g, bg=seg_sum(r * k_h * rk_ref[:, cols]) * v * g, p_l=jnp.exp(cum_l))

    def prologue_items(bi, tiles):
        S = {}

        def load_shift():
            S["lora"] = lora_inputs(bi)
            for name, buf, mu_ref in (("r", br_ref, mur_ref), ("k", bk_ref, muk_ref), ("v", bv_ref, muv_ref)):
                p = buf[bi, 8:8 + L, :]
                S[name] = p + mu_ref[...] * (buf[bi, 7:7 + L, :] - p)
            S["kk"] = S["k"] * kk_ref[...]

        def matmuls_1():
            td, da, sg = S["lora"]
            S["zlog"] = w0_ref[...] + _dot(td, wl_ref[...])
            S["apre"] = a0_ref[...] + _dot(da, al_ref[...])
            S["g"] = _dot(sg, gl_ref[...])
            S["n2"] = seg_sum(S["kk"] * S["kk"])

        def vector_1(i):
            s = sls[i]
            logw = -math.exp(-0.5) / (1.0 + jnp.exp(-S["zlog"][:, s]))
            a_sig = _sigmoid(S["apre"][:, s])
            kk = S["kk"][:, s] * jnp.minimum(lax.rsqrt(jnp.maximum(S["n2"][:, s], 0.0)), 1e12)
            k_h = S["k"][:, s] * (1.0 + (a_sig - 1.0) * ka_ref[:, s])
            hi, lo = _split2(logw)
            S[("v1", i)] = dict(logw=logw, bvec=kk * a_sig, kk=kk, k_h=k_h, hi=hi, lo=lo,
                                prod=(S["r"][:, s] * k_h * rk_ref[:, s]).astype(BF16))

        def matmuls_2():
            cat = lambda n: jnp.concatenate([S[("v1", i)][n] for i in P], axis=1)
            S["cum"] = _dot(tri, cat("hi")) + _dot(tri, cat("lo"))
            xs = jnp.concatenate([S[("v1", i)]["prod"] for i in P], axis=0)
            S["bsum"] = _dot(xs, seg)

        def vector_2(i):
            s = sls[i]
            t1 = S[("v1", i)]
            cum = S["cum"][:, s]
            cum_l = cum[L - 1:L, :]
            p_inv = jnp.exp(-cum)
            p_end = jnp.exp(cum_l - cum)
            v = S["v"][:, s]
            g = S["g"][:, s]
            tiles[i] = dict(
                a_t=(-t1["kk"] * jnp.exp(cum - t1["logw"])).astype(BF16),
                r_t=(S["r"][:, s] * jnp.exp(cum)).astype(BF16),
                b_t=(t1["bvec"] * p_inv).astype(BF16), k_t=(t1["k_h"] * p_inv).astype(BF16),
                b_e=(t1["bvec"] * p_end).astype(BF16), k_e=(t1["k_h"] * p_end).astype(BF16),
                v_b=v.astype(BF16), g=g, bg=S["bsum"][i * L:(i + 1) * L] * v * g, p_l=jnp.exp(cum_l))

        return ([load_shift, matmuls_1] + [functools.partial(vector_1, i) for i in P] + [matmuls_2]
                + [functools.partial(vector_2, i) for i in P])

    def chain_stages(t, sd, lnw, lnb, res):
        a_t, r_t, b_t, k_t, b_e, k_e, v_b = (t[n] for n in STAGED[:7])
        lhs = [jnp.concatenate([a_t[:, s], r_t[:, s]], axis=0) for s in sls]
        rhs = [jnp.concatenate([stack_heads(b_t[:, s], head1), stack_heads(k_t[:, s], head1)], axis=0) for s in sls]
        aa = [_dot_nt(lhs[i], rhs[i]) for i in P]
        yield
        sa = [_dot_nt(lhs[i], sd[i].astype(BF16)) for i in P]
        yield
        v_st = [stack_heads(v_b[:, s], head1) for s in sls]
        a_ak = [jnp.where(strict, aa[i][0:L, 2 * L:4 * L], 0.0).astype(BF16) for i in P]
        x = [sa[i][0:L] + _dot(a_ak[i], v_st[i]) for i in P]
        ap = [jnp.where(strict, aa[i][0:L, 0:2 * L], 0.0).astype(BF16) for i in P]
        yield
        for d in range(n_dbl):
            x = [x[i] + _dot(ap[i], stack_heads(x[i].astype(BF16), head1)) for i in P]
            yield
            if d + 1 < n_dbl:
                ap = [_dot(ap[i], stack_heads(ap[i], colh1)).astype(BF16) for i in P]
                yield
        u_b = [x[i].astype(BF16) for i in P]
        a_rb = [jnp.where(incl, aa[i][L:2 * L, 0:2 * L], 0.0).astype(BF16) for i in P]
        a_rk = [jnp.where(incl, aa[i][L:2 * L, 2 * L:4 * L], 0.0).astype(BF16) for i in P]
        y = [sa[i][L:2 * L] + _dot(a_rb[i], stack_heads(u_b[i], head1)) + _dot(a_rk[i], v_st[i]) for i in P]
        yield
        ds = [_dot_tn(jnp.concatenate([u_b[i], v_b[:, sls[i]]], axis=0),
                      jnp.concatenate([b_e[:, sls[i]], k_e[:, sls[i]]], axis=0)) for i in P]
        res["sd"] = [sd[i] * t["p_l"][:, sls[i]] + jnp.where(same_head, ds[i], 0.0) for i in P]
        yield
        y = jnp.concatenate(y, axis=1)
        mean = seg_sum(y) * (1.0 / H)
        yield
        dlt = y - mean
        var = seg_sum(dlt * dlt) * (1.0 / H)
        yield
        yn = dlt * lax.rsqrt(var + GN_EPS) * lnw + lnb
        res["out"] = yn * t["g"] + t["bg"]

    def one_batch(bi):
        if NC == 1:
            init(bi)
        else:
            pl.when(c == 0)(lambda: init(bi))
        stage_inputs(bi)
        t = prologue(bi, lora_inputs(bi), slice(None))
        res = {}
        for _ in chain_stages(t, [sd_ref[bi, i] for i in P], lnw_ref[...], lnb_ref[...], res):
            pass
        for i in P:
            sd_ref[bi, i] = res["sd"][i]
        y_ref[bi] = res["out"][0:Lv].astype(y_ref.dtype)
        carry_rows(bi)
        if NC == 1:
            finish(bi)
        else:
            pl.when(c == NC - 1)(lambda: finish(bi))

    def one_batch_pipelined():
        st = dict(zip(STAGED, stage_refs))

        @pl.when(c == 0)
        def _():
            init(0)
            for ref in stage_refs:
                ref[...] = jnp.zeros(ref.shape, ref.dtype)

        prev = {n: st[n][...] for n in STAGED}
        sd = [sd_ref[0, i] for i in P]
        lnw, lnb = lnw_ref[...], lnb_ref[...]
        stage_inputs(0)
        res, tiles = {}, [None] * HPS
        items = prologue_items(0, tiles)
        quota = iter([2] + [HPS // 2] * 2 + [1] + [HPS // 8] * 8)
        for _ in chain_stages(prev, sd, lnw, lnb, res):
            for _ in range(next(quota, 0)):
                if items:
                    items.pop(0)()
        while items:
            items.pop(0)()
        for n in STAGED:
            for i in P:
                st[n][:, sls[i]] = tiles[i][n]
        live = c > 0
        for i in P:
            sd_ref[0, i] = jnp.where(live, res["sd"][i], sd[i])
        y_ref[0] = res["out"].astype(y_ref.dtype)
        carry_rows(0)
        pl.when(c == NC)(lambda: finish(0))

    if pipelined:
        one_batch_pipelined()
    elif BB == 1:
        one_batch(0)
    else:
        def loop_body(bi, carry):
            one_batch(bi)
            return carry
        lax.fori_loop(0, BB, loop_body, 0)


def _rwkv_mix(proj3, shift3, s0, p, *, L, Lv, HPS, BB, y_dtype, pipelined):
    b, t, _ = proj3.shape
    assert t % Lv == 0 and b % BB == 0
    assert not pipelined or (BB == 1 and Lv == L)
    nc = t // Lv
    w = HPS * LANES
    nhg = D_RWKV // w
    kb = D_RWKV // w
    lora_blk = 3 * D_RWKV // LORA_W
    n_dbl = max(1, math.ceil(math.log2(Lv)))
    if pipelined:
        cin = lambda c: jnp.minimum(c, nc - 1)
        cout = lambda c: jnp.maximum(c - 1, 0)
    else:
        cin = cout = lambda c: c
    col = lambda o: (lambda bi, g, c: (bi, cin(c), o + g))
    sh = lambda o: (lambda bi, g, c: (bi, 0, o + g))
    par = lambda o: (lambda bi, g, c: (0, o + g))
    in_specs = [
        pl.BlockSpec((BB, Lv, w), col(0)), pl.BlockSpec((BB, Lv, w), col(kb)), pl.BlockSpec((BB, Lv, w), col(2 * kb)),
        pl.BlockSpec((BB, Lv, LORA_W), lambda bi, g, c: (bi, cin(c), lora_blk)),
        pl.BlockSpec((BB, 1, w), sh(0)), pl.BlockSpec((BB, 1, w), sh(kb)), pl.BlockSpec((BB, 1, w), sh(2 * kb)),
        pl.BlockSpec((BB, 1, LORA_W), lambda bi, g, c: (bi, 0, lora_blk)),
        pl.BlockSpec((BB, 2 * HPS, HEAD_DIM, HEAD_DIM), lambda bi, g, c: (bi, g, 0, 0)),
        pl.BlockSpec((1, w), par(0)), pl.BlockSpec((1, w), par(kb)), pl.BlockSpec((1, w), par(2 * kb)),
        pl.BlockSpec((1, LORA_W), lambda bi, g, c: (0, lora_blk)),
    ] + [pl.BlockSpec((1, w), par(0))] * 7 + [
        pl.BlockSpec((LANES, w), par(0)), pl.BlockSpec((LORA_A_WIN, w), par(0)),
        pl.BlockSpec((LORA_W - LORA_G_START, w), par(0)),
    ]
    out_specs = [pl.BlockSpec((BB, Lv, w), lambda bi, g, c: (bi, cout(c), g)),
                 pl.BlockSpec((BB, 2 * HPS, HEAD_DIM, HEAD_DIM), lambda bi, g, c: (bi, g, 0, 0))]
    scratch = [pltpu.VMEM((BB, HPS, LANES, LANES), F32),
               pltpu.VMEM((BB, L + 8, w), F32), pltpu.VMEM((BB, L + 8, w), F32),
               pltpu.VMEM((BB, L + 8, w), F32), pltpu.VMEM((BB, L + 8, LORA_W), F32)]
    if pipelined:
        scratch += [pltpu.VMEM((L, w), BF16)] * 7 + [pltpu.VMEM((L, w), F32)] * 2 + [pltpu.VMEM((1, w), F32)]
    kern = functools.partial(_rwkv_kernel, L=L, Lv=Lv, HPS=HPS, BB=BB, NC=nc, n_dbl=n_dbl, pipelined=pipelined)
    return pl.pallas_call(
        kern,
        grid=(b // BB, nhg, nc + 1 if pipelined else nc),
        in_specs=in_specs,
        out_specs=out_specs,
        out_shape=[jax.ShapeDtypeStruct((b, t, D_RWKV), y_dtype),
                   jax.ShapeDtypeStruct((b, N_HEADS, HEAD_DIM, HEAD_DIM), F32)],
        scratch_shapes=scratch,
        compiler_params=_cparams(("parallel", "parallel", "arbitrary")),
        name="rwkv_mix",
    )(proj3, proj3, proj3, proj3, shift3, shift3, shift3, shift3, s0,
      p["mu"], p["mu"], p["mu"], p["mu"], p["w0"], p["a0"], p["k_k"], p["k_a"], p["r_k"],
      p["ln_w"], p["ln_b"], p["w_lora"], p["a_lora"], p["g_lora"])


SAMPLE_PREP_W = 512
PREP_OUT = ("r", "w", "k", "v", "a", "b", "g", "bonus")


def _rwkv_sample_prep_kernel(pr_ref, pk_ref, pv_ref, plo_ref, shr_ref, shk_ref, shv_ref, shl_ref,
                             mur_ref, muk_ref, muv_ref, mul_ref, w0_ref, a0_ref, kk_ref, ka_ref, rk_ref,
                             wl_ref, al_ref, gl_ref, *out_refs, nb, nt):
    W = pr_ref.shape[1]
    n_tiles = W // LANES
    rows = nb * nt
    ji = lax.broadcasted_iota(jnp.int32, (LANES, LANES), 0)
    jj = lax.broadcasted_iota(jnp.int32, (LANES, LANES), 1)
    seg = ((ji < HEAD_DIM) == (jj < HEAD_DIM)).astype(BF16)

    def seg_sum(x):
        xs = jnp.concatenate([x[:, i * LANES:(i + 1) * LANES] for i in range(n_tiles)], axis=0).astype(BF16)
        ys = _dot(xs, seg)
        return jnp.concatenate([ys[i * rows:(i + 1) * rows] for i in range(n_tiles)], axis=1)

    def shifted(p_ref, s_ref, mu_ref):
        p = p_ref[...]
        prev = jnp.concatenate([s_ref[...], p[0:rows - nb]], axis=0)
        return p + mu_ref[...] * (prev - p)

    ps_l = shifted(plo_ref, shl_ref, mul_ref)
    td = (1.0 - 2.0 / (1.0 + jnp.exp(2.0 * ps_l[:, 0:LANES]))).astype(BF16)
    da = ps_l[:, 0:LORA_A_WIN].astype(BF16)
    sg = _sigmoid(ps_l[:, LORA_G_START:]).astype(BF16)
    r = shifted(pr_ref, shr_ref, mur_ref)
    k = shifted(pk_ref, shk_ref, muk_ref)
    v = shifted(pv_ref, shv_ref, muv_ref)
    zlog = w0_ref[...] + _dot(td, wl_ref[...])
    w = jnp.exp(-math.exp(-0.5) / (1.0 + jnp.exp(-zlog)))
    a_sig = _sigmoid(a0_ref[...] + _dot(da, al_ref[...]))
    kk = k * kk_ref[...]
    kk = kk * jnp.minimum(lax.rsqrt(jnp.maximum(seg_sum(kk * kk), 0.0)), 1e12)
    k_h = k * (1.0 + (a_sig - 1.0) * ka_ref[...])
    g = _dot(sg, gl_ref[...])
    bonus = seg_sum(r * k_h * rk_ref[...]) * v
    vals = dict(r=r, w=w, k=k_h, v=v, a=-kk, b=kk * a_sig, g=g, bonus=bonus)
    for name, o_ref in zip(PREP_OUT, out_refs):
        x = vals[name]
        for t in range(nt):
            for c in range(n_tiles):
                o_ref[t, c * LANES:(c + 1) * LANES, :] = x[t * nb:(t + 1) * nb, c * LANES:(c + 1) * LANES].T


def _rwkv_sample_prep(proj_r, shift, p, nb, nt):
    w = SAMPLE_PREP_W
    kb = D_RWKV // w
    lora_blk = 3 * D_RWKV // LORA_W
    rows = nt * nb
    col = lambda o: (lambda g: (0, o + g))
    in_specs = (
        [pl.BlockSpec((rows, w), col(0)), pl.BlockSpec((rows, w), col(kb)), pl.BlockSpec((rows, w), col(2 * kb)),
         pl.BlockSpec((rows, LORA_W), lambda g: (0, lora_blk))]
        + [pl.BlockSpec((nb, w), col(0)), pl.BlockSpec((nb, w), col(kb)), pl.BlockSpec((nb, w), col(2 * kb)),
           pl.BlockSpec((nb, LORA_W), lambda g: (0, lora_blk))]
        + [pl.BlockSpec((1, w), col(0)), pl.BlockSpec((1, w), col(kb)), pl.BlockSpec((1, w), col(2 * kb)),
           pl.BlockSpec((1, LORA_W), lambda g: (0, lora_blk))]
        + [pl.BlockSpec((1, w), col(0))] * 5
        + [pl.BlockSpec((LANES, w), col(0)), pl.BlockSpec((LORA_A_WIN, w), col(0)),
           pl.BlockSpec((LORA_W - LORA_G_START, w), col(0))])
    out_spec = pl.BlockSpec((nt, w, nb), lambda g: (0, g, 0))
    return pl.pallas_call(
        functools.partial(_rwkv_sample_prep_kernel, nb=nb, nt=nt),
        grid=(kb,),
        in_specs=in_specs,
        out_specs=[out_spec] * len(PREP_OUT),
        out_shape=[jax.ShapeDtypeStruct((nt, D_RWKV, nb), F32)] * len(PREP_OUT),
        compiler_params=_cparams(("parallel",)),
        name="rwkv_sample_prep",
    )(proj_r, proj_r, proj_r, proj_r, shift, shift, shift, shift, p["mu"], p["mu"], p["mu"], p["mu"],
      p["w0"], p["a0"], p["k_k"], p["k_a"], p["r_k"], p["w_lora"], p["a_lora"], p["g_lora"])


def _wkv_sample_kernel(s_ref, r_ref, w_ref, k_ref, v_ref, a_ref, b_ref, g_ref, bonus_ref, lnw_ref, lnb_ref,
                       so_ref, o_ref, y_ref, *, nt):
    H = HEAD_DIM
    SUB = 8
    PAR = 4
    rowid = lax.broadcasted_iota(jnp.int32, (SUB, LANES), 0)

    def body(i8, carry):
        base = pl.multiple_of(i8 * SUB, SUB)
        v8 = [v_ref[t, pl.ds(base, SUB), :] for t in range(nt)]
        y8 = [jnp.zeros((SUB, LANES), F32) for _ in range(nt)]
        for h0 in range(0, SUB, PAR):
            ids = list(range(h0, h0 + PAR))
            S = [s_ref[0, base + ii] for ii in ids]
            for t in range(nt):
                a, w, b, k, r = a_ref[t], w_ref[t], b_ref[t], k_ref[t], r_ref[t]
                for n, ii in enumerate(ids):
                    sa = jnp.sum(S[n] * a, axis=0, keepdims=True)
                    S[n] = S[n] * w + sa * b + v8[t][ii:ii + 1, :] * k
                    y = jnp.sum(S[n] * r, axis=0, keepdims=True)
                    y8[t] = jnp.where(rowid == ii, y, y8[t])
            for n, ii in enumerate(ids):
                so_ref[0, base + ii] = S[n]
        for t in range(nt):
            y_ref[t, pl.ds(base, SUB), :] = y8[t]
        return carry

    lax.fori_loop(0, H // SUB, body, 0)
    for t in range(nt):
        y = y_ref[t]
        mean = jnp.sum(y, axis=0, keepdims=True) * (1.0 / H)
        d = y - mean
        var = jnp.sum(d * d, axis=0, keepdims=True) * (1.0 / H)
        o_ref[t] = (d * lax.rsqrt(var + GN_EPS) * lnw_ref[...] + lnb_ref[...] + bonus_ref[t]) * g_ref[t]


def _wkv_sample(state_t, prep, lnw_b, lnb_b):
    nh, hd, _, nb = state_t.shape
    nt = prep[0].shape[0]
    st_spec = pl.BlockSpec((1, hd, hd, nb), lambda h: (h, 0, 0, 0))
    ch_spec = pl.BlockSpec((nt, hd, nb), lambda h: (0, h, 0))
    ln_spec = pl.BlockSpec((hd, nb), lambda h: (h, 0))
    return pl.pallas_call(
        functools.partial(_wkv_sample_kernel, nt=nt),
        grid=(nh,),
        in_specs=[st_spec] + [ch_spec] * len(PREP_OUT) + [ln_spec, ln_spec],
        out_specs=[st_spec, ch_spec],
        out_shape=[jax.ShapeDtypeStruct(state_t.shape, F32), jax.ShapeDtypeStruct((nt, nh * hd, nb), F32)],
        scratch_shapes=[pltpu.VMEM((nt, hd, nb), F32)],
        compiler_params=_cparams(("parallel",)),
        name="wkv_sample",
    )(state_t, *prep, lnw_b, lnb_b)


def _swa_prompt_kernel(slope_ref, sink_ref, q_ref, kc_ref, kp_ref, vc_ref, vp_ref, o_ref, bias_ref):
    n = pl.program_id(1)
    blk = WINDOW
    H = HEAD_DIM

    @pl.when((pl.program_id(0) == 0) & (n == 0))
    def _():
        t = lax.broadcasted_iota(jnp.int32, (blk, 2 * blk), 0)
        j = lax.broadcasted_iota(jnp.int32, (blk, 2 * blk), 1)
        dist = t - j + blk
        band = (dist >= 0) & (dist <= WINDOW)
        first = band & (j >= blk)
        distf = dist.astype(F32)
        for h in range(N_HEADS):
            ab = -slope_ref[h] * distf
            bias_ref[0, h] = jnp.where(first, ab, -jnp.inf)
            bias_ref[1, h] = jnp.where(band, ab, -jnp.inf)

    sel = jnp.where(n == 0, 0, 1)
    low = lax.broadcasted_iota(jnp.int32, (blk, LANES), 1) < H
    scale = H ** -0.5
    tile = lambda i: slice(i * LANES, (i + 1) * LANES)

    def kv_group(KV):
        kslab = {hk: jnp.concatenate([kp_ref[0, :, tile(hk // 2)], kc_ref[0, :, tile(hk // 2)]],
                                     axis=0).astype(BF16) for hk in KV}
        vslab = {hk: jnp.concatenate([vp_ref[0, :, tile(hk // 2)], vc_ref[0, :, tile(hk // 2)]],
                                     axis=0).astype(BF16) for hk in KV}
        lhs = {}
        for hk in KV:
            parts = []
            for s2 in range(2):
                xs = q_ref[0, :, tile(2 * hk + s2)].astype(F32) * scale
                xr = pltpu.roll(xs, H, axis=1)
                if hk % 2 == 0:
                    parts += [jnp.where(low, xs, 0.0), jnp.where(low, xr, 0.0)]
                else:
                    parts += [jnp.where(low, 0.0, xr), jnp.where(low, 0.0, xs)]
            lhs[hk] = jnp.concatenate(parts, axis=0).astype(BF16)
        s = {hk: _dot_nt(lhs[hk], kslab[hk]) for hk in KV}
        p, rden = {}, {}
        for hk in KV:
            ps, rs = [], []
            for g in range(GQA_GROUP):
                h = hk * GQA_GROUP + g
                sg = s[hk][g * blk:(g + 1) * blk] + bias_ref[sel, h]
                m = jnp.maximum(jnp.max(sg, axis=-1, keepdims=True), sink_ref[h])
                e = jnp.exp(sg - m)
                rs.append(1.0 / (jnp.sum(e, axis=-1, keepdims=True) + jnp.exp(sink_ref[h] - m)))
                ps.append(e.astype(BF16))
            p[hk] = jnp.concatenate(ps, axis=0)
            rden[hk] = rs
        o = {hk: _dot(p[hk], vslab[hk]) for hk in KV}
        for hk in KV:
            for s2 in range(2):
                ga, gb = 2 * s2, 2 * s2 + 1
                oa = o[hk][ga * blk:(ga + 1) * blk] * rden[hk][ga]
                ob = o[hk][gb * blk:(gb + 1) * blk] * rden[hk][gb]
                if hk % 2 == 0:
                    out = jnp.where(low, oa, pltpu.roll(ob, H, axis=1))
                else:
                    out = jnp.where(low, pltpu.roll(oa, H, axis=1), ob)
                o_ref[0, :, tile(2 * hk + s2)] = out.astype(o_ref.dtype)

    for g0 in range(0, N_KV_HEADS, KV_PER_GROUP):
        kv_group(range(g0, g0 + KV_PER_GROUP))


def _swa_prompt(q3, kv3, slopes, sinks):
    b, t, _ = q3.shape
    nb = t // WINDOW
    smem = pl.BlockSpec(memory_space=pltpu.SMEM)
    prev = lambda n: jnp.maximum(n - 1, 0)
    return pl.pallas_call(
        _swa_prompt_kernel,
        grid=(b, nb),
        in_specs=[smem, smem,
                  pl.BlockSpec((1, WINDOW, D_MODEL), lambda bi, n: (bi, n, 0)),
                  pl.BlockSpec((1, WINDOW, D_KV), lambda bi, n: (bi, n, 0)),
                  pl.BlockSpec((1, WINDOW, D_KV), lambda bi, n: (bi, prev(n), 0)),
                  pl.BlockSpec((1, WINDOW, D_KV), lambda bi, n: (bi, n, 1)),
                  pl.BlockSpec((1, WINDOW, D_KV), lambda bi, n: (bi, prev(n), 1))],
        out_specs=pl.BlockSpec((1, WINDOW, D_MODEL), lambda bi, n: (bi, n, 0)),
        out_shape=jax.ShapeDtypeStruct((b, t, D_MODEL), BF16),
        scratch_shapes=[pltpu.VMEM((2, N_HEADS, WINDOW, 2 * WINDOW), F32)],
        compiler_params=_cparams(("arbitrary", "arbitrary")),
        name="swa_prompt",
    )(slopes, sinks, q3, kv3, kv3, kv3, kv3)


def _swa_sample_kernel(slope_ref, sink_ref, q_ref, kc_ref, vc_ref, knew_ref, vnew_ref,
                       o_ref, kwin_ref, vwin_ref, nbuf_ref, *, tq, BB, UNR):
    GT = GQA_GROUP * tq
    R = N_KV_HEADS * GT
    NP = 16
    C = D_KV
    row = lax.broadcasted_iota(jnp.int32, (R, WINDOW), 0)
    wcol = lax.broadcasted_iota(jnp.int32, (R, WINDOW), 1)
    t = lax.rem(row, tq)
    slope = slope_ref[...]
    sink = sink_ref[...]
    dist_o = WINDOW + t - wcol
    bias_old = jnp.where(dist_o <= WINDOW, -slope * dist_o.astype(F32), -jnp.inf)
    s_idx = wcol - (WINDOW - tq)
    dist_n = t - s_idx
    bias_new = jnp.where((s_idx >= 0) & (dist_n >= 0), -slope * dist_n.astype(F32), -jnp.inf)
    hkmask = (lax.broadcasted_iota(jnp.int32, (R, C), 0) // GT) == (lax.broadcasted_iota(jnp.int32, (R, C), 1) // HEAD_DIM)
    srow = lax.broadcasted_iota(jnp.int32, (NP, WINDOW), 0)
    scol = lax.broadcasted_iota(jnp.int32, (NP, WINDOW), 1)
    selw = ((scol == srow + (WINDOW - tq)) & (srow < tq)).astype(BF16)
    lane_new = lax.broadcasted_iota(jnp.int32, (C, WINDOW), 1) >= WINDOW - tq
    scale = HEAD_DIM ** -0.5
    for u in range(UNR):
        nbuf_ref[u, :, tq:NP, :] = jnp.zeros((2, NP - tq, C), F32)

    def transposed_new(x):
        return sum(_dot_tn(part, selw) for part in _split3(x))

    def body(i, carry):
        bs = [i * UNR + u for u in range(UNR)]
        U = range(UNR)
        for u in U:
            nbuf_ref[u, 0, 0:tq, :] = knew_ref[bs[u]]
            nbuf_ref[u, 1, 0:tq, :] = vnew_ref[bs[u]]
        kt = [kc_ref[b] for b in bs]
        vt = [vc_ref[b] for b in bs]
        knt = [transposed_new(nbuf_ref[u, 0]) for u in U]
        vnt = [transposed_new(nbuf_ref[u, 1]) for u in U]
        qbd = [jnp.where(hkmask, jnp.concatenate([q_ref[b] * scale] * N_KV_HEADS, axis=0), 0.0).astype(BF16)
               for b in bs]
        s_o = [_dot(qbd[u], kt[u].astype(BF16)) + bias_old for u in U]
        s_n = [_dot(qbd[u], knt[u].astype(BF16)) + bias_new for u in U]
        outs = []
        for u in U:
            m = jnp.maximum(jnp.maximum(jnp.max(s_o[u], axis=-1, keepdims=True),
                                        jnp.max(s_n[u], axis=-1, keepdims=True)), sink)
            p_o = jnp.exp(s_o[u] - m)
            p_n = jnp.exp(s_n[u] - m)
            rden = 1.0 / (jnp.sum(p_o, axis=-1, keepdims=True) + jnp.sum(p_n, axis=-1, keepdims=True)
                          + jnp.exp(sink - m))
            o = _dot_nt(p_o.astype(BF16), vt[u].astype(BF16)) + _dot_nt(p_n.astype(BF16), vnt[u].astype(BF16))
            o = jnp.where(hkmask, o * rden, 0.0)
            acc = o[0:GT]
            for hk in range(1, N_KV_HEADS):
                acc = acc + o[hk * GT:(hk + 1) * GT]
            outs.append(acc)
        for u in U:
            kwin_ref[bs[u]] = jnp.where(lane_new, knt[u], pltpu.roll(kt[u], WINDOW - tq, axis=1))
            vwin_ref[bs[u]] = jnp.where(lane_new, vnt[u], pltpu.roll(vt[u], WINDOW - tq, axis=1))
            o_ref[bs[u]] = outs[u]
        return carry

    lax.fori_loop(0, BB // UNR, body, 0)


def _swa_sample(q16, knew3, vnew3, kct, vct, slope_rows, sink_rows):
    b, gt, c = q16.shape
    tq = gt // GQA_GROUP
    bb = 8 if b % 8 == 0 else 1
    unr = 2 if bb % 2 == 0 else 1
    rows = N_KV_HEADS * gt
    blk3 = lambda shape: pl.BlockSpec(shape, lambda i: (i, 0, 0))
    full2 = pl.BlockSpec((rows, 1), lambda i: (0, 0))
    kern = functools.partial(_swa_sample_kernel, tq=tq, BB=bb, UNR=unr)
    return pl.pallas_call(
        kern,
        grid=(b // bb,),
        in_specs=[full2, full2, blk3((bb, gt, c)), blk3((bb, c, WINDOW)), blk3((bb, c, WINDOW)),
                  blk3((bb, tq, c)), blk3((bb, tq, c))],
        out_specs=[blk3((bb, gt, c)), blk3((bb, c, WINDOW)), blk3((bb, c, WINDOW))],
        out_shape=[jax.ShapeDtypeStruct((b, gt, c), F32),
                   jax.ShapeDtypeStruct((b, c, WINDOW), F32),
                   jax.ShapeDtypeStruct((b, c, WINDOW), F32)],
        scratch_shapes=[pltpu.VMEM((unr, 2, 16, c), F32)],
        compiler_params=_cparams(("parallel",)),
        name="swa_sample",
    )(slope_rows, sink_rows, q16, kct, vct, knew3, vnew3)


def _merge_out_kernel(x_ref, ya_ref, yb_ref, ga_ref, gb_ref, wo_ref, nw_ref, h_ref, hn_ref):
    f32 = lambda ref: ref[...].astype(F32)
    mixed = _sigmoid(f32(ga_ref)) * f32(ya_ref) + _sigmoid(f32(gb_ref)) * f32(yb_ref)
    h = x_ref[...] + _dot(mixed.astype(BF16), wo_ref[...])
    h_ref[...] = h
    ms = jnp.mean(h * h, axis=-1, keepdims=True)
    hn_ref[...] = (h * lax.rsqrt(ms + RMS_EPS) * nw_ref[...]).astype(BF16)


def _merge_out(x, ya, yb, gates, w_out, nw, tm):
    m, d = x.shape
    row = lambda o: (lambda i: (i, o))
    return pl.pallas_call(
        _merge_out_kernel,
        grid=(m // tm,),
        in_specs=[pl.BlockSpec((tm, d), row(0)), pl.BlockSpec((tm, d), row(0)), pl.BlockSpec((tm, d), row(0)),
                  pl.BlockSpec((tm, d), row(0)), pl.BlockSpec((tm, d), row(1)),
                  pl.BlockSpec((d, d), lambda i: (0, 0)), pl.BlockSpec((1, d), lambda i: (0, 0))],
        out_specs=[pl.BlockSpec((tm, d), row(0)), pl.BlockSpec((tm, d), row(0))],
        out_shape=[jax.ShapeDtypeStruct((m, d), F32), jax.ShapeDtypeStruct((m, d), BF16)],
        compiler_params=_cparams(("parallel",)),
        name="merge_out_proj",
    )(x, ya, yb, gates, gates, w_out, nw)


def _mlp_kernel(hn_ref, h_ref, wu_ref, wd_ref, nw_ref, o_ref, acc_ref):
    j = pl.program_id(1)

    @pl.when(j == 0)
    def _():
        acc_ref[...] = jnp.zeros_like(acc_ref)

    u = jnp.maximum(_dot(hn_ref[...], wu_ref[...]), 0.0)
    acc_ref[...] += _dot((u * u).astype(BF16), wd_ref[...])

    @pl.when(j == pl.num_programs(1) - 1)
    def _():
        h = h_ref[...] + acc_ref[...]
        ms = jnp.mean(h * h, axis=-1, keepdims=True)
        o_ref[...] = h * lax.rsqrt(ms + RMS_EPS) * nw_ref[...]


def _mlp(hn, h, w_up, w_down, nw, tm, tf):
    m, d = h.shape
    f = w_up.shape[1]
    return pl.pallas_call(
        _mlp_kernel,
        grid=(m // tm, f // tf),
        in_specs=[pl.BlockSpec((tm, d), lambda i, j: (i, 0)), pl.BlockSpec((tm, d), lambda i, j: (i, 0)),
                  pl.BlockSpec((d, tf), lambda i, j: (0, j)), pl.BlockSpec((tf, d), lambda i, j: (j, 0)),
                  pl.BlockSpec((1, d), lambda i, j: (0, 0))],
        out_specs=pl.BlockSpec((tm, d), lambda i, j: (i, 0)),
        out_shape=jax.ShapeDtypeStruct((m, d), F32),
        scratch_shapes=[pltpu.VMEM((tm, d), F32)],
        compiler_params=_cparams(("parallel", "arbitrary")),
        name="mlp_final_norm",
    )(hn, h, w_up, w_down, nw)


def _pick(m, prefs):
    for t in prefs:
        if m % t == 0:
            return t
    return m


def _pad_cols(v, n):
    return jnp.concatenate([v, jnp.zeros(v.shape[:-1] + (n - v.shape[-1],), v.dtype)], axis=-1)


def _place_rows(w, start, rows):
    n, d = w.shape
    return jnp.concatenate([jnp.zeros((start, d), w.dtype), w, jnp.zeros((rows - start - n, d), w.dtype)], axis=0)


def _layer(x, shift_prev, wkv0, lw, *, prompt, k_cache_t=None, v_cache_t=None):
    b, t, d = x.shape
    m = b * t
    x2 = x.reshape(m, d) if prompt else x.transpose(1, 0, 2).reshape(m, d)
    proj_r, proj_q, proj_kv, proj_g = _in_proj(
        x2, lw["norm_mix_w"], [lw["w_r"], lw["w_q"], lw["w_kv"], lw["w_g"]], [F32, BF16, F32, BF16],
        _pick(m, (1024, 512, 256, 128, 8)))
    shift_pad = _pad_cols(shift_prev, R_PAD)

    if prompt:
        proj_r3 = proj_r.reshape(b, t, R_PAD)
        kv3 = proj_kv.reshape(b, t, 2 * D_KV)
        ya, wkv_new = _rwkv_mix(proj_r3, shift_pad[:, None], wkv0, lw, L=64, Lv=64, HPS=16, BB=1, y_dtype=BF16,
                                pipelined=False)
        yb = _swa_prompt(proj_q.reshape(b, t, D_MODEL), kv3, lw["slopes"], lw["sinks"])
        k_win = kv3[:, t - WINDOW:, :D_KV].reshape(b, WINDOW, N_KV_HEADS, HEAD_DIM)
        v_win = kv3[:, t - WINDOW:, D_KV:].reshape(b, WINDOW, N_KV_HEADS, HEAD_DIM)
        shift_new = proj_r3[:, t - 1, :R_COLS]
    else:
        prep = _rwkv_sample_prep(proj_r, shift_pad, lw, b, t)
        wkv_t, ya_t = _wkv_sample(wkv0.transpose(1, 2, 3, 0), prep,
                                  jnp.broadcast_to(lw["ln_w"].reshape(D_RWKV, 1), (D_RWKV, b)),
                                  jnp.broadcast_to(lw["ln_b"].reshape(D_RWKV, 1), (D_RWKV, b)))
        ya = ya_t.transpose(0, 2, 1)
        wkv_new = wkv_t.transpose(3, 0, 1, 2)
        q16 = proj_q.reshape(t, b, N_KV_HEADS, GQA_GROUP, HEAD_DIM).transpose(1, 3, 0, 2, 4)
        q16 = q16.reshape(b, GQA_GROUP * t, D_KV)
        kv_bt = proj_kv.reshape(t, b, 2 * D_KV).transpose(1, 0, 2)
        gt_head = (jnp.arange(N_KV_HEADS)[:, None] * GQA_GROUP + jnp.arange(GQA_GROUP)[None, :])
        row_head = jnp.repeat(gt_head, t, axis=1).reshape(-1)
        o16, kwt, vwt = _swa_sample(q16, kv_bt[:, :, :D_KV], kv_bt[:, :, D_KV:], k_cache_t, v_cache_t,
                                    lw["slopes"][row_head][:, None], lw["sinks"][row_head][:, None])
        yb = o16.reshape(b, GQA_GROUP, t, N_KV_HEADS, HEAD_DIM).transpose(2, 0, 3, 1, 4)
        k_win = kwt.reshape(b, N_KV_HEADS, HEAD_DIM, WINDOW).transpose(0, 3, 1, 2)
        v_win = vwt.reshape(b, N_KV_HEADS, HEAD_DIM, WINDOW).transpose(0, 3, 1, 2)
        shift_new = proj_r[(t - 1) * b:, :R_COLS]

    h, hn = _merge_out(x2, ya.reshape(m, d), yb.reshape(m, d), proj_g, lw["w_out"], lw["norm_mlp_w"],
                       _pick(m, (256, 128, 8)))
    y = _mlp(hn, h, lw["w_up"], lw["w_down"], lw["norm_final_w"], _pick(m, (512, 256, 128, 8)), 1024)
    y = y.reshape(b, t, d) if prompt else y.reshape(t, b, d).transpose(1, 0, 2)
    return y, shift_new, wkv_new, k_win, v_win


def kernel(x_prompt, x_sample, state_shift, state_wkv, cache_k_win, cache_v_win, norm_mix_w, w_in, tshift_mu, w0, w_lora, a0, a_lora, g_lora, k_k, k_a, r_k, ln_x_w, ln_x_b, attn_sinks, w_out, norm_mlp_w, w_up, w_down, norm_final_w):
    depth = w_in.shape[0]
    assert depth == 1
    l = 0
    bp = x_prompt.shape[0]
    db = x_sample.shape[0]
    hh = jnp.arange(N_HEADS, dtype=F32)
    w_t = jnp.swapaxes(w_in[l], 0, 1)
    o_q, o_kv, o_g = R_COLS, R_COLS + D_MODEL, R_COLS + D_MODEL + 2 * D_KV
    lw = dict(
        norm_mix_w=norm_mix_w[l][None],
        w_r=_place_rows(w_t[:R_COLS], 0, R_PAD).astype(BF16), w_q=w_t[o_q:o_kv].astype(BF16),
        w_kv=w_t[o_kv:o_g].astype(BF16), w_g=w_t[o_g:].astype(BF16),
        mu=_pad_cols(tshift_mu[l][None], R_PAD), w0=w0[l][None], a0=a0[l][None], k_k=k_k[l][None],
        k_a=k_a[l][None], r_k=r_k[l].reshape(1, D_RWKV), ln_w=ln_x_w[l][None], ln_b=ln_x_b[l][None],
        w_lora=_place_rows(w_lora[l], 0, LANES).astype(BF16),
        a_lora=_place_rows(a_lora[l], LORA_DECAY_END, LORA_A_WIN).astype(BF16),
        g_lora=_place_rows(g_lora[l], LORA_A_END - LORA_G_START, LORA_W - LORA_G_START).astype(BF16),
        slopes=jnp.exp2(-8.0 * (hh + 1.0) / N_HEADS), sinks=attn_sinks[l].astype(F32),
        w_out=w_out[l].astype(BF16), norm_mlp_w=norm_mlp_w[l][None],
        w_up=w_up[l].astype(BF16), w_down=w_down[l].astype(BF16), norm_final_w=norm_final_w[None],
    )
    yp, sp, wp, kp, vp = _layer(
        x_prompt, jnp.zeros((bp, R_COLS), F32), jnp.zeros((bp, N_HEADS, HEAD_DIM, HEAD_DIM), F32), lw, prompt=True)
    kct = cache_k_win[l].transpose(0, 2, 3, 1).reshape(db, D_KV, WINDOW)
    vct = cache_v_win[l].transpose(0, 2, 3, 1).reshape(db, D_KV, WINDOW)
    ys, ss, ws, ksm, vsm = _layer(x_sample, state_shift[l], state_wkv[l], lw, prompt=False,
                                  k_cache_t=kct, v_cache_t=vct)
    return (yp, ys, sp[None], wp[None], kp[None], vp[None], ss[None], ws[None], ksm[None], vsm[None])
```

```python
import functools
import math

import jax
import jax.numpy as jnp
from jax import lax
from jax.experimental import pallas as pl
from jax.experimental.pallas import tpu as pltpu

F32 = jnp.float32
BF16 = jnp.bfloat16

D_MODEL = 2048
HEAD_DIM = 64
N_HEADS = D_MODEL // HEAD_DIM
N_KV_HEADS = 8
GQA_GROUP = N_HEADS // N_KV_HEADS
D_KV = N_KV_HEADS * HEAD_DIM
WINDOW = 128
D_FF = 4 * D_MODEL
D_DECAY_LORA = 96
D_A_LORA = 96
D_GATE_LORA = 256
D_RWKV = D_MODEL
R_COLS = 3 * D_RWKV + D_DECAY_LORA + D_A_LORA + D_GATE_LORA
C_IN = R_COLS + D_MODEL + 2 * D_KV + 2 * D_MODEL
RMS_EPS = 1e-5
GN_EPS = 64e-5

LANES = 128
PROJ_TILE = 512
R_PAD = -(-R_COLS // PROJ_TILE) * PROJ_TILE
LORA_W = R_PAD - 3 * D_RWKV
LORA_DECAY_END = D_DECAY_LORA
LORA_A_END = D_DECAY_LORA + D_A_LORA
LORA_G_END = LORA_A_END + D_GATE_LORA
LORA_A_WIN = -(-LORA_A_END // LANES) * LANES
LORA_G_START = (LORA_A_END // LANES) * LANES

VMEM_LIMIT = 56 * 1024 * 1024
KV_PER_GROUP = 2


def _cparams(sem):
    return pltpu.CompilerParams(dimension_semantics=sem, vmem_limit_bytes=VMEM_LIMIT)


def _dot(a, b):
    return jnp.dot(a, b, preferred_element_type=F32)


def _dot_nt(a, b):
    return lax.dot_general(a, b, (((1,), (1,)), ((), ())), preferred_element_type=F32)


def _dot_tn(a, b):
    return lax.dot_general(a, b, (((0,), (0,)), ((), ())), preferred_element_type=F32)


def _softplus(x):
    return jnp.maximum(x, 0.0) + jnp.log(1.0 + jnp.exp(-jnp.abs(x)))


def _sigmoid(x):
    return 1.0 / (1.0 + jnp.exp(-x))


def _split2(x):
    hi = x.astype(BF16)
    lo = (x - hi.astype(F32)).astype(BF16)
    return hi, lo


def _split3(x):
    hi = x.astype(BF16)
    r1 = x - hi.astype(F32)
    mid = r1.astype(BF16)
    lo = (r1 - mid.astype(F32)).astype(BF16)
    return hi, mid, lo


def _in_proj_kernel(x_ref, nw_ref, *refs, bounds):
    n = len(bounds)
    w_refs, o_refs, xn_ref = refs[:n], refs[n:2 * n], refs[2 * n]
    j = pl.program_id(1)

    @pl.when(j == 0)
    def _():
        x = x_ref[...]
        ms = jnp.mean(x * x, axis=-1, keepdims=True)
        xn_ref[...] = (x * lax.rsqrt(ms + RMS_EPS) * nw_ref[...]).astype(BF16)

    for w_ref, o_ref, (lo, hi) in zip(w_refs, o_refs, bounds):
        @pl.when((j >= lo) & (j < hi))
        def _(w_ref=w_ref, o_ref=o_ref):
            o_ref[...] = _dot_nt(xn_ref[...], w_ref[...]).astype(o_ref.dtype)


def _in_proj(x, nw, weights, out_dtypes, tm):
    m, d = x.shape
    tn = PROJ_TILE
    bounds, lo = [], 0
    for w in weights:
        bounds.append((lo, lo + w.shape[0] // tn))
        lo = bounds[-1][1]

    def clamp(lo_, hi_):
        return lambda j: jnp.clip(j - lo_, 0, hi_ - lo_ - 1)

    in_specs = [pl.BlockSpec((tm, d), lambda i, j: (i, 0)), pl.BlockSpec((1, d), lambda i, j: (0, 0))]
    out_specs, out_shape = [], []
    for w, dt, (lo_, hi_) in zip(weights, out_dtypes, bounds):
        c = clamp(lo_, hi_)
        in_specs.append(pl.BlockSpec((tn, d), lambda i, j, c=c: (c(j), 0)))
        out_specs.append(pl.BlockSpec((tm, tn), lambda i, j, c=c: (i, c(j))))
        out_shape.append(jax.ShapeDtypeStruct((m, w.shape[0]), dt))
    return pl.pallas_call(
        functools.partial(_in_proj_kernel, bounds=tuple(bounds)),
        grid=(m // tm, lo),
        in_specs=in_specs,
        out_specs=out_specs,
        out_shape=out_shape,
        scratch_shapes=[pltpu.VMEM((tm, d), BF16)],
        compiler_params=_cparams(("parallel", "arbitrary")),
        name="norm_in_proj",
    )(x, nw, *weights)


def _rwkv_kernel_unpipelined(pr_ref, pk_ref, pv_ref, plo_ref, shr_ref, shk_ref, shv_ref, shl_ref, s0_ref,
                 mur_ref, muk_ref, muv_ref, mul_ref, w0_ref, a0_ref, kk_ref, ka_ref, rk_ref, lnw_ref, lnb_ref,
                 wl_ref, al_ref, gl_ref,
                 y_ref, so_ref,
                 sd_ref, br_ref, bk_ref, bv_ref, bl_ref, *, L, Lv, HPS, BB, NC, n_dbl):
    c = pl.program_id(2)
    H = HEAD_DIM
    W = HPS * LANES
    sls = [slice(i * LANES, (i + 1) * LANES) for i in range(HPS)]

    lane = lax.broadcasted_iota(jnp.int32, (L, LANES), 1)
    head1 = lane < H
    row_i = lax.broadcasted_iota(jnp.int32, (L, 2 * L), 0)
    col_i = lax.broadcasted_iota(jnp.int32, (L, 2 * L), 1)
    col_t = jnp.where(col_i >= L, col_i - L, col_i)
    strict = col_t < row_i
    incl = col_t <= row_i
    colh1 = col_i < L
    tri = (lax.broadcasted_iota(jnp.int32, (L, L), 1) <= lax.broadcasted_iota(jnp.int32, (L, L), 0)).astype(BF16)
    ji = lax.broadcasted_iota(jnp.int32, (LANES, LANES), 0)
    jj = lax.broadcasted_iota(jnp.int32, (LANES, LANES), 1)
    same_head = (ji < H) == (jj < H)
    seg = same_head.astype(BF16)
    rowv = lax.broadcasted_iota(jnp.int32, (L, W), 0) < Lv

    def seg_sum(x):
        xs = jnp.concatenate([x[:, s] for s in sls], axis=0)
        hi, lo = _split2(xs)
        ys = _dot(hi, seg) + _dot(lo, seg)
        return jnp.concatenate([ys[i * L:(i + 1) * L] for i in range(HPS)], axis=1)

    def stack_heads(x, m):
        zero = jnp.zeros_like(x)
        return jnp.concatenate([jnp.where(m, x, zero), jnp.where(m, zero, x)], axis=0)

    def init(bi):
        br_ref[bi, 7:8, :] = shr_ref[bi]
        bk_ref[bi, 7:8, :] = shk_ref[bi]
        bv_ref[bi, 7:8, :] = shv_ref[bi]
        bl_ref[bi, 7:8, :] = shl_ref[bi]
        if Lv < L:
            for ref in (br_ref, bk_ref, bv_ref, bl_ref):
                ref[bi, 8 + Lv:8 + L, :] = jnp.zeros((L - Lv, ref.shape[2]), F32)
        z = jnp.zeros((H, H), F32)
        for i in range(HPS):
            top = jnp.concatenate([s0_ref[bi, 2 * i], z], axis=1)
            bot = jnp.concatenate([z, s0_ref[bi, 2 * i + 1]], axis=1)
            sd_ref[bi, i] = jnp.concatenate([top, bot], axis=0)

    def finish(bi):
        for i in range(HPS):
            sd = sd_ref[bi, i]
            so_ref[bi, 2 * i] = sd[0:H, 0:H]
            so_ref[bi, 2 * i + 1] = sd[H:2 * H, H:2 * H]

    def one_batch(bi):
        if NC == 1:
            init(bi)
        else:
            pl.when(c == 0)(lambda: init(bi))

        br_ref[bi, 8:8 + Lv, :] = pr_ref[bi]
        bk_ref[bi, 8:8 + Lv, :] = pk_ref[bi]
        bv_ref[bi, 8:8 + Lv, :] = pv_ref[bi]
        bl_ref[bi, 8:8 + Lv, :] = plo_ref[bi]

        def shifted(buf, mu):
            p = buf[bi, 8:8 + L, :]
            return p + mu * (buf[bi, 7:7 + L, :] - p)

        ps_l = shifted(bl_ref, mul_ref[...])
        td = jnp.tanh(ps_l[:, 0:LANES]).astype(BF16)
        da = ps_l[:, 0:LORA_A_WIN].astype(BF16)
        sg = _sigmoid(ps_l[:, LORA_G_START:]).astype(BF16)
        r = shifted(br_ref, mur_ref[...])
        k = shifted(bk_ref, muk_ref[...])
        v = shifted(bv_ref, muv_ref[...])

        zlog = w0_ref[...] + _dot(td, wl_ref[...])
        logw = -jnp.exp(-_softplus(-zlog) - 0.5)
        a_sig = _sigmoid(a0_ref[...] + _dot(da, al_ref[...]))
        kk = k * kk_ref[...]
        nrm = jnp.sqrt(seg_sum(kk * kk))
        kk = kk / jnp.maximum(nrm, 1e-12)
        k_h = k * (1.0 + (a_sig - 1.0) * ka_ref[...])
        if Lv < L:
            logw = jnp.where(rowv, logw, 0.0)
            kk = jnp.where(rowv, kk, 0.0)
            k_h = jnp.where(rowv, k_h, 0.0)
            v = jnp.where(rowv, v, 0.0)

        hi, mid, lo = _split3(logw)
        cum = _dot(tri, hi) + _dot(tri, mid) + _dot(tri, lo)
        cum_l = cum[L - 1:L, :]
        p_inv = jnp.exp(-cum)
        p_end = jnp.exp(cum_l - cum)
        bvec = kk * a_sig
        a_t = (-kk * jnp.exp(cum - logw)).astype(BF16)
        r_t = (r * jnp.exp(cum)).astype(BF16)
        b_t = (bvec * p_inv).astype(BF16)
        k_t = (k_h * p_inv).astype(BF16)
        b_e = (bvec * p_end).astype(BF16)
        k_e = (k_h * p_end).astype(BF16)
        v_b = v.astype(BF16)
        p_l = jnp.exp(cum_l)

        P = range(HPS)
        lhs = [jnp.concatenate([a_t[:, s], r_t[:, s]], axis=0) for s in sls]
        rhs = [jnp.concatenate([stack_heads(b_t[:, s], head1), stack_heads(k_t[:, s], head1)], axis=0) for s in sls]
        sd = [sd_ref[bi, i] for i in P]
        aa = [_dot_nt(lhs[i], rhs[i]) for i in P]
        sa = [_dot_nt(lhs[i], sd[i].astype(BF16)) for i in P]
        v_st = [stack_heads(v_b[:, s], head1) for s in sls]
        a_ak = [jnp.where(strict, aa[i][0:L, 2 * L:4 * L], 0.0).astype(BF16) for i in P]
        x = [sa[i][0:L] + _dot(a_ak[i], v_st[i]) for i in P]
        ap = [jnp.where(strict, aa[i][0:L, 0:2 * L], 0.0).astype(BF16) for i in P]
        for d in range(n_dbl):
            x = [x[i] + _dot(ap[i], stack_heads(x[i].astype(BF16), head1)) for i in P]
            if d + 1 < n_dbl:
                ap = [_dot(ap[i], stack_heads(ap[i], colh1)).astype(BF16) for i in P]
        u_b = [x[i].astype(BF16) for i in P]
        a_rb = [jnp.where(incl, aa[i][L:2 * L, 0:2 * L], 0.0).astype(BF16) for i in P]
        a_rk = [jnp.where(incl, aa[i][L:2 * L, 2 * L:4 * L], 0.0).astype(BF16) for i in P]
        y = [sa[i][L:2 * L] + _dot(a_rb[i], stack_heads(u_b[i], head1)) + _dot(a_rk[i], v_st[i]) for i in P]
        ds = [_dot_tn(jnp.concatenate([u_b[i], v_b[:, sls[i]]], axis=0),
                      jnp.concatenate([b_e[:, sls[i]], k_e[:, sls[i]]], axis=0)) for i in P]
        for i in P:
            sd_ref[bi, i] = sd[i] * p_l[:, sls[i]] + jnp.where(same_head, ds[i], 0.0)

        y = jnp.concatenate(y, axis=1)
        mean = seg_sum(y) * (1.0 / H)
        dlt = y - mean
        var = seg_sum(dlt * dlt) * (1.0 / H)
        yn = dlt * lax.rsqrt(var + GN_EPS) * lnw_ref[...] + lnb_ref[...]
        bonus = seg_sum(r * k_h * rk_ref[...]) * v
        g = _dot(sg, gl_ref[...])
        out = (yn + bonus) * g
        y_ref[bi] = out[0:Lv].astype(y_ref.dtype)

        br_ref[bi, 7:8, :] = br_ref[bi, 7 + Lv:8 + Lv, :]
        bk_ref[bi, 7:8, :] = bk_ref[bi, 7 + Lv:8 + Lv, :]
        bv_ref[bi, 7:8, :] = bv_ref[bi, 7 + Lv:8 + Lv, :]
        bl_ref[bi, 7:8, :] = bl_ref[bi, 7 + Lv:8 + Lv, :]

        if NC == 1:
            finish(bi)
        else:
            pl.when(c == NC - 1)(lambda: finish(bi))

    if BB == 1:
        one_batch(0)
    else:
        def loop_body(bi, carry):
            one_batch(bi)
            return carry
        lax.fori_loop(0, BB, loop_body, 0)


def _rwkv_mix_unpipelined(proj3, shift3, s0, p, *, L, Lv, HPS, BB, y_dtype):
    b, t, _ = proj3.shape
    assert t % Lv == 0 and b % BB == 0
    nc = t // Lv
    w = HPS * LANES
    nhg = D_RWKV // w
    kb = D_RWKV // w
    lora_blk = 3 * D_RWKV // LORA_W
    n_dbl = max(1, math.ceil(math.log2(Lv)))
    col = lambda o: (lambda bi, g, c: (bi, c, o + g))
    sh = lambda o: (lambda bi, g, c: (bi, 0, o + g))
    par = lambda o: (lambda bi, g, c: (0, o + g))
    in_specs = [
        pl.BlockSpec((BB, Lv, w), col(0)), pl.BlockSpec((BB, Lv, w), col(kb)), pl.BlockSpec((BB, Lv, w), col(2 * kb)),
        pl.BlockSpec((BB, Lv, LORA_W), lambda bi, g, c: (bi, c, lora_blk)),
        pl.BlockSpec((BB, 1, w), sh(0)), pl.BlockSpec((BB, 1, w), sh(kb)), pl.BlockSpec((BB, 1, w), sh(2 * kb)),
        pl.BlockSpec((BB, 1, LORA_W), lambda bi, g, c: (bi, 0, lora_blk)),
        pl.BlockSpec((BB, 2 * HPS, HEAD_DIM, HEAD_DIM), lambda bi, g, c: (bi, g, 0, 0)),
        pl.BlockSpec((1, w), par(0)), pl.BlockSpec((1, w), par(kb)), pl.BlockSpec((1, w), par(2 * kb)),
        pl.BlockSpec((1, LORA_W), lambda bi, g, c: (0, lora_blk)),
    ] + [pl.BlockSpec((1, w), par(0))] * 7 + [
        pl.BlockSpec((LANES, w), par(0)), pl.BlockSpec((LORA_A_WIN, w), par(0)),
        pl.BlockSpec((LORA_W - LORA_G_START, w), par(0)),
    ]
    out_specs = [pl.BlockSpec((BB, Lv, w), col(0)),
                 pl.BlockSpec((BB, 2 * HPS, HEAD_DIM, HEAD_DIM), lambda bi, g, c: (bi, g, 0, 0))]
    kern = functools.partial(_rwkv_kernel, L=L, Lv=Lv, HPS=HPS, BB=BB, NC=nc, n_dbl=n_dbl)
    return pl.pallas_call(
        kern,
        grid=(b // BB, nhg, nc),
        in_specs=in_specs,
        out_specs=out_specs,
        out_shape=[jax.ShapeDtypeStruct((b, t, D_RWKV), y_dtype),
                   jax.ShapeDtypeStruct((b, N_HEADS, HEAD_DIM, HEAD_DIM), F32)],
        scratch_shapes=[pltpu.VMEM((BB, HPS, LANES, LANES), F32),
                        pltpu.VMEM((BB, L + 8, w), F32), pltpu.VMEM((BB, L + 8, w), F32),
                        pltpu.VMEM((BB, L + 8, w), F32), pltpu.VMEM((BB, L + 8, LORA_W), F32)],
        compiler_params=_cparams(("parallel", "parallel", "arbitrary")),
        name="rwkv_mix",
    )(proj3, proj3, proj3, proj3, shift3, shift3, shift3, shift3, s0,
      p["mu"], p["mu"], p["mu"], p["mu"], p["w0"], p["a0"], p["k_k"], p["k_a"], p["r_k"],
      p["ln_w"], p["ln_b"], p["w_lora"], p["a_lora"], p["g_lora"])


def _rwkv_kernel(pr_ref, pk_ref, pv_ref, plo_ref, shr_ref, shk_ref, shv_ref, shl_ref, s0_ref,
                 mur_ref, muk_ref, muv_ref, mul_ref, w0_ref, a0_ref, kk_ref, ka_ref, rk_ref, lnw_ref, lnb_ref,
                 wl_ref, al_ref, gl_ref,
                 y_ref, so_ref,
                 sd_ref, br_ref, bk_ref, bv_ref, bl_ref, *stage_refs, L, Lv, HPS, BB, NC, n_dbl, pipelined):
    c = pl.program_id(2)
    H = HEAD_DIM
    W = HPS * LANES
    sls = [slice(i * LANES, (i + 1) * LANES) for i in range(HPS)]
    P = range(HPS)
    STAGED = ("a_t", "r_t", "b_t", "k_t", "b_e", "k_e", "v_b", "g", "bg", "p_l")

    lane = lax.broadcasted_iota(jnp.int32, (L, LANES), 1)
    head1 = lane < H
    row_i = lax.broadcasted_iota(jnp.int32, (L, 2 * L), 0)
    col_i = lax.broadcasted_iota(jnp.int32, (L, 2 * L), 1)
    col_t = jnp.where(col_i >= L, col_i - L, col_i)
    strict = col_t < row_i
    incl = col_t <= row_i
    colh1 = col_i < L
    tri = (lax.broadcasted_iota(jnp.int32, (L, L), 1) <= lax.broadcasted_iota(jnp.int32, (L, L), 0)).astype(BF16)
    ji = lax.broadcasted_iota(jnp.int32, (LANES, LANES), 0)
    jj = lax.broadcasted_iota(jnp.int32, (LANES, LANES), 1)
    same_head = (ji < H) == (jj < H)
    seg = same_head.astype(BF16)
    rowv = lax.broadcasted_iota(jnp.int32, (L, W), 0) < Lv

    def seg_sum(x):
        n = x.shape[1] // LANES
        xs = jnp.concatenate([x[:, s] for s in sls[:n]], axis=0)
        ys = _dot(xs.astype(BF16), seg)
        return jnp.concatenate([ys[i * L:(i + 1) * L] for i in range(n)], axis=1)

    def stack_heads(x, m):
        zero = jnp.zeros_like(x)
        return jnp.concatenate([jnp.where(m, x, zero), jnp.where(m, zero, x)], axis=0)

    def init(bi):
        br_ref[bi, 7:8, :] = shr_ref[bi]
        bk_ref[bi, 7:8, :] = shk_ref[bi]
        bv_ref[bi, 7:8, :] = shv_ref[bi]
        bl_ref[bi, 7:8, :] = shl_ref[bi]
        if Lv < L:
            for ref in (br_ref, bk_ref, bv_ref, bl_ref):
                ref[bi, 8 + Lv:8 + L, :] = jnp.zeros((L - Lv, ref.shape[2]), F32)
        z = jnp.zeros((H, H), F32)
        for i in P:
            top = jnp.concatenate([s0_ref[bi, 2 * i], z], axis=1)
            bot = jnp.concatenate([z, s0_ref[bi, 2 * i + 1]], axis=1)
            sd_ref[bi, i] = jnp.concatenate([top, bot], axis=0)

    def finish(bi):
        for i in P:
            sd = sd_ref[bi, i]
            so_ref[bi, 2 * i] = sd[0:H, 0:H]
            so_ref[bi, 2 * i + 1] = sd[H:2 * H, H:2 * H]

    def stage_inputs(bi):
        br_ref[bi, 8:8 + Lv, :] = pr_ref[bi]
        bk_ref[bi, 8:8 + Lv, :] = pk_ref[bi]
        bv_ref[bi, 8:8 + Lv, :] = pv_ref[bi]
        bl_ref[bi, 8:8 + Lv, :] = plo_ref[bi]

    def carry_rows(bi):
        br_ref[bi, 7:8, :] = br_ref[bi, 7 + Lv:8 + Lv, :]
        bk_ref[bi, 7:8, :] = bk_ref[bi, 7 + Lv:8 + Lv, :]
        bv_ref[bi, 7:8, :] = bv_ref[bi, 7 + Lv:8 + Lv, :]
        bl_ref[bi, 7:8, :] = bl_ref[bi, 7 + Lv:8 + Lv, :]

    def lora_inputs(bi):
        p = bl_ref[bi, 8:8 + L, :]
        ps_l = p + mul_ref[...] * (bl_ref[bi, 7:7 + L, :] - p)
        td = (1.0 - 2.0 / (1.0 + jnp.exp(2.0 * ps_l[:, 0:LANES]))).astype(BF16)
        da = ps_l[:, 0:LORA_A_WIN].astype(BF16)
        sg = _sigmoid(ps_l[:, LORA_G_START:]).astype(BF16)
        return td, da, sg

    def prologue(bi, lora, cols):
        td, da, sg = lora

        def shifted(buf, mu_ref):
            p = buf[bi, 8:8 + L, cols]
            return p + mu_ref[:, cols] * (buf[bi, 7:7 + L, cols] - p)

        r = shifted(br_ref, mur_ref)
        k = shifted(bk_ref, muk_ref)
        v = shifted(bv_ref, muv_ref)
        zlog = w0_ref[:, cols] + _dot(td, wl_ref[:, cols])
        logw = -math.exp(-0.5) / (1.0 + jnp.exp(-zlog))
        a_sig = _sigmoid(a0_ref[:, cols] + _dot(da, al_ref[:, cols]))
        kk = k * kk_ref[:, cols]
        kk = kk * jnp.minimum(lax.rsqrt(jnp.maximum(seg_sum(kk * kk), 0.0)), 1e12)
        k_h = k * (1.0 + (a_sig - 1.0) * ka_ref[:, cols])
        if Lv < L:
            valid = rowv[:, 0:r.shape[1]]
            logw = jnp.where(valid, logw, 0.0)
            kk = jnp.where(valid, kk, 0.0)
            k_h = jnp.where(valid, k_h, 0.0)
            v = jnp.where(valid, v, 0.0)

        hi, lo = _split2(logw)
        cum = _dot(tri, hi) + _dot(tri, lo)
        cum_l = cum[L - 1:L, :]
        p_inv = jnp.exp(-cum)
        p_end = jnp.exp(cum_l - cum)
        bvec = kk * a_sig
        g = _dot(sg, gl_ref[:, cols])
        return dict(
            a_t=(-kk * jnp.exp(cum - logw)).astype(BF16), r_t=(r * jnp.exp(cum)).astype(BF16),
            b_t=(bvec * p_inv).astype(BF16), k_t=(k_h * p_inv).astype(BF16),
            b_e=(bvec * p_end).astype(BF16), k_e=(k_h * p_end).astype(BF16), v_b=v.astype(BF16),
            g=g, bg=seg_sum(r * k_h * rk_ref[:, cols]) * v * g, p_l=jnp.exp(cum_l))

    def prologue_items(bi, tiles):
        S = {}

        def load_shift():
            S["lora"] = lora_inputs(bi)
            for name, buf, mu_ref in (("r", br_ref, mur_ref), ("k", bk_ref, muk_ref), ("v", bv_ref, muv_ref)):
                p = buf[bi, 8:8 + L, :]
                S[name] = p + mu_ref[...] * (buf[bi, 7:7 + L, :] - p)
            S["kk"] = S["k"] * kk_ref[...]

        def matmuls_1():
            td, da, sg = S["lora"]
            S["zlog"] = w0_ref[...] + _dot(td, wl_ref[...])
            S["apre"] = a0_ref[...] + _dot(da, al_ref[...])
            S["g"] = _dot(sg, gl_ref[...])
            S["n2"] = seg_sum(S["kk"] * S["kk"])

        def vector_1(i):
            s = sls[i]
            logw = -math.exp(-0.5) / (1.0 + jnp.exp(-S["zlog"][:, s]))
            a_sig = _sigmoid(S["apre"][:, s])
            kk = S["kk"][:, s] * jnp.minimum(lax.rsqrt(jnp.maximum(S["n2"][:, s], 0.0)), 1e12)
            k_h = S["k"][:, s] * (1.0 + (a_sig - 1.0) * ka_ref[:, s])
            hi, lo = _split2(logw)
            S[("v1", i)] = dict(logw=logw, bvec=kk * a_sig, kk=kk, k_h=k_h, hi=hi, lo=lo,
                                prod=(S["r"][:, s] * k_h * rk_ref[:, s]).astype(BF16))

        def matmuls_2():
            cat = lambda n: jnp.concatenate([S[("v1", i)][n] for i in P], axis=1)
            S["cum"] = _dot(tri, cat("hi")) + _dot(tri, cat("lo"))
            xs = jnp.concatenate([S[("v1", i)]["prod"] for i in P], axis=0)
            S["bsum"] = _dot(xs, seg)

        def vector_2(i):
            s = sls[i]
            t1 = S[("v1", i)]
            cum = S["cum"][:, s]
            cum_l = cum[L - 1:L, :]
            p_inv = jnp.exp(-cum)
            p_end = jnp.exp(cum_l - cum)
            v = S["v"][:, s]
            g = S["g"][:, s]
            tiles[i] = dict(
                a_t=(-t1["kk"] * jnp.exp(cum - t1["logw"])).astype(BF16),
                r_t=(S["r"][:, s] * jnp.exp(cum)).astype(BF16),
                b_t=(t1["bvec"] * p_inv).astype(BF16), k_t=(t1["k_h"] * p_inv).astype(BF16),
                b_e=(t1["bvec"] * p_end).astype(BF16), k_e=(t1["k_h"] * p_end).astype(BF16),
                v_b=v.astype(BF16), g=g, bg=S["bsum"][i * L:(i + 1) * L] * v * g, p_l=jnp.exp(cum_l))

        return ([load_shift, matmuls_1] + [functools.partial(vector_1, i) for i in P] + [matmuls_2]
                + [functools.partial(vector_2, i) for i in P])

    def chain_stages(t, sd, lnw, lnb, res):
        a_t, r_t, b_t, k_t, b_e, k_e, v_b = (t[n] for n in STAGED[:7])
        lhs = [jnp.concatenate([a_t[:, s], r_t[:, s]], axis=0) for s in sls]
        rhs = [jnp.concatenate([stack_heads(b_t[:, s], head1), stack_heads(k_t[:, s], head1)], axis=0) for s in sls]
        aa = [_dot_nt(lhs[i], rhs[i]) for i in P]
        yield
        sa = [_dot_nt(lhs[i], sd[i].astype(BF16)) for i in P]
        yield
        v_st = [stack_heads(v_b[:, s], head1) for s in sls]
        a_ak = [jnp.where(strict, aa[i][0:L, 2 * L:4 * L], 0.0).astype(BF16) for i in P]
        x = [sa[i][0:L] + _dot(a_ak[i], v_st[i]) for i in P]
        ap = [jnp.where(strict, aa[i][0:L, 0:2 * L], 0.0).astype(BF16) for i in P]
        yield
        for d in range(n_dbl):
            xs = [stack_heads(x[i].astype(BF16), head1) for i in P]
            if d + 1 < n_dbl:
                both = [_dot(ap[i], jnp.concatenate([xs[i], stack_heads(ap[i], colh1)], axis=1)) for i in P]
                x = [x[i] + both[i][:, 0:LANES] for i in P]
                ap = [both[i][:, LANES:].astype(BF16) for i in P]
            else:
                x = [x[i] + _dot(ap[i], xs[i]) for i in P]
            yield
        u_b = [x[i].astype(BF16) for i in P]
        a_rb = [jnp.where(incl, aa[i][L:2 * L, 0:2 * L], 0.0).astype(BF16) for i in P]
        a_rk = [jnp.where(incl, aa[i][L:2 * L, 2 * L:4 * L], 0.0).astype(BF16) for i in P]
        y = [sa[i][L:2 * L] + _dot(jnp.concatenate([a_rb[i], a_rk[i]], axis=1),
                                   jnp.concatenate([stack_heads(u_b[i], head1), v_st[i]], axis=0)) for i in P]
        yield
        ds = [_dot_tn(jnp.concatenate([u_b[i], v_b[:, sls[i]]], axis=0),
                      jnp.concatenate([b_e[:, sls[i]], k_e[:, sls[i]]], axis=0)) for i in P]
        res["sd"] = [sd[i] * t["p_l"][:, sls[i]] + jnp.where(same_head, ds[i], 0.0) for i in P]
        yield
        y = jnp.concatenate(y, axis=1)
        mean = seg_sum(y) * (1.0 / H)
        yield
        dlt = y - mean
        var = seg_sum(dlt * dlt) * (1.0 / H)
        yield
        yn = dlt * lax.rsqrt(var + GN_EPS) * lnw + lnb
        res["out"] = yn * t["g"] + t["bg"]

    def one_batch(bi):
        if NC == 1:
            init(bi)
        else:
            pl.when(c == 0)(lambda: init(bi))
        stage_inputs(bi)
        t = prologue(bi, lora_inputs(bi), slice(None))
        res = {}
        for _ in chain_stages(t, [sd_ref[bi, i] for i in P], lnw_ref[...], lnb_ref[...], res):
            pass
        for i in P:
            sd_ref[bi, i] = res["sd"][i]
        y_ref[bi] = res["out"][0:Lv].astype(y_ref.dtype)
        carry_rows(bi)
        if NC == 1:
            finish(bi)
        else:
            pl.when(c == NC - 1)(lambda: finish(bi))

    def one_batch_pipelined():
        st = dict(zip(STAGED, stage_refs))

        @pl.when(c == 0)
        def _():
            init(0)
            for ref in stage_refs:
                ref[...] = jnp.zeros(ref.shape, ref.dtype)

        prev = {n: st[n][...] for n in STAGED}
        sd = [sd_ref[0, i] for i in P]
        lnw, lnb = lnw_ref[...], lnb_ref[...]
        stage_inputs(0)
        res, tiles = {}, [None] * HPS
        items = prologue_items(0, tiles)
        quota = iter([2] + [HPS // 2] * 2 + [1] + [HPS // 8] * 8)
        for _ in chain_stages(prev, sd, lnw, lnb, res):
            for _ in range(next(quota, 0)):
                if items:
                    items.pop(0)()
        while items:
            items.pop(0)()
        for n in STAGED:
            for i in P:
                st[n][:, sls[i]] = tiles[i][n]
        live = c > 0
        for i in P:
            sd_ref[0, i] = jnp.where(live, res["sd"][i], sd[i])
        y_ref[0] = res["out"].astype(y_ref.dtype)
        carry_rows(0)
        pl.when(c == NC)(lambda: finish(0))

    if pipelined:
        one_batch_pipelined()
    elif BB == 1:
        one_batch(0)
    else:
        def loop_body(bi, carry):
            one_batch(bi)
            return carry
        lax.fori_loop(0, BB, loop_body, 0)


def _rwkv_mix(proj3, shift3, s0, p, *, L, Lv, HPS, BB, y_dtype, pipelined):
    b, t, _ = proj3.shape
    assert t % Lv == 0 and b % BB == 0
    assert not pipelined or (BB == 1 and Lv == L)
    nc = t // Lv
    w = HPS * LANES
    nhg = D_RWKV // w
    kb = D_RWKV // w
    lora_blk = 3 * D_RWKV // LORA_W
    n_dbl = max(1, math.ceil(math.log2(Lv)))
    if pipelined:
        cin = lambda c: jnp.minimum(c, nc - 1)
        cout = lambda c: jnp.maximum(c - 1, 0)
    else:
        cin = cout = lambda c: c
    col = lambda o: (lambda bi, g, c: (bi, cin(c), o + g))
    sh = lambda o: (lambda bi, g, c: (bi, 0, o + g))
    par = lambda o: (lambda bi, g, c: (0, o + g))
    in_specs = [
        pl.BlockSpec((BB, Lv, w), col(0)), pl.BlockSpec((BB, Lv, w), col(kb)), pl.BlockSpec((BB, Lv, w), col(2 * kb)),
        pl.BlockSpec((BB, Lv, LORA_W), lambda bi, g, c: (bi, cin(c), lora_blk)),
        pl.BlockSpec((BB, 1, w), sh(0)), pl.BlockSpec((BB, 1, w), sh(kb)), pl.BlockSpec((BB, 1, w), sh(2 * kb)),
        pl.BlockSpec((BB, 1, LORA_W), lambda bi, g, c: (bi, 0, lora_blk)),
        pl.BlockSpec((BB, 2 * HPS, HEAD_DIM, HEAD_DIM), lambda bi, g, c: (bi, g, 0, 0)),
        pl.BlockSpec((1, w), par(0)), pl.BlockSpec((1, w), par(kb)), pl.BlockSpec((1, w), par(2 * kb)),
        pl.BlockSpec((1, LORA_W), lambda bi, g, c: (0, lora_blk)),
    ] + [pl.BlockSpec((1, w), par(0))] * 7 + [
        pl.BlockSpec((LANES, w), par(0)), pl.BlockSpec((LORA_A_WIN, w), par(0)),
        pl.BlockSpec((LORA_W - LORA_G_START, w), par(0)),
    ]
    out_specs = [pl.BlockSpec((BB, Lv, w), lambda bi, g, c: (bi, cout(c), g)),
                 pl.BlockSpec((BB, 2 * HPS, HEAD_DIM, HEAD_DIM), lambda bi, g, c: (bi, g, 0, 0))]
    scratch = [pltpu.VMEM((BB, HPS, LANES, LANES), F32),
               pltpu.VMEM((BB, L + 8, w), F32), pltpu.VMEM((BB, L + 8, w), F32),
               pltpu.VMEM((BB, L + 8, w), F32), pltpu.VMEM((BB, L + 8, LORA_W), F32)]
    if pipelined:
        scratch += [pltpu.VMEM((L, w), BF16)] * 7 + [pltpu.VMEM((L, w), F32)] * 2 + [pltpu.VMEM((1, w), F32)]
    kern = functools.partial(_rwkv_kernel, L=L, Lv=Lv, HPS=HPS, BB=BB, NC=nc, n_dbl=n_dbl, pipelined=pipelined)
    return pl.pallas_call(
        kern,
        grid=(b // BB, nhg, nc + 1 if pipelined else nc),
        in_specs=in_specs,
        out_specs=out_specs,
        out_shape=[jax.ShapeDtypeStruct((b, t, D_RWKV), y_dtype),
                   jax.ShapeDtypeStruct((b, N_HEADS, HEAD_DIM, HEAD_DIM), F32)],
        scratch_shapes=scratch,
        compiler_params=_cparams(("parallel", "parallel", "arbitrary")),
        name="rwkv_mix",
    )(proj3, proj3, proj3, proj3, shift3, shift3, shift3, shift3, s0,
      p["mu"], p["mu"], p["mu"], p["mu"], p["w0"], p["a0"], p["k_k"], p["k_a"], p["r_k"],
      p["ln_w"], p["ln_b"], p["w_lora"], p["a_lora"], p["g_lora"])


SAMPLE_PREP_W = 512
PREP_OUT = ("r", "w", "k", "v", "a", "b", "g", "bonus")


def _rwkv_sample_prep_kernel(pr_ref, pk_ref, pv_ref, plo_ref, shr_ref, shk_ref, shv_ref, shl_ref,
                             mur_ref, muk_ref, muv_ref, mul_ref, w0_ref, a0_ref, kk_ref, ka_ref, rk_ref,
                             wl_ref, al_ref, gl_ref, *out_refs, nb, nt):
    W = pr_ref.shape[1]
    n_tiles = W // LANES
    rows = nb * nt
    ji = lax.broadcasted_iota(jnp.int32, (LANES, LANES), 0)
    jj = lax.broadcasted_iota(jnp.int32, (LANES, LANES), 1)
    seg = ((ji < HEAD_DIM) == (jj < HEAD_DIM)).astype(BF16)

    def seg_sum(x):
        xs = jnp.concatenate([x[:, i * LANES:(i + 1) * LANES] for i in range(n_tiles)], axis=0).astype(BF16)
        ys = _dot(xs, seg)
        return jnp.concatenate([ys[i * rows:(i + 1) * rows] for i in range(n_tiles)], axis=1)

    def shifted(p_ref, s_ref, mu_ref):
        p = p_ref[...]
        prev = jnp.concatenate([s_ref[...], p[0:rows - nb]], axis=0)
        return p + mu_ref[...] * (prev - p)

    ps_l = shifted(plo_ref, shl_ref, mul_ref)
    td = (1.0 - 2.0 / (1.0 + jnp.exp(2.0 * ps_l[:, 0:LANES]))).astype(BF16)
    da = ps_l[:, 0:LORA_A_WIN].astype(BF16)
    sg = _sigmoid(ps_l[:, LORA_G_START:]).astype(BF16)
    r = shifted(pr_ref, shr_ref, mur_ref)
    k = shifted(pk_ref, shk_ref, muk_ref)
    v = shifted(pv_ref, shv_ref, muv_ref)
    zlog = w0_ref[...] + _dot(td, wl_ref[...])
    w = jnp.exp(-math.exp(-0.5) / (1.0 + jnp.exp(-zlog)))
    a_sig = _sigmoid(a0_ref[...] + _dot(da, al_ref[...]))
    kk = k * kk_ref[...]
    kk = kk * jnp.minimum(lax.rsqrt(jnp.maximum(seg_sum(kk * kk), 0.0)), 1e12)
    k_h = k * (1.0 + (a_sig - 1.0) * ka_ref[...])
    g = _dot(sg, gl_ref[...])
    bonus = seg_sum(r * k_h * rk_ref[...]) * v
    vals = dict(r=r, w=w, k=k_h, v=v, a=-kk, b=kk * a_sig, g=g, bonus=bonus)
    for name, o_ref in zip(PREP_OUT, out_refs):
        x = vals[name]
        for t in range(nt):
            for c in range(n_tiles):
                o_ref[t, c * LANES:(c + 1) * LANES, :] = x[t * nb:(t + 1) * nb, c * LANES:(c + 1) * LANES].T


def _rwkv_sample_prep(proj_r, shift, p, nb, nt):
    w = SAMPLE_PREP_W
    kb = D_RWKV // w
    lora_blk = 3 * D_RWKV // LORA_W
    rows = nt * nb
    col = lambda o: (lambda g: (0, o + g))
    in_specs = (
        [pl.BlockSpec((rows, w), col(0)), pl.BlockSpec((rows, w), col(kb)), pl.BlockSpec((rows, w), col(2 * kb)),
         pl.BlockSpec((rows, LORA_W), lambda g: (0, lora_blk))]
        + [pl.BlockSpec((nb, w), col(0)), pl.BlockSpec((nb, w), col(kb)), pl.BlockSpec((nb, w), col(2 * kb)),
           pl.BlockSpec((nb, LORA_W), lambda g: (0, lora_blk))]
        + [pl.BlockSpec((1, w), col(0)), pl.BlockSpec((1, w), col(kb)), pl.BlockSpec((1, w), col(2 * kb)),
           pl.BlockSpec((1, LORA_W), lambda g: (0, lora_blk))]
        + [pl.BlockSpec((1, w), col(0))] * 5
        + [pl.BlockSpec((LANES, w), col(0)), pl.BlockSpec((LORA_A_WIN, w), col(0)),
           pl.BlockSpec((LORA_W - LORA_G_START, w), col(0))])
    out_spec = pl.BlockSpec((nt, w, nb), lambda g: (0, g, 0))
    return pl.pallas_call(
        functools.partial(_rwkv_sample_prep_kernel, nb=nb, nt=nt),
        grid=(kb,),
        in_specs=in_specs,
        out_specs=[out_spec] * len(PREP_OUT),
        out_shape=[jax.ShapeDtypeStruct((nt, D_RWKV, nb), F32)] * len(PREP_OUT),
        compiler_params=_cparams(("parallel",)),
        name="rwkv_sample_prep",
    )(proj_r, proj_r, proj_r, proj_r, shift, shift, shift, shift, p["mu"], p["mu"], p["mu"], p["mu"],
      p["w0"], p["a0"], p["k_k"], p["k_a"], p["r_k"], p["w_lora"], p["a_lora"], p["g_lora"])


def _wkv_sample_kernel(s_ref, r_ref, w_ref, k_ref, v_ref, a_ref, b_ref, g_ref, bonus_ref, lnw_ref, lnb_ref,
                       so_ref, o_ref, y_ref, *, nt):
    H = HEAD_DIM
    SUB = 8
    PAR = 4
    rowid = lax.broadcasted_iota(jnp.int32, (SUB, LANES), 0)

    def body(i8, carry):
        base = pl.multiple_of(i8 * SUB, SUB)
        v8 = [v_ref[t, pl.ds(base, SUB), :] for t in range(nt)]
        y8 = [jnp.zeros((SUB, LANES), F32) for _ in range(nt)]
        for h0 in range(0, SUB, PAR):
            ids = list(range(h0, h0 + PAR))
            S = [s_ref[0, base + ii] for ii in ids]
            for t in range(nt):
                a, w, b, k, r = a_ref[t], w_ref[t], b_ref[t], k_ref[t], r_ref[t]
                for n, ii in enumerate(ids):
                    sa = jnp.sum(S[n] * a, axis=0, keepdims=True)
                    S[n] = S[n] * w + sa * b + v8[t][ii:ii + 1, :] * k
                    y = jnp.sum(S[n] * r, axis=0, keepdims=True)
                    y8[t] = jnp.where(rowid == ii, y, y8[t])
            for n, ii in enumerate(ids):
                so_ref[0, base + ii] = S[n]
        for t in range(nt):
            y_ref[t, pl.ds(base, SUB), :] = y8[t]
        return carry

    lax.fori_loop(0, H // SUB, body, 0)
    for t in range(nt):
        y = y_ref[t]
        mean = jnp.sum(y, axis=0, keepdims=True) * (1.0 / H)
        d = y - mean
        var = jnp.sum(d * d, axis=0, keepdims=True) * (1.0 / H)
        o_ref[t] = (d * lax.rsqrt(var + GN_EPS) * lnw_ref[...] + lnb_ref[...] + bonus_ref[t]) * g_ref[t]


def _wkv_sample(state_t, prep, lnw_b, lnb_b):
    nh, hd, _, nb = state_t.shape
    nt = prep[0].shape[0]
    st_spec = pl.BlockSpec((1, hd, hd, nb), lambda h: (h, 0, 0, 0))
    ch_spec = pl.BlockSpec((nt, hd, nb), lambda h: (0, h, 0))
    ln_spec = pl.BlockSpec((hd, nb), lambda h: (h, 0))
    return pl.pallas_call(
        functools.partial(_wkv_sample_kernel, nt=nt),
        grid=(nh,),
        in_specs=[st_spec] + [ch_spec] * len(PREP_OUT) + [ln_spec, ln_spec],
        out_specs=[st_spec, ch_spec],
        out_shape=[jax.ShapeDtypeStruct(state_t.shape, F32), jax.ShapeDtypeStruct((nt, nh * hd, nb), F32)],
        scratch_shapes=[pltpu.VMEM((nt, hd, nb), F32)],
        compiler_params=_cparams(("parallel",)),
        name="wkv_sample",
    )(state_t, *prep, lnw_b, lnb_b)


def _swa_prompt_kernel(slope_ref, sink_ref, q_ref, kc_ref, kp_ref, vc_ref, vp_ref, o_ref, bias_ref):
    n = pl.program_id(1)
    blk = WINDOW
    H = HEAD_DIM

    @pl.when((pl.program_id(0) == 0) & (n == 0))
    def _():
        t = lax.broadcasted_iota(jnp.int32, (blk, 2 * blk), 0)
        j = lax.broadcasted_iota(jnp.int32, (blk, 2 * blk), 1)
        dist = t - j + blk
        band = (dist >= 0) & (dist <= WINDOW)
        first = band & (j >= blk)
        distf = dist.astype(F32)
        for h in range(N_HEADS):
            ab = -slope_ref[h] * distf
            bias_ref[0, h] = jnp.where(first, ab, -jnp.inf)
            bias_ref[1, h] = jnp.where(band, ab, -jnp.inf)

    sel = jnp.where(n == 0, 0, 1)
    low = lax.broadcasted_iota(jnp.int32, (blk, LANES), 1) < H
    scale = H ** -0.5
    tile = lambda i: slice(i * LANES, (i + 1) * LANES)

    def kv_group(KV):
        kslab = {hk: jnp.concatenate([kp_ref[0, :, tile(hk // 2)], kc_ref[0, :, tile(hk // 2)]],
                                     axis=0).astype(BF16) for hk in KV}
        vslab = {hk: jnp.concatenate([vp_ref[0, :, tile(hk // 2)], vc_ref[0, :, tile(hk // 2)]],
                                     axis=0).astype(BF16) for hk in KV}
        lhs = {}
        for hk in KV:
            parts = []
            for s2 in range(2):
                xs = q_ref[0, :, tile(2 * hk + s2)].astype(F32) * scale
                xr = pltpu.roll(xs, H, axis=1)
                if hk % 2 == 0:
                    parts += [jnp.where(low, xs, 0.0), jnp.where(low, xr, 0.0)]
                else:
                    parts += [jnp.where(low, 0.0, xr), jnp.where(low, 0.0, xs)]
            lhs[hk] = jnp.concatenate(parts, axis=0).astype(BF16)
        s = {hk: _dot_nt(lhs[hk], kslab[hk]) for hk in KV}
        p, rden = {}, {}
        for hk in KV:
            ps, rs = [], []
            for g in range(GQA_GROUP):
                h = hk * GQA_GROUP + g
                sg = s[hk][g * blk:(g + 1) * blk] + bias_ref[sel, h]
                m = jnp.maximum(jnp.max(sg, axis=-1, keepdims=True), sink_ref[h])
                e = jnp.exp(sg - m)
                rs.append(1.0 / (jnp.sum(e, axis=-1, keepdims=True) + jnp.exp(sink_ref[h] - m)))
                ps.append(e.astype(BF16))
            p[hk] = jnp.concatenate(ps, axis=0)
            rden[hk] = rs
        o = {hk: _dot(p[hk], vslab[hk]) for hk in KV}
        for hk in KV:
            for s2 in range(2):
                ga, gb = 2 * s2, 2 * s2 + 1
                oa = o[hk][ga * blk:(ga + 1) * blk] * rden[hk][ga]
                ob = o[hk][gb * blk:(gb + 1) * blk] * rden[hk][gb]
                if hk % 2 == 0:
                    out = jnp.where(low, oa, pltpu.roll(ob, H, axis=1))
                else:
                    out = jnp.where(low, pltpu.roll(oa, H, axis=1), ob)
                o_ref[0, :, tile(2 * hk + s2)] = out.astype(o_ref.dtype)

    for g0 in range(0, N_KV_HEADS, KV_PER_GROUP):
        kv_group(range(g0, g0 + KV_PER_GROUP))


def _swa_prompt(q3, kv3, slopes, sinks):
    b, t, _ = q3.shape
    nb = t // WINDOW
    smem = pl.BlockSpec(memory_space=pltpu.SMEM)
    prev = lambda n: jnp.maximum(n - 1, 0)
    return pl.pallas_call(
        _swa_prompt_kernel,
        grid=(b, nb),
        in_specs=[smem, smem,
                  pl.BlockSpec((1, WINDOW, D_MODEL), lambda bi, n: (bi, n, 0)),
                  pl.BlockSpec((1, WINDOW, D_KV), lambda bi, n: (bi, n, 0)),
                  pl.BlockSpec((1, WINDOW, D_KV), lambda bi, n: (bi, prev(n), 0)),
                  pl.BlockSpec((1, WINDOW, D_KV), lambda bi, n: (bi, n, 1)),
                  pl.BlockSpec((1, WINDOW, D_KV), lambda bi, n: (bi, prev(n), 1))],
        out_specs=pl.BlockSpec((1, WINDOW, D_MODEL), lambda bi, n: (bi, n, 0)),
        out_shape=jax.ShapeDtypeStruct((b, t, D_MODEL), BF16),
        scratch_shapes=[pltpu.VMEM((2, N_HEADS, WINDOW, 2 * WINDOW), F32)],
        compiler_params=_cparams(("arbitrary", "arbitrary")),
        name="swa_prompt",
    )(slopes, sinks, q3, kv3, kv3, kv3, kv3)


def _swa_sample_kernel(slope_ref, sink_ref, q_ref, kc_ref, vc_ref, knew_ref, vnew_ref,
                       o_ref, kwin_ref, vwin_ref, nbuf_ref, *, tq, BB, UNR):
    GT = GQA_GROUP * tq
    R = N_KV_HEADS * GT
    NP = 16
    C = D_KV
    row = lax.broadcasted_iota(jnp.int32, (R, WINDOW), 0)
    wcol = lax.broadcasted_iota(jnp.int32, (R, WINDOW), 1)
    t = lax.rem(row, tq)
    slope = slope_ref[...]
    sink = sink_ref[...]
    dist_o = WINDOW + t - wcol
    bias_old = jnp.where(dist_o <= WINDOW, -slope * dist_o.astype(F32), -jnp.inf)
    s_idx = wcol - (WINDOW - tq)
    dist_n = t - s_idx
    bias_new = jnp.where((s_idx >= 0) & (dist_n >= 0), -slope * dist_n.astype(F32), -jnp.inf)
    hkmask = (lax.broadcasted_iota(jnp.int32, (R, C), 0) // GT) == (lax.broadcasted_iota(jnp.int32, (R, C), 1) // HEAD_DIM)
    srow = lax.broadcasted_iota(jnp.int32, (NP, WINDOW), 0)
    scol = lax.broadcasted_iota(jnp.int32, (NP, WINDOW), 1)
    selw = ((scol == srow + (WINDOW - tq)) & (srow < tq)).astype(BF16)
    lane_new = lax.broadcasted_iota(jnp.int32, (C, WINDOW), 1) >= WINDOW - tq
    scale = HEAD_DIM ** -0.5
    for u in range(UNR):
        nbuf_ref[u, :, tq:NP, :] = jnp.zeros((2, NP - tq, C), F32)

    def transposed_new(x):
        return sum(_dot_tn(part, selw) for part in _split3(x))

    def body(i, carry):
        bs = [i * UNR + u for u in range(UNR)]
        U = range(UNR)
        for u in U:
            nbuf_ref[u, 0, 0:tq, :] = knew_ref[bs[u]]
            nbuf_ref[u, 1, 0:tq, :] = vnew_ref[bs[u]]
        kt = [kc_ref[b] for b in bs]
        vt = [vc_ref[b] for b in bs]
        knt = [transposed_new(nbuf_ref[u, 0]) for u in U]
        vnt = [transposed_new(nbuf_ref[u, 1]) for u in U]
        qbd = [jnp.where(hkmask, jnp.concatenate([q_ref[b] * scale] * N_KV_HEADS, axis=0), 0.0).astype(BF16)
               for b in bs]
        s_o = [_dot(qbd[u], kt[u].astype(BF16)) + bias_old for u in U]
        s_n = [_dot(qbd[u], knt[u].astype(BF16)) + bias_new for u in U]
        outs = []
        for u in U:
            m = jnp.maximum(jnp.maximum(jnp.max(s_o[u], axis=-1, keepdims=True),
                                        jnp.max(s_n[u], axis=-1, keepdims=True)), sink)
            p_o = jnp.exp(s_o[u] - m)
            p_n = jnp.exp(s_n[u] - m)
            rden = 1.0 / (jnp.sum(p_o, axis=-1, keepdims=True) + jnp.sum(p_n, axis=-1, keepdims=True)
                          + jnp.exp(sink - m))
            o = _dot_nt(p_o.astype(BF16), vt[u].astype(BF16)) + _dot_nt(p_n.astype(BF16), vnt[u].astype(BF16))
            o = jnp.where(hkmask, o * rden, 0.0)
            acc = o[0:GT]
            for hk in range(1, N_KV_HEADS):
                acc = acc + o[hk * GT:(hk + 1) * GT]
            outs.append(acc)
        for u in U:
            kwin_ref[bs[u]] = jnp.where(lane_new, knt[u], pltpu.roll(kt[u], WINDOW - tq, axis=1))
            vwin_ref[bs[u]] = jnp.where(lane_new, vnt[u], pltpu.roll(vt[u], WINDOW - tq, axis=1))
            o_ref[bs[u]] = outs[u]
        return carry

    lax.fori_loop(0, BB // UNR, body, 0)


def _swa_sample(q16, knew3, vnew3, kct, vct, slope_rows, sink_rows):
    b, gt, c = q16.shape
    tq = gt // GQA_GROUP
    bb = 8 if b % 8 == 0 else 1
    unr = 2 if bb % 2 == 0 else 1
    rows = N_KV_HEADS * gt
    blk3 = lambda shape: pl.BlockSpec(shape, lambda i: (i, 0, 0))
    full2 = pl.BlockSpec((rows, 1), lambda i: (0, 0))
    kern = functools.partial(_swa_sample_kernel, tq=tq, BB=bb, UNR=unr)
    return pl.pallas_call(
        kern,
        grid=(b // bb,),
        in_specs=[full2, full2, blk3((bb, gt, c)), blk3((bb, c, WINDOW)), blk3((bb, c, WINDOW)),
                  blk3((bb, tq, c)), blk3((bb, tq, c))],
        out_specs=[blk3((bb, gt, c)), blk3((bb, c, WINDOW)), blk3((bb, c, WINDOW))],
        out_shape=[jax.ShapeDtypeStruct((b, gt, c), F32),
                   jax.ShapeDtypeStruct((b, c, WINDOW), F32),
                   jax.ShapeDtypeStruct((b, c, WINDOW), F32)],
        scratch_shapes=[pltpu.VMEM((unr, 2, 16, c), F32)],
        compiler_params=_cparams(("parallel",)),
        name="swa_sample",
    )(slope_rows, sink_rows, q16, kct, vct, knew3, vnew3)


def _merge_out_kernel(x_ref, ya_ref, yb_ref, ga_ref, gb_ref, wo_ref, nw_ref, h_ref, hn_ref):
    f32 = lambda ref: ref[...].astype(F32)
    mixed = _sigmoid(f32(ga_ref)) * f32(ya_ref) + _sigmoid(f32(gb_ref)) * f32(yb_ref)
    h = x_ref[...] + _dot(mixed.astype(BF16), wo_ref[...])
    h_ref[...] = h
    ms = jnp.mean(h * h, axis=-1, keepdims=True)
    hn_ref[...] = (h * lax.rsqrt(ms + RMS_EPS) * nw_ref[...]).astype(BF16)


def _merge_out(x, ya, yb, gates, w_out, nw, tm):
    m, d = x.shape
    row = lambda o: (lambda i: (i, o))
    return pl.pallas_call(
        _merge_out_kernel,
        grid=(m // tm,),
        in_specs=[pl.BlockSpec((tm, d), row(0)), pl.BlockSpec((tm, d), row(0)), pl.BlockSpec((tm, d), row(0)),
                  pl.BlockSpec((tm, d), row(0)), pl.BlockSpec((tm, d), row(1)),
                  pl.BlockSpec((d, d), lambda i: (0, 0)), pl.BlockSpec((1, d), lambda i: (0, 0))],
        out_specs=[pl.BlockSpec((tm, d), row(0)), pl.BlockSpec((tm, d), row(0))],
        out_shape=[jax.ShapeDtypeStruct((m, d), F32), jax.ShapeDtypeStruct((m, d), BF16)],
        compiler_params=_cparams(("parallel",)),
        name="merge_out_proj",
    )(x, ya, yb, gates, gates, w_out, nw)


def _mlp_kernel(hn_ref, h_ref, wu_ref, wd_ref, nw_ref, o_ref, acc_ref):
    j = pl.program_id(1)

    @pl.when(j == 0)
    def _():
        acc_ref[...] = jnp.zeros_like(acc_ref)

    u = jnp.maximum(_dot(hn_ref[...], wu_ref[...]), 0.0)
    acc_ref[...] += _dot((u * u).astype(BF16), wd_ref[...])

    @pl.when(j == pl.num_programs(1) - 1)
    def _():
        h = h_ref[...] + acc_ref[...]
        ms = jnp.mean(h * h, axis=-1, keepdims=True)
        o_ref[...] = h * lax.rsqrt(ms + RMS_EPS) * nw_ref[...]


def _mlp(hn, h, w_up, w_down, nw, tm, tf):
    m, d = h.shape
    f = w_up.shape[1]
    return pl.pallas_call(
        _mlp_kernel,
        grid=(m // tm, f // tf),
        in_specs=[pl.BlockSpec((tm, d), lambda i, j: (i, 0)), pl.BlockSpec((tm, d), lambda i, j: (i, 0)),
                  pl.BlockSpec((d, tf), lambda i, j: (0, j)), pl.BlockSpec((tf, d), lambda i, j: (j, 0)),
                  pl.BlockSpec((1, d), lambda i, j: (0, 0))],
        out_specs=pl.BlockSpec((tm, d), lambda i, j: (i, 0)),
        out_shape=jax.ShapeDtypeStruct((m, d), F32),
        scratch_shapes=[pltpu.VMEM((tm, d), F32)],
        compiler_params=_cparams(("parallel", "arbitrary")),
        name="mlp_final_norm",
    )(hn, h, w_up, w_down, nw)


def _pick(m, prefs):
    for t in prefs:
        if m % t == 0:
            return t
    return m


def _pad_cols(v, n):
    return jnp.concatenate([v, jnp.zeros(v.shape[:-1] + (n - v.shape[-1],), v.dtype)], axis=-1)


def _place_rows(w, start, rows):
    n, d = w.shape
    return jnp.concatenate([jnp.zeros((start, d), w.dtype), w, jnp.zeros((rows - start - n, d), w.dtype)], axis=0)


def _layer(x, shift_prev, wkv0, lw, *, prompt, k_cache_t=None, v_cache_t=None):
    b, t, d = x.shape
    m = b * t
    x2 = x.reshape(m, d) if prompt else x.transpose(1, 0, 2).reshape(m, d)
    proj_r, proj_q, proj_kv, proj_g = _in_proj(
        x2, lw["norm_mix_w"], [lw["w_r"], lw["w_q"], lw["w_kv"], lw["w_g"]], [F32, BF16, F32, BF16],
        _pick(m, (1024, 512, 256, 128, 8)))
    shift_pad = _pad_cols(shift_prev, R_PAD)

    if prompt:
        proj_r3 = proj_r.reshape(b, t, R_PAD)
        kv3 = proj_kv.reshape(b, t, 2 * D_KV)
        ya, wkv_new = _rwkv_mix(proj_r3, shift_pad[:, None], wkv0, lw, L=64, Lv=64, HPS=16, BB=1, y_dtype=BF16,
                                pipelined=False)
        yb = _swa_prompt(proj_q.reshape(b, t, D_MODEL), kv3, lw["slopes"], lw["sinks"])
        k_win = kv3[:, t - WINDOW:, :D_KV].reshape(b, WINDOW, N_KV_HEADS, HEAD_DIM)
        v_win = kv3[:, t - WINDOW:, D_KV:].reshape(b, WINDOW, N_KV_HEADS, HEAD_DIM)
        shift_new = proj_r3[:, t - 1, :R_COLS]
    else:
        prep = _rwkv_sample_prep(proj_r, shift_pad, lw, b, t)
        wkv_t, ya_t = _wkv_sample(wkv0.transpose(1, 2, 3, 0), prep,
                                  jnp.broadcast_to(lw["ln_w"].reshape(D_RWKV, 1), (D_RWKV, b)),
                                  jnp.broadcast_to(lw["ln_b"].reshape(D_RWKV, 1), (D_RWKV, b)))
        ya = ya_t.transpose(0, 2, 1)
        wkv_new = wkv_t.transpose(3, 0, 1, 2)
        q16 = proj_q.reshape(t, b, N_KV_HEADS, GQA_GROUP, HEAD_DIM).transpose(1, 3, 0, 2, 4)
        q16 = q16.reshape(b, GQA_GROUP * t, D_KV)
        kv_bt = proj_kv.reshape(t, b, 2 * D_KV).transpose(1, 0, 2)
        gt_head = (jnp.arange(N_KV_HEADS)[:, None] * GQA_GROUP + jnp.arange(GQA_GROUP)[None, :])
        row_head = jnp.repeat(gt_head, t, axis=1).reshape(-1)
        o16, kwt, vwt = _swa_sample(q16, kv_bt[:, :, :D_KV], kv_bt[:, :, D_KV:], k_cache_t, v_cache_t,
                                    lw["slopes"][row_head][:, None], lw["sinks"][row_head][:, None])
        yb = o16.reshape(b, GQA_GROUP, t, N_KV_HEADS, HEAD_DIM).transpose(2, 0, 3, 1, 4)
        k_win = kwt.reshape(b, N_KV_HEADS, HEAD_DIM, WINDOW).transpose(0, 3, 1, 2)
        v_win = vwt.reshape(b, N_KV_HEADS, HEAD_DIM, WINDOW).transpose(0, 3, 1, 2)
        shift_new = proj_r[(t - 1) * b:, :R_COLS]

    h, hn = _merge_out(x2, ya.reshape(m, d), yb.reshape(m, d), proj_g, lw["w_out"], lw["norm_mlp_w"],
                       _pick(m, (256, 128, 8)))
    y = _mlp(hn, h, lw["w_up"], lw["w_down"], lw["norm_final_w"], _pick(m, (512, 256, 128, 8)), 1024)
    y = y.reshape(b, t, d) if prompt else y.reshape(t, b, d).transpose(1, 0, 2)
    return y, shift_new, wkv_new, k_win, v_win


def kernel(x_prompt, x_sample, state_shift, state_wkv, cache_k_win, cache_v_win, norm_mix_w, w_in, tshift_mu, w0, w_lora, a0, a_lora, g_lora, k_k, k_a, r_k, ln_x_w, ln_x_b, attn_sinks, w_out, norm_mlp_w, w_up, w_down, norm_final_w):
    depth = w_in.shape[0]
    assert depth == 1
    l = 0
    bp = x_prompt.shape[0]
    db = x_sample.shape[0]
    hh = jnp.arange(N_HEADS, dtype=F32)
    w_t = jnp.swapaxes(w_in[l], 0, 1)
    o_q, o_kv, o_g = R_COLS, R_COLS + D_MODEL, R_COLS + D_MODEL + 2 * D_KV
    lw = dict(
        norm_mix_w=norm_mix_w[l][None],
        w_r=_place_rows(w_t[:R_COLS], 0, R_PAD).astype(BF16), w_q=w_t[o_q:o_kv].astype(BF16),
        w_kv=w_t[o_kv:o_g].astype(BF16), w_g=w_t[o_g:].astype(BF16),
        mu=_pad_cols(tshift_mu[l][None], R_PAD), w0=w0[l][None], a0=a0[l][None], k_k=k_k[l][None],
        k_a=k_a[l][None], r_k=r_k[l].reshape(1, D_RWKV), ln_w=ln_x_w[l][None], ln_b=ln_x_b[l][None],
        w_lora=_place_rows(w_lora[l], 0, LANES).astype(BF16),
        a_lora=_place_rows(a_lora[l], LORA_DECAY_END, LORA_A_WIN).astype(BF16),
        g_lora=_place_rows(g_lora[l], LORA_A_END - LORA_G_START, LORA_W - LORA_G_START).astype(BF16),
        slopes=jnp.exp2(-8.0 * (hh + 1.0) / N_HEADS), sinks=attn_sinks[l].astype(F32),
        w_out=w_out[l].astype(BF16), norm_mlp_w=norm_mlp_w[l][None],
        w_up=w_up[l].astype(BF16), w_down=w_down[l].astype(BF16), norm_final_w=norm_final_w[None],
    )
    yp, sp, wp, kp, vp = _layer(
        x_prompt, jnp.zeros((bp, R_COLS), F32), jnp.zeros((bp, N_HEADS, HEAD_DIM, HEAD_DIM), F32), lw, prompt=True)
    kct = cache_k_win[l].transpose(0, 2, 3, 1).reshape(db, D_KV, WINDOW)
    vct = cache_v_win[l].transpose(0, 2, 3, 1).reshape(db, D_KV, WINDOW)
    ys, ss, ws, ksm, vsm = _layer(x_sample, state_shift[l], state_wkv[l], lw, prompt=False,
                                  k_cache_t=kct, v_cache_t=vct)
    return (yp, ys, sp[None], wp[None], kp[None], vp[None], ss[None], ws[None], ksm[None], vsm[None])
```

```python
import functools
import math

import jax
import jax.numpy as jnp
from jax import lax
from jax.experimental import pallas as pl
from jax.experimental.pallas import tpu as pltpu

F32 = jnp.float32
BF16 = jnp.bfloat16

D_MODEL = 2048
HEAD_DIM = 64
N_HEADS = D_MODEL // HEAD_DIM
N_KV_HEADS = 8
GQA_GROUP = N_HEADS // N_KV_HEADS
D_KV = N_KV_HEADS * HEAD_DIM
WINDOW = 128
D_FF = 4 * D_MODEL
D_DECAY_LORA = 96
D_A_LORA = 96
D_GATE_LORA = 256
D_RWKV = D_MODEL
R_COLS = 3 * D_RWKV + D_DECAY_LORA + D_A_LORA + D_GATE_LORA
C_IN = R_COLS + D_MODEL + 2 * D_KV + 2 * D_MODEL
RMS_EPS = 1e-5
GN_EPS = 64e-5

LANES = 128
PROJ_TILE = 512
R_PAD = -(-R_COLS // PROJ_TILE) * PROJ_TILE
LORA_W = R_PAD - 3 * D_RWKV
LORA_DECAY_END = D_DECAY_LORA
LORA_A_END = D_DECAY_LORA + D_A_LORA
LORA_A_WIN = -(-LORA_A_END // LANES) * LANES
LORA_G_START = (LORA_A_END // LANES) * LANES
IN_GROUPS = ((0, R_PAD), (R_COLS, D_MODEL), (R_COLS + D_MODEL, 2 * D_KV), (R_COLS + D_MODEL + 2 * D_KV, 2 * D_MODEL))

VMEM_LIMIT = 56 * 1024 * 1024
RWKV_CHUNK = 64
KV_PER_GROUP = 2
SAMPLE_PREP_W = 512
PREP_OUT = ("r", "w", "k", "v", "a", "b", "g", "bonus")


def _cparams(sem):
    return pltpu.CompilerParams(dimension_semantics=sem, vmem_limit_bytes=VMEM_LIMIT)


def _dot(a, b):
    return jnp.dot(a, b, preferred_element_type=F32)


def _dot_nt(a, b):
    return lax.dot_general(a, b, (((1,), (1,)), ((), ())), preferred_element_type=F32)


def _dot_tn(a, b):
    return lax.dot_general(a, b, (((0,), (0,)), ((), ())), preferred_element_type=F32)


def _sigmoid(x):
    return 1.0 / (1.0 + jnp.exp(-x))


def _tanh(x):
    return 1.0 - 2.0 / (1.0 + jnp.exp(2.0 * x))


def _split2(x):
    hi = x.astype(BF16)
    lo = (x - hi.astype(F32)).astype(BF16)
    return hi, lo


def _split3(x):
    hi = x.astype(BF16)
    r1 = x - hi.astype(F32)
    mid = r1.astype(BF16)
    lo = (r1 - mid.astype(F32)).astype(BF16)
    return hi, mid, lo


def _head_pair_mask():
    ji = lax.broadcasted_iota(jnp.int32, (LANES, LANES), 0)
    jj = lax.broadcasted_iota(jnp.int32, (LANES, LANES), 1)
    return (ji < HEAD_DIM) == (jj < HEAD_DIM)


def _seg_sum(x, seg, rows):
    n = x.shape[1] // LANES
    xs = jnp.concatenate([x[:, i * LANES:(i + 1) * LANES] for i in range(n)], axis=0).astype(BF16)
    ys = _dot(xs, seg)
    return jnp.concatenate([ys[i * rows:(i + 1) * rows] for i in range(n)], axis=1)


def _in_proj_kernel(x_ref, nw_ref, *refs, bounds):
    n = len(bounds)
    w_refs, o_refs, xn_ref = refs[:n], refs[n:2 * n], refs[2 * n]
    j = pl.program_id(1)

    @pl.when(j == 0)
    def _():
        x = x_ref[...]
        ms = jnp.mean(x * x, axis=-1, keepdims=True)
        xn_ref[...] = (x * lax.rsqrt(ms + RMS_EPS) * nw_ref[...]).astype(BF16)

    for w_ref, o_ref, (lo, hi) in zip(w_refs, o_refs, bounds):
        @pl.when((j >= lo) & (j < hi))
        def _(w_ref=w_ref, o_ref=o_ref):
            o_ref[...] = _dot_nt(xn_ref[...], w_ref[...]).astype(o_ref.dtype)


def _in_proj(x, nw, w_t, groups, out_dtypes, tm):
    m, d = x.shape
    tn = PROJ_TILE
    bounds, lo = [], 0
    for _, width in groups:
        bounds.append((lo, lo + width // tn))
        lo = bounds[-1][1]
    ROW_ALIGN = 16
    assert all(start % ROW_ALIGN == 0 and start + width <= w_t.shape[0] for start, width in groups)

    def clamp(lo_, hi_):
        return lambda j: jnp.clip(j - lo_, 0, hi_ - lo_ - 1)

    in_specs = [pl.BlockSpec((tm, d), lambda i, j: (i, 0)), pl.BlockSpec((1, d), lambda i, j: (0, 0))]
    out_specs, out_shape = [], []
    for (start, width), dt, (lo_, hi_) in zip(groups, out_dtypes, bounds):
        c = clamp(lo_, hi_)
        in_specs.append(pl.BlockSpec((pl.Element(tn), pl.Element(d)),
                                     lambda i, j, c=c, start=start: (pl.multiple_of(start + c(j) * tn, ROW_ALIGN), 0)))
        out_specs.append(pl.BlockSpec((tm, tn), lambda i, j, c=c: (i, c(j))))
        out_shape.append(jax.ShapeDtypeStruct((m, width), dt))
    return pl.pallas_call(
        functools.partial(_in_proj_kernel, bounds=tuple(bounds)),
        grid=(m // tm, lo),
        in_specs=in_specs,
        out_specs=out_specs,
        out_shape=out_shape,
        scratch_shapes=[pltpu.VMEM((tm, d), BF16)],
        compiler_params=_cparams(("parallel", "arbitrary")),
        name="norm_in_proj",
    )(x, nw, *([w_t] * len(groups)))


def _rwkv_prompt_kernel(pr_ref, pk_ref, pv_ref, plo_ref, shr_ref, shk_ref, shv_ref, shl_ref, s0_ref,
                        mur_ref, muk_ref, muv_ref, mul_ref, w0_ref, a0_ref, kk_ref, ka_ref, rk_ref, lnw_ref, lnb_ref,
                        wl_ref, al_ref, gl_ref,
                        y_ref, so_ref,
                        sd_ref, br_ref, bk_ref, bv_ref, bl_ref, *, L, NC):
    c = pl.program_id(1)
    H = HEAD_DIM
    NP = D_RWKV // LANES
    P = range(NP)
    sls = [slice(i * LANES, (i + 1) * LANES) for i in P]
    n_dbl = max(1, math.ceil(math.log2(L)))

    head1 = lax.broadcasted_iota(jnp.int32, (L, LANES), 1) < H
    row_i = lax.broadcasted_iota(jnp.int32, (L, 2 * L), 0)
    col_i = lax.broadcasted_iota(jnp.int32, (L, 2 * L), 1)
    col_t = jnp.where(col_i >= L, col_i - L, col_i)
    strict = col_t < row_i
    incl = col_t <= row_i
    colh1 = col_i < L
    tri = (lax.broadcasted_iota(jnp.int32, (L, L), 1) <= lax.broadcasted_iota(jnp.int32, (L, L), 0)).astype(BF16)
    same_head = _head_pair_mask()
    seg = same_head.astype(BF16)
    seg_sum = functools.partial(_seg_sum, seg=seg, rows=L)

    def stack_heads(x, m):
        zero = jnp.zeros_like(x)
        return jnp.concatenate([jnp.where(m, x, zero), jnp.where(m, zero, x)], axis=0)

    @pl.when(c == 0)
    def _init():
        br_ref[7:8, :] = shr_ref[0]
        bk_ref[7:8, :] = shk_ref[0]
        bv_ref[7:8, :] = shv_ref[0]
        bl_ref[7:8, :] = shl_ref[0]
        z = jnp.zeros((H, H), F32)
        for i in P:
            top = jnp.concatenate([s0_ref[0, 2 * i], z], axis=1)
            bot = jnp.concatenate([z, s0_ref[0, 2 * i + 1]], axis=1)
            sd_ref[i] = jnp.concatenate([top, bot], axis=0)

    br_ref[8:8 + L, :] = pr_ref[0]
    bk_ref[8:8 + L, :] = pk_ref[0]
    bv_ref[8:8 + L, :] = pv_ref[0]
    bl_ref[8:8 + L, :] = plo_ref[0]

    def shifted(buf, mu_ref):
        p = buf[8:8 + L, :]
        return p + mu_ref[...] * (buf[7:7 + L, :] - p)

    ps_l = shifted(bl_ref, mul_ref)
    td = _tanh(ps_l[:, 0:LANES]).astype(BF16)
    da = ps_l[:, 0:LORA_A_WIN].astype(BF16)
    sg = _sigmoid(ps_l[:, LORA_G_START:]).astype(BF16)
    r = shifted(br_ref, mur_ref)
    k = shifted(bk_ref, muk_ref)
    v = shifted(bv_ref, muv_ref)

    zlog = w0_ref[...] + _dot(td, wl_ref[...])
    logw = -math.exp(-0.5) / (1.0 + jnp.exp(-zlog))
    a_sig = _sigmoid(a0_ref[...] + _dot(da, al_ref[...]))
    kk = k * kk_ref[...]
    kk = kk * jnp.minimum(lax.rsqrt(jnp.maximum(seg_sum(kk * kk), 0.0)), 1e12)
    k_h = k * (1.0 + (a_sig - 1.0) * ka_ref[...])
    hi, lo = _split2(logw)
    cum = _dot(tri, hi) + _dot(tri, lo)
    cum_l = cum[L - 1:L, :]
    p_inv = jnp.exp(-cum)
    p_end = jnp.exp(cum_l - cum)
    p_l = jnp.exp(cum_l)
    bvec = kk * a_sig
    a_t = (-kk * jnp.exp(cum - logw)).astype(BF16)
    r_t = (r * jnp.exp(cum)).astype(BF16)
    b_t = (bvec * p_inv).astype(BF16)
    k_t = (k_h * p_inv).astype(BF16)
    b_e = (bvec * p_end).astype(BF16)
    k_e = (k_h * p_end).astype(BF16)
    v_b = v.astype(BF16)
    g = _dot(sg, gl_ref[...])
    bonus = seg_sum(r * k_h * rk_ref[...]) * v

    sd = [sd_ref[i] for i in P]
    lhs = [jnp.concatenate([a_t[:, s], r_t[:, s]], axis=0) for s in sls]
    rhs = [jnp.concatenate([stack_heads(b_t[:, s], head1), stack_heads(k_t[:, s], head1)], axis=0) for s in sls]
    aa = [_dot_nt(lhs[i], rhs[i]) for i in P]
    sa = [_dot_nt(lhs[i], sd[i].astype(BF16)) for i in P]
    v_st = [stack_heads(v_b[:, s], head1) for s in sls]
    a_ak = [jnp.where(strict, aa[i][0:L, 2 * L:4 * L], 0.0).astype(BF16) for i in P]
    x = [sa[i][0:L] + _dot(a_ak[i], v_st[i]) for i in P]
    ap = [jnp.where(strict, aa[i][0:L, 0:2 * L], 0.0).astype(BF16) for i in P]
    for d in range(n_dbl):
        xs = [stack_heads(x[i].astype(BF16), head1) for i in P]
        if d + 1 < n_dbl:
            both = [_dot(ap[i], jnp.concatenate([xs[i], stack_heads(ap[i], colh1)], axis=1)) for i in P]
            x = [x[i] + both[i][:, 0:2 * L] for i in P]
            ap = [both[i][:, 2 * L:].astype(BF16) for i in P]
        else:
            x = [x[i] + _dot(ap[i], xs[i]) for i in P]
    u_b = [x[i].astype(BF16) for i in P]
    a_rb = [jnp.where(incl, aa[i][L:2 * L, 0:2 * L], 0.0).astype(BF16) for i in P]
    a_rk = [jnp.where(incl, aa[i][L:2 * L, 2 * L:4 * L], 0.0).astype(BF16) for i in P]
    y = [sa[i][L:2 * L] + _dot(jnp.concatenate([a_rb[i], a_rk[i]], axis=1),
                               jnp.concatenate([stack_heads(u_b[i], head1), v_st[i]], axis=0)) for i in P]
    ds = [_dot_tn(jnp.concatenate([u_b[i], v_b[:, sls[i]]], axis=0),
                  jnp.concatenate([b_e[:, sls[i]], k_e[:, sls[i]]], axis=0)) for i in P]
    for i in P:
        sd_ref[i] = sd[i] * p_l[:, sls[i]] + jnp.where(same_head, ds[i], 0.0)

    y = jnp.concatenate(y, axis=1)
    mean = seg_sum(y) * (1.0 / H)
    dlt = y - mean
    var = seg_sum(dlt * dlt) * (1.0 / H)
    yn = dlt * lax.rsqrt(var + GN_EPS) * lnw_ref[...] + lnb_ref[...]
    y_ref[0] = ((yn + bonus) * g).astype(y_ref.dtype)

    br_ref[7:8, :] = br_ref[7 + L:8 + L, :]
    bk_ref[7:8, :] = bk_ref[7 + L:8 + L, :]
    bv_ref[7:8, :] = bv_ref[7 + L:8 + L, :]
    bl_ref[7:8, :] = bl_ref[7 + L:8 + L, :]

    @pl.when(c == NC - 1)
    def _finish():
        for i in P:
            s = sd_ref[i]
            so_ref[0, 2 * i] = s[0:H, 0:H]
            so_ref[0, 2 * i + 1] = s[H:2 * H, H:2 * H]


def _rwkv_prompt(proj3, shift3, s0, p):
    b, t, _ = proj3.shape
    L = RWKV_CHUNK
    assert t % L == 0 and 2 * L == LANES
    nc = t // L
    w = D_RWKV
    lora_blk = 3 * D_RWKV // LORA_W
    col = lambda o: (lambda bi, c: (bi, c, o))
    sh = lambda o: (lambda bi, c: (bi, 0, o))
    par = lambda o: (lambda bi, c: (0, o))
    in_specs = [
        pl.BlockSpec((1, L, w), col(0)), pl.BlockSpec((1, L, w), col(1)), pl.BlockSpec((1, L, w), col(2)),
        pl.BlockSpec((1, L, LORA_W), col(lora_blk)),
        pl.BlockSpec((1, 1, w), sh(0)), pl.BlockSpec((1, 1, w), sh(1)), pl.BlockSpec((1, 1, w), sh(2)),
        pl.BlockSpec((1, 1, LORA_W), sh(lora_blk)),
        pl.BlockSpec((1, N_HEADS, HEAD_DIM, HEAD_DIM), lambda bi, c: (bi, 0, 0, 0)),
        pl.BlockSpec((1, w), par(0)), pl.BlockSpec((1, w), par(1)), pl.BlockSpec((1, w), par(2)),
        pl.BlockSpec((1, LORA_W), par(lora_blk)),
    ] + [pl.BlockSpec((1, w), par(0))] * 7 + [
        pl.BlockSpec((LANES, w), par(0)), pl.BlockSpec((LORA_A_WIN, w), par(0)),
        pl.BlockSpec((LORA_W - LORA_G_START, w), par(0)),
    ]
    out_specs = [pl.BlockSpec((1, L, w), col(0)),
                 pl.BlockSpec((1, N_HEADS, HEAD_DIM, HEAD_DIM), lambda bi, c: (bi, 0, 0, 0))]
    return pl.pallas_call(
        functools.partial(_rwkv_prompt_kernel, L=L, NC=nc),
        grid=(b, nc),
        in_specs=in_specs,
        out_specs=out_specs,
        out_shape=[jax.ShapeDtypeStruct((b, t, D_RWKV), BF16),
                   jax.ShapeDtypeStruct((b, N_HEADS, HEAD_DIM, HEAD_DIM), F32)],
        scratch_shapes=[pltpu.VMEM((D_RWKV // LANES, LANES, LANES), F32),
                        pltpu.VMEM((L + 8, w), F32), pltpu.VMEM((L + 8, w), F32),
                        pltpu.VMEM((L + 8, w), F32), pltpu.VMEM((L + 8, LORA_W), F32)],
        compiler_params=_cparams(("parallel", "arbitrary")),
        name="rwkv_mix",
    )(proj3, proj3, proj3, proj3, shift3, shift3, shift3, shift3, s0,
      p["mu"], p["mu"], p["mu"], p["mu"], p["w0"], p["a0"], p["k_k"], p["k_a"], p["r_k"],
      p["ln_w"], p["ln_b"], p["w_lora"], p["a_lora"], p["g_lora"])


def _rwkv_sample_prep_kernel(pr_ref, pk_ref, pv_ref, plo_ref, shr_ref, shk_ref, shv_ref, shl_ref,
                             mur_ref, muk_ref, muv_ref, mul_ref, w0_ref, a0_ref, kk_ref, ka_ref, rk_ref,
                             wl_ref, al_ref, gl_ref, *out_refs, nb, nt):
    W = pr_ref.shape[1]
    n_tiles = W // LANES
    rows = nb * nt
    seg_sum = functools.partial(_seg_sum, seg=_head_pair_mask().astype(BF16), rows=rows)

    def shifted(p_ref, s_ref, mu_ref):
        p = p_ref[...]
        prev = jnp.concatenate([s_ref[...], p[0:rows - nb]], axis=0)
        return p + mu_ref[...] * (prev - p)

    ps_l = shifted(plo_ref, shl_ref, mul_ref)
    td = _tanh(ps_l[:, 0:LANES]).astype(BF16)
    da = ps_l[:, 0:LORA_A_WIN].astype(BF16)
    sg = _sigmoid(ps_l[:, LORA_G_START:]).astype(BF16)
    r = shifted(pr_ref, shr_ref, mur_ref)
    k = shifted(pk_ref, shk_ref, muk_ref)
    v = shifted(pv_ref, shv_ref, muv_ref)
    zlog = w0_ref[...] + _dot(td, wl_ref[...])
    w = jnp.exp(-math.exp(-0.5) / (1.0 + jnp.exp(-zlog)))
    a_sig = _sigmoid(a0_ref[...] + _dot(da, al_ref[...]))
    kk = k * kk_ref[...]
    kk = kk * jnp.minimum(lax.rsqrt(jnp.maximum(seg_sum(kk * kk), 0.0)), 1e12)
    k_h = k * (1.0 + (a_sig - 1.0) * ka_ref[...])
    g = _dot(sg, gl_ref[...])
    bonus = seg_sum(r * k_h * rk_ref[...]) * v
    vals = dict(r=r, w=w, k=k_h, v=v, a=-kk, b=kk * a_sig, g=g, bonus=bonus)
    for name, o_ref in zip(PREP_OUT, out_refs):
        x = vals[name]
        for t in range(nt):
            for c in range(n_tiles):
                o_ref[t, c * LANES:(c + 1) * LANES, :] = x[t * nb:(t + 1) * nb, c * LANES:(c + 1) * LANES].T


def _rwkv_sample_prep(proj_r, shift, p, nb, nt):
    w = SAMPLE_PREP_W
    kb = D_RWKV // w
    lora_blk = 3 * D_RWKV // LORA_W
    rows = nt * nb
    col = lambda o: (lambda g: (0, o + g))
    in_specs = (
        [pl.BlockSpec((rows, w), col(0)), pl.BlockSpec((rows, w), col(kb)), pl.BlockSpec((rows, w), col(2 * kb)),
         pl.BlockSpec((rows, LORA_W), lambda g: (0, lora_blk))]
        + [pl.BlockSpec((nb, w), col(0)), pl.BlockSpec((nb, w), col(kb)), pl.BlockSpec((nb, w), col(2 * kb)),
           pl.BlockSpec((nb, LORA_W), lambda g: (0, lora_blk))]
        + [pl.BlockSpec((1, w), col(0)), pl.BlockSpec((1, w), col(kb)), pl.BlockSpec((1, w), col(2 * kb)),
           pl.BlockSpec((1, LORA_W), lambda g: (0, lora_blk))]
        + [pl.BlockSpec((1, w), col(0))] * 5
        + [pl.BlockSpec((LANES, w), col(0)), pl.BlockSpec((LORA_A_WIN, w), col(0)),
           pl.BlockSpec((LORA_W - LORA_G_START, w), col(0))])
    out_spec = pl.BlockSpec((nt, w, nb), lambda g: (0, g, 0))
    return pl.pallas_call(
        functools.partial(_rwkv_sample_prep_kernel, nb=nb, nt=nt),
        grid=(kb,),
        in_specs=in_specs,
        out_specs=[out_spec] * len(PREP_OUT),
        out_shape=[jax.ShapeDtypeStruct((nt, D_RWKV, nb), F32)] * len(PREP_OUT),
        compiler_params=_cparams(("parallel",)),
        name="rwkv_sample_prep",
    )(proj_r, proj_r, proj_r, proj_r, shift, shift, shift, shift, p["mu"], p["mu"], p["mu"], p["mu"],
      p["w0"], p["a0"], p["k_k"], p["k_a"], p["r_k"], p["w_lora"], p["a_lora"], p["g_lora"])


def _wkv_sample_kernel(s_ref, r_ref, w_ref, k_ref, v_ref, a_ref, b_ref, g_ref, bonus_ref, lnw_ref, lnb_ref,
                       so_ref, o_ref, y_ref, *, nt):
    H = HEAD_DIM
    SUB = 8
    PAR = 4
    rowid = lax.broadcasted_iota(jnp.int32, (SUB, LANES), 0)

    def body(i8, carry):
        base = pl.multiple_of(i8 * SUB, SUB)
        v8 = [v_ref[t, pl.ds(base, SUB), :] for t in range(nt)]
        y8 = [jnp.zeros((SUB, LANES), F32) for _ in range(nt)]
        for h0 in range(0, SUB, PAR):
            ids = list(range(h0, h0 + PAR))
            S = [s_ref[0, base + ii] for ii in ids]
            for t in range(nt):
                a, w, b, k, r = a_ref[t], w_ref[t], b_ref[t], k_ref[t], r_ref[t]
                for n, ii in enumerate(ids):
                    sa = jnp.sum(S[n] * a, axis=0, keepdims=True)
                    S[n] = S[n] * w + sa * b + v8[t][ii:ii + 1, :] * k
                    y = jnp.sum(S[n] * r, axis=0, keepdims=True)
                    y8[t] = jnp.where(rowid == ii, y, y8[t])
            for n, ii in enumerate(ids):
                so_ref[0, base + ii] = S[n]
        for t in range(nt):
            y_ref[t, pl.ds(base, SUB), :] = y8[t]
        return carry

    lax.fori_loop(0, H // SUB, body, 0)
    for t in range(nt):
        y = y_ref[t]
        mean = jnp.sum(y, axis=0, keepdims=True) * (1.0 / H)
        d = y - mean
        var = jnp.sum(d * d, axis=0, keepdims=True) * (1.0 / H)
        o_ref[t] = (d * lax.rsqrt(var + GN_EPS) * lnw_ref[...] + lnb_ref[...] + bonus_ref[t]) * g_ref[t]


def _wkv_sample(state_t, prep, lnw_b, lnb_b):
    nh, hd, _, nb = state_t.shape
    nt = prep[0].shape[0]
    st_spec = pl.BlockSpec((1, hd, hd, nb), lambda h: (h, 0, 0, 0))
    ch_spec = pl.BlockSpec((nt, hd, nb), lambda h: (0, h, 0))
    ln_spec = pl.BlockSpec((hd, nb), lambda h: (h, 0))
    return pl.pallas_call(
        functools.partial(_wkv_sample_kernel, nt=nt),
        grid=(nh,),
        in_specs=[st_spec] + [ch_spec] * len(PREP_OUT) + [ln_spec, ln_spec],
        out_specs=[st_spec, ch_spec],
        out_shape=[jax.ShapeDtypeStruct(state_t.shape, F32), jax.ShapeDtypeStruct((nt, nh * hd, nb), F32)],
        scratch_shapes=[pltpu.VMEM((nt, hd, nb), F32)],
        compiler_params=_cparams(("parallel",)),
        name="wkv_sample",
    )(state_t, *prep, lnw_b, lnb_b)


def _swa_prompt_kernel(slope_ref, sink_ref, q_ref, kc_ref, kp_ref, vc_ref, vp_ref, o_ref, bias_ref):
    n = pl.program_id(1)
    blk = WINDOW
    H = HEAD_DIM

    @pl.when((pl.program_id(0) == 0) & (n == 0))
    def _():
        t = lax.broadcasted_iota(jnp.int32, (blk, 2 * blk), 0)
        j = lax.broadcasted_iota(jnp.int32, (blk, 2 * blk), 1)
        dist = t - j + blk
        band = (dist >= 0) & (dist <= WINDOW)
        first = band & (j >= blk)
        distf = dist.astype(F32)
        for h in range(N_HEADS):
            ab = -slope_ref[h] * distf
            bias_ref[0, h] = jnp.where(first, ab, -jnp.inf)
            bias_ref[1, h] = jnp.where(band, ab, -jnp.inf)

    sel = jnp.where(n == 0, 0, 1)
    low = lax.broadcasted_iota(jnp.int32, (blk, LANES), 1) < H
    scale = H ** -0.5
    tile = lambda i: slice(i * LANES, (i + 1) * LANES)

    def kv_group(KV):
        kslab = {hk: jnp.concatenate([kp_ref[0, :, tile(hk // 2)], kc_ref[0, :, tile(hk // 2)]],
                                     axis=0).astype(BF16) for hk in KV}
        vslab = {hk: jnp.concatenate([vp_ref[0, :, tile(hk // 2)], vc_ref[0, :, tile(hk // 2)]],
                                     axis=0).astype(BF16) for hk in KV}
        lhs = {}
        for hk in KV:
            parts = []
            for s2 in range(2):
                xs = q_ref[0, :, tile(2 * hk + s2)].astype(F32) * scale
                xr = pltpu.roll(xs, H, axis=1)
                if hk % 2 == 0:
                    parts += [jnp.where(low, xs, 0.0), jnp.where(low, xr, 0.0)]
                else:
                    parts += [jnp.where(low, 0.0, xr), jnp.where(low, 0.0, xs)]
            lhs[hk] = jnp.concatenate(parts, axis=0).astype(BF16)
        s = {hk: _dot_nt(lhs[hk], kslab[hk]) for hk in KV}
        p, rden = {}, {}
        for hk in KV:
            ps, rs = [], []
            for g in range(GQA_GROUP):
                h = hk * GQA_GROUP + g
                sg = s[hk][g * blk:(g + 1) * blk] + bias_ref[sel, h]
                m = jnp.maximum(jnp.max(sg, axis=-1, keepdims=True), sink_ref[h])
                e = jnp.exp(sg - m)
                rs.append(1.0 / (jnp.sum(e, axis=-1, keepdims=True) + jnp.exp(sink_ref[h] - m)))
                ps.append(e.astype(BF16))
            p[hk] = jnp.concatenate(ps, axis=0)
            rden[hk] = rs
        o = {hk: _dot(p[hk], vslab[hk]) for hk in KV}
        for hk in KV:
            for s2 in range(2):
                ga, gb = 2 * s2, 2 * s2 + 1
                oa = o[hk][ga * blk:(ga + 1) * blk] * rden[hk][ga]
                ob = o[hk][gb * blk:(gb + 1) * blk] * rden[hk][gb]
                if hk % 2 == 0:
                    out = jnp.where(low, oa, pltpu.roll(ob, H, axis=1))
                else:
                    out = jnp.where(low, pltpu.roll(oa, H, axis=1), ob)
                o_ref[0, :, tile(2 * hk + s2)] = out.astype(o_ref.dtype)

    for g0 in range(0, N_KV_HEADS, KV_PER_GROUP):
        kv_group(range(g0, g0 + KV_PER_GROUP))


def _swa_prompt(q3, kv3, slopes, sinks):
    b, t, _ = q3.shape
    nb = t // WINDOW
    smem = pl.BlockSpec(memory_space=pltpu.SMEM)
    prev = lambda n: jnp.maximum(n - 1, 0)
    return pl.pallas_call(
        _swa_prompt_kernel,
        grid=(b, nb),
        in_specs=[smem, smem,
                  pl.BlockSpec((1, WINDOW, D_MODEL), lambda bi, n: (bi, n, 0)),
                  pl.BlockSpec((1, WINDOW, D_KV), lambda bi, n: (bi, n, 0)),
                  pl.BlockSpec((1, WINDOW, D_KV), lambda bi, n: (bi, prev(n), 0)),
                  pl.BlockSpec((1, WINDOW, D_KV), lambda bi, n: (bi, n, 1)),
                  pl.BlockSpec((1, WINDOW, D_KV), lambda bi, n: (bi, prev(n), 1))],
        out_specs=pl.BlockSpec((1, WINDOW, D_MODEL), lambda bi, n: (bi, n, 0)),
        out_shape=jax.ShapeDtypeStruct((b, t, D_MODEL), BF16),
        scratch_shapes=[pltpu.VMEM((2, N_HEADS, WINDOW, 2 * WINDOW), F32)],
        compiler_params=_cparams(("arbitrary", "arbitrary")),
        name="swa_prompt",
    )(slopes, sinks, q3, kv3, kv3, kv3, kv3)


def _swa_sample_kernel(slope_ref, sink_ref, q_ref, kc_ref, vc_ref, knew_ref, vnew_ref,
                       o_ref, kwin_ref, vwin_ref, nbuf_ref, *, tq, BB, UNR):
    GT = GQA_GROUP * tq
    R = N_KV_HEADS * GT
    NP = 16
    C = D_KV
    row = lax.broadcasted_iota(jnp.int32, (R, WINDOW), 0)
    wcol = lax.broadcasted_iota(jnp.int32, (R, WINDOW), 1)
    t = lax.rem(row, tq)
    slope = slope_ref[...]
    sink = sink_ref[...]
    dist_o = WINDOW + t - wcol
    bias_old = jnp.where(dist_o <= WINDOW, -slope * dist_o.astype(F32), -jnp.inf)
    s_idx = wcol - (WINDOW - tq)
    dist_n = t - s_idx
    bias_new = jnp.where((s_idx >= 0) & (dist_n >= 0), -slope * dist_n.astype(F32), -jnp.inf)
    hkmask = (lax.broadcasted_iota(jnp.int32, (R, C), 0) // GT) == (lax.broadcasted_iota(jnp.int32, (R, C), 1) // HEAD_DIM)
    srow = lax.broadcasted_iota(jnp.int32, (NP, WINDOW), 0)
    scol = lax.broadcasted_iota(jnp.int32, (NP, WINDOW), 1)
    selw = ((scol == srow + (WINDOW - tq)) & (srow < tq)).astype(BF16)
    lane_new = lax.broadcasted_iota(jnp.int32, (C, WINDOW), 1) >= WINDOW - tq
    scale = HEAD_DIM ** -0.5
    for u in range(UNR):
        nbuf_ref[u, :, tq:NP, :] = jnp.zeros((2, NP - tq, C), F32)

    def transposed_new(x):
        return sum(_dot_tn(part, selw) for part in _split3(x))

    def body(i, carry):
        bs = [i * UNR + u for u in range(UNR)]
        U = range(UNR)
        for u in U:
            nbuf_ref[u, 0, 0:tq, :] = knew_ref[bs[u]]
            nbuf_ref[u, 1, 0:tq, :] = vnew_ref[bs[u]]
        kt = [kc_ref[b] for b in bs]
        vt = [vc_ref[b] for b in bs]
        knt = [transposed_new(nbuf_ref[u, 0]) for u in U]
        vnt = [transposed_new(nbuf_ref[u, 1]) for u in U]
        qbd = [jnp.where(hkmask, jnp.concatenate([q_ref[b] * scale] * N_KV_HEADS, axis=0), 0.0).astype(BF16)
               for b in bs]
        s_o = [_dot(qbd[u], kt[u].astype(BF16)) + bias_old for u in U]
        s_n = [_dot(qbd[u], knt[u].astype(BF16)) + bias_new for u in U]
        outs = []
        for u in U:
            m = jnp.maximum(jnp.maximum(jnp.max(s_o[u], axis=-1, keepdims=True),
                                        jnp.max(s_n[u], axis=-1, keepdims=True)), sink)
            p_o = jnp.exp(s_o[u] - m)
            p_n = jnp.exp(s_n[u] - m)
            rden = 1.0 / (jnp.sum(p_o, axis=-1, keepdims=True) + jnp.sum(p_n, axis=-1, keepdims=True)
                          + jnp.exp(sink - m))
            o = _dot_nt(p_o.astype(BF16), vt[u].astype(BF16)) + _dot_nt(p_n.astype(BF16), vnt[u].astype(BF16))
            o = jnp.where(hkmask, o * rden, 0.0)
            acc = o[0:GT]
            for hk in range(1, N_KV_HEADS):
                acc = acc + o[hk * GT:(hk + 1) * GT]
            outs.append(acc)
        for u in U:
            kwin_ref[bs[u]] = jnp.where(lane_new, knt[u], pltpu.roll(kt[u], WINDOW - tq, axis=1))
            vwin_ref[bs[u]] = jnp.where(lane_new, vnt[u], pltpu.roll(vt[u], WINDOW - tq, axis=1))
            o_ref[bs[u]] = outs[u]
        return carry

    lax.fori_loop(0, BB // UNR, body, 0)


def _swa_sample(q16, knew3, vnew3, kct, vct, slope_rows, sink_rows):
    b, gt, c = q16.shape
    tq = gt // GQA_GROUP
    bb = 8 if b % 8 == 0 else 1
    unr = 2 if bb % 2 == 0 else 1
    rows = N_KV_HEADS * gt
    blk3 = lambda shape: pl.BlockSpec(shape, lambda i: (i, 0, 0))
    full2 = pl.BlockSpec((rows, 1), lambda i: (0, 0))
    kern = functools.partial(_swa_sample_kernel, tq=tq, BB=bb, UNR=unr)
    return pl.pallas_call(
        kern,
        grid=(b // bb,),
        in_specs=[full2, full2, blk3((bb, gt, c)), blk3((bb, c, WINDOW)), blk3((bb, c, WINDOW)),
                  blk3((bb, tq, c)), blk3((bb, tq, c))],
        out_specs=[blk3((bb, gt, c)), blk3((bb, c, WINDOW)), blk3((bb, c, WINDOW))],
        out_shape=[jax.ShapeDtypeStruct((b, gt, c), F32),
                   jax.ShapeDtypeStruct((b, c, WINDOW), F32),
                   jax.ShapeDtypeStruct((b, c, WINDOW), F32)],
        scratch_shapes=[pltpu.VMEM((unr, 2, 16, c), F32)],
        compiler_params=_cparams(("parallel",)),
        name="swa_sample",
    )(slope_rows, sink_rows, q16, kct, vct, knew3, vnew3)


def _merge_out_kernel(x_ref, ya_ref, yb_ref, ga_ref, gb_ref, wo_ref, nw_ref, h_ref, hn_ref):
    f32 = lambda ref: ref[...].astype(F32)
    mixed = _sigmoid(f32(ga_ref)) * f32(ya_ref) + _sigmoid(f32(gb_ref)) * f32(yb_ref)
    h = x_ref[...] + _dot(mixed.astype(BF16), wo_ref[...])
    h_ref[...] = h
    ms = jnp.mean(h * h, axis=-1, keepdims=True)
    hn_ref[...] = (h * lax.rsqrt(ms + RMS_EPS) * nw_ref[...]).astype(BF16)


def _merge_out(x, ya, yb, gates, w_out, nw, tm):
    m, d = x.shape
    row = lambda o: (lambda i: (i, o))
    return pl.pallas_call(
        _merge_out_kernel,
        grid=(m // tm,),
        in_specs=[pl.BlockSpec((tm, d), row(0)), pl.BlockSpec((tm, d), row(0)), pl.BlockSpec((tm, d), row(0)),
                  pl.BlockSpec((tm, d), row(0)), pl.BlockSpec((tm, d), row(1)),
                  pl.BlockSpec((d, d), lambda i: (0, 0)), pl.BlockSpec((1, d), lambda i: (0, 0))],
        out_specs=[pl.BlockSpec((tm, d), row(0)), pl.BlockSpec((tm, d), row(0))],
        out_shape=[jax.ShapeDtypeStruct((m, d), F32), jax.ShapeDtypeStruct((m, d), BF16)],
        compiler_params=_cparams(("parallel",)),
        name="merge_out_proj",
    )(x, ya, yb, gates, gates, w_out, nw)


def _mlp_kernel(hn_ref, h_ref, wu_ref, wd_ref, nw_ref, o_ref, acc_ref):
    j = pl.program_id(1)

    @pl.when(j == 0)
    def _():
        acc_ref[...] = jnp.zeros_like(acc_ref)

    u = jnp.maximum(_dot(hn_ref[...], wu_ref[...]), 0.0)
    acc_ref[...] += _dot((u * u).astype(BF16), wd_ref[...])

    @pl.when(j == pl.num_programs(1) - 1)
    def _():
        h = h_ref[...] + acc_ref[...]
        ms = jnp.mean(h * h, axis=-1, keepdims=True)
        o_ref[...] = h * lax.rsqrt(ms + RMS_EPS) * nw_ref[...]


def _mlp(hn, h, w_up, w_down, nw, tm, tf):
    m, d = h.shape
    f = w_up.shape[1]
    return pl.pallas_call(
        _mlp_kernel,
        grid=(m // tm, f // tf),
        in_specs=[pl.BlockSpec((tm, d), lambda i, j: (i, 0)), pl.BlockSpec((tm, d), lambda i, j: (i, 0)),
                  pl.BlockSpec((d, tf), lambda i, j: (0, j)), pl.BlockSpec((tf, d), lambda i, j: (j, 0)),
                  pl.BlockSpec((1, d), lambda i, j: (0, 0))],
        out_specs=pl.BlockSpec((tm, d), lambda i, j: (i, 0)),
        out_shape=jax.ShapeDtypeStruct((m, d), F32),
        scratch_shapes=[pltpu.VMEM((tm, d), F32)],
        compiler_params=_cparams(("parallel", "arbitrary")),
        name="mlp_final_norm",
    )(hn, h, w_up, w_down, nw)


def _pick(m, prefs):
    for t in prefs:
        if m % t == 0:
            return t
    return m


def _pad_cols(v, n):
    return jnp.concatenate([v, jnp.zeros(v.shape[:-1] + (n - v.shape[-1],), v.dtype)], axis=-1)


def _place_rows(w, start, rows):
    n, d = w.shape
    return jnp.concatenate([jnp.zeros((start, d), w.dtype), w, jnp.zeros((rows - start - n, d), w.dtype)], axis=0)


def _layer(x, shift_prev, wkv0, lw, *, prompt, k_cache_t=None, v_cache_t=None):
    b, t, d = x.shape
    m = b * t
    x2 = x.reshape(m, d) if prompt else x.transpose(1, 0, 2).reshape(m, d)
    proj_r, proj_q, proj_kv, proj_g = _in_proj(
        x2, lw["norm_mix_w"], lw["w_t"], IN_GROUPS, [F32, BF16, F32, BF16], _pick(m, (1024, 512, 256, 128, 8)))
    shift_pad = _pad_cols(shift_prev, R_PAD)

    if prompt:
        proj_r3 = proj_r.reshape(b, t, R_PAD)
        kv3 = proj_kv.reshape(b, t, 2 * D_KV)
        ya, wkv_new = _rwkv_prompt(proj_r3, shift_pad[:, None], wkv0, lw)
        yb = _swa_prompt(proj_q.reshape(b, t, D_MODEL), kv3, lw["slopes"], lw["sinks"])
        k_win = kv3[:, t - WINDOW:, :D_KV].reshape(b, WINDOW, N_KV_HEADS, HEAD_DIM)
        v_win = kv3[:, t - WINDOW:, D_KV:].reshape(b, WINDOW, N_KV_HEADS, HEAD_DIM)
        shift_new = proj_r3[:, t - 1, :R_COLS]
    else:
        prep = _rwkv_sample_prep(proj_r, shift_pad, lw, b, t)
        wkv_t, ya_t = _wkv_sample(wkv0.transpose(1, 2, 3, 0), prep,
                                  jnp.broadcast_to(lw["ln_w"].reshape(D_RWKV, 1), (D_RWKV, b)),
                                  jnp.broadcast_to(lw["ln_b"].reshape(D_RWKV, 1), (D_RWKV, b)))
        ya = ya_t.transpose(0, 2, 1)
        wkv_new = wkv_t.transpose(3, 0, 1, 2)
        q16 = proj_q.reshape(t, b, N_KV_HEADS, GQA_GROUP, HEAD_DIM).transpose(1, 3, 0, 2, 4)
        q16 = q16.reshape(b, GQA_GROUP * t, D_KV)
        kv_bt = proj_kv.reshape(t, b, 2 * D_KV).transpose(1, 0, 2)
        gt_head = (jnp.arange(N_KV_HEADS)[:, None] * GQA_GROUP + jnp.arange(GQA_GROUP)[None, :])
        row_head = jnp.repeat(gt_head, t, axis=1).reshape(-1)
        o16, kwt, vwt = _swa_sample(q16, kv_bt[:, :, :D_KV], kv_bt[:, :, D_KV:], k_cache_t, v_cache_t,
                                    lw["slopes"][row_head][:, None], lw["sinks"][row_head][:, None])
        yb = o16.reshape(b, GQA_GROUP, t, N_KV_HEADS, HEAD_DIM).transpose(2, 0, 3, 1, 4)
        k_win = kwt.reshape(b, N_KV_HEADS, HEAD_DIM, WINDOW).transpose(0, 3, 1, 2)
        v_win = vwt.reshape(b, N_KV_HEADS, HEAD_DIM, WINDOW).transpose(0, 3, 1, 2)
        shift_new = proj_r[(t - 1) * b:, :R_COLS]

    h, hn = _merge_out(x2, ya.reshape(m, d), yb.reshape(m, d), proj_g, lw["w_out"], lw["norm_mlp_w"],
                       _pick(m, (256, 128, 8)))
    y = _mlp(hn, h, lw["w_up"], lw["w_down"], lw["norm_final_w"], _pick(m, (512, 256, 128, 8)), 1024)
    y = y.reshape(b, t, d) if prompt else y.reshape(t, b, d).transpose(1, 0, 2)
    return y, shift_new, wkv_new, k_win, v_win


def kernel(x_prompt, x_sample, state_shift, state_wkv, cache_k_win, cache_v_win, norm_mix_w, w_in, tshift_mu, w0, w_lora, a0, a_lora, g_lora, k_k, k_a, r_k, ln_x_w, ln_x_b, attn_sinks, w_out, norm_mlp_w, w_up, w_down, norm_final_w):
    depth = w_in.shape[0]
    assert depth == 1
    l = 0
    bp = x_prompt.shape[0]
    db = x_sample.shape[0]
    hh = jnp.arange(N_HEADS, dtype=F32)
    lw = dict(
        norm_mix_w=norm_mix_w[l][None],
        w_t=jnp.swapaxes(w_in[l], 0, 1).astype(BF16),
        mu=_pad_cols(tshift_mu[l][None], R_PAD), w0=w0[l][None], a0=a0[l][None], k_k=k_k[l][None],
        k_a=k_a[l][None], r_k=r_k[l].reshape(1, D_RWKV), ln_w=ln_x_w[l][None], ln_b=ln_x_b[l][None],
        w_lora=_place_rows(w_lora[l], 0, LANES).astype(BF16),
        a_lora=_place_rows(a_lora[l], LORA_DECAY_END, LORA_A_WIN).astype(BF16),
        g_lora=_place_rows(g_lora[l], LORA_A_END - LORA_G_START, LORA_W - LORA_G_START).astype(BF16),
        slopes=jnp.exp2(-8.0 * (hh + 1.0) / N_HEADS), sinks=attn_sinks[l].astype(F32),
        w_out=w_out[l].astype(BF16), norm_mlp_w=norm_mlp_w[l][None],
        w_up=w_up[l].astype(BF16), w_down=w_down[l].astype(BF16), norm_final_w=norm_final_w[None],
    )
    yp, sp, wp, kp, vp = _layer(
        x_prompt, jnp.zeros((bp, R_COLS), F32), jnp.zeros((bp, N_HEADS, HEAD_DIM, HEAD_DIM), F32), lw, prompt=True)
    kct = cache_k_win[l].transpose(0, 2, 3, 1).reshape(db, D_KV, WINDOW)
    vct = cache_v_win[l].transpose(0, 2, 3, 1).reshape(db, D_KV, WINDOW)
    ys, ss, ws, ksm, vsm = _layer(x_sample, state_shift[l], state_wkv[l], lw, prompt=False,
                                  k_cache_t=kct, v_cache_t=vct)
    return (yp, ys, sp[None], wp[None], kp[None], vp[None], ss[None], ws[None], ksm[None], vsm[None])
```

```python
import functools
import math

import jax
import jax.numpy as jnp
from jax import lax
from jax.experimental import pallas as pl
from jax.experimental.pallas import tpu as pltpu

F32 = jnp.float32
BF16 = jnp.bfloat16

D_MODEL = 2048
HEAD_DIM = 64
N_HEADS = D_MODEL // HEAD_DIM
N_KV_HEADS = 8
GQA_GROUP = N_HEADS // N_KV_HEADS
D_KV = N_KV_HEADS * HEAD_DIM
WINDOW = 128
D_FF = 4 * D_MODEL
D_DECAY_LORA = 96
D_A_LORA = 96
D_GATE_LORA = 256
D_RWKV = D_MODEL
R_COLS = 3 * D_RWKV + D_DECAY_LORA + D_A_LORA + D_GATE_LORA
C_IN = R_COLS + D_MODEL + 2 * D_KV + 2 * D_MODEL
RMS_EPS = 1e-5
GN_EPS = 64e-5

LANES = 128
PROJ_TILE = 512
R_PAD = -(-R_COLS // PROJ_TILE) * PROJ_TILE
LORA_W = R_PAD - 3 * D_RWKV
LORA_DECAY_END = D_DECAY_LORA
LORA_A_END = D_DECAY_LORA + D_A_LORA
LORA_A_WIN = -(-LORA_A_END // LANES) * LANES
LORA_G_START = (LORA_A_END // LANES) * LANES
IN_GROUPS = ((0, R_PAD), (R_COLS, D_MODEL), (R_COLS + D_MODEL, 2 * D_KV), (R_COLS + D_MODEL + 2 * D_KV, 2 * D_MODEL))

VMEM_LIMIT = 56 * 1024 * 1024
RWKV_CHUNK = 64
KV_PER_GROUP = 2
SAMPLE_PREP_W = 512
PREP_OUT = ("r", "w", "k", "v", "a", "b", "g", "bonus")


def _cparams(sem):
    return pltpu.CompilerParams(dimension_semantics=sem, vmem_limit_bytes=VMEM_LIMIT)


def _dot(a, b):
    return jnp.dot(a, b, preferred_element_type=F32)


def _dot_nt(a, b):
    return lax.dot_general(a, b, (((1,), (1,)), ((), ())), preferred_element_type=F32)


def _dot_tn(a, b):
    return lax.dot_general(a, b, (((0,), (0,)), ((), ())), preferred_element_type=F32)


def _sigmoid(x):
    return 1.0 / (1.0 + jnp.exp(-x))


def _tanh(x):
    return 1.0 - 2.0 / (1.0 + jnp.exp(2.0 * x))


def _split2(x):
    hi = x.astype(BF16)
    lo = (x - hi.astype(F32)).astype(BF16)
    return hi, lo


def _split3(x):
    hi = x.astype(BF16)
    r1 = x - hi.astype(F32)
    mid = r1.astype(BF16)
    lo = (r1 - mid.astype(F32)).astype(BF16)
    return hi, mid, lo


def _head_pair_mask():
    ji = lax.broadcasted_iota(jnp.int32, (LANES, LANES), 0)
    jj = lax.broadcasted_iota(jnp.int32, (LANES, LANES), 1)
    return (ji < HEAD_DIM) == (jj < HEAD_DIM)


def _seg_sum(x, seg, rows):
    n = x.shape[1] // LANES
    xs = jnp.concatenate([x[:, i * LANES:(i + 1) * LANES] for i in range(n)], axis=0).astype(BF16)
    ys = _dot(xs, seg)
    return jnp.concatenate([ys[i * rows:(i + 1) * rows] for i in range(n)], axis=1)


def _in_proj_kernel(x_ref, nw_ref, *refs, bounds):
    n = len(bounds)
    w_refs, o_refs, xn_ref = refs[:n], refs[n:2 * n], refs[2 * n]
    j = pl.program_id(1)

    @pl.when(j == 0)
    def _():
        x = x_ref[...]
        ms = jnp.mean(x * x, axis=-1, keepdims=True)
        xn_ref[...] = (x * lax.rsqrt(ms + RMS_EPS) * nw_ref[...]).astype(BF16)

    for w_ref, o_ref, (lo, hi) in zip(w_refs, o_refs, bounds):
        @pl.when((j >= lo) & (j < hi))
        def _(w_ref=w_ref, o_ref=o_ref):
            o_ref[...] = _dot_nt(xn_ref[...], w_ref[...]).astype(o_ref.dtype)


def _in_proj(x, nw, w_t, groups, out_dtypes, tm):
    m, d = x.shape
    tn = PROJ_TILE
    bounds, lo = [], 0
    for _, width in groups:
        bounds.append((lo, lo + width // tn))
        lo = bounds[-1][1]
    ROW_ALIGN = 16
    assert all(start % ROW_ALIGN == 0 and start + width <= w_t.shape[0] for start, width in groups)

    def clamp(lo_, hi_):
        return lambda j: jnp.clip(j - lo_, 0, hi_ - lo_ - 1)

    in_specs = [pl.BlockSpec((tm, d), lambda i, j: (i, 0)), pl.BlockSpec((1, d), lambda i, j: (0, 0))]
    out_specs, out_shape = [], []
    for (start, width), dt, (lo_, hi_) in zip(groups, out_dtypes, bounds):
        c = clamp(lo_, hi_)
        in_specs.append(pl.BlockSpec((pl.Element(tn), pl.Element(d)),
                                     lambda i, j, c=c, start=start: (pl.multiple_of(start + c(j) * tn, ROW_ALIGN), 0)))
        out_specs.append(pl.BlockSpec((tm, tn), lambda i, j, c=c: (i, c(j))))
        out_shape.append(jax.ShapeDtypeStruct((m, width), dt))
    return pl.pallas_call(
        functools.partial(_in_proj_kernel, bounds=tuple(bounds)),
        grid=(m // tm, lo),
        in_specs=in_specs,
        out_specs=out_specs,
        out_shape=out_shape,
        scratch_shapes=[pltpu.VMEM((tm, d), BF16)],
        compiler_params=_cparams(("parallel", "arbitrary")),
        name="norm_in_proj",
    )(x, nw, *([w_t] * len(groups)))


def _rwkv_prompt_kernel(pr_ref, pk_ref, pv_ref, plo_ref, shr_ref, shk_ref, shv_ref, shl_ref, s0_ref,
                        mur_ref, muk_ref, muv_ref, mul_ref, w0_ref, a0_ref, kk_ref, ka_ref, rk_ref, lnw_ref, lnb_ref,
                        wl_ref, al_ref, gl_ref,
                        y_ref, so_ref,
                        sd_ref, br_ref, bk_ref, bv_ref, bl_ref, *, L, NC):
    c = pl.program_id(1)
    H = HEAD_DIM
    NP = D_RWKV // LANES
    P = range(NP)
    sls = [slice(i * LANES, (i + 1) * LANES) for i in P]
    n_dbl = max(1, math.ceil(math.log2(L)))

    head1 = lax.broadcasted_iota(jnp.int32, (L, LANES), 1) < H
    row_i = lax.broadcasted_iota(jnp.int32, (L, 2 * L), 0)
    col_i = lax.broadcasted_iota(jnp.int32, (L, 2 * L), 1)
    col_t = jnp.where(col_i >= L, col_i - L, col_i)
    strict = col_t < row_i
    incl = col_t <= row_i
    colh1 = col_i < L
    tri = (lax.broadcasted_iota(jnp.int32, (L, L), 1) <= lax.broadcasted_iota(jnp.int32, (L, L), 0)).astype(BF16)
    same_head = _head_pair_mask()
    seg = same_head.astype(BF16)
    seg_sum = functools.partial(_seg_sum, seg=seg, rows=L)

    def stack_heads(x, m):
        zero = jnp.zeros_like(x)
        return jnp.concatenate([jnp.where(m, x, zero), jnp.where(m, zero, x)], axis=0)

    @pl.when(c == 0)
    def _init():
        br_ref[7:8, :] = shr_ref[0]
        bk_ref[7:8, :] = shk_ref[0]
        bv_ref[7:8, :] = shv_ref[0]
        bl_ref[7:8, :] = shl_ref[0]
        z = jnp.zeros((H, H), F32)
        for i in P:
            top = jnp.concatenate([s0_ref[0, 2 * i], z], axis=1)
            bot = jnp.concatenate([z, s0_ref[0, 2 * i + 1]], axis=1)
            sd_ref[i] = jnp.concatenate([top, bot], axis=0)

    br_ref[8:8 + L, :] = pr_ref[0]
    bk_ref[8:8 + L, :] = pk_ref[0]
    bv_ref[8:8 + L, :] = pv_ref[0]
    bl_ref[8:8 + L, :] = plo_ref[0]

    def shifted(buf, mu_ref):
        p = buf[8:8 + L, :]
        return p + mu_ref[...] * (buf[7:7 + L, :] - p)

    ps_l = shifted(bl_ref, mul_ref)
    td = _tanh(ps_l[:, 0:LANES]).astype(BF16)
    da = ps_l[:, 0:LORA_A_WIN].astype(BF16)
    sg = _sigmoid(ps_l[:, LORA_G_START:]).astype(BF16)
    r = shifted(br_ref, mur_ref)
    k = shifted(bk_ref, muk_ref)
    v = shifted(bv_ref, muv_ref)

    zlog = w0_ref[...] + _dot(td, wl_ref[...])
    logw = -math.exp(-0.5) / (1.0 + jnp.exp(-zlog))
    a_sig = _sigmoid(a0_ref[...] + _dot(da, al_ref[...]))
    kk = k * kk_ref[...]
    kk = kk * jnp.minimum(lax.rsqrt(jnp.maximum(seg_sum(kk * kk), 0.0)), 1e12)
    k_h = k * (1.0 + (a_sig - 1.0) * ka_ref[...])
    hi, lo = _split2(logw)
    cum = _dot(tri, hi) + _dot(tri, lo)
    cum_l = cum[L - 1:L, :]
    p_inv = jnp.exp(-cum)
    p_end = jnp.exp(cum_l - cum)
    p_l = jnp.exp(cum_l)
    bvec = kk * a_sig
    a_t = (-kk * jnp.exp(cum - logw)).astype(BF16)
    r_t = (r * jnp.exp(cum)).astype(BF16)
    b_t = (bvec * p_inv).astype(BF16)
    k_t = (k_h * p_inv).astype(BF16)
    b_e = (bvec * p_end).astype(BF16)
    k_e = (k_h * p_end).astype(BF16)
    v_b = v.astype(BF16)
    g = _dot(sg, gl_ref[...])
    bonus = seg_sum(r * k_h * rk_ref[...]) * v

    sd = [sd_ref[i] for i in P]
    lhs = [jnp.concatenate([a_t[:, s], r_t[:, s]], axis=0) for s in sls]
    rhs = [jnp.concatenate([stack_heads(b_t[:, s], head1), stack_heads(k_t[:, s], head1)], axis=0) for s in sls]
    aa = [_dot_nt(lhs[i], rhs[i]) for i in P]
    sa = [_dot_nt(lhs[i], sd[i].astype(BF16)) for i in P]
    v_st = [stack_heads(v_b[:, s], head1) for s in sls]
    a_ak = [jnp.where(strict, aa[i][0:L, 2 * L:4 * L], 0.0).astype(BF16) for i in P]
    x = [sa[i][0:L] + _dot(a_ak[i], v_st[i]) for i in P]
    ap = [jnp.where(strict, aa[i][0:L, 0:2 * L], 0.0).astype(BF16) for i in P]
    for d in range(n_dbl):
        xs = [stack_heads(x[i].astype(BF16), head1) for i in P]
        if d + 1 < n_dbl:
            both = [_dot(ap[i], jnp.concatenate([xs[i], stack_heads(ap[i], colh1)], axis=1)) for i in P]
            x = [x[i] + both[i][:, 0:2 * L] for i in P]
            ap = [both[i][:, 2 * L:].astype(BF16) for i in P]
        else:
            x = [x[i] + _dot(ap[i], xs[i]) for i in P]
    u_b = [x[i].astype(BF16) for i in P]
    a_rb = [jnp.where(incl, aa[i][L:2 * L, 0:2 * L], 0.0).astype(BF16) for i in P]
    a_rk = [jnp.where(incl, aa[i][L:2 * L, 2 * L:4 * L], 0.0).astype(BF16) for i in P]
    y = [sa[i][L:2 * L] + _dot(jnp.concatenate([a_rb[i], a_rk[i]], axis=1),
                               jnp.concatenate([stack_heads(u_b[i], head1), v_st[i]], axis=0)) for i in P]
    ds = [_dot_tn(jnp.concatenate([u_b[i], v_b[:, sls[i]]], axis=0),
                  jnp.concatenate([b_e[:, sls[i]], k_e[:, sls[i]]], axis=0)) for i in P]
    for i in P:
        sd_ref[i] = sd[i] * p_l[:, sls[i]] + jnp.where(same_head, ds[i], 0.0)

    y = jnp.concatenate(y, axis=1)
    mean = seg_sum(y) * (1.0 / H)
    dlt = y - mean
    var = seg_sum(dlt * dlt) * (1.0 / H)
    yn = dlt * lax.rsqrt(var + GN_EPS) * lnw_ref[...] + lnb_ref[...]
    y_ref[0] = ((yn + bonus) * g).astype(y_ref.dtype)

    br_ref[7:8, :] = br_ref[7 + L:8 + L, :]
    bk_ref[7:8, :] = bk_ref[7 + L:8 + L, :]
    bv_ref[7:8, :] = bv_ref[7 + L:8 + L, :]
    bl_ref[7:8, :] = bl_ref[7 + L:8 + L, :]

    @pl.when(c == NC - 1)
    def _finish():
        for i in P:
            s = sd_ref[i]
            so_ref[0, 2 * i] = s[0:H, 0:H]
            so_ref[0, 2 * i + 1] = s[H:2 * H, H:2 * H]


def _rwkv_prompt(proj3, shift3, s0, p):
    b, t, _ = proj3.shape
    L = RWKV_CHUNK
    assert t % L == 0 and 2 * L == LANES
    nc = t // L
    w = D_RWKV
    lora_blk = 3 * D_RWKV // LORA_W
    col = lambda o: (lambda bi, c: (bi, c, o))
    sh = lambda o: (lambda bi, c: (bi, 0, o))
    par = lambda o: (lambda bi, c: (0, o))
    in_specs = [
        pl.BlockSpec((1, L, w), col(0)), pl.BlockSpec((1, L, w), col(1)), pl.BlockSpec((1, L, w), col(2)),
        pl.BlockSpec((1, L, LORA_W), col(lora_blk)),
        pl.BlockSpec((1, 1, w), sh(0)), pl.BlockSpec((1, 1, w), sh(1)), pl.BlockSpec((1, 1, w), sh(2)),
        pl.BlockSpec((1, 1, LORA_W), sh(lora_blk)),
        pl.BlockSpec((1, N_HEADS, HEAD_DIM, HEAD_DIM), lambda bi, c: (bi, 0, 0, 0)),
        pl.BlockSpec((1, w), par(0)), pl.BlockSpec((1, w), par(1)), pl.BlockSpec((1, w), par(2)),
        pl.BlockSpec((1, LORA_W), par(lora_blk)),
    ] + [pl.BlockSpec((1, w), par(0))] * 7 + [
        pl.BlockSpec((LANES, w), par(0)), pl.BlockSpec((LORA_A_WIN, w), par(0)),
        pl.BlockSpec((LORA_W - LORA_G_START, w), par(0)),
    ]
    out_specs = [pl.BlockSpec((1, L, w), col(0)),
                 pl.BlockSpec((1, N_HEADS, HEAD_DIM, HEAD_DIM), lambda bi, c: (bi, 0, 0, 0))]
    return pl.pallas_call(
        functools.partial(_rwkv_prompt_kernel, L=L, NC=nc),
        grid=(b, nc),
        in_specs=in_specs,
        out_specs=out_specs,
        out_shape=[jax.ShapeDtypeStruct((b, t, D_RWKV), BF16),
                   jax.ShapeDtypeStruct((b, N_HEADS, HEAD_DIM, HEAD_DIM), F32)],
        scratch_shapes=[pltpu.VMEM((D_RWKV // LANES, LANES, LANES), F32),
                        pltpu.VMEM((L + 8, w), F32), pltpu.VMEM((L + 8, w), F32),
                        pltpu.VMEM((L + 8, w), F32), pltpu.VMEM((L + 8, LORA_W), F32)],
        compiler_params=_cparams(("parallel", "arbitrary")),
        name="rwkv_mix",
    )(proj3, proj3, proj3, proj3, shift3, shift3, shift3, shift3, s0,
      p["mu"], p["mu"], p["mu"], p["mu"], p["w0"], p["a0"], p["k_k"], p["k_a"], p["r_k"],
      p["ln_w"], p["ln_b"], p["w_lora"], p["a_lora"], p["g_lora"])


def _rwkv_sample_prep_kernel(pr_ref, pk_ref, pv_ref, plo_ref, shr_ref, shk_ref, shv_ref, shl_ref,
                             mur_ref, muk_ref, muv_ref, mul_ref, w0_ref, a0_ref, kk_ref, ka_ref, rk_ref,
                             wl_ref, al_ref, gl_ref, *out_refs, nb, nt):
    W = pr_ref.shape[1]
    n_tiles = W // LANES
    rows = nb * nt
    seg_sum = functools.partial(_seg_sum, seg=_head_pair_mask().astype(BF16), rows=rows)

    def shifted(p_ref, s_ref, mu_ref):
        p = p_ref[...]
        prev = jnp.concatenate([s_ref[...], p[0:rows - nb]], axis=0)
        return p + mu_ref[...] * (prev - p)

    ps_l = shifted(plo_ref, shl_ref, mul_ref)
    td = _tanh(ps_l[:, 0:LANES]).astype(BF16)
    da = ps_l[:, 0:LORA_A_WIN].astype(BF16)
    sg = _sigmoid(ps_l[:, LORA_G_START:]).astype(BF16)
    r = shifted(pr_ref, shr_ref, mur_ref)
    k = shifted(pk_ref, shk_ref, muk_ref)
    v = shifted(pv_ref, shv_ref, muv_ref)
    zlog = w0_ref[...] + _dot(td, wl_ref[...])
    w = jnp.exp(-math.exp(-0.5) / (1.0 + jnp.exp(-zlog)))
    a_sig = _sigmoid(a0_ref[...] + _dot(da, al_ref[...]))
    kk = k * kk_ref[...]
    kk = kk * jnp.minimum(lax.rsqrt(jnp.maximum(seg_sum(kk * kk), 0.0)), 1e12)
    k_h = k * (1.0 + (a_sig - 1.0) * ka_ref[...])
    g = _dot(sg, gl_ref[...])
    bonus = seg_sum(r * k_h * rk_ref[...]) * v
    vals = dict(r=r, w=w, k=k_h, v=v, a=-kk, b=kk * a_sig, g=g, bonus=bonus)
    for name, o_ref in zip(PREP_OUT, out_refs):
        x = vals[name]
        for t in range(nt):
            for c in range(n_tiles):
                o_ref[t, c * LANES:(c + 1) * LANES, :] = x[t * nb:(t + 1) * nb, c * LANES:(c + 1) * LANES].T


def _rwkv_sample_prep(proj_r, shift, p, nb, nt):
    w = SAMPLE_PREP_W
    kb = D_RWKV // w
    lora_blk = 3 * D_RWKV // LORA_W
    rows = nt * nb
    col = lambda o: (lambda g: (0, o + g))
    in_specs = (
        [pl.BlockSpec((rows, w), col(0)), pl.BlockSpec((rows, w), col(kb)), pl.BlockSpec((rows, w), col(2 * kb)),
         pl.BlockSpec((rows, LORA_W), lambda g: (0, lora_blk))]
        + [pl.BlockSpec((nb, w), col(0)), pl.BlockSpec((nb, w), col(kb)), pl.BlockSpec((nb, w), col(2 * kb)),
           pl.BlockSpec((nb, LORA_W), lambda g: (0, lora_blk))]
        + [pl.BlockSpec((1, w), col(0)), pl.BlockSpec((1, w), col(kb)), pl.BlockSpec((1, w), col(2 * kb)),
           pl.BlockSpec((1, LORA_W), lambda g: (0, lora_blk))]
        + [pl.BlockSpec((1, w), col(0))] * 5
        + [pl.BlockSpec((LANES, w), col(0)), pl.BlockSpec((LORA_A_WIN, w), col(0)),
           pl.BlockSpec((LORA_W - LORA_G_START, w), col(0))])
    out_spec = pl.BlockSpec((nt, w, nb), lambda g: (0, g, 0))
    return pl.pallas_call(
        functools.partial(_rwkv_sample_prep_kernel, nb=nb, nt=nt),
        grid=(kb,),
        in_specs=in_specs,
        out_specs=[out_spec] * len(PREP_OUT),
        out_shape=[jax.ShapeDtypeStruct((nt, D_RWKV, nb), F32)] * len(PREP_OUT),
        compiler_params=_cparams(("parallel",)),
        name="rwkv_sample_prep",
    )(proj_r, proj_r, proj_r, proj_r, shift, shift, shift, shift, p["mu"], p["mu"], p["mu"], p["mu"],
      p["w0"], p["a0"], p["k_k"], p["k_a"], p["r_k"], p["w_lora"], p["a_lora"], p["g_lora"])


def _wkv_sample_kernel(s_ref, r_ref, w_ref, k_ref, v_ref, a_ref, b_ref, g_ref, bonus_ref, lnw_ref, lnb_ref,
                       so_ref, o_ref, y_ref, *, nt):
    H = HEAD_DIM
    SUB = 8
    PAR = 4
    rowid = lax.broadcasted_iota(jnp.int32, (SUB, LANES), 0)

    def body(i8, carry):
        base = pl.multiple_of(i8 * SUB, SUB)
        v8 = [v_ref[t, pl.ds(base, SUB), :] for t in range(nt)]
        y8 = [jnp.zeros((SUB, LANES), F32) for _ in range(nt)]
        for h0 in range(0, SUB, PAR):
            ids = list(range(h0, h0 + PAR))
            S = [s_ref[0, base + ii] for ii in ids]
            for t in range(nt):
                a, w, b, k, r = a_ref[t], w_ref[t], b_ref[t], k_ref[t], r_ref[t]
                for n, ii in enumerate(ids):
                    sa = jnp.sum(S[n] * a, axis=0, keepdims=True)
                    S[n] = S[n] * w + sa * b + v8[t][ii:ii + 1, :] * k
                    y = jnp.sum(S[n] * r, axis=0, keepdims=True)
                    y8[t] = jnp.where(rowid == ii, y, y8[t])
            for n, ii in enumerate(ids):
                so_ref[0, base + ii] = S[n]
        for t in range(nt):
            y_ref[t, pl.ds(base, SUB), :] = y8[t]
        return carry

    lax.fori_loop(0, H // SUB, body, 0)
    for t in range(nt):
        y = y_ref[t]
        mean = jnp.sum(y, axis=0, keepdims=True) * (1.0 / H)
        d = y - mean
        var = jnp.sum(d * d, axis=0, keepdims=True) * (1.0 / H)
        out = (d * lax.rsqrt(var + GN_EPS) * lnw_ref[...] + lnb_ref[...] + bonus_ref[t]) * g_ref[t]
        o_ref[t] = out.astype(o_ref.dtype)


def _wkv_sample(state_t, prep, lnw_b, lnb_b):
    nh, hd, _, nb = state_t.shape
    nt = prep[0].shape[0]
    st_spec = pl.BlockSpec((1, hd, hd, nb), lambda h: (h, 0, 0, 0))
    ch_spec = pl.BlockSpec((nt, hd, nb), lambda h: (0, h, 0))
    ln_spec = pl.BlockSpec((hd, nb), lambda h: (h, 0))
    return pl.pallas_call(
        functools.partial(_wkv_sample_kernel, nt=nt),
        grid=(nh,),
        in_specs=[st_spec] + [ch_spec] * len(PREP_OUT) + [ln_spec, ln_spec],
        out_specs=[st_spec, ch_spec],
        out_shape=[jax.ShapeDtypeStruct(state_t.shape, F32), jax.ShapeDtypeStruct((nt, nh * hd, nb), BF16)],
        scratch_shapes=[pltpu.VMEM((nt, hd, nb), F32)],
        compiler_params=_cparams(("parallel",)),
        name="wkv_sample",
    )(state_t, *prep, lnw_b, lnb_b)


def _swa_prompt_kernel(slope_ref, sink_ref, q_ref, kc_ref, kp_ref, vc_ref, vp_ref, o_ref, bias_ref):
    n = pl.program_id(1)
    blk = WINDOW
    H = HEAD_DIM

    @pl.when((pl.program_id(0) == 0) & (n == 0))
    def _():
        t = lax.broadcasted_iota(jnp.int32, (blk, 2 * blk), 0)
        j = lax.broadcasted_iota(jnp.int32, (blk, 2 * blk), 1)
        dist = t - j + blk
        band = (dist >= 0) & (dist <= WINDOW)
        first = band & (j >= blk)
        distf = dist.astype(F32)
        for h in range(N_HEADS):
            ab = -slope_ref[h] * distf
            bias_ref[0, h] = jnp.where(first, ab, -jnp.inf)
            bias_ref[1, h] = jnp.where(band, ab, -jnp.inf)

    sel = jnp.where(n == 0, 0, 1)
    low = lax.broadcasted_iota(jnp.int32, (blk, LANES), 1) < H
    scale = H ** -0.5
    tile = lambda i: slice(i * LANES, (i + 1) * LANES)

    def kv_group(KV):
        kslab = {hk: jnp.concatenate([kp_ref[0, :, tile(hk // 2)], kc_ref[0, :, tile(hk // 2)]],
                                     axis=0).astype(BF16) for hk in KV}
        vslab = {hk: jnp.concatenate([vp_ref[0, :, tile(hk // 2)], vc_ref[0, :, tile(hk // 2)]],
                                     axis=0).astype(BF16) for hk in KV}
        lhs = {}
        for hk in KV:
            parts = []
            for s2 in range(2):
                xs = q_ref[0, :, tile(2 * hk + s2)].astype(F32) * scale
                xr = pltpu.roll(xs, H, axis=1)
                if hk % 2 == 0:
                    parts += [jnp.where(low, xs, 0.0), jnp.where(low, xr, 0.0)]
                else:
                    parts += [jnp.where(low, 0.0, xr), jnp.where(low, 0.0, xs)]
            lhs[hk] = jnp.concatenate(parts, axis=0).astype(BF16)
        s = {hk: _dot_nt(lhs[hk], kslab[hk]) for hk in KV}
        p, rden = {}, {}
        for hk in KV:
            ps, rs = [], []
            for g in range(GQA_GROUP):
                h = hk * GQA_GROUP + g
                sg = s[hk][g * blk:(g + 1) * blk] + bias_ref[sel, h]
                m = jnp.maximum(jnp.max(sg, axis=-1, keepdims=True), sink_ref[h])
                e = jnp.exp(sg - m)
                rs.append(1.0 / (jnp.sum(e, axis=-1, keepdims=True) + jnp.exp(sink_ref[h] - m)))
                ps.append(e.astype(BF16))
            p[hk] = jnp.concatenate(ps, axis=0)
            rden[hk] = rs
        o = {hk: _dot(p[hk], vslab[hk]) for hk in KV}
        for hk in KV:
            for s2 in range(2):
                ga, gb = 2 * s2, 2 * s2 + 1
                oa = o[hk][ga * blk:(ga + 1) * blk] * rden[hk][ga]
                ob = o[hk][gb * blk:(gb + 1) * blk] * rden[hk][gb]
                if hk % 2 == 0:
                    out = jnp.where(low, oa, pltpu.roll(ob, H, axis=1))
                else:
                    out = jnp.where(low, pltpu.roll(oa, H, axis=1), ob)
                o_ref[0, :, tile(2 * hk + s2)] = out.astype(o_ref.dtype)

    for g0 in range(0, N_KV_HEADS, KV_PER_GROUP):
        kv_group(range(g0, g0 + KV_PER_GROUP))


def _swa_prompt(q3, kv3, slopes, sinks):
    b, t, _ = q3.shape
    nb = t // WINDOW
    smem = pl.BlockSpec(memory_space=pltpu.SMEM)
    prev = lambda n: jnp.maximum(n - 1, 0)
    return pl.pallas_call(
        _swa_prompt_kernel,
        grid=(b, nb),
        in_specs=[smem, smem,
                  pl.BlockSpec((1, WINDOW, D_MODEL), lambda bi, n: (bi, n, 0)),
                  pl.BlockSpec((1, WINDOW, D_KV), lambda bi, n: (bi, n, 0)),
                  pl.BlockSpec((1, WINDOW, D_KV), lambda bi, n: (bi, prev(n), 0)),
                  pl.BlockSpec((1, WINDOW, D_KV), lambda bi, n: (bi, n, 1)),
                  pl.BlockSpec((1, WINDOW, D_KV), lambda bi, n: (bi, prev(n), 1))],
        out_specs=pl.BlockSpec((1, WINDOW, D_MODEL), lambda bi, n: (bi, n, 0)),
        out_shape=jax.ShapeDtypeStruct((b, t, D_MODEL), BF16),
        scratch_shapes=[pltpu.VMEM((2, N_HEADS, WINDOW, 2 * WINDOW), F32)],
        compiler_params=_cparams(("arbitrary", "arbitrary")),
        name="swa_prompt",
    )(slopes, sinks, q3, kv3, kv3, kv3, kv3)


def _swa_sample_kernel(slope_ref, sink_ref, q_ref, kc_ref, vc_ref, knew_ref, vnew_ref,
                       o_ref, kwin_ref, vwin_ref, nbuf_ref, *, tq, BB, UNR):
    GT = GQA_GROUP * tq
    R = N_KV_HEADS * GT
    NP = 16
    C = D_KV
    row = lax.broadcasted_iota(jnp.int32, (R, WINDOW), 0)
    wcol = lax.broadcasted_iota(jnp.int32, (R, WINDOW), 1)
    t = lax.rem(row, tq)
    slope = slope_ref[...]
    sink = sink_ref[...]
    dist_o = WINDOW + t - wcol
    bias_old = jnp.where(dist_o <= WINDOW, -slope * dist_o.astype(F32), -jnp.inf)
    s_idx = wcol - (WINDOW - tq)
    dist_n = t - s_idx
    bias_new = jnp.where((s_idx >= 0) & (dist_n >= 0), -slope * dist_n.astype(F32), -jnp.inf)
    hkmask = (lax.broadcasted_iota(jnp.int32, (R, C), 0) // GT) == (lax.broadcasted_iota(jnp.int32, (R, C), 1) // HEAD_DIM)
    srow = lax.broadcasted_iota(jnp.int32, (NP, WINDOW), 0)
    scol = lax.broadcasted_iota(jnp.int32, (NP, WINDOW), 1)
    selw = ((scol == srow + (WINDOW - tq)) & (srow < tq)).astype(BF16)
    lane_new = lax.broadcasted_iota(jnp.int32, (C, WINDOW), 1) >= WINDOW - tq
    scale = HEAD_DIM ** -0.5
    for u in range(UNR):
        nbuf_ref[u, :, tq:NP, :] = jnp.zeros((2, NP - tq, C), F32)

    def transposed_new(x):
        return sum(_dot_tn(part, selw) for part in _split3(x))

    def body(i, carry):
        bs = [i * UNR + u for u in range(UNR)]
        U = range(UNR)
        for u in U:
            nbuf_ref[u, 0, 0:tq, :] = knew_ref[bs[u]]
            nbuf_ref[u, 1, 0:tq, :] = vnew_ref[bs[u]]
        kt = [kc_ref[b] for b in bs]
        vt = [vc_ref[b] for b in bs]
        knt = [transposed_new(nbuf_ref[u, 0]) for u in U]
        vnt = [transposed_new(nbuf_ref[u, 1]) for u in U]
        qbd = [jnp.where(hkmask, jnp.concatenate([q_ref[b] * scale] * N_KV_HEADS, axis=0), 0.0).astype(BF16)
               for b in bs]
        s_o = [_dot(qbd[u], kt[u].astype(BF16)) + bias_old for u in U]
        s_n = [_dot(qbd[u], knt[u].astype(BF16)) + bias_new for u in U]
        outs = []
        for u in U:
            m = jnp.maximum(jnp.maximum(jnp.max(s_o[u], axis=-1, keepdims=True),
                                        jnp.max(s_n[u], axis=-1, keepdims=True)), sink)
            p_o = jnp.exp(s_o[u] - m)
            p_n = jnp.exp(s_n[u] - m)
            rden = 1.0 / (jnp.sum(p_o, axis=-1, keepdims=True) + jnp.sum(p_n, axis=-1, keepdims=True)
                          + jnp.exp(sink - m))
            o = _dot_nt(p_o.astype(BF16), vt[u].astype(BF16)) + _dot_nt(p_n.astype(BF16), vnt[u].astype(BF16))
            o = jnp.where(hkmask, o * rden, 0.0)
            acc = o[0:GT]
            for hk in range(1, N_KV_HEADS):
                acc = acc + o[hk * GT:(hk + 1) * GT]
            outs.append(acc)
        for u in U:
            kwin_ref[bs[u]] = jnp.where(lane_new, knt[u], pltpu.roll(kt[u], WINDOW - tq, axis=1))
            vwin_ref[bs[u]] = jnp.where(lane_new, vnt[u], pltpu.roll(vt[u], WINDOW - tq, axis=1))
            o_ref[bs[u]] = outs[u].astype(o_ref.dtype)
        return carry

    lax.fori_loop(0, BB // UNR, body, 0)


def _swa_sample(q16, knew3, vnew3, kct, vct, slope_rows, sink_rows):
    b, gt, c = q16.shape
    tq = gt // GQA_GROUP
    bb = 8 if b % 8 == 0 else 1
    unr = 2 if bb % 2 == 0 else 1
    rows = N_KV_HEADS * gt
    blk3 = lambda shape: pl.BlockSpec(shape, lambda i: (i, 0, 0))
    full2 = pl.BlockSpec((rows, 1), lambda i: (0, 0))
    kern = functools.partial(_swa_sample_kernel, tq=tq, BB=bb, UNR=unr)
    return pl.pallas_call(
        kern,
        grid=(b // bb,),
        in_specs=[full2, full2, blk3((bb, gt, c)), blk3((bb, c, WINDOW)), blk3((bb, c, WINDOW)),
                  blk3((bb, tq, c)), blk3((bb, tq, c))],
        out_specs=[blk3((bb, gt, c)), blk3((bb, c, WINDOW)), blk3((bb, c, WINDOW))],
        out_shape=[jax.ShapeDtypeStruct((b, gt, c), BF16),
                   jax.ShapeDtypeStruct((b, c, WINDOW), F32),
                   jax.ShapeDtypeStruct((b, c, WINDOW), F32)],
        scratch_shapes=[pltpu.VMEM((unr, 2, 16, c), F32)],
        compiler_params=_cparams(("parallel",)),
        name="swa_sample",
    )(slope_rows, sink_rows, q16, kct, vct, knew3, vnew3)


def _merge_out_kernel(x_ref, ya_ref, yb_ref, ga_ref, gb_ref, wo_ref, nw_ref, h_ref, hn_ref):
    f32 = lambda ref: ref[...].astype(F32)
    mixed = _sigmoid(f32(ga_ref)) * f32(ya_ref) + _sigmoid(f32(gb_ref)) * f32(yb_ref)
    h = x_ref[...] + _dot(mixed.astype(BF16), wo_ref[...])
    h_ref[...] = h
    ms = jnp.mean(h * h, axis=-1, keepdims=True)
    hn_ref[...] = (h * lax.rsqrt(ms + RMS_EPS) * nw_ref[...]).astype(BF16)


def _merge_out(x, ya, yb, gates, w_out, nw, tm):
    m, d = x.shape
    row = lambda o: (lambda i: (i, o))
    return pl.pallas_call(
        _merge_out_kernel,
        grid=(m // tm,),
        in_specs=[pl.BlockSpec((tm, d), row(0)), pl.BlockSpec((tm, d), row(0)), pl.BlockSpec((tm, d), row(0)),
                  pl.BlockSpec((tm, d), row(0)), pl.BlockSpec((tm, d), row(1)),
                  pl.BlockSpec((d, d), lambda i: (0, 0)), pl.BlockSpec((1, d), lambda i: (0, 0))],
        out_specs=[pl.BlockSpec((tm, d), row(0)), pl.BlockSpec((tm, d), row(0))],
        out_shape=[jax.ShapeDtypeStruct((m, d), F32), jax.ShapeDtypeStruct((m, d), BF16)],
        compiler_params=_cparams(("parallel",)),
        name="merge_out_proj",
    )(x, ya, yb, gates, gates, w_out, nw)


def _mlp_kernel(hn_ref, h_ref, wu_ref, wd_ref, nw_ref, o_ref, acc_ref):
    j = pl.program_id(1)

    @pl.when(j == 0)
    def _():
        acc_ref[...] = jnp.zeros_like(acc_ref)

    u = jnp.maximum(_dot(hn_ref[...], wu_ref[...]), 0.0)
    acc_ref[...] += _dot((u * u).astype(BF16), wd_ref[...])

    @pl.when(j == pl.num_programs(1) - 1)
    def _():
        h = h_ref[...] + acc_ref[...]
        ms = jnp.mean(h * h, axis=-1, keepdims=True)
        o_ref[...] = h * lax.rsqrt(ms + RMS_EPS) * nw_ref[...]


def _mlp(hn, h, w_up, w_down, nw, tm, tf):
    m, d = h.shape
    f = w_up.shape[1]
    return pl.pallas_call(
        _mlp_kernel,
        grid=(m // tm, f // tf),
        in_specs=[pl.BlockSpec((tm, d), lambda i, j: (i, 0)), pl.BlockSpec((tm, d), lambda i, j: (i, 0)),
                  pl.BlockSpec((d, tf), lambda i, j: (0, j)), pl.BlockSpec((tf, d), lambda i, j: (j, 0)),
                  pl.BlockSpec((1, d), lambda i, j: (0, 0))],
        out_specs=pl.BlockSpec((tm, d), lambda i, j: (i, 0)),
        out_shape=jax.ShapeDtypeStruct((m, d), F32),
        scratch_shapes=[pltpu.VMEM((tm, d), F32)],
        compiler_params=_cparams(("parallel", "arbitrary")),
        name="mlp_final_norm",
    )(hn, h, w_up, w_down, nw)


def _pick(m, prefs):
    for t in prefs:
        if m % t == 0:
            return t
    return m


def _pad_cols(v, n):
    return jnp.concatenate([v, jnp.zeros(v.shape[:-1] + (n - v.shape[-1],), v.dtype)], axis=-1)


def _place_rows(w, start, rows):
    n, d = w.shape
    return jnp.concatenate([jnp.zeros((start, d), w.dtype), w, jnp.zeros((rows - start - n, d), w.dtype)], axis=0)


def _layer(x, shift_prev, wkv0, lw, *, prompt, k_cache_t=None, v_cache_t=None):
    b, t, d = x.shape
    m = b * t
    x2 = x.reshape(m, d) if prompt else x.transpose(1, 0, 2).reshape(m, d)
    proj_r, proj_q, proj_kv, proj_g = _in_proj(
        x2, lw["norm_mix_w"], lw["w_t"], IN_GROUPS, [F32, BF16, F32, BF16], _pick(m, (1024, 512, 256, 128, 8)))
    shift_pad = _pad_cols(shift_prev, R_PAD)

    if prompt:
        proj_r3 = proj_r.reshape(b, t, R_PAD)
        kv3 = proj_kv.reshape(b, t, 2 * D_KV)
        ya, wkv_new = _rwkv_prompt(proj_r3, shift_pad[:, None], wkv0, lw)
        yb = _swa_prompt(proj_q.reshape(b, t, D_MODEL), kv3, lw["slopes"], lw["sinks"])
        k_win = kv3[:, t - WINDOW:, :D_KV].reshape(b, WINDOW, N_KV_HEADS, HEAD_DIM)
        v_win = kv3[:, t - WINDOW:, D_KV:].reshape(b, WINDOW, N_KV_HEADS, HEAD_DIM)
        shift_new = proj_r3[:, t - 1, :R_COLS]
    else:
        prep = _rwkv_sample_prep(proj_r, shift_pad, lw, b, t)
        wkv_t, ya_t = _wkv_sample(wkv0.transpose(1, 2, 3, 0), prep,
                                  jnp.broadcast_to(lw["ln_w"].reshape(D_RWKV, 1), (D_RWKV, b)),
                                  jnp.broadcast_to(lw["ln_b"].reshape(D_RWKV, 1), (D_RWKV, b)))
        ya = ya_t.transpose(0, 2, 1)
        wkv_new = wkv_t.transpose(3, 0, 1, 2)
        q16 = proj_q.reshape(t, b, N_KV_HEADS, GQA_GROUP, HEAD_DIM).transpose(1, 3, 0, 2, 4)
        q16 = q16.reshape(b, GQA_GROUP * t, D_KV)
        kv_bt = proj_kv.reshape(t, b, 2 * D_KV).transpose(1, 0, 2)
        gt_head = (jnp.arange(N_KV_HEADS)[:, None] * GQA_GROUP + jnp.arange(GQA_GROUP)[None, :])
        row_head = jnp.repeat(gt_head, t, axis=1).reshape(-1)
        o16, kwt, vwt = _swa_sample(q16, kv_bt[:, :, :D_KV], kv_bt[:, :, D_KV:], k_cache_t, v_cache_t,
                                    lw["slopes"][row_head][:, None], lw["sinks"][row_head][:, None])
        yb = o16.reshape(b, GQA_GROUP, t, N_KV_HEADS, HEAD_DIM).transpose(2, 0, 3, 1, 4)
        k_win = kwt.reshape(b, N_KV_HEADS, HEAD_DIM, WINDOW).transpose(0, 3, 1, 2)
        v_win = vwt.reshape(b, N_KV_HEADS, HEAD_DIM, WINDOW).transpose(0, 3, 1, 2)
        shift_new = proj_r[(t - 1) * b:, :R_COLS]

    h, hn = _merge_out(x2, ya.reshape(m, d), yb.reshape(m, d), proj_g, lw["w_out"], lw["norm_mlp_w"],
                       _pick(m, (512, 256, 128, 8)))
    y = _mlp(hn, h, lw["w_up"], lw["w_down"], lw["norm_final_w"], _pick(m, (512, 256, 128, 8)), 1024)
    y = y.reshape(b, t, d) if prompt else y.reshape(t, b, d).transpose(1, 0, 2)
    return y, shift_new, wkv_new, k_win, v_win


def kernel(x_prompt, x_sample, state_shift, state_wkv, cache_k_win, cache_v_win, norm_mix_w, w_in, tshift_mu, w0, w_lora, a0, a_lora, g_lora, k_k, k_a, r_k, ln_x_w, ln_x_b, attn_sinks, w_out, norm_mlp_w, w_up, w_down, norm_final_w):
    depth = w_in.shape[0]
    assert depth == 1
    l = 0
    bp = x_prompt.shape[0]
    db = x_sample.shape[0]
    hh = jnp.arange(N_HEADS, dtype=F32)
    lw = dict(
        norm_mix_w=norm_mix_w[l][None],
        w_t=jnp.swapaxes(w_in[l], 0, 1).astype(BF16),
        mu=_pad_cols(tshift_mu[l][None], R_PAD), w0=w0[l][None], a0=a0[l][None], k_k=k_k[l][None],
        k_a=k_a[l][None], r_k=r_k[l].reshape(1, D_RWKV), ln_w=ln_x_w[l][None], ln_b=ln_x_b[l][None],
        w_lora=_place_rows(w_lora[l], 0, LANES).astype(BF16),
        a_lora=_place_rows(a_lora[l], LORA_DECAY_END, LORA_A_WIN).astype(BF16),
        g_lora=_place_rows(g_lora[l], LORA_A_END - LORA_G_START, LORA_W - LORA_G_START).astype(BF16),
        slopes=jnp.exp2(-8.0 * (hh + 1.0) / N_HEADS), sinks=attn_sinks[l].astype(F32),
        w_out=w_out[l].astype(BF16), norm_mlp_w=norm_mlp_w[l][None],
        w_up=w_up[l].astype(BF16), w_down=w_down[l].astype(BF16), norm_final_w=norm_final_w[None],
    )
    yp, sp, wp, kp, vp = _layer(
        x_prompt, jnp.zeros((bp, R_COLS), F32), jnp.zeros((bp, N_HEADS, HEAD_DIM, HEAD_DIM), F32), lw, prompt=True)
    kct = cache_k_win[l].transpose(0, 2, 3, 1).reshape(db, D_KV, WINDOW)
    vct = cache_v_win[l].transpose(0, 2, 3, 1).reshape(db, D_KV, WINDOW)
    ys, ss, ws, ksm, vsm = _layer(x_sample, state_shift[l], state_wkv[l], lw, prompt=False,
                                  k_cache_t=kct, v_cache_t=vct)
    return (yp, ys, sp[None], wp[None], kp[None], vp[None], ss[None], ws[None], ksm[None], vsm[None])
```

```python
import functools
import math

import jax
import jax.numpy as jnp
from jax import lax
from jax.experimental import pallas as pl
from jax.experimental.pallas import tpu as pltpu

F32 = jnp.float32
BF16 = jnp.bfloat16

D_MODEL = 2048
HEAD_DIM = 64
N_HEADS = D_MODEL // HEAD_DIM
N_KV_HEADS = 8
GQA_GROUP = N_HEADS // N_KV_HEADS
D_KV = N_KV_HEADS * HEAD_DIM
WINDOW = 128
D_FF = 4 * D_MODEL
D_DECAY_LORA = 96
D_A_LORA = 96
D_GATE_LORA = 256
D_RWKV = D_MODEL
R_COLS = 3 * D_RWKV + D_DECAY_LORA + D_A_LORA + D_GATE_LORA
C_IN = R_COLS + D_MODEL + 2 * D_KV + 2 * D_MODEL
RMS_EPS = 1e-5
GN_EPS = 64e-5

LANES = 128
PROJ_TILE = 512
R_PAD = -(-R_COLS // PROJ_TILE) * PROJ_TILE
LORA_W = R_PAD - 3 * D_RWKV
LORA_DECAY_END = D_DECAY_LORA
LORA_A_END = D_DECAY_LORA + D_A_LORA
LORA_A_WIN = -(-LORA_A_END // LANES) * LANES
LORA_G_START = (LORA_A_END // LANES) * LANES
IN_GROUPS = ((0, R_PAD), (R_COLS, D_MODEL), (R_COLS + D_MODEL, 2 * D_KV), (R_COLS + D_MODEL + 2 * D_KV, 2 * D_MODEL))

VMEM_LIMIT = 56 * 1024 * 1024
RWKV_CHUNK = 64
KV_PER_GROUP = 2
SAMPLE_PREP_W = 512
PREP_OUT = ("r", "w", "k", "v", "a", "b", "g", "bonus")


def _cparams(sem):
    return pltpu.CompilerParams(dimension_semantics=sem, vmem_limit_bytes=VMEM_LIMIT)


def _dot(a, b):
    return jnp.dot(a, b, preferred_element_type=F32)


def _dot_nt(a, b):
    return lax.dot_general(a, b, (((1,), (1,)), ((), ())), preferred_element_type=F32)


def _dot_tn(a, b):
    return lax.dot_general(a, b, (((0,), (0,)), ((), ())), preferred_element_type=F32)


def _sigmoid(x):
    return 1.0 / (1.0 + jnp.exp(-x))


def _tanh(x):
    return 1.0 - 2.0 / (1.0 + jnp.exp(2.0 * x))


def _split2(x):
    hi = x.astype(BF16)
    lo = (x - hi.astype(F32)).astype(BF16)
    return hi, lo


def _split3(x):
    hi = x.astype(BF16)
    r1 = x - hi.astype(F32)
    mid = r1.astype(BF16)
    lo = (r1 - mid.astype(F32)).astype(BF16)
    return hi, mid, lo


def _head_pair_mask():
    ji = lax.broadcasted_iota(jnp.int32, (LANES, LANES), 0)
    jj = lax.broadcasted_iota(jnp.int32, (LANES, LANES), 1)
    return (ji < HEAD_DIM) == (jj < HEAD_DIM)


def _seg_sum(x, seg, rows):
    n = x.shape[1] // LANES
    xs = jnp.concatenate([x[:, i * LANES:(i + 1) * LANES] for i in range(n)], axis=0).astype(BF16)
    ys = _dot(xs, seg)
    return jnp.concatenate([ys[i * rows:(i + 1) * rows] for i in range(n)], axis=1)


def _in_proj_kernel(x_ref, nw_ref, *refs, bounds):
    n = len(bounds)
    w_refs, o_refs, xn_ref = refs[:n], refs[n:2 * n], refs[2 * n]
    j = pl.program_id(1)

    @pl.when(j == 0)
    def _():
        x = x_ref[...]
        ms = jnp.mean(x * x, axis=-1, keepdims=True)
        xn_ref[...] = (x * lax.rsqrt(ms + RMS_EPS) * nw_ref[...]).astype(BF16)

    for w_ref, o_ref, (lo, hi) in zip(w_refs, o_refs, bounds):
        @pl.when((j >= lo) & (j < hi))
        def _(w_ref=w_ref, o_ref=o_ref):
            o_ref[...] = _dot_nt(xn_ref[...], w_ref[...]).astype(o_ref.dtype)


def _in_proj(x, nw, w_t, groups, out_dtypes, tm):
    m, d = x.shape
    tn = PROJ_TILE
    bounds, lo = [], 0
    for _, width in groups:
        bounds.append((lo, lo + width // tn))
        lo = bounds[-1][1]
    ROW_ALIGN = 16
    assert all(start % ROW_ALIGN == 0 and start + width <= w_t.shape[0] for start, width in groups)

    def clamp(lo_, hi_):
        return lambda j: jnp.clip(j - lo_, 0, hi_ - lo_ - 1)

    in_specs = [pl.BlockSpec((tm, d), lambda i, j: (i, 0)), pl.BlockSpec((1, d), lambda i, j: (0, 0))]
    out_specs, out_shape = [], []
    for (start, width), dt, (lo_, hi_) in zip(groups, out_dtypes, bounds):
        c = clamp(lo_, hi_)
        in_specs.append(pl.BlockSpec((pl.Element(tn), pl.Element(d)),
                                     lambda i, j, c=c, start=start: (pl.multiple_of(start + c(j) * tn, ROW_ALIGN), 0)))
        out_specs.append(pl.BlockSpec((tm, tn), lambda i, j, c=c: (i, c(j))))
        out_shape.append(jax.ShapeDtypeStruct((m, width), dt))
    return pl.pallas_call(
        functools.partial(_in_proj_kernel, bounds=tuple(bounds)),
        grid=(m // tm, lo),
        in_specs=in_specs,
        out_specs=out_specs,
        out_shape=out_shape,
        scratch_shapes=[pltpu.VMEM((tm, d), BF16)],
        compiler_params=_cparams(("parallel", "arbitrary")),
        name="norm_in_proj",
    )(x, nw, *([w_t] * len(groups)))


def _rwkv_prompt_kernel(pr_ref, pk_ref, pv_ref, plo_ref, shr_ref, shk_ref, shv_ref, shl_ref, s0_ref,
                        mur_ref, muk_ref, muv_ref, mul_ref, w0_ref, a0_ref, kk_ref, ka_ref, rk_ref, lnw_ref, lnb_ref,
                        wl_ref, al_ref, gl_ref,
                        y_ref, so_ref,
                        sd_ref, br_ref, bk_ref, bv_ref, bl_ref, *, L, NC):
    c = pl.program_id(1)
    H = HEAD_DIM
    NP = D_RWKV // LANES
    P = range(NP)
    sls = [slice(i * LANES, (i + 1) * LANES) for i in P]
    n_dbl = max(1, math.ceil(math.log2(L)))

    head1 = lax.broadcasted_iota(jnp.int32, (L, LANES), 1) < H
    row_i = lax.broadcasted_iota(jnp.int32, (L, 2 * L), 0)
    col_i = lax.broadcasted_iota(jnp.int32, (L, 2 * L), 1)
    col_t = jnp.where(col_i >= L, col_i - L, col_i)
    strict = col_t < row_i
    incl = col_t <= row_i
    colh1 = col_i < L
    tri = (lax.broadcasted_iota(jnp.int32, (L, L), 1) <= lax.broadcasted_iota(jnp.int32, (L, L), 0)).astype(BF16)
    same_head = _head_pair_mask()
    seg = same_head.astype(BF16)
    seg_sum = functools.partial(_seg_sum, seg=seg, rows=L)

    def stack_heads(x, m):
        zero = jnp.zeros_like(x)
        return jnp.concatenate([jnp.where(m, x, zero), jnp.where(m, zero, x)], axis=0)

    @pl.when(c == 0)
    def _init():
        br_ref[0:1, :] = shr_ref[0]
        bk_ref[0:1, :] = shk_ref[0]
        bv_ref[0:1, :] = shv_ref[0]
        bl_ref[0:1, :] = shl_ref[0]
        z = jnp.zeros((H, H), F32)
        for i in P:
            top = jnp.concatenate([s0_ref[0, 2 * i], z], axis=1)
            bot = jnp.concatenate([z, s0_ref[0, 2 * i + 1]], axis=1)
            sd_ref[i] = jnp.concatenate([top, bot], axis=0)

    first_row = lax.broadcasted_iota(jnp.int32, (L, 1), 0) == 0

    def shifted(p_ref, carry_ref, mu_ref):
        p = p_ref[0]
        prev = jnp.where(first_row, carry_ref[0:1, :], pltpu.roll(p, 1, axis=0))
        return p + mu_ref[...] * (prev - p)

    ps_l = shifted(plo_ref, bl_ref, mul_ref)
    td = _tanh(ps_l[:, 0:LANES]).astype(BF16)
    da = ps_l[:, 0:LORA_A_WIN].astype(BF16)
    sg = _sigmoid(ps_l[:, LORA_G_START:]).astype(BF16)
    r = shifted(pr_ref, br_ref, mur_ref)
    k = shifted(pk_ref, bk_ref, muk_ref)
    v = shifted(pv_ref, bv_ref, muv_ref)

    zlog = w0_ref[...] + _dot(td, wl_ref[...])
    logw = -math.exp(-0.5) / (1.0 + jnp.exp(-zlog))
    a_sig = _sigmoid(a0_ref[...] + _dot(da, al_ref[...]))
    kk = k * kk_ref[...]
    kk = kk * jnp.minimum(lax.rsqrt(jnp.maximum(seg_sum(kk * kk), 0.0)), 1e12)
    k_h = k * (1.0 + (a_sig - 1.0) * ka_ref[...])
    hi, lo = _split2(logw)
    cum = _dot(tri, hi) + _dot(tri, lo)
    cum_l = cum[L - 1:L, :]
    p_inv = jnp.exp(-cum)
    p_end = jnp.exp(cum_l - cum)
    p_l = jnp.exp(cum_l)
    bvec = kk * a_sig
    a_t = (-kk * jnp.exp(cum - logw)).astype(BF16)
    r_t = (r * jnp.exp(cum)).astype(BF16)
    b_t = (bvec * p_inv).astype(BF16)
    k_t = (k_h * p_inv).astype(BF16)
    b_e = (bvec * p_end).astype(BF16)
    k_e = (k_h * p_end).astype(BF16)
    v_b = v.astype(BF16)
    g = _dot(sg, gl_ref[...])
    bonus = seg_sum(r * k_h * rk_ref[...]) * v

    sd = [sd_ref[i] for i in P]
    lhs = [jnp.concatenate([a_t[:, s], r_t[:, s]], axis=0) for s in sls]
    rhs = [jnp.concatenate([stack_heads(b_t[:, s], head1), stack_heads(k_t[:, s], head1)], axis=0) for s in sls]
    aa = [_dot_nt(lhs[i], rhs[i]) for i in P]
    sa = [_dot_nt(lhs[i], sd[i].astype(BF16)) for i in P]
    v_st = [stack_heads(v_b[:, s], head1) for s in sls]
    a_ak = [jnp.where(strict, aa[i][0:L, 2 * L:4 * L], 0.0).astype(BF16) for i in P]
    x = [sa[i][0:L] + _dot(a_ak[i], v_st[i]) for i in P]
    ap = [jnp.where(strict, aa[i][0:L, 0:2 * L], 0.0).astype(BF16) for i in P]
    for d in range(n_dbl):
        xs = [stack_heads(x[i].astype(BF16), head1) for i in P]
        if d + 1 < n_dbl:
            both = [_dot(ap[i], jnp.concatenate([xs[i], stack_heads(ap[i], colh1)], axis=1)) for i in P]
            x = [x[i] + both[i][:, 0:2 * L] for i in P]
            ap = [both[i][:, 2 * L:].astype(BF16) for i in P]
        else:
            x = [x[i] + _dot(ap[i], xs[i]) for i in P]
    u_b = [x[i].astype(BF16) for i in P]
    a_rb = [jnp.where(incl, aa[i][L:2 * L, 0:2 * L], 0.0).astype(BF16) for i in P]
    a_rk = [jnp.where(incl, aa[i][L:2 * L, 2 * L:4 * L], 0.0).astype(BF16) for i in P]
    y = [sa[i][L:2 * L] + _dot(jnp.concatenate([a_rb[i], a_rk[i]], axis=1),
                               jnp.concatenate([stack_heads(u_b[i], head1), v_st[i]], axis=0)) for i in P]
    ds = [_dot_tn(jnp.concatenate([u_b[i], v_b[:, sls[i]]], axis=0),
                  jnp.concatenate([b_e[:, sls[i]], k_e[:, sls[i]]], axis=0)) for i in P]
    for i in P:
        sd_ref[i] = sd[i] * p_l[:, sls[i]] + jnp.where(same_head, ds[i], 0.0)

    y = jnp.concatenate(y, axis=1)
    mean = seg_sum(y) * (1.0 / H)
    dlt = y - mean
    var = seg_sum(dlt * dlt) * (1.0 / H)
    yn = dlt * lax.rsqrt(var + GN_EPS) * lnw_ref[...] + lnb_ref[...]
    y_ref[0] = ((yn + bonus) * g).astype(y_ref.dtype)

    br_ref[0:1, :] = pr_ref[0, L - 1:L, :]
    bk_ref[0:1, :] = pk_ref[0, L - 1:L, :]
    bv_ref[0:1, :] = pv_ref[0, L - 1:L, :]
    bl_ref[0:1, :] = plo_ref[0, L - 1:L, :]

    @pl.when(c == NC - 1)
    def _finish():
        for i in P:
            s = sd_ref[i]
            so_ref[0, 2 * i] = s[0:H, 0:H]
            so_ref[0, 2 * i + 1] = s[H:2 * H, H:2 * H]


def _rwkv_prompt(proj3, shift3, s0, p):
    b, t, _ = proj3.shape
    L = RWKV_CHUNK
    assert t % L == 0 and 2 * L == LANES
    nc = t // L
    w = D_RWKV
    lora_blk = 3 * D_RWKV // LORA_W
    col = lambda o: (lambda bi, c: (bi, c, o))
    sh = lambda o: (lambda bi, c: (bi, 0, o))
    par = lambda o: (lambda bi, c: (0, o))
    in_specs = [
        pl.BlockSpec((1, L, w), col(0)), pl.BlockSpec((1, L, w), col(1)), pl.BlockSpec((1, L, w), col(2)),
        pl.BlockSpec((1, L, LORA_W), col(lora_blk)),
        pl.BlockSpec((1, 1, w), sh(0)), pl.BlockSpec((1, 1, w), sh(1)), pl.BlockSpec((1, 1, w), sh(2)),
        pl.BlockSpec((1, 1, LORA_W), sh(lora_blk)),
        pl.BlockSpec((1, N_HEADS, HEAD_DIM, HEAD_DIM), lambda bi, c: (bi, 0, 0, 0)),
        pl.BlockSpec((1, w), par(0)), pl.BlockSpec((1, w), par(1)), pl.BlockSpec((1, w), par(2)),
        pl.BlockSpec((1, LORA_W), par(lora_blk)),
    ] + [pl.BlockSpec((1, w), par(0))] * 7 + [
        pl.BlockSpec((LANES, w), par(0)), pl.BlockSpec((LORA_A_WIN, w), par(0)),
        pl.BlockSpec((LORA_W - LORA_G_START, w), par(0)),
    ]
    out_specs = [pl.BlockSpec((1, L, w), col(0)),
                 pl.BlockSpec((1, N_HEADS, HEAD_DIM, HEAD_DIM), lambda bi, c: (bi, 0, 0, 0))]
    return pl.pallas_call(
        functools.partial(_rwkv_prompt_kernel, L=L, NC=nc),
        grid=(b, nc),
        in_specs=in_specs,
        out_specs=out_specs,
        out_shape=[jax.ShapeDtypeStruct((b, t, D_RWKV), BF16),
                   jax.ShapeDtypeStruct((b, N_HEADS, HEAD_DIM, HEAD_DIM), F32)],
        scratch_shapes=[pltpu.VMEM((D_RWKV // LANES, LANES, LANES), F32),
                        pltpu.VMEM((8, w), F32), pltpu.VMEM((8, w), F32),
                        pltpu.VMEM((8, w), F32), pltpu.VMEM((8, LORA_W), F32)],
        compiler_params=_cparams(("parallel", "arbitrary")),
        name="rwkv_mix",
    )(proj3, proj3, proj3, proj3, shift3, shift3, shift3, shift3, s0,
      p["mu"], p["mu"], p["mu"], p["mu"], p["w0"], p["a0"], p["k_k"], p["k_a"], p["r_k"],
      p["ln_w"], p["ln_b"], p["w_lora"], p["a_lora"], p["g_lora"])


def _rwkv_sample_prep_kernel(pr_ref, pk_ref, pv_ref, plo_ref, shr_ref, shk_ref, shv_ref, shl_ref,
                             mur_ref, muk_ref, muv_ref, mul_ref, w0_ref, a0_ref, kk_ref, ka_ref, rk_ref,
                             wl_ref, al_ref, gl_ref, *out_refs, nb, nt):
    W = pr_ref.shape[1]
    n_tiles = W // LANES
    rows = nb * nt
    seg_sum = functools.partial(_seg_sum, seg=_head_pair_mask().astype(BF16), rows=rows)

    def shifted(p_ref, s_ref, mu_ref):
        p = p_ref[...]
        prev = jnp.concatenate([s_ref[...], p[0:rows - nb]], axis=0)
        return p + mu_ref[...] * (prev - p)

    ps_l = shifted(plo_ref, shl_ref, mul_ref)
    td = _tanh(ps_l[:, 0:LANES]).astype(BF16)
    da = ps_l[:, 0:LORA_A_WIN].astype(BF16)
    sg = _sigmoid(ps_l[:, LORA_G_START:]).astype(BF16)
    r = shifted(pr_ref, shr_ref, mur_ref)
    k = shifted(pk_ref, shk_ref, muk_ref)
    v = shifted(pv_ref, shv_ref, muv_ref)
    zlog = w0_ref[...] + _dot(td, wl_ref[...])
    w = jnp.exp(-math.exp(-0.5) / (1.0 + jnp.exp(-zlog)))
    a_sig = _sigmoid(a0_ref[...] + _dot(da, al_ref[...]))
    kk = k * kk_ref[...]
    kk = kk * jnp.minimum(lax.rsqrt(jnp.maximum(seg_sum(kk * kk), 0.0)), 1e12)
    k_h = k * (1.0 + (a_sig - 1.0) * ka_ref[...])
    g = _dot(sg, gl_ref[...])
    bonus = seg_sum(r * k_h * rk_ref[...]) * v
    vals = dict(r=r, w=w, k=k_h, v=v, a=-kk, b=kk * a_sig, g=g, bonus=bonus)
    for name, o_ref in zip(PREP_OUT, out_refs):
        x = vals[name]
        for t in range(nt):
            for c in range(n_tiles):
                o_ref[t, c * LANES:(c + 1) * LANES, :] = x[t * nb:(t + 1) * nb, c * LANES:(c + 1) * LANES].T


def _rwkv_sample_prep(proj_r, shift, p, nb, nt):
    w = SAMPLE_PREP_W
    kb = D_RWKV // w
    lora_blk = 3 * D_RWKV // LORA_W
    rows = nt * nb
    col = lambda o: (lambda g: (0, o + g))
    in_specs = (
        [pl.BlockSpec((rows, w), col(0)), pl.BlockSpec((rows, w), col(kb)), pl.BlockSpec((rows, w), col(2 * kb)),
         pl.BlockSpec((rows, LORA_W), lambda g: (0, lora_blk))]
        + [pl.BlockSpec((nb, w), col(0)), pl.BlockSpec((nb, w), col(kb)), pl.BlockSpec((nb, w), col(2 * kb)),
           pl.BlockSpec((nb, LORA_W), lambda g: (0, lora_blk))]
        + [pl.BlockSpec((1, w), col(0)), pl.BlockSpec((1, w), col(kb)), pl.BlockSpec((1, w), col(2 * kb)),
           pl.BlockSpec((1, LORA_W), lambda g: (0, lora_blk))]
        + [pl.BlockSpec((1, w), col(0))] * 5
        + [pl.BlockSpec((LANES, w), col(0)), pl.BlockSpec((LORA_A_WIN, w), col(0)),
           pl.BlockSpec((LORA_W - LORA_G_START, w), col(0))])
    out_spec = pl.BlockSpec((nt, w, nb), lambda g: (0, g, 0))
    return pl.pallas_call(
        functools.partial(_rwkv_sample_prep_kernel, nb=nb, nt=nt),
        grid=(kb,),
        in_specs=in_specs,
        out_specs=[out_spec] * len(PREP_OUT),
        out_shape=[jax.ShapeDtypeStruct((nt, D_RWKV, nb), F32)] * len(PREP_OUT),
        compiler_params=_cparams(("parallel",)),
        name="rwkv_sample_prep",
    )(proj_r, proj_r, proj_r, proj_r, shift, shift, shift, shift, p["mu"], p["mu"], p["mu"], p["mu"],
      p["w0"], p["a0"], p["k_k"], p["k_a"], p["r_k"], p["w_lora"], p["a_lora"], p["g_lora"])


def _wkv_sample_kernel(s_ref, r_ref, w_ref, k_ref, v_ref, a_ref, b_ref, g_ref, bonus_ref, lnw_ref, lnb_ref,
                       so_ref, o_ref, y_ref, *, nt):
    H = HEAD_DIM
    SUB = 8
    PAR = 4
    rowid = lax.broadcasted_iota(jnp.int32, (SUB, LANES), 0)

    def body(i8, carry):
        base = pl.multiple_of(i8 * SUB, SUB)
        v8 = [v_ref[t, pl.ds(base, SUB), :] for t in range(nt)]
        y8 = [jnp.zeros((SUB, LANES), F32) for _ in range(nt)]
        for h0 in range(0, SUB, PAR):
            ids = list(range(h0, h0 + PAR))
            S = [s_ref[0, base + ii] for ii in ids]
            for t in range(nt):
                a, w, b, k, r = a_ref[t], w_ref[t], b_ref[t], k_ref[t], r_ref[t]
                for n, ii in enumerate(ids):
                    sa = jnp.sum(S[n] * a, axis=0, keepdims=True)
                    S[n] = S[n] * w + sa * b + v8[t][ii:ii + 1, :] * k
                    y = jnp.sum(S[n] * r, axis=0, keepdims=True)
                    y8[t] = jnp.where(rowid == ii, y, y8[t])
            for n, ii in enumerate(ids):
                so_ref[0, base + ii] = S[n]
        for t in range(nt):
            y_ref[t, pl.ds(base, SUB), :] = y8[t]
        return carry

    lax.fori_loop(0, H // SUB, body, 0)
    for t in range(nt):
        y = y_ref[t]
        mean = jnp.sum(y, axis=0, keepdims=True) * (1.0 / H)
        d = y - mean
        var = jnp.sum(d * d, axis=0, keepdims=True) * (1.0 / H)
        out = (d * lax.rsqrt(var + GN_EPS) * lnw_ref[...] + lnb_ref[...] + bonus_ref[t]) * g_ref[t]
        o_ref[t] = out.astype(o_ref.dtype)


def _wkv_sample(state_t, prep, lnw_b, lnb_b):
    nh, hd, _, nb = state_t.shape
    nt = prep[0].shape[0]
    st_spec = pl.BlockSpec((1, hd, hd, nb), lambda h: (h, 0, 0, 0))
    ch_spec = pl.BlockSpec((nt, hd, nb), lambda h: (0, h, 0))
    ln_spec = pl.BlockSpec((hd, nb), lambda h: (h, 0))
    return pl.pallas_call(
        functools.partial(_wkv_sample_kernel, nt=nt),
        grid=(nh,),
        in_specs=[st_spec] + [ch_spec] * len(PREP_OUT) + [ln_spec, ln_spec],
        out_specs=[st_spec, ch_spec],
        out_shape=[jax.ShapeDtypeStruct(state_t.shape, F32), jax.ShapeDtypeStruct((nt, nh * hd, nb), BF16)],
        scratch_shapes=[pltpu.VMEM((nt, hd, nb), F32)],
        compiler_params=_cparams(("parallel",)),
        name="wkv_sample",
    )(state_t, *prep, lnw_b, lnb_b)


def _swa_prompt_kernel(slope_ref, sink_ref, q_ref, kc_ref, kp_ref, vc_ref, vp_ref, o_ref, bias_ref):
    n = pl.program_id(1)
    blk = WINDOW
    H = HEAD_DIM

    @pl.when((pl.program_id(0) == 0) & (n == 0))
    def _():
        t = lax.broadcasted_iota(jnp.int32, (blk, 2 * blk), 0)
        j = lax.broadcasted_iota(jnp.int32, (blk, 2 * blk), 1)
        dist = t - j + blk
        band = (dist >= 0) & (dist <= WINDOW)
        first = band & (j >= blk)
        distf = dist.astype(F32)
        for h in range(N_HEADS):
            ab = -slope_ref[h] * distf
            bias_ref[0, h] = jnp.where(first, ab, -jnp.inf)
            bias_ref[1, h] = jnp.where(band, ab, -jnp.inf)

    sel = jnp.where(n == 0, 0, 1)
    low = lax.broadcasted_iota(jnp.int32, (blk, LANES), 1) < H
    scale = H ** -0.5
    tile = lambda i: slice(i * LANES, (i + 1) * LANES)

    def kv_group(KV):
        kslab = {hk: jnp.concatenate([kp_ref[0, :, tile(hk // 2)], kc_ref[0, :, tile(hk // 2)]],
                                     axis=0).astype(BF16) for hk in KV}
        vslab = {hk: jnp.concatenate([vp_ref[0, :, tile(hk // 2)], vc_ref[0, :, tile(hk // 2)]],
                                     axis=0).astype(BF16) for hk in KV}
        lhs = {}
        for hk in KV:
            parts = []
            for s2 in range(2):
                xs = q_ref[0, :, tile(2 * hk + s2)].astype(F32) * scale
                xr = pltpu.roll(xs, H, axis=1)
                if hk % 2 == 0:
                    parts += [jnp.where(low, xs, 0.0), jnp.where(low, xr, 0.0)]
                else:
                    parts += [jnp.where(low, 0.0, xr), jnp.where(low, 0.0, xs)]
            lhs[hk] = jnp.concatenate(parts, axis=0).astype(BF16)
        s = {hk: _dot_nt(lhs[hk], kslab[hk]) for hk in KV}
        p, rden = {}, {}
        for hk in KV:
            ps, rs = [], []
            for g in range(GQA_GROUP):
                h = hk * GQA_GROUP + g
                sg = s[hk][g * blk:(g + 1) * blk] + bias_ref[sel, h]
                m = jnp.maximum(jnp.max(sg, axis=-1, keepdims=True), sink_ref[h])
                e = jnp.exp(sg - m)
                rs.append(1.0 / (jnp.sum(e, axis=-1, keepdims=True) + jnp.exp(sink_ref[h] - m)))
                ps.append(e.astype(BF16))
            p[hk] = jnp.concatenate(ps, axis=0)
            rden[hk] = rs
        o = {hk: _dot(p[hk], vslab[hk]) for hk in KV}
        for hk in KV:
            for s2 in range(2):
                ga, gb = 2 * s2, 2 * s2 + 1
                oa = o[hk][ga * blk:(ga + 1) * blk] * rden[hk][ga]
                ob = o[hk][gb * blk:(gb + 1) * blk] * rden[hk][gb]
                if hk % 2 == 0:
                    out = jnp.where(low, oa, pltpu.roll(ob, H, axis=1))
                else:
                    out = jnp.where(low, pltpu.roll(oa, H, axis=1), ob)
                o_ref[0, :, tile(2 * hk + s2)] = out.astype(o_ref.dtype)

    for g0 in range(0, N_KV_HEADS, KV_PER_GROUP):
        kv_group(range(g0, g0 + KV_PER_GROUP))


def _swa_prompt(q3, kv3, slopes, sinks):
    b, t, _ = q3.shape
    nb = t // WINDOW
    smem = pl.BlockSpec(memory_space=pltpu.SMEM)
    prev = lambda n: jnp.maximum(n - 1, 0)
    return pl.pallas_call(
        _swa_prompt_kernel,
        grid=(b, nb),
        in_specs=[smem, smem,
                  pl.BlockSpec((1, WINDOW, D_MODEL), lambda bi, n: (bi, n, 0)),
                  pl.BlockSpec((1, WINDOW, D_KV), lambda bi, n: (bi, n, 0)),
                  pl.BlockSpec((1, WINDOW, D_KV), lambda bi, n: (bi, prev(n), 0)),
                  pl.BlockSpec((1, WINDOW, D_KV), lambda bi, n: (bi, n, 1)),
                  pl.BlockSpec((1, WINDOW, D_KV), lambda bi, n: (bi, prev(n), 1))],
        out_specs=pl.BlockSpec((1, WINDOW, D_MODEL), lambda bi, n: (bi, n, 0)),
        out_shape=jax.ShapeDtypeStruct((b, t, D_MODEL), BF16),
        scratch_shapes=[pltpu.VMEM((2, N_HEADS, WINDOW, 2 * WINDOW), F32)],
        compiler_params=_cparams(("arbitrary", "arbitrary")),
        name="swa_prompt",
    )(slopes, sinks, q3, kv3, kv3, kv3, kv3)


def _swa_sample_kernel(slope_ref, sink_ref, q_ref, kc_ref, vc_ref, knew_ref, vnew_ref,
                       o_ref, kwin_ref, vwin_ref, nbuf_ref, *, tq, BB, UNR):
    GT = GQA_GROUP * tq
    R = N_KV_HEADS * GT
    NP = 16
    C = D_KV
    row = lax.broadcasted_iota(jnp.int32, (R, WINDOW), 0)
    wcol = lax.broadcasted_iota(jnp.int32, (R, WINDOW), 1)
    t = lax.rem(row, tq)
    slope = slope_ref[...]
    sink = sink_ref[...]
    dist_o = WINDOW + t - wcol
    bias_old = jnp.where(dist_o <= WINDOW, -slope * dist_o.astype(F32), -jnp.inf)
    s_idx = wcol - (WINDOW - tq)
    dist_n = t - s_idx
    bias_new = jnp.where((s_idx >= 0) & (dist_n >= 0), -slope * dist_n.astype(F32), -jnp.inf)
    hkmask = (lax.broadcasted_iota(jnp.int32, (R, C), 0) // GT) == (lax.broadcasted_iota(jnp.int32, (R, C), 1) // HEAD_DIM)
    srow = lax.broadcasted_iota(jnp.int32, (NP, WINDOW), 0)
    scol = lax.broadcasted_iota(jnp.int32, (NP, WINDOW), 1)
    selw = ((scol == srow + (WINDOW - tq)) & (srow < tq)).astype(BF16)
    lane_new = lax.broadcasted_iota(jnp.int32, (C, WINDOW), 1) >= WINDOW - tq
    scale = HEAD_DIM ** -0.5
    for u in range(UNR):
        nbuf_ref[u, :, tq:NP, :] = jnp.zeros((2, NP - tq, C), F32)

    def transposed_new(x):
        return sum(_dot_tn(part, selw) for part in _split3(x))

    def body(i, carry):
        bs = [i * UNR + u for u in range(UNR)]
        U = range(UNR)
        for u in U:
            nbuf_ref[u, 0, 0:tq, :] = knew_ref[bs[u]]
            nbuf_ref[u, 1, 0:tq, :] = vnew_ref[bs[u]]
        kt = [kc_ref[b] for b in bs]
        vt = [vc_ref[b] for b in bs]
        knt = [transposed_new(nbuf_ref[u, 0]) for u in U]
        vnt = [transposed_new(nbuf_ref[u, 1]) for u in U]
        qbd = [jnp.where(hkmask, jnp.concatenate([q_ref[b] * scale] * N_KV_HEADS, axis=0), 0.0).astype(BF16)
               for b in bs]
        s_o = [_dot(qbd[u], kt[u].astype(BF16)) + bias_old for u in U]
        s_n = [_dot(qbd[u], knt[u].astype(BF16)) + bias_new for u in U]
        outs = []
        for u in U:
            m = jnp.maximum(jnp.maximum(jnp.max(s_o[u], axis=-1, keepdims=True),
                                        jnp.max(s_n[u], axis=-1, keepdims=True)), sink)
            p_o = jnp.exp(s_o[u] - m)
            p_n = jnp.exp(s_n[u] - m)
            rden = 1.0 / (jnp.sum(p_o, axis=-1, keepdims=True) + jnp.sum(p_n, axis=-1, keepdims=True)
                          + jnp.exp(sink - m))
            o = _dot_nt(p_o.astype(BF16), vt[u].astype(BF16)) + _dot_nt(p_n.astype(BF16), vnt[u].astype(BF16))
            o = jnp.where(hkmask, o * rden, 0.0)
            acc = o[0:GT]
            for hk in range(1, N_KV_HEADS):
                acc = acc + o[hk * GT:(hk + 1) * GT]
            outs.append(acc)
        for u in U:
            kwin_ref[bs[u]] = jnp.where(lane_new, knt[u], pltpu.roll(kt[u], WINDOW - tq, axis=1))
            vwin_ref[bs[u]] = jnp.where(lane_new, vnt[u], pltpu.roll(vt[u], WINDOW - tq, axis=1))
            o_ref[bs[u]] = outs[u].astype(o_ref.dtype)
        return carry

    lax.fori_loop(0, BB // UNR, body, 0)


def _swa_sample(q16, knew3, vnew3, kct, vct, slope_rows, sink_rows):
    b, gt, c = q16.shape
    tq = gt // GQA_GROUP
    bb = 8 if b % 8 == 0 else 1
    unr = 2 if bb % 2 == 0 else 1
    rows = N_KV_HEADS * gt
    blk3 = lambda shape: pl.BlockSpec(shape, lambda i: (i, 0, 0))
    full2 = pl.BlockSpec((rows, 1), lambda i: (0, 0))
    kern = functools.partial(_swa_sample_kernel, tq=tq, BB=bb, UNR=unr)
    return pl.pallas_call(
        kern,
        grid=(b // bb,),
        in_specs=[full2, full2, blk3((bb, gt, c)), blk3((bb, c, WINDOW)), blk3((bb, c, WINDOW)),
                  blk3((bb, tq, c)), blk3((bb, tq, c))],
        out_specs=[blk3((bb, gt, c)), blk3((bb, c, WINDOW)), blk3((bb, c, WINDOW))],
        out_shape=[jax.ShapeDtypeStruct((b, gt, c), BF16),
                   jax.ShapeDtypeStruct((b, c, WINDOW), F32),
                   jax.ShapeDtypeStruct((b, c, WINDOW), F32)],
        scratch_shapes=[pltpu.VMEM((unr, 2, 16, c), F32)],
        compiler_params=_cparams(("parallel",)),
        name="swa_sample",
    )(slope_rows, sink_rows, q16, kct, vct, knew3, vnew3)


def _merge_out_kernel(x_ref, ya_ref, yb_ref, ga_ref, gb_ref, wo_ref, nw_ref, h_ref, hn_ref):
    f32 = lambda ref: ref[...].astype(F32)
    mixed = _sigmoid(f32(ga_ref)) * f32(ya_ref) + _sigmoid(f32(gb_ref)) * f32(yb_ref)
    h = x_ref[...] + _dot(mixed.astype(BF16), wo_ref[...])
    h_ref[...] = h
    ms = jnp.mean(h * h, axis=-1, keepdims=True)
    hn_ref[...] = (h * lax.rsqrt(ms + RMS_EPS) * nw_ref[...]).astype(BF16)


def _merge_out(x, ya, yb, gates, w_out, nw, tm):
    m, d = x.shape
    row = lambda o: (lambda i: (i, o))
    return pl.pallas_call(
        _merge_out_kernel,
        grid=(m // tm,),
        in_specs=[pl.BlockSpec((tm, d), row(0)), pl.BlockSpec((tm, d), row(0)), pl.BlockSpec((tm, d), row(0)),
                  pl.BlockSpec((tm, d), row(0)), pl.BlockSpec((tm, d), row(1)),
                  pl.BlockSpec((d, d), lambda i: (0, 0)), pl.BlockSpec((1, d), lambda i: (0, 0))],
        out_specs=[pl.BlockSpec((tm, d), row(0)), pl.BlockSpec((tm, d), row(0))],
        out_shape=[jax.ShapeDtypeStruct((m, d), F32), jax.ShapeDtypeStruct((m, d), BF16)],
        compiler_params=_cparams(("parallel",)),
        name="merge_out_proj",
    )(x, ya, yb, gates, gates, w_out, nw)


def _mlp_kernel(hn_ref, h_ref, wu_ref, wd_ref, nw_ref, o_ref, acc_ref):
    j = pl.program_id(1)

    @pl.when(j == 0)
    def _():
        acc_ref[...] = jnp.zeros_like(acc_ref)

    u = jnp.maximum(_dot(hn_ref[...], wu_ref[...]), 0.0)
    acc_ref[...] += _dot((u * u).astype(BF16), wd_ref[...])

    @pl.when(j == pl.num_programs(1) - 1)
    def _():
        h = h_ref[...] + acc_ref[...]
        ms = jnp.mean(h * h, axis=-1, keepdims=True)
        o_ref[...] = h * lax.rsqrt(ms + RMS_EPS) * nw_ref[...]


def _mlp(hn, h, w_up, w_down, nw, tm, tf):
    m, d = h.shape
    f = w_up.shape[1]
    return pl.pallas_call(
        _mlp_kernel,
        grid=(m // tm, f // tf),
        in_specs=[pl.BlockSpec((tm, d), lambda i, j: (i, 0)), pl.BlockSpec((tm, d), lambda i, j: (i, 0)),
                  pl.BlockSpec((d, tf), lambda i, j: (0, j)), pl.BlockSpec((tf, d), lambda i, j: (j, 0)),
                  pl.BlockSpec((1, d), lambda i, j: (0, 0))],
        out_specs=pl.BlockSpec((tm, d), lambda i, j: (i, 0)),
        out_shape=jax.ShapeDtypeStruct((m, d), F32),
        scratch_shapes=[pltpu.VMEM((tm, d), F32)],
        compiler_params=_cparams(("parallel", "arbitrary")),
        name="mlp_final_norm",
    )(hn, h, w_up, w_down, nw)


def _pick(m, prefs):
    for t in prefs:
        if m % t == 0:
            return t
    return m


def _pad_cols(v, n):
    return jnp.concatenate([v, jnp.zeros(v.shape[:-1] + (n - v.shape[-1],), v.dtype)], axis=-1)


def _place_rows(w, start, rows):
    n, d = w.shape
    return jnp.concatenate([jnp.zeros((start, d), w.dtype), w, jnp.zeros((rows - start - n, d), w.dtype)], axis=0)


def _layer(x, shift_prev, wkv0, lw, *, prompt, k_cache_t=None, v_cache_t=None):
    b, t, d = x.shape
    m = b * t
    x2 = x.reshape(m, d) if prompt else x.transpose(1, 0, 2).reshape(m, d)
    proj_r, proj_q, proj_kv, proj_g = _in_proj(
        x2, lw["norm_mix_w"], lw["w_t"], IN_GROUPS, [F32, BF16, F32, BF16], _pick(m, (1024, 512, 256, 128, 8)))
    shift_pad = _pad_cols(shift_prev, R_PAD)

    if prompt:
        proj_r3 = proj_r.reshape(b, t, R_PAD)
        kv3 = proj_kv.reshape(b, t, 2 * D_KV)
        ya, wkv_new = _rwkv_prompt(proj_r3, shift_pad[:, None], wkv0, lw)
        yb = _swa_prompt(proj_q.reshape(b, t, D_MODEL), kv3, lw["slopes"], lw["sinks"])
        k_win = kv3[:, t - WINDOW:, :D_KV].reshape(b, WINDOW, N_KV_HEADS, HEAD_DIM)
        v_win = kv3[:, t - WINDOW:, D_KV:].reshape(b, WINDOW, N_KV_HEADS, HEAD_DIM)
        shift_new = proj_r3[:, t - 1, :R_COLS]
    else:
        prep = _rwkv_sample_prep(proj_r, shift_pad, lw, b, t)
        wkv_t, ya_t = _wkv_sample(wkv0.transpose(1, 2, 3, 0), prep,
                                  jnp.broadcast_to(lw["ln_w"].reshape(D_RWKV, 1), (D_RWKV, b)),
                                  jnp.broadcast_to(lw["ln_b"].reshape(D_RWKV, 1), (D_RWKV, b)))
        ya = ya_t.transpose(0, 2, 1)
        wkv_new = wkv_t.transpose(3, 0, 1, 2)
        q16 = proj_q.reshape(t, b, N_KV_HEADS, GQA_GROUP, HEAD_DIM).transpose(1, 3, 0, 2, 4)
        q16 = q16.reshape(b, GQA_GROUP * t, D_KV)
        kv_bt = proj_kv.reshape(t, b, 2 * D_KV).transpose(1, 0, 2)
        gt_head = (jnp.arange(N_KV_HEADS)[:, None] * GQA_GROUP + jnp.arange(GQA_GROUP)[None, :])
        row_head = jnp.repeat(gt_head, t, axis=1).reshape(-1)
        o16, kwt, vwt = _swa_sample(q16, kv_bt[:, :, :D_KV], kv_bt[:, :, D_KV:], k_cache_t, v_cache_t,
                                    lw["slopes"][row_head][:, None], lw["sinks"][row_head][:, None])
        yb = o16.reshape(b, GQA_GROUP, t, N_KV_HEADS, HEAD_DIM).transpose(2, 0, 3, 1, 4)
        k_win = kwt.reshape(b, N_KV_HEADS, HEAD_DIM, WINDOW).transpose(0, 3, 1, 2)
        v_win = vwt.reshape(b, N_KV_HEADS, HEAD_DIM, WINDOW).transpose(0, 3, 1, 2)
        shift_new = proj_r[(t - 1) * b:, :R_COLS]

    h, hn = _merge_out(x2, ya.reshape(m, d), yb.reshape(m, d), proj_g, lw["w_out"], lw["norm_mlp_w"],
                       _pick(m, (512, 256, 128, 8)))
    y = _mlp(hn, h, lw["w_up"], lw["w_down"], lw["norm_final_w"], _pick(m, (512, 256, 128, 8)), 1024)
    y = y.reshape(b, t, d) if prompt else y.reshape(t, b, d).transpose(1, 0, 2)
    return y, shift_new, wkv_new, k_win, v_win


def kernel(x_prompt, x_sample, state_shift, state_wkv, cache_k_win, cache_v_win, norm_mix_w, w_in, tshift_mu, w0, w_lora, a0, a_lora, g_lora, k_k, k_a, r_k, ln_x_w, ln_x_b, attn_sinks, w_out, norm_mlp_w, w_up, w_down, norm_final_w):
    depth = w_in.shape[0]
    assert depth == 1
    l = 0
    bp = x_prompt.shape[0]
    db = x_sample.shape[0]
    hh = jnp.arange(N_HEADS, dtype=F32)
    lw = dict(
        norm_mix_w=norm_mix_w[l][None],
        w_t=jnp.swapaxes(w_in[l], 0, 1).astype(BF16),
        mu=_pad_cols(tshift_mu[l][None], R_PAD), w0=w0[l][None], a0=a0[l][None], k_k=k_k[l][None],
        k_a=k_a[l][None], r_k=r_k[l].reshape(1, D_RWKV), ln_w=ln_x_w[l][None], ln_b=ln_x_b[l][None],
        w_lora=_place_rows(w_lora[l], 0, LANES).astype(BF16),
        a_lora=_place_rows(a_lora[l], LORA_DECAY_END, LORA_A_WIN).astype(BF16),
        g_lora=_place_rows(g_lora[l], LORA_A_END - LORA_G_START, LORA_W - LORA_G_START).astype(BF16),
        slopes=jnp.exp2(-8.0 * (hh + 1.0) / N_HEADS), sinks=attn_sinks[l].astype(F32),
        w_out=w_out[l].astype(BF16), norm_mlp_w=norm_mlp_w[l][None],
        w_up=w_up[l].astype(BF16), w_down=w_down[l].astype(BF16), norm_final_w=norm_final_w[None],
    )
    yp, sp, wp, kp, vp = _layer(
        x_prompt, jnp.zeros((bp, R_COLS), F32), jnp.zeros((bp, N_HEADS, HEAD_DIM, HEAD_DIM), F32), lw, prompt=True)
    kct = cache_k_win[l].transpose(0, 2, 3, 1).reshape(db, D_KV, WINDOW)
    vct = cache_v_win[l].transpose(0, 2, 3, 1).reshape(db, D_KV, WINDOW)
    ys, ss, ws, ksm, vsm = _layer(x_sample, state_shift[l], state_wkv[l], lw, prompt=False,
                                  k_cache_t=kct, v_cache_t=vct)
    return (yp, ys, sp[None], wp[None], kp[None], vp[None], ss[None], ws[None], ksm[None], vsm[None])
```

```python
import functools
import math

import jax
import jax.numpy as jnp
from jax import lax
from jax.experimental import pallas as pl
from jax.experimental.pallas import tpu as pltpu

F32 = jnp.float32
BF16 = jnp.bfloat16

D_MODEL = 2048
HEAD_DIM = 64
N_HEADS = D_MODEL // HEAD_DIM
N_KV_HEADS = 8
GQA_GROUP = N_HEADS // N_KV_HEADS
D_KV = N_KV_HEADS * HEAD_DIM
WINDOW = 128
D_FF = 4 * D_MODEL
D_DECAY_LORA = 96
D_A_LORA = 96
D_GATE_LORA = 256
D_RWKV = D_MODEL
R_COLS = 3 * D_RWKV + D_DECAY_LORA + D_A_LORA + D_GATE_LORA
C_IN = R_COLS + D_MODEL + 2 * D_KV + 2 * D_MODEL
RMS_EPS = 1e-5
GN_EPS = 64e-5

LANES = 128
PROJ_TILE = 512
R_PAD = -(-R_COLS // PROJ_TILE) * PROJ_TILE
LORA_W = R_PAD - 3 * D_RWKV
LORA_DECAY_END = D_DECAY_LORA
LORA_A_END = D_DECAY_LORA + D_A_LORA
LORA_A_WIN = -(-LORA_A_END // LANES) * LANES
LORA_G_START = (LORA_A_END // LANES) * LANES
IN_GROUPS = ((0, R_PAD), (R_COLS, D_MODEL), (R_COLS + D_MODEL, 2 * D_KV), (R_COLS + D_MODEL + 2 * D_KV, 2 * D_MODEL))

VMEM_LIMIT = 56 * 1024 * 1024
RWKV_CHUNK = 64
RWKV_SKEW_PLAN = (1, 1, 4, 4, 4, 4, 1, 4, 4, 4, 4)
KV_PER_GROUP = 2
SAMPLE_PREP_W = 512
PREP_OUT = ("r", "w", "k", "v", "a", "b", "g", "bonus")


def _cparams(sem):
    return pltpu.CompilerParams(dimension_semantics=sem, vmem_limit_bytes=VMEM_LIMIT)


def _dot(a, b):
    return jnp.dot(a, b, preferred_element_type=F32)


def _dot_nt(a, b):
    return lax.dot_general(a, b, (((1,), (1,)), ((), ())), preferred_element_type=F32)


def _dot_tn(a, b):
    return lax.dot_general(a, b, (((0,), (0,)), ((), ())), preferred_element_type=F32)


def _sigmoid(x):
    return 1.0 / (1.0 + jnp.exp(-x))


def _tanh(x):
    return 1.0 - 2.0 / (1.0 + jnp.exp(2.0 * x))


def _split2(x):
    hi = x.astype(BF16)
    lo = (x - hi.astype(F32)).astype(BF16)
    return hi, lo


def _split3(x):
    hi = x.astype(BF16)
    r1 = x - hi.astype(F32)
    mid = r1.astype(BF16)
    lo = (r1 - mid.astype(F32)).astype(BF16)
    return hi, mid, lo


def _head_pair_mask():
    ji = lax.broadcasted_iota(jnp.int32, (LANES, LANES), 0)
    jj = lax.broadcasted_iota(jnp.int32, (LANES, LANES), 1)
    return (ji < HEAD_DIM) == (jj < HEAD_DIM)


def _seg_sum(x, seg, rows):
    n = x.shape[1] // LANES
    xs = jnp.concatenate([x[:, i * LANES:(i + 1) * LANES] for i in range(n)], axis=0).astype(BF16)
    ys = _dot(xs, seg)
    return jnp.concatenate([ys[i * rows:(i + 1) * rows] for i in range(n)], axis=1)


def _in_proj_kernel(x_ref, nw_ref, *refs, bounds):
    n = len(bounds)
    w_refs, o_refs, xn_ref = refs[:n], refs[n:2 * n], refs[2 * n]
    j = pl.program_id(1)

    @pl.when(j == 0)
    def _():
        x = x_ref[...]
        ms = jnp.mean(x * x, axis=-1, keepdims=True)
        xn_ref[...] = (x * lax.rsqrt(ms + RMS_EPS) * nw_ref[...]).astype(BF16)

    for w_ref, o_ref, (lo, hi) in zip(w_refs, o_refs, bounds):
        @pl.when((j >= lo) & (j < hi))
        def _(w_ref=w_ref, o_ref=o_ref):
            o_ref[...] = _dot_nt(xn_ref[...], w_ref[...]).astype(o_ref.dtype)


def _in_proj(x, nw, w_t, groups, out_dtypes, tm):
    m, d = x.shape
    tn = PROJ_TILE
    bounds, lo = [], 0
    for _, width in groups:
        bounds.append((lo, lo + width // tn))
        lo = bounds[-1][1]
    ROW_ALIGN = 16
    assert all(start % ROW_ALIGN == 0 and start + width <= w_t.shape[0] for start, width in groups)

    def clamp(lo_, hi_):
        return lambda j: jnp.clip(j - lo_, 0, hi_ - lo_ - 1)

    in_specs = [pl.BlockSpec((tm, d), lambda i, j: (i, 0)), pl.BlockSpec((1, d), lambda i, j: (0, 0))]
    out_specs, out_shape = [], []
    for (start, width), dt, (lo_, hi_) in zip(groups, out_dtypes, bounds):
        c = clamp(lo_, hi_)
        in_specs.append(pl.BlockSpec((pl.Element(tn), pl.Element(d)),
                                     lambda i, j, c=c, start=start: (pl.multiple_of(start + c(j) * tn, ROW_ALIGN), 0)))
        out_specs.append(pl.BlockSpec((tm, tn), lambda i, j, c=c: (i, c(j))))
        out_shape.append(jax.ShapeDtypeStruct((m, width), dt))
    return pl.pallas_call(
        functools.partial(_in_proj_kernel, bounds=tuple(bounds)),
        grid=(m // tm, lo),
        in_specs=in_specs,
        out_specs=out_specs,
        out_shape=out_shape,
        scratch_shapes=[pltpu.VMEM((tm, d), BF16)],
        compiler_params=_cparams(("parallel", "arbitrary")),
        name="norm_in_proj",
    )(x, nw, *([w_t] * len(groups)))


def _rwkv_prompt_kernel(pr_ref, pk_ref, pv_ref, plo_ref, shr_ref, shk_ref, shv_ref, shl_ref, s0_ref,
                        mur_ref, muk_ref, muv_ref, mul_ref, w0_ref, a0_ref, kk_ref, ka_ref, rk_ref, lnw_ref, lnb_ref,
                        wl_ref, al_ref, gl_ref,
                        y_ref, so_ref,
                        sd_ref, br_ref, bk_ref, bv_ref, bl_ref, *, L, NC):
    c = pl.program_id(1)
    H = HEAD_DIM
    NP = D_RWKV // LANES
    P = range(NP)
    sls = [slice(i * LANES, (i + 1) * LANES) for i in P]
    n_dbl = max(1, math.ceil(math.log2(L)))

    head1 = lax.broadcasted_iota(jnp.int32, (L, LANES), 1) < H
    row_i = lax.broadcasted_iota(jnp.int32, (L, 2 * L), 0)
    col_i = lax.broadcasted_iota(jnp.int32, (L, 2 * L), 1)
    col_t = jnp.where(col_i >= L, col_i - L, col_i)
    strict = col_t < row_i
    incl = col_t <= row_i
    colh1 = col_i < L
    tri = (lax.broadcasted_iota(jnp.int32, (L, L), 1) <= lax.broadcasted_iota(jnp.int32, (L, L), 0)).astype(BF16)
    same_head = _head_pair_mask()
    seg = same_head.astype(BF16)
    seg_sum = functools.partial(_seg_sum, seg=seg, rows=L)

    def stack_heads(x, m):
        zero = jnp.zeros_like(x)
        return jnp.concatenate([jnp.where(m, x, zero), jnp.where(m, zero, x)], axis=0)

    @pl.when(c == 0)
    def _init():
        br_ref[0:1, :] = shr_ref[0]
        bk_ref[0:1, :] = shk_ref[0]
        bv_ref[0:1, :] = shv_ref[0]
        bl_ref[0:1, :] = shl_ref[0]
        z = jnp.zeros((H, H), F32)
        for i in P:
            top = jnp.concatenate([s0_ref[0, 2 * i], z], axis=1)
            bot = jnp.concatenate([z, s0_ref[0, 2 * i + 1]], axis=1)
            sd_ref[i] = jnp.concatenate([top, bot], axis=0)

    first_row = lax.broadcasted_iota(jnp.int32, (L, 1), 0) == 0

    def shifted(p_ref, carry_ref, mu_ref):
        p = p_ref[0]
        prev = jnp.where(first_row, carry_ref[0:1, :], pltpu.roll(p, 1, axis=0))
        return p + mu_ref[...] * (prev - p)

    ps_l = shifted(plo_ref, bl_ref, mul_ref)
    td = _tanh(ps_l[:, 0:LANES]).astype(BF16)
    da = ps_l[:, 0:LORA_A_WIN].astype(BF16)
    sg = _sigmoid(ps_l[:, LORA_G_START:]).astype(BF16)
    r = shifted(pr_ref, br_ref, mur_ref)
    k = shifted(pk_ref, bk_ref, muk_ref)
    v = shifted(pv_ref, bv_ref, muv_ref)

    zlog = w0_ref[...] + _dot(td, wl_ref[...])
    logw = -math.exp(-0.5) / (1.0 + jnp.exp(-zlog))
    a_sig = _sigmoid(a0_ref[...] + _dot(da, al_ref[...]))
    kk = k * kk_ref[...]
    kk = kk * jnp.minimum(lax.rsqrt(jnp.maximum(seg_sum(kk * kk), 0.0)), 1e12)
    k_h = k * (1.0 + (a_sig - 1.0) * ka_ref[...])
    hi, lo = _split2(logw)
    cum = _dot(tri, hi) + _dot(tri, lo)
    cum_l = cum[L - 1:L, :]
    p_inv = jnp.exp(-cum)
    p_end = jnp.exp(cum_l - cum)
    p_l = jnp.exp(cum_l)
    bvec = kk * a_sig
    a_t = (-kk * jnp.exp(cum - logw)).astype(BF16)
    r_t = (r * jnp.exp(cum)).astype(BF16)
    b_t = (bvec * p_inv).astype(BF16)
    k_t = (k_h * p_inv).astype(BF16)
    b_e = (bvec * p_end).astype(BF16)
    k_e = (k_h * p_end).astype(BF16)
    v_b = v.astype(BF16)
    g = _dot(sg, gl_ref[...])
    bonus = seg_sum(r * k_h * rk_ref[...]) * v

    sd = [sd_ref[i] for i in P]
    lhs = [jnp.concatenate([a_t[:, s], r_t[:, s]], axis=0) for s in sls]
    rhs = [jnp.concatenate([stack_heads(b_t[:, s], head1), stack_heads(k_t[:, s], head1)], axis=0) for s in sls]
    aa = [_dot_nt(lhs[i], rhs[i]) for i in P]
    sa = [_dot_nt(lhs[i], sd[i].astype(BF16)) for i in P]
    v_st = [stack_heads(v_b[:, s], head1) for s in sls]
    a_ak = [jnp.where(strict, aa[i][0:L, 2 * L:4 * L], 0.0).astype(BF16) for i in P]
    x = [sa[i][0:L] + _dot(a_ak[i], v_st[i]) for i in P]
    ap = [jnp.where(strict, aa[i][0:L, 0:2 * L], 0.0).astype(BF16) for i in P]
    for d in range(n_dbl):
        xs = [stack_heads(x[i].astype(BF16), head1) for i in P]
        if d + 1 < n_dbl:
            both = [_dot(ap[i], jnp.concatenate([xs[i], stack_heads(ap[i], colh1)], axis=1)) for i in P]
            x = [x[i] + both[i][:, 0:2 * L] for i in P]
            ap = [both[i][:, 2 * L:].astype(BF16) for i in P]
        else:
            x = [x[i] + _dot(ap[i], xs[i]) for i in P]
    u_b = [x[i].astype(BF16) for i in P]
    a_rb = [jnp.where(incl, aa[i][L:2 * L, 0:2 * L], 0.0).astype(BF16) for i in P]
    a_rk = [jnp.where(incl, aa[i][L:2 * L, 2 * L:4 * L], 0.0).astype(BF16) for i in P]
    y = [sa[i][L:2 * L] + _dot(jnp.concatenate([a_rb[i], a_rk[i]], axis=1),
                               jnp.concatenate([stack_heads(u_b[i], head1), v_st[i]], axis=0)) for i in P]
    ds = [_dot_tn(jnp.concatenate([u_b[i], v_b[:, sls[i]]], axis=0),
                  jnp.concatenate([b_e[:, sls[i]], k_e[:, sls[i]]], axis=0)) for i in P]
    for i in P:
        sd_ref[i] = sd[i] * p_l[:, sls[i]] + jnp.where(same_head, ds[i], 0.0)

    y = jnp.concatenate(y, axis=1)
    mean = seg_sum(y) * (1.0 / H)
    dlt = y - mean
    var = seg_sum(dlt * dlt) * (1.0 / H)
    yn = dlt * lax.rsqrt(var + GN_EPS) * lnw_ref[...] + lnb_ref[...]
    y_ref[0] = ((yn + bonus) * g).astype(y_ref.dtype)

    br_ref[0:1, :] = pr_ref[0, L - 1:L, :]
    bk_ref[0:1, :] = pk_ref[0, L - 1:L, :]
    bv_ref[0:1, :] = pv_ref[0, L - 1:L, :]
    bl_ref[0:1, :] = plo_ref[0, L - 1:L, :]

    @pl.when(c == NC - 1)
    def _finish():
        for i in P:
            s = sd_ref[i]
            so_ref[0, 2 * i] = s[0:H, 0:H]
            so_ref[0, 2 * i + 1] = s[H:2 * H, H:2 * H]


def _rwkv_prompt(proj3, shift3, s0, p):
    b, t, _ = proj3.shape
    L = RWKV_CHUNK
    assert t % L == 0 and 2 * L == LANES
    nc = t // L
    w = D_RWKV
    lora_blk = 3 * D_RWKV // LORA_W
    col = lambda o: (lambda bi, c: (bi, c, o))
    sh = lambda o: (lambda bi, c: (bi, 0, o))
    par = lambda o: (lambda bi, c: (0, o))
    in_specs = [
        pl.BlockSpec((1, L, w), col(0)), pl.BlockSpec((1, L, w), col(1)), pl.BlockSpec((1, L, w), col(2)),
        pl.BlockSpec((1, L, LORA_W), col(lora_blk)),
        pl.BlockSpec((1, 1, w), sh(0)), pl.BlockSpec((1, 1, w), sh(1)), pl.BlockSpec((1, 1, w), sh(2)),
        pl.BlockSpec((1, 1, LORA_W), sh(lora_blk)),
        pl.BlockSpec((1, N_HEADS, HEAD_DIM, HEAD_DIM), lambda bi, c: (bi, 0, 0, 0)),
        pl.BlockSpec((1, w), par(0)), pl.BlockSpec((1, w), par(1)), pl.BlockSpec((1, w), par(2)),
        pl.BlockSpec((1, LORA_W), par(lora_blk)),
    ] + [pl.BlockSpec((1, w), par(0))] * 7 + [
        pl.BlockSpec((LANES, w), par(0)), pl.BlockSpec((LORA_A_WIN, w), par(0)),
        pl.BlockSpec((LORA_W - LORA_G_START, w), par(0)),
    ]
    out_specs = [pl.BlockSpec((1, L, w), col(0)),
                 pl.BlockSpec((1, N_HEADS, HEAD_DIM, HEAD_DIM), lambda bi, c: (bi, 0, 0, 0))]
    return pl.pallas_call(
        functools.partial(_rwkv_prompt_kernel, L=L, NC=nc),
        grid=(b, nc),
        in_specs=in_specs,
        out_specs=out_specs,
        out_shape=[jax.ShapeDtypeStruct((b, t, D_RWKV), BF16),
                   jax.ShapeDtypeStruct((b, N_HEADS, HEAD_DIM, HEAD_DIM), F32)],
        scratch_shapes=[pltpu.VMEM((D_RWKV // LANES, LANES, LANES), F32),
                        pltpu.VMEM((8, w), F32), pltpu.VMEM((8, w), F32),
                        pltpu.VMEM((8, w), F32), pltpu.VMEM((8, LORA_W), F32)],
        compiler_params=_cparams(("parallel", "arbitrary")),
        name="rwkv_mix",
    )(proj3, proj3, proj3, proj3, shift3, shift3, shift3, shift3, s0,
      p["mu"], p["mu"], p["mu"], p["mu"], p["w0"], p["a0"], p["k_k"], p["k_a"], p["r_k"],
      p["ln_w"], p["ln_b"], p["w_lora"], p["a_lora"], p["g_lora"])


def _rwkv_prompt2_kernel(pr_ref, pk_ref, pv_ref, plo_ref, shr_ref, shk_ref, shv_ref, shl_ref, s0_ref,
                         mur_ref, muk_ref, muv_ref, mul_ref, w0_ref, a0_ref, kk_ref, ka_ref, rk_ref, lnw_ref, lnb_ref,
                         wl_ref, al_ref, gl_ref,
                         y_ref, so_ref,
                         sd_ref, br_ref, bk_ref, bv_ref, bl_ref, *, L, NC, BB, plan):
    c = pl.program_id(1)
    H = HEAD_DIM
    NP = D_RWKV // LANES
    P = range(NP)
    sls = [slice(i * LANES, (i + 1) * LANES) for i in P]
    n_dbl = max(1, math.ceil(math.log2(L)))

    head1 = lax.broadcasted_iota(jnp.int32, (L, LANES), 1) < H
    row_i = lax.broadcasted_iota(jnp.int32, (L, 2 * L), 0)
    col_i = lax.broadcasted_iota(jnp.int32, (L, 2 * L), 1)
    col_t = jnp.where(col_i >= L, col_i - L, col_i)
    strict = col_t < row_i
    incl = col_t <= row_i
    colh1 = col_i < L
    tri = (lax.broadcasted_iota(jnp.int32, (L, L), 1) <= lax.broadcasted_iota(jnp.int32, (L, L), 0)).astype(BF16)
    same_head = _head_pair_mask()
    seg = same_head.astype(BF16)
    seg_sum = functools.partial(_seg_sum, seg=seg, rows=L)
    first_row = lax.broadcasted_iota(jnp.int32, (L, 1), 0) == 0

    def stack_heads(x, m):
        zero = jnp.zeros_like(x)
        return jnp.concatenate([jnp.where(m, x, zero), jnp.where(m, zero, x)], axis=0)

    @pl.when(c == 0)
    def _init():
        z = jnp.zeros((H, H), F32)
        for bi in range(BB):
            br_ref[bi, 0:1, :] = shr_ref[bi]
            bk_ref[bi, 0:1, :] = shk_ref[bi]
            bv_ref[bi, 0:1, :] = shv_ref[bi]
            bl_ref[bi, 0:1, :] = shl_ref[bi]
            for i in P:
                top = jnp.concatenate([s0_ref[bi, 2 * i], z], axis=1)
                bot = jnp.concatenate([z, s0_ref[bi, 2 * i + 1]], axis=1)
                sd_ref[bi, i] = jnp.concatenate([top, bot], axis=0)

    def prologue_items(bi, T):
        S = {}

        def shifted(p_ref, carry_ref, mu_ref):
            p = p_ref[bi]
            prev = jnp.where(first_row, carry_ref[bi, 0:1, :], pltpu.roll(p, 1, axis=0))
            return p + mu_ref[...] * (prev - p)

        def load_shift():
            ps_l = shifted(plo_ref, bl_ref, mul_ref)
            S["td"] = _tanh(ps_l[:, 0:LANES]).astype(BF16)
            S["da"] = ps_l[:, 0:LORA_A_WIN].astype(BF16)
            S["sg"] = _sigmoid(ps_l[:, LORA_G_START:]).astype(BF16)
            S["r"] = shifted(pr_ref, br_ref, mur_ref)
            S["k"] = shifted(pk_ref, bk_ref, muk_ref)
            S["v"] = shifted(pv_ref, bv_ref, muv_ref)
            S["kk"] = S["k"] * kk_ref[...]

        def matmuls_1():
            S["zlog"] = w0_ref[...] + _dot(S["td"], wl_ref[...])
            S["apre"] = a0_ref[...] + _dot(S["da"], al_ref[...])
            S["g"] = _dot(S["sg"], gl_ref[...])
            S["n2"] = seg_sum(S["kk"] * S["kk"])

        def vector_1(i):
            s = sls[i]
            logw = -math.exp(-0.5) / (1.0 + jnp.exp(-S["zlog"][:, s]))
            a_sig = _sigmoid(S["apre"][:, s])
            kk = S["kk"][:, s] * jnp.minimum(lax.rsqrt(jnp.maximum(S["n2"][:, s], 0.0)), 1e12)
            k_h = S["k"][:, s] * (1.0 + (a_sig - 1.0) * ka_ref[:, s])
            hi, lo = _split2(logw)
            S[("v1", i)] = dict(logw=logw, bvec=kk * a_sig, kk=kk, k_h=k_h, hi=hi, lo=lo,
                                prod=(S["r"][:, s] * k_h * rk_ref[:, s]).astype(BF16))

        def matmuls_2():
            cat = lambda n: jnp.concatenate([S[("v1", i)][n] for i in P], axis=1)
            S["cum"] = _dot(tri, cat("hi")) + _dot(tri, cat("lo"))
            S["bsum"] = _dot(jnp.concatenate([S[("v1", i)]["prod"] for i in P], axis=0), seg)

        def vector_2(i):
            s = sls[i]
            t1 = S[("v1", i)]
            cum = S["cum"][:, s]
            cum_l = cum[L - 1:L, :]
            p_inv = jnp.exp(-cum)
            p_end = jnp.exp(cum_l - cum)
            v = S["v"][:, s]
            T[i] = dict(
                a_t=(-t1["kk"] * jnp.exp(cum - t1["logw"])).astype(BF16),
                r_t=(S["r"][:, s] * jnp.exp(cum)).astype(BF16),
                b_t=(t1["bvec"] * p_inv).astype(BF16), k_t=(t1["k_h"] * p_inv).astype(BF16),
                b_e=(t1["bvec"] * p_end).astype(BF16), k_e=(t1["k_h"] * p_end).astype(BF16),
                v_b=v.astype(BF16), g=S["g"][:, s], bonus=S["bsum"][i * L:(i + 1) * L] * v, p_l=jnp.exp(cum_l))

        return ([load_shift, matmuls_1] + [functools.partial(vector_1, i) for i in P] + [matmuls_2]
                + [functools.partial(vector_2, i) for i in P])

    def chain_stages(bi, T, sd, res):
        lhs = [jnp.concatenate([T[i]["a_t"], T[i]["r_t"]], axis=0) for i in P]
        rhs = [jnp.concatenate([stack_heads(T[i]["b_t"], head1), stack_heads(T[i]["k_t"], head1)], axis=0) for i in P]
        aa = [_dot_nt(lhs[i], rhs[i]) for i in P]
        yield
        sa = [_dot_nt(lhs[i], sd[i].astype(BF16)) for i in P]
        yield
        v_st = [stack_heads(T[i]["v_b"], head1) for i in P]
        a_ak = [jnp.where(strict, aa[i][0:L, 2 * L:4 * L], 0.0).astype(BF16) for i in P]
        x = [sa[i][0:L] + _dot(a_ak[i], v_st[i]) for i in P]
        ap = [jnp.where(strict, aa[i][0:L, 0:2 * L], 0.0).astype(BF16) for i in P]
        yield
        for d in range(n_dbl):
            xs = [stack_heads(x[i].astype(BF16), head1) for i in P]
            if d + 1 < n_dbl:
                both = [_dot(ap[i], jnp.concatenate([xs[i], stack_heads(ap[i], colh1)], axis=1)) for i in P]
                x = [x[i] + both[i][:, 0:2 * L] for i in P]
                ap = [both[i][:, 2 * L:].astype(BF16) for i in P]
            else:
                x = [x[i] + _dot(ap[i], xs[i]) for i in P]
            yield
        u_b = [x[i].astype(BF16) for i in P]
        a_rb = [jnp.where(incl, aa[i][L:2 * L, 0:2 * L], 0.0).astype(BF16) for i in P]
        a_rk = [jnp.where(incl, aa[i][L:2 * L, 2 * L:4 * L], 0.0).astype(BF16) for i in P]
        y = [sa[i][L:2 * L] + _dot(jnp.concatenate([a_rb[i], a_rk[i]], axis=1),
                                   jnp.concatenate([stack_heads(u_b[i], head1), v_st[i]], axis=0)) for i in P]
        yield
        ds = [_dot_tn(jnp.concatenate([u_b[i], T[i]["v_b"]], axis=0),
                      jnp.concatenate([T[i]["b_e"], T[i]["k_e"]], axis=0)) for i in P]
        res["sd"] = [sd[i] * T[i]["p_l"] + jnp.where(same_head, ds[i], 0.0) for i in P]
        yield
        y = jnp.concatenate(y, axis=1)
        mean = seg_sum(y) * (1.0 / H)
        yield
        dlt = y - mean
        var = seg_sum(dlt * dlt) * (1.0 / H)
        yield
        yn = dlt * lax.rsqrt(var + GN_EPS) * lnw_ref[...] + lnb_ref[...]
        g = jnp.concatenate([T[i]["g"] for i in P], axis=1)
        bonus = jnp.concatenate([T[i]["bonus"] for i in P], axis=1)
        res["out"] = (yn + bonus) * g

    sds = [[sd_ref[bi, i] for i in P] for bi in range(BB)]
    Ts = [[None] * NP for _ in range(BB)]
    results = [{} for _ in range(BB)]
    for item in prologue_items(0, Ts[0]):
        item()
    for bi in range(BB):
        items = prologue_items(bi + 1, Ts[bi + 1]) if bi + 1 < BB else []
        quota = iter(plan)
        for _ in chain_stages(bi, Ts[bi], sds[bi], results[bi]):
            for _ in range(next(quota, 0)):
                if items:
                    items.pop(0)()
        while items:
            items.pop(0)()

    for bi in range(BB):
        for i in P:
            sd_ref[bi, i] = results[bi]["sd"][i]
        y_ref[bi] = results[bi]["out"].astype(y_ref.dtype)
        br_ref[bi, 0:1, :] = pr_ref[bi, L - 1:L, :]
        bk_ref[bi, 0:1, :] = pk_ref[bi, L - 1:L, :]
        bv_ref[bi, 0:1, :] = pv_ref[bi, L - 1:L, :]
        bl_ref[bi, 0:1, :] = plo_ref[bi, L - 1:L, :]

    @pl.when(c == NC - 1)
    def _finish():
        for bi in range(BB):
            for i in P:
                s = sd_ref[bi, i]
                so_ref[bi, 2 * i] = s[0:H, 0:H]
                so_ref[bi, 2 * i + 1] = s[H:2 * H, H:2 * H]


def _rwkv_prompt2(proj3, shift3, s0, p, BB, plan):
    b, t, _ = proj3.shape
    L = RWKV_CHUNK
    assert t % L == 0 and 2 * L == LANES and b % BB == 0
    nc = t // L
    w = D_RWKV
    lora_blk = 3 * D_RWKV // LORA_W
    col = lambda o: (lambda bi, c: (bi, c, o))
    sh = lambda o: (lambda bi, c: (bi, 0, o))
    par = lambda o: (lambda bi, c: (0, o))
    in_specs = [
        pl.BlockSpec((BB, L, w), col(0)), pl.BlockSpec((BB, L, w), col(1)), pl.BlockSpec((BB, L, w), col(2)),
        pl.BlockSpec((BB, L, LORA_W), col(lora_blk)),
        pl.BlockSpec((BB, 1, w), sh(0)), pl.BlockSpec((BB, 1, w), sh(1)), pl.BlockSpec((BB, 1, w), sh(2)),
        pl.BlockSpec((BB, 1, LORA_W), sh(lora_blk)),
        pl.BlockSpec((BB, N_HEADS, HEAD_DIM, HEAD_DIM), lambda bi, c: (bi, 0, 0, 0)),
        pl.BlockSpec((1, w), par(0)), pl.BlockSpec((1, w), par(1)), pl.BlockSpec((1, w), par(2)),
        pl.BlockSpec((1, LORA_W), par(lora_blk)),
    ] + [pl.BlockSpec((1, w), par(0))] * 7 + [
        pl.BlockSpec((LANES, w), par(0)), pl.BlockSpec((LORA_A_WIN, w), par(0)),
        pl.BlockSpec((LORA_W - LORA_G_START, w), par(0)),
    ]
    out_specs = [pl.BlockSpec((BB, L, w), col(0)),
                 pl.BlockSpec((BB, N_HEADS, HEAD_DIM, HEAD_DIM), lambda bi, c: (bi, 0, 0, 0))]
    return pl.pallas_call(
        functools.partial(_rwkv_prompt2_kernel, L=L, NC=nc, BB=BB, plan=plan),
        grid=(b // BB, nc),
        in_specs=in_specs,
        out_specs=out_specs,
        out_shape=[jax.ShapeDtypeStruct((b, t, D_RWKV), BF16),
                   jax.ShapeDtypeStruct((b, N_HEADS, HEAD_DIM, HEAD_DIM), F32)],
        scratch_shapes=[pltpu.VMEM((BB, D_RWKV // LANES, LANES, LANES), F32),
                        pltpu.VMEM((BB, 8, w), F32), pltpu.VMEM((BB, 8, w), F32),
                        pltpu.VMEM((BB, 8, w), F32), pltpu.VMEM((BB, 8, LORA_W), F32)],
        compiler_params=_cparams(("parallel", "arbitrary")),
        name="rwkv_mix",
    )(proj3, proj3, proj3, proj3, shift3, shift3, shift3, shift3, s0,
      p["mu"], p["mu"], p["mu"], p["mu"], p["w0"], p["a0"], p["k_k"], p["k_a"], p["r_k"],
      p["ln_w"], p["ln_b"], p["w_lora"], p["a_lora"], p["g_lora"])


def _rwkv_sample_prep_kernel(pr_ref, pk_ref, pv_ref, plo_ref, shr_ref, shk_ref, shv_ref, shl_ref,
                             mur_ref, muk_ref, muv_ref, mul_ref, w0_ref, a0_ref, kk_ref, ka_ref, rk_ref,
                             wl_ref, al_ref, gl_ref, *out_refs, nb, nt):
    W = pr_ref.shape[1]
    n_tiles = W // LANES
    rows = nb * nt
    seg_sum = functools.partial(_seg_sum, seg=_head_pair_mask().astype(BF16), rows=rows)

    def shifted(p_ref, s_ref, mu_ref):
        p = p_ref[...]
        prev = jnp.concatenate([s_ref[...], p[0:rows - nb]], axis=0)
        return p + mu_ref[...] * (prev - p)

    ps_l = shifted(plo_ref, shl_ref, mul_ref)
    td = _tanh(ps_l[:, 0:LANES]).astype(BF16)
    da = ps_l[:, 0:LORA_A_WIN].astype(BF16)
    sg = _sigmoid(ps_l[:, LORA_G_START:]).astype(BF16)
    r = shifted(pr_ref, shr_ref, mur_ref)
    k = shifted(pk_ref, shk_ref, muk_ref)
    v = shifted(pv_ref, shv_ref, muv_ref)
    zlog = w0_ref[...] + _dot(td, wl_ref[...])
    w = jnp.exp(-math.exp(-0.5) / (1.0 + jnp.exp(-zlog)))
    a_sig = _sigmoid(a0_ref[...] + _dot(da, al_ref[...]))
    kk = k * kk_ref[...]
    kk = kk * jnp.minimum(lax.rsqrt(jnp.maximum(seg_sum(kk * kk), 0.0)), 1e12)
    k_h = k * (1.0 + (a_sig - 1.0) * ka_ref[...])
    g = _dot(sg, gl_ref[...])
    bonus = seg_sum(r * k_h * rk_ref[...]) * v
    vals = dict(r=r, w=w, k=k_h, v=v, a=-kk, b=kk * a_sig, g=g, bonus=bonus)
    for name, o_ref in zip(PREP_OUT, out_refs):
        x = vals[name]
        for t in range(nt):
            for c in range(n_tiles):
                o_ref[t, c * LANES:(c + 1) * LANES, :] = x[t * nb:(t + 1) * nb, c * LANES:(c + 1) * LANES].T


def _rwkv_sample_prep(proj_r, shift, p, nb, nt):
    w = SAMPLE_PREP_W
    kb = D_RWKV // w
    lora_blk = 3 * D_RWKV // LORA_W
    rows = nt * nb
    col = lambda o: (lambda g: (0, o + g))
    in_specs = (
        [pl.BlockSpec((rows, w), col(0)), pl.BlockSpec((rows, w), col(kb)), pl.BlockSpec((rows, w), col(2 * kb)),
         pl.BlockSpec((rows, LORA_W), lambda g: (0, lora_blk))]
        + [pl.BlockSpec((nb, w), col(0)), pl.BlockSpec((nb, w), col(kb)), pl.BlockSpec((nb, w), col(2 * kb)),
           pl.BlockSpec((nb, LORA_W), lambda g: (0, lora_blk))]
        + [pl.BlockSpec((1, w), col(0)), pl.BlockSpec((1, w), col(kb)), pl.BlockSpec((1, w), col(2 * kb)),
           pl.BlockSpec((1, LORA_W), lambda g: (0, lora_blk))]
        + [pl.BlockSpec((1, w), col(0))] * 5
        + [pl.BlockSpec((LANES, w), col(0)), pl.BlockSpec((LORA_A_WIN, w), col(0)),
           pl.BlockSpec((LORA_W - LORA_G_START, w), col(0))])
    out_spec = pl.BlockSpec((nt, w, nb), lambda g: (0, g, 0))
    return pl.pallas_call(
        functools.partial(_rwkv_sample_prep_kernel, nb=nb, nt=nt),
        grid=(kb,),
        in_specs=in_specs,
        out_specs=[out_spec] * len(PREP_OUT),
        out_shape=[jax.ShapeDtypeStruct((nt, D_RWKV, nb), F32)] * len(PREP_OUT),
        compiler_params=_cparams(("parallel",)),
        name="rwkv_sample_prep",
    )(proj_r, proj_r, proj_r, proj_r, shift, shift, shift, shift, p["mu"], p["mu"], p["mu"], p["mu"],
      p["w0"], p["a0"], p["k_k"], p["k_a"], p["r_k"], p["w_lora"], p["a_lora"], p["g_lora"])


def _wkv_sample_kernel(s_ref, r_ref, w_ref, k_ref, v_ref, a_ref, b_ref, g_ref, bonus_ref, lnw_ref, lnb_ref,
                       so_ref, o_ref, y_ref, *, nt):
    H = HEAD_DIM
    SUB = 8
    PAR = 4
    rowid = lax.broadcasted_iota(jnp.int32, (SUB, LANES), 0)

    def body(i8, carry):
        base = pl.multiple_of(i8 * SUB, SUB)
        v8 = [v_ref[t, pl.ds(base, SUB), :] for t in range(nt)]
        y8 = [jnp.zeros((SUB, LANES), F32) for _ in range(nt)]
        for h0 in range(0, SUB, PAR):
            ids = list(range(h0, h0 + PAR))
            S = [s_ref[0, base + ii] for ii in ids]
            for t in range(nt):
                a, w, b, k, r = a_ref[t], w_ref[t], b_ref[t], k_ref[t], r_ref[t]
                for n, ii in enumerate(ids):
                    sa = jnp.sum(S[n] * a, axis=0, keepdims=True)
                    S[n] = S[n] * w + sa * b + v8[t][ii:ii + 1, :] * k
                    y = jnp.sum(S[n] * r, axis=0, keepdims=True)
                    y8[t] = jnp.where(rowid == ii, y, y8[t])
            for n, ii in enumerate(ids):
                so_ref[0, base + ii] = S[n]
        for t in range(nt):
            y_ref[t, pl.ds(base, SUB), :] = y8[t]
        return carry

    lax.fori_loop(0, H // SUB, body, 0)
    for t in range(nt):
        y = y_ref[t]
        mean = jnp.sum(y, axis=0, keepdims=True) * (1.0 / H)
        d = y - mean
        var = jnp.sum(d * d, axis=0, keepdims=True) * (1.0 / H)
        out = (d * lax.rsqrt(var + GN_EPS) * lnw_ref[...] + lnb_ref[...] + bonus_ref[t]) * g_ref[t]
        o_ref[t] = out.astype(o_ref.dtype)


def _wkv_sample(state_t, prep, lnw_b, lnb_b):
    nh, hd, _, nb = state_t.shape
    nt = prep[0].shape[0]
    st_spec = pl.BlockSpec((1, hd, hd, nb), lambda h: (h, 0, 0, 0))
    ch_spec = pl.BlockSpec((nt, hd, nb), lambda h: (0, h, 0))
    ln_spec = pl.BlockSpec((hd, nb), lambda h: (h, 0))
    return pl.pallas_call(
        functools.partial(_wkv_sample_kernel, nt=nt),
        grid=(nh,),
        in_specs=[st_spec] + [ch_spec] * len(PREP_OUT) + [ln_spec, ln_spec],
        out_specs=[st_spec, ch_spec],
        out_shape=[jax.ShapeDtypeStruct(state_t.shape, F32), jax.ShapeDtypeStruct((nt, nh * hd, nb), BF16)],
        scratch_shapes=[pltpu.VMEM((nt, hd, nb), F32)],
        compiler_params=_cparams(("parallel",)),
        name="wkv_sample",
    )(state_t, *prep, lnw_b, lnb_b)


def _swa_prompt_kernel(slope_ref, sink_ref, q_ref, kc_ref, kp_ref, vc_ref, vp_ref, o_ref, bias_ref):
    n = pl.program_id(1)
    blk = WINDOW
    H = HEAD_DIM

    @pl.when((pl.program_id(0) == 0) & (n == 0))
    def _():
        t = lax.broadcasted_iota(jnp.int32, (blk, 2 * blk), 0)
        j = lax.broadcasted_iota(jnp.int32, (blk, 2 * blk), 1)
        dist = t - j + blk
        band = (dist >= 0) & (dist <= WINDOW)
        first = band & (j >= blk)
        distf = dist.astype(F32)
        for h in range(N_HEADS):
            ab = -slope_ref[h] * distf
            bias_ref[0, h] = jnp.where(first, ab, -jnp.inf)
            bias_ref[1, h] = jnp.where(band, ab, -jnp.inf)

    sel = jnp.where(n == 0, 0, 1)
    low = lax.broadcasted_iota(jnp.int32, (blk, LANES), 1) < H
    scale = H ** -0.5
    tile = lambda i: slice(i * LANES, (i + 1) * LANES)

    def kv_group(KV):
        kslab = {hk: jnp.concatenate([kp_ref[0, :, tile(hk // 2)], kc_ref[0, :, tile(hk // 2)]],
                                     axis=0).astype(BF16) for hk in KV}
        vslab = {hk: jnp.concatenate([vp_ref[0, :, tile(hk // 2)], vc_ref[0, :, tile(hk // 2)]],
                                     axis=0).astype(BF16) for hk in KV}
        lhs = {}
        for hk in KV:
            parts = []
            for s2 in range(2):
                xs = q_ref[0, :, tile(2 * hk + s2)].astype(F32) * scale
                xr = pltpu.roll(xs, H, axis=1)
                if hk % 2 == 0:
                    parts += [jnp.where(low, xs, 0.0), jnp.where(low, xr, 0.0)]
                else:
                    parts += [jnp.where(low, 0.0, xr), jnp.where(low, 0.0, xs)]
            lhs[hk] = jnp.concatenate(parts, axis=0).astype(BF16)
        s = {hk: _dot_nt(lhs[hk], kslab[hk]) for hk in KV}
        p, rden = {}, {}
        for hk in KV:
            ps, rs = [], []
            for g in range(GQA_GROUP):
                h = hk * GQA_GROUP + g
                sg = s[hk][g * blk:(g + 1) * blk] + bias_ref[sel, h]
                m = jnp.maximum(jnp.max(sg, axis=-1, keepdims=True), sink_ref[h])
                e = jnp.exp(sg - m)
                rs.append(1.0 / (jnp.sum(e, axis=-1, keepdims=True) + jnp.exp(sink_ref[h] - m)))
                ps.append(e.astype(BF16))
            p[hk] = jnp.concatenate(ps, axis=0)
            rden[hk] = rs
        o = {hk: _dot(p[hk], vslab[hk]) for hk in KV}
        for hk in KV:
            for s2 in range(2):
                ga, gb = 2 * s2, 2 * s2 + 1
                oa = o[hk][ga * blk:(ga + 1) * blk] * rden[hk][ga]
                ob = o[hk][gb * blk:(gb + 1) * blk] * rden[hk][gb]
                if hk % 2 == 0:
                    out = jnp.where(low, oa, pltpu.roll(ob, H, axis=1))
                else:
                    out = jnp.where(low, pltpu.roll(oa, H, axis=1), ob)
                o_ref[0, :, tile(2 * hk + s2)] = out.astype(o_ref.dtype)

    for g0 in range(0, N_KV_HEADS, KV_PER_GROUP):
        kv_group(range(g0, g0 + KV_PER_GROUP))


def _swa_prompt(q3, kv3, slopes, sinks):
    b, t, _ = q3.shape
    nb = t // WINDOW
    smem = pl.BlockSpec(memory_space=pltpu.SMEM)
    prev = lambda n: jnp.maximum(n - 1, 0)
    return pl.pallas_call(
        _swa_prompt_kernel,
        grid=(b, nb),
        in_specs=[smem, smem,
                  pl.BlockSpec((1, WINDOW, D_MODEL), lambda bi, n: (bi, n, 0)),
                  pl.BlockSpec((1, WINDOW, D_KV), lambda bi, n: (bi, n, 0)),
                  pl.BlockSpec((1, WINDOW, D_KV), lambda bi, n: (bi, prev(n), 0)),
                  pl.BlockSpec((1, WINDOW, D_KV), lambda bi, n: (bi, n, 1)),
                  pl.BlockSpec((1, WINDOW, D_KV), lambda bi, n: (bi, prev(n), 1))],
        out_specs=pl.BlockSpec((1, WINDOW, D_MODEL), lambda bi, n: (bi, n, 0)),
        out_shape=jax.ShapeDtypeStruct((b, t, D_MODEL), BF16),
        scratch_shapes=[pltpu.VMEM((2, N_HEADS, WINDOW, 2 * WINDOW), F32)],
        compiler_params=_cparams(("arbitrary", "arbitrary")),
        name="swa_prompt",
    )(slopes, sinks, q3, kv3, kv3, kv3, kv3)


def _swa_sample_kernel(slope_ref, sink_ref, q_ref, kc_ref, vc_ref, knew_ref, vnew_ref,
                       o_ref, kwin_ref, vwin_ref, nbuf_ref, *, tq, BB, UNR):
    GT = GQA_GROUP * tq
    R = N_KV_HEADS * GT
    NP = 16
    C = D_KV
    row = lax.broadcasted_iota(jnp.int32, (R, WINDOW), 0)
    wcol = lax.broadcasted_iota(jnp.int32, (R, WINDOW), 1)
    t = lax.rem(row, tq)
    slope = slope_ref[...]
    sink = sink_ref[...]
    dist_o = WINDOW + t - wcol
    bias_old = jnp.where(dist_o <= WINDOW, -slope * dist_o.astype(F32), -jnp.inf)
    s_idx = wcol - (WINDOW - tq)
    dist_n = t - s_idx
    bias_new = jnp.where((s_idx >= 0) & (dist_n >= 0), -slope * dist_n.astype(F32), -jnp.inf)
    hkmask = (lax.broadcasted_iota(jnp.int32, (R, C), 0) // GT) == (lax.broadcasted_iota(jnp.int32, (R, C), 1) // HEAD_DIM)
    srow = lax.broadcasted_iota(jnp.int32, (NP, WINDOW), 0)
    scol = lax.broadcasted_iota(jnp.int32, (NP, WINDOW), 1)
    selw = ((scol == srow + (WINDOW - tq)) & (srow < tq)).astype(BF16)
    lane_new = lax.broadcasted_iota(jnp.int32, (C, WINDOW), 1) >= WINDOW - tq
    scale = HEAD_DIM ** -0.5
    for u in range(UNR):
        nbuf_ref[u, :, tq:NP, :] = jnp.zeros((2, NP - tq, C), F32)

    def transposed_new(x):
        return sum(_dot_tn(part, selw) for part in _split3(x))

    def body(i, carry):
        bs = [i * UNR + u for u in range(UNR)]
        U = range(UNR)
        for u in U:
            nbuf_ref[u, 0, 0:tq, :] = knew_ref[bs[u]]
            nbuf_ref[u, 1, 0:tq, :] = vnew_ref[bs[u]]
        kt = [kc_ref[b] for b in bs]
        vt = [vc_ref[b] for b in bs]
        knt = [transposed_new(nbuf_ref[u, 0]) for u in U]
        vnt = [transposed_new(nbuf_ref[u, 1]) for u in U]
        qbd = [jnp.where(hkmask, jnp.concatenate([q_ref[b] * scale] * N_KV_HEADS, axis=0), 0.0).astype(BF16)
               for b in bs]
        s_o = [_dot(qbd[u], kt[u].astype(BF16)) + bias_old for u in U]
        s_n = [_dot(qbd[u], knt[u].astype(BF16)) + bias_new for u in U]
        outs = []
        for u in U:
            m = jnp.maximum(jnp.maximum(jnp.max(s_o[u], axis=-1, keepdims=True),
                                        jnp.max(s_n[u], axis=-1, keepdims=True)), sink)
            p_o = jnp.exp(s_o[u] - m)
            p_n = jnp.exp(s_n[u] - m)
            rden = 1.0 / (jnp.sum(p_o, axis=-1, keepdims=True) + jnp.sum(p_n, axis=-1, keepdims=True)
                          + jnp.exp(sink - m))
            o = _dot_nt(p_o.astype(BF16), vt[u].astype(BF16)) + _dot_nt(p_n.astype(BF16), vnt[u].astype(BF16))
            o = jnp.where(hkmask, o * rden, 0.0)
            acc = o[0:GT]
            for hk in range(1, N_KV_HEADS):
                acc = acc + o[hk * GT:(hk + 1) * GT]
            outs.append(acc)
        for u in U:
            kwin_ref[bs[u]] = jnp.where(lane_new, knt[u], pltpu.roll(kt[u], WINDOW - tq, axis=1))
            vwin_ref[bs[u]] = jnp.where(lane_new, vnt[u], pltpu.roll(vt[u], WINDOW - tq, axis=1))
            o_ref[bs[u]] = outs[u].astype(o_ref.dtype)
        return carry

    lax.fori_loop(0, BB // UNR, body, 0)


def _swa_sample(q16, knew3, vnew3, kct, vct, slope_rows, sink_rows):
    b, gt, c = q16.shape
    tq = gt // GQA_GROUP
    bb = 8 if b % 8 == 0 else 1
    unr = 2 if bb % 2 == 0 else 1
    rows = N_KV_HEADS * gt
    blk3 = lambda shape: pl.BlockSpec(shape, lambda i: (i, 0, 0))
    full2 = pl.BlockSpec((rows, 1), lambda i: (0, 0))
    kern = functools.partial(_swa_sample_kernel, tq=tq, BB=bb, UNR=unr)
    return pl.pallas_call(
        kern,
        grid=(b // bb,),
        in_specs=[full2, full2, blk3((bb, gt, c)), blk3((bb, c, WINDOW)), blk3((bb, c, WINDOW)),
                  blk3((bb, tq, c)), blk3((bb, tq, c))],
        out_specs=[blk3((bb, gt, c)), blk3((bb, c, WINDOW)), blk3((bb, c, WINDOW))],
        out_shape=[jax.ShapeDtypeStruct((b, gt, c), BF16),
                   jax.ShapeDtypeStruct((b, c, WINDOW), F32),
                   jax.ShapeDtypeStruct((b, c, WINDOW), F32)],
        scratch_shapes=[pltpu.VMEM((unr, 2, 16, c), F32)],
        compiler_params=_cparams(("parallel",)),
        name="swa_sample",
    )(slope_rows, sink_rows, q16, kct, vct, knew3, vnew3)


def _merge_out_kernel(x_ref, ya_ref, yb_ref, ga_ref, gb_ref, wo_ref, nw_ref, h_ref, hn_ref):
    f32 = lambda ref: ref[...].astype(F32)
    mixed = _sigmoid(f32(ga_ref)) * f32(ya_ref) + _sigmoid(f32(gb_ref)) * f32(yb_ref)
    h = x_ref[...] + _dot(mixed.astype(BF16), wo_ref[...])
    h_ref[...] = h
    ms = jnp.mean(h * h, axis=-1, keepdims=True)
    hn_ref[...] = (h * lax.rsqrt(ms + RMS_EPS) * nw_ref[...]).astype(BF16)


def _merge_out(x, ya, yb, gates, w_out, nw, tm):
    m, d = x.shape
    row = lambda o: (lambda i: (i, o))
    return pl.pallas_call(
        _merge_out_kernel,
        grid=(m // tm,),
        in_specs=[pl.BlockSpec((tm, d), row(0)), pl.BlockSpec((tm, d), row(0)), pl.BlockSpec((tm, d), row(0)),
                  pl.BlockSpec((tm, d), row(0)), pl.BlockSpec((tm, d), row(1)),
                  pl.BlockSpec((d, d), lambda i: (0, 0)), pl.BlockSpec((1, d), lambda i: (0, 0))],
        out_specs=[pl.BlockSpec((tm, d), row(0)), pl.BlockSpec((tm, d), row(0))],
        out_shape=[jax.ShapeDtypeStruct((m, d), F32), jax.ShapeDtypeStruct((m, d), BF16)],
        compiler_params=_cparams(("parallel",)),
        name="merge_out_proj",
    )(x, ya, yb, gates, gates, w_out, nw)


def _mlp_kernel(hn_ref, h_ref, wu_ref, wd_ref, nw_ref, o_ref, acc_ref):
    j = pl.program_id(1)

    @pl.when(j == 0)
    def _():
        acc_ref[...] = jnp.zeros_like(acc_ref)

    u = jnp.maximum(_dot(hn_ref[...], wu_ref[...]), 0.0)
    acc_ref[...] += _dot((u * u).astype(BF16), wd_ref[...])

    @pl.when(j == pl.num_programs(1) - 1)
    def _():
        h = h_ref[...] + acc_ref[...]
        ms = jnp.mean(h * h, axis=-1, keepdims=True)
        o_ref[...] = h * lax.rsqrt(ms + RMS_EPS) * nw_ref[...]


def _mlp(hn, h, w_up, w_down, nw, tm, tf):
    m, d = h.shape
    f = w_up.shape[1]
    return pl.pallas_call(
        _mlp_kernel,
        grid=(m // tm, f // tf),
        in_specs=[pl.BlockSpec((tm, d), lambda i, j: (i, 0)), pl.BlockSpec((tm, d), lambda i, j: (i, 0)),
                  pl.BlockSpec((d, tf), lambda i, j: (0, j)), pl.BlockSpec((tf, d), lambda i, j: (j, 0)),
                  pl.BlockSpec((1, d), lambda i, j: (0, 0))],
        out_specs=pl.BlockSpec((tm, d), lambda i, j: (i, 0)),
        out_shape=jax.ShapeDtypeStruct((m, d), F32),
        scratch_shapes=[pltpu.VMEM((tm, d), F32)],
        compiler_params=_cparams(("parallel", "arbitrary")),
        name="mlp_final_norm",
    )(hn, h, w_up, w_down, nw)


def _pick(m, prefs):
    for t in prefs:
        if m % t == 0:
            return t
    return m


def _pad_cols(v, n):
    return jnp.concatenate([v, jnp.zeros(v.shape[:-1] + (n - v.shape[-1],), v.dtype)], axis=-1)


def _place_rows(w, start, rows):
    n, d = w.shape
    return jnp.concatenate([jnp.zeros((start, d), w.dtype), w, jnp.zeros((rows - start - n, d), w.dtype)], axis=0)


def _layer(x, shift_prev, wkv0, lw, *, prompt, k_cache_t=None, v_cache_t=None):
    b, t, d = x.shape
    m = b * t
    x2 = x.reshape(m, d) if prompt else x.transpose(1, 0, 2).reshape(m, d)
    proj_r, proj_q, proj_kv, proj_g = _in_proj(
        x2, lw["norm_mix_w"], lw["w_t"], IN_GROUPS, [F32, BF16, F32, BF16], _pick(m, (1024, 512, 256, 128, 8)))
    shift_pad = _pad_cols(shift_prev, R_PAD)

    if prompt:
        proj_r3 = proj_r.reshape(b, t, R_PAD)
        kv3 = proj_kv.reshape(b, t, 2 * D_KV)
        ya, wkv_new = _rwkv_prompt2(proj_r3, shift_pad[:, None], wkv0, lw, _pick(b, (4, 2, 1)), RWKV_SKEW_PLAN)
        yb = _swa_prompt(proj_q.reshape(b, t, D_MODEL), kv3, lw["slopes"], lw["sinks"])
        k_win = kv3[:, t - WINDOW:, :D_KV].reshape(b, WINDOW, N_KV_HEADS, HEAD_DIM)
        v_win = kv3[:, t - WINDOW:, D_KV:].reshape(b, WINDOW, N_KV_HEADS, HEAD_DIM)
        shift_new = proj_r3[:, t - 1, :R_COLS]
    else:
        prep = _rwkv_sample_prep(proj_r, shift_pad, lw, b, t)
        wkv_t, ya_t = _wkv_sample(wkv0.transpose(1, 2, 3, 0), prep,
                                  jnp.broadcast_to(lw["ln_w"].reshape(D_RWKV, 1), (D_RWKV, b)),
                                  jnp.broadcast_to(lw["ln_b"].reshape(D_RWKV, 1), (D_RWKV, b)))
        ya = ya_t.transpose(0, 2, 1)
        wkv_new = wkv_t.transpose(3, 0, 1, 2)
        q16 = proj_q.reshape(t, b, N_KV_HEADS, GQA_GROUP, HEAD_DIM).transpose(1, 3, 0, 2, 4)
        q16 = q16.reshape(b, GQA_GROUP * t, D_KV)
        kv_bt = proj_kv.reshape(t, b, 2 * D_KV).transpose(1, 0, 2)
        gt_head = (jnp.arange(N_KV_HEADS)[:, None] * GQA_GROUP + jnp.arange(GQA_GROUP)[None, :])
        row_head = jnp.repeat(gt_head, t, axis=1).reshape(-1)
        o16, kwt, vwt = _swa_sample(q16, kv_bt[:, :, :D_KV], kv_bt[:, :, D_KV:], k_cache_t, v_cache_t,
                                    lw["slopes"][row_head][:, None], lw["sinks"][row_head][:, None])
        yb = o16.reshape(b, GQA_GROUP, t, N_KV_HEADS, HEAD_DIM).transpose(2, 0, 3, 1, 4)
        k_win = kwt.reshape(b, N_KV_HEADS, HEAD_DIM, WINDOW).transpose(0, 3, 1, 2)
        v_win = vwt.reshape(b, N_KV_HEADS, HEAD_DIM, WINDOW).transpose(0, 3, 1, 2)
        shift_new = proj_r[(t - 1) * b:, :R_COLS]

    h, hn = _merge_out(x2, ya.reshape(m, d), yb.reshape(m, d), proj_g, lw["w_out"], lw["norm_mlp_w"],
                       _pick(m, (512, 256, 128, 8)))
    y = _mlp(hn, h, lw["w_up"], lw["w_down"], lw["norm_final_w"], _pick(m, (512, 256, 128, 8)), 1024)
    y = y.reshape(b, t, d) if prompt else y.reshape(t, b, d).transpose(1, 0, 2)
    return y, shift_new, wkv_new, k_win, v_win


def kernel(x_prompt, x_sample, state_shift, state_wkv, cache_k_win, cache_v_win, norm_mix_w, w_in, tshift_mu, w0, w_lora, a0, a_lora, g_lora, k_k, k_a, r_k, ln_x_w, ln_x_b, attn_sinks, w_out, norm_mlp_w, w_up, w_down, norm_final_w):
    depth = w_in.shape[0]
    assert depth == 1
    l = 0
    bp = x_prompt.shape[0]
    db = x_sample.shape[0]
    hh = jnp.arange(N_HEADS, dtype=F32)
    lw = dict(
        norm_mix_w=norm_mix_w[l][None],
        w_t=jnp.swapaxes(w_in[l], 0, 1).astype(BF16),
        mu=_pad_cols(tshift_mu[l][None], R_PAD), w0=w0[l][None], a0=a0[l][None], k_k=k_k[l][None],
        k_a=k_a[l][None], r_k=r_k[l].reshape(1, D_RWKV), ln_w=ln_x_w[l][None], ln_b=ln_x_b[l][None],
        w_lora=_place_rows(w_lora[l], 0, LANES).astype(BF16),
        a_lora=_place_rows(a_lora[l], LORA_DECAY_END, LORA_A_WIN).astype(BF16),
        g_lora=_place_rows(g_lora[l], LORA_A_END - LORA_G_START, LORA_W - LORA_G_START).astype(BF16),
        slopes=jnp.exp2(-8.0 * (hh + 1.0) / N_HEADS), sinks=attn_sinks[l].astype(F32),
        w_out=w_out[l].astype(BF16), norm_mlp_w=norm_mlp_w[l][None],
        w_up=w_up[l].astype(BF16), w_down=w_down[l].astype(BF16), norm_final_w=norm_final_w[None],
    )
    yp, sp, wp, kp, vp = _layer(
        x_prompt, jnp.zeros((bp, R_COLS), F32), jnp.zeros((bp, N_HEADS, HEAD_DIM, HEAD_DIM), F32), lw, prompt=True)
    kct = cache_k_win[l].transpose(0, 2, 3, 1).reshape(db, D_KV, WINDOW)
    vct = cache_v_win[l].transpose(0, 2, 3, 1).reshape(db, D_KV, WINDOW)
    ys, ss, ws, ksm, vsm = _layer(x_sample, state_shift[l], state_wkv[l], lw, prompt=False,
                                  k_cache_t=kct, v_cache_t=vct)
    return (yp, ys, sp[None], wp[None], kp[None], vp[None], ss[None], ws[None], ksm[None], vsm[None])
```

```python
import functools
import math

import jax
import jax.numpy as jnp
from jax import lax
from jax.experimental import pallas as pl
from jax.experimental.pallas import tpu as pltpu

F32 = jnp.float32
BF16 = jnp.bfloat16

D_MODEL = 2048
HEAD_DIM = 64
N_HEADS = D_MODEL // HEAD_DIM
N_KV_HEADS = 8
GQA_GROUP = N_HEADS // N_KV_HEADS
D_KV = N_KV_HEADS * HEAD_DIM
WINDOW = 128
D_FF = 4 * D_MODEL
D_DECAY_LORA = 96
D_A_LORA = 96
D_GATE_LORA = 256
D_RWKV = D_MODEL
R_COLS = 3 * D_RWKV + D_DECAY_LORA + D_A_LORA + D_GATE_LORA
C_IN = R_COLS + D_MODEL + 2 * D_KV + 2 * D_MODEL
RMS_EPS = 1e-5
GN_EPS = 64e-5

LANES = 128
PROJ_TILE = 512
R_PAD = -(-R_COLS // PROJ_TILE) * PROJ_TILE
LORA_W = R_PAD - 3 * D_RWKV
LORA_DECAY_END = D_DECAY_LORA
LORA_A_END = D_DECAY_LORA + D_A_LORA
LORA_A_WIN = -(-LORA_A_END // LANES) * LANES
LORA_G_START = (LORA_A_END // LANES) * LANES
IN_GROUPS = ((0, R_PAD), (R_COLS, D_MODEL), (R_COLS + D_MODEL, 2 * D_KV), (R_COLS + D_MODEL + 2 * D_KV, 2 * D_MODEL))

VMEM_LIMIT = 56 * 1024 * 1024
RWKV_CHUNK = 64
RWKV_SKEW_PLAN = (1, 1, 4, 4, 4, 4, 1, 4, 4, 4, 4)
KV_PER_GROUP = 2
SAMPLE_PREP_W = 512
PREP_OUT = ("r", "w", "k", "v", "a", "b", "g", "bonus")


def _cparams(sem):
    return pltpu.CompilerParams(dimension_semantics=sem, vmem_limit_bytes=VMEM_LIMIT)


def _dot(a, b):
    return jnp.dot(a, b, preferred_element_type=F32)


def _dot_nt(a, b):
    return lax.dot_general(a, b, (((1,), (1,)), ((), ())), preferred_element_type=F32)


def _dot_tn(a, b):
    return lax.dot_general(a, b, (((0,), (0,)), ((), ())), preferred_element_type=F32)


def _sigmoid(x):
    return 1.0 / (1.0 + jnp.exp(-x))


def _tanh(x):
    return 1.0 - 2.0 / (1.0 + jnp.exp(2.0 * x))


def _split2(x):
    hi = x.astype(BF16)
    lo = (x - hi.astype(F32)).astype(BF16)
    return hi, lo


def _split3(x):
    hi = x.astype(BF16)
    r1 = x - hi.astype(F32)
    mid = r1.astype(BF16)
    lo = (r1 - mid.astype(F32)).astype(BF16)
    return hi, mid, lo


def _head_pair_mask():
    ji = lax.broadcasted_iota(jnp.int32, (LANES, LANES), 0)
    jj = lax.broadcasted_iota(jnp.int32, (LANES, LANES), 1)
    return (ji < HEAD_DIM) == (jj < HEAD_DIM)


def _seg_sum(x, seg, rows):
    n = x.shape[1] // LANES
    xs = jnp.concatenate([x[:, i * LANES:(i + 1) * LANES] for i in range(n)], axis=0).astype(BF16)
    ys = _dot(xs, seg)
    return jnp.concatenate([ys[i * rows:(i + 1) * rows] for i in range(n)], axis=1)


def _in_proj_kernel(x_ref, nw_ref, *refs, bounds):
    n = len(bounds)
    w_refs, o_refs, xn_ref = refs[:n], refs[n:2 * n], refs[2 * n]
    j = pl.program_id(1)

    @pl.when(j == 0)
    def _():
        x = x_ref[...]
        ms = jnp.mean(x * x, axis=-1, keepdims=True)
        xn_ref[...] = (x * lax.rsqrt(ms + RMS_EPS) * nw_ref[...]).astype(BF16)

    for w_ref, o_ref, (lo, hi) in zip(w_refs, o_refs, bounds):
        @pl.when((j >= lo) & (j < hi))
        def _(w_ref=w_ref, o_ref=o_ref):
            o_ref[...] = _dot_nt(xn_ref[...], w_ref[...]).astype(o_ref.dtype)


def _in_proj(x, nw, w_t, groups, out_dtypes, tm):
    m, d = x.shape
    tn = PROJ_TILE
    bounds, lo = [], 0
    for _, width in groups:
        bounds.append((lo, lo + width // tn))
        lo = bounds[-1][1]
    ROW_ALIGN = 16
    assert all(start % ROW_ALIGN == 0 and start + width <= w_t.shape[0] for start, width in groups)

    def clamp(lo_, hi_):
        return lambda j: jnp.clip(j - lo_, 0, hi_ - lo_ - 1)

    in_specs = [pl.BlockSpec((tm, d), lambda i, j: (i, 0)), pl.BlockSpec((1, d), lambda i, j: (0, 0))]
    out_specs, out_shape = [], []
    for (start, width), dt, (lo_, hi_) in zip(groups, out_dtypes, bounds):
        c = clamp(lo_, hi_)
        in_specs.append(pl.BlockSpec((pl.Element(tn), pl.Element(d)),
                                     lambda i, j, c=c, start=start: (pl.multiple_of(start + c(j) * tn, ROW_ALIGN), 0)))
        out_specs.append(pl.BlockSpec((tm, tn), lambda i, j, c=c: (i, c(j))))
        out_shape.append(jax.ShapeDtypeStruct((m, width), dt))
    return pl.pallas_call(
        functools.partial(_in_proj_kernel, bounds=tuple(bounds)),
        grid=(m // tm, lo),
        in_specs=in_specs,
        out_specs=out_specs,
        out_shape=out_shape,
        scratch_shapes=[pltpu.VMEM((tm, d), BF16)],
        compiler_params=_cparams(("parallel", "arbitrary")),
        name="norm_in_proj",
    )(x, nw, *([w_t] * len(groups)))


def _rwkv_prompt_kernel(pr_ref, pk_ref, pv_ref, plo_ref, shr_ref, shk_ref, shv_ref, shl_ref, s0_ref,
                         mur_ref, muk_ref, muv_ref, mul_ref, w0_ref, a0_ref, kk_ref, ka_ref, rk_ref, lnw_ref, lnb_ref,
                         wl_ref, al_ref, gl_ref,
                         y_ref, so_ref,
                         sd_ref, br_ref, bk_ref, bv_ref, bl_ref, *, L, NC, BB, plan):
    c = pl.program_id(1)
    H = HEAD_DIM
    NP = D_RWKV // LANES
    P = range(NP)
    sls = [slice(i * LANES, (i + 1) * LANES) for i in P]
    n_dbl = max(1, math.ceil(math.log2(L)))

    head1 = lax.broadcasted_iota(jnp.int32, (L, LANES), 1) < H
    row_i = lax.broadcasted_iota(jnp.int32, (L, 2 * L), 0)
    col_i = lax.broadcasted_iota(jnp.int32, (L, 2 * L), 1)
    col_t = jnp.where(col_i >= L, col_i - L, col_i)
    strict = col_t < row_i
    incl = col_t <= row_i
    colh1 = col_i < L
    tri = (lax.broadcasted_iota(jnp.int32, (L, L), 1) <= lax.broadcasted_iota(jnp.int32, (L, L), 0)).astype(BF16)
    same_head = _head_pair_mask()
    seg = same_head.astype(BF16)
    seg_sum = functools.partial(_seg_sum, seg=seg, rows=L)
    first_row = lax.broadcasted_iota(jnp.int32, (L, 1), 0) == 0

    def stack_heads(x, m):
        zero = jnp.zeros_like(x)
        return jnp.concatenate([jnp.where(m, x, zero), jnp.where(m, zero, x)], axis=0)

    @pl.when(c == 0)
    def _init():
        z = jnp.zeros((H, H), F32)
        for bi in range(BB):
            br_ref[bi, 0:1, :] = shr_ref[bi]
            bk_ref[bi, 0:1, :] = shk_ref[bi]
            bv_ref[bi, 0:1, :] = shv_ref[bi]
            bl_ref[bi, 0:1, :] = shl_ref[bi]
            for i in P:
                top = jnp.concatenate([s0_ref[bi, 2 * i], z], axis=1)
                bot = jnp.concatenate([z, s0_ref[bi, 2 * i + 1]], axis=1)
                sd_ref[bi, i] = jnp.concatenate([top, bot], axis=0)

    def prologue_items(bi, T):
        S = {}

        def shifted(p_ref, carry_ref, mu_ref):
            p = p_ref[bi]
            prev = jnp.where(first_row, carry_ref[bi, 0:1, :], pltpu.roll(p, 1, axis=0))
            return p + mu_ref[...] * (prev - p)

        def load_shift():
            ps_l = shifted(plo_ref, bl_ref, mul_ref)
            S["td"] = _tanh(ps_l[:, 0:LANES]).astype(BF16)
            S["da"] = ps_l[:, 0:LORA_A_WIN].astype(BF16)
            S["sg"] = _sigmoid(ps_l[:, LORA_G_START:]).astype(BF16)
            S["r"] = shifted(pr_ref, br_ref, mur_ref)
            S["k"] = shifted(pk_ref, bk_ref, muk_ref)
            S["v"] = shifted(pv_ref, bv_ref, muv_ref)
            S["kk"] = S["k"] * kk_ref[...]

        def matmuls_1():
            S["zlog"] = w0_ref[...] + _dot(S["td"], wl_ref[...])
            S["apre"] = a0_ref[...] + _dot(S["da"], al_ref[...])
            S["g"] = _dot(S["sg"], gl_ref[...])
            S["n2"] = seg_sum(S["kk"] * S["kk"])

        def vector_1(i):
            s = sls[i]
            logw = -math.exp(-0.5) / (1.0 + jnp.exp(-S["zlog"][:, s]))
            a_sig = _sigmoid(S["apre"][:, s])
            kk = S["kk"][:, s] * jnp.minimum(lax.rsqrt(jnp.maximum(S["n2"][:, s], 0.0)), 1e12)
            k_h = S["k"][:, s] * (1.0 + (a_sig - 1.0) * ka_ref[:, s])
            hi, lo = _split2(logw)
            S[("v1", i)] = dict(logw=logw, bvec=kk * a_sig, kk=kk, k_h=k_h, hi=hi, lo=lo,
                                prod=(S["r"][:, s] * k_h * rk_ref[:, s]).astype(BF16))

        def matmuls_2():
            cat = lambda n: jnp.concatenate([S[("v1", i)][n] for i in P], axis=1)
            S["cum"] = _dot(tri, cat("hi")) + _dot(tri, cat("lo"))
            S["bsum"] = _dot(jnp.concatenate([S[("v1", i)]["prod"] for i in P], axis=0), seg)

        def vector_2(i):
            s = sls[i]
            t1 = S[("v1", i)]
            cum = S["cum"][:, s]
            cum_l = cum[L - 1:L, :]
            p_inv = jnp.exp(-cum)
            p_end = jnp.exp(cum_l - cum)
            v = S["v"][:, s]
            T[i] = dict(
                a_t=(-t1["kk"] * jnp.exp(cum - t1["logw"])).astype(BF16),
                r_t=(S["r"][:, s] * jnp.exp(cum)).astype(BF16),
                b_t=(t1["bvec"] * p_inv).astype(BF16), k_t=(t1["k_h"] * p_inv).astype(BF16),
                b_e=(t1["bvec"] * p_end).astype(BF16), k_e=(t1["k_h"] * p_end).astype(BF16),
                v_b=v.astype(BF16), g=S["g"][:, s], bonus=S["bsum"][i * L:(i + 1) * L] * v, p_l=jnp.exp(cum_l))

        return ([load_shift, matmuls_1] + [functools.partial(vector_1, i) for i in P] + [matmuls_2]
                + [functools.partial(vector_2, i) for i in P])

    def chain_stages(bi, T, sd, res):
        lhs = [jnp.concatenate([T[i]["a_t"], T[i]["r_t"]], axis=0) for i in P]
        rhs = [jnp.concatenate([stack_heads(T[i]["b_t"], head1), stack_heads(T[i]["k_t"], head1)], axis=0) for i in P]
        aa = [_dot_nt(lhs[i], rhs[i]) for i in P]
        yield
        sa = [_dot_nt(lhs[i], sd[i].astype(BF16)) for i in P]
        yield
        v_st = [stack_heads(T[i]["v_b"], head1) for i in P]
        a_ak = [jnp.where(strict, aa[i][0:L, 2 * L:4 * L], 0.0).astype(BF16) for i in P]
        x = [sa[i][0:L] + _dot(a_ak[i], v_st[i]) for i in P]
        ap = [jnp.where(strict, aa[i][0:L, 0:2 * L], 0.0).astype(BF16) for i in P]
        yield
        for d in range(n_dbl):
            xs = [stack_heads(x[i].astype(BF16), head1) for i in P]
            if d + 1 < n_dbl:
                both = [_dot(ap[i], jnp.concatenate([xs[i], stack_heads(ap[i], colh1)], axis=1)) for i in P]
                x = [x[i] + both[i][:, 0:2 * L] for i in P]
                ap = [both[i][:, 2 * L:].astype(BF16) for i in P]
            else:
                x = [x[i] + _dot(ap[i], xs[i]) for i in P]
            yield
        u_b = [x[i].astype(BF16) for i in P]
        a_rb = [jnp.where(incl, aa[i][L:2 * L, 0:2 * L], 0.0).astype(BF16) for i in P]
        a_rk = [jnp.where(incl, aa[i][L:2 * L, 2 * L:4 * L], 0.0).astype(BF16) for i in P]
        y = [sa[i][L:2 * L] + _dot(jnp.concatenate([a_rb[i], a_rk[i]], axis=1),
                                   jnp.concatenate([stack_heads(u_b[i], head1), v_st[i]], axis=0)) for i in P]
        yield
        ds = [_dot_tn(jnp.concatenate([u_b[i], T[i]["v_b"]], axis=0),
                      jnp.concatenate([T[i]["b_e"], T[i]["k_e"]], axis=0)) for i in P]
        res["sd"] = [sd[i] * T[i]["p_l"] + jnp.where(same_head, ds[i], 0.0) for i in P]
        yield
        y = jnp.concatenate(y, axis=1)
        mean = seg_sum(y) * (1.0 / H)
        yield
        dlt = y - mean
        var = seg_sum(dlt * dlt) * (1.0 / H)
        yield
        yn = dlt * lax.rsqrt(var + GN_EPS) * lnw_ref[...] + lnb_ref[...]
        g = jnp.concatenate([T[i]["g"] for i in P], axis=1)
        bonus = jnp.concatenate([T[i]["bonus"] for i in P], axis=1)
        res["out"] = (yn + bonus) * g

    sds = [[sd_ref[bi, i] for i in P] for bi in range(BB)]
    Ts = [[None] * NP for _ in range(BB)]
    results = [{} for _ in range(BB)]
    for item in prologue_items(0, Ts[0]):
        item()
    for bi in range(BB):
        items = prologue_items(bi + 1, Ts[bi + 1]) if bi + 1 < BB else []
        quota = iter(plan)
        for _ in chain_stages(bi, Ts[bi], sds[bi], results[bi]):
            for _ in range(next(quota, 0)):
                if items:
                    items.pop(0)()
        while items:
            items.pop(0)()

    for bi in range(BB):
        for i in P:
            sd_ref[bi, i] = results[bi]["sd"][i]
        y_ref[bi] = results[bi]["out"].astype(y_ref.dtype)
        br_ref[bi, 0:1, :] = pr_ref[bi, L - 1:L, :]
        bk_ref[bi, 0:1, :] = pk_ref[bi, L - 1:L, :]
        bv_ref[bi, 0:1, :] = pv_ref[bi, L - 1:L, :]
        bl_ref[bi, 0:1, :] = plo_ref[bi, L - 1:L, :]

    @pl.when(c == NC - 1)
    def _finish():
        for bi in range(BB):
            for i in P:
                s = sd_ref[bi, i]
                so_ref[bi, 2 * i] = s[0:H, 0:H]
                so_ref[bi, 2 * i + 1] = s[H:2 * H, H:2 * H]


def _rwkv_prompt(proj3, shift3, s0, p, BB, plan):
    b, t, _ = proj3.shape
    L = RWKV_CHUNK
    assert t % L == 0 and 2 * L == LANES and b % BB == 0
    nc = t // L
    w = D_RWKV
    lora_blk = 3 * D_RWKV // LORA_W
    col = lambda o: (lambda bi, c: (bi, c, o))
    sh = lambda o: (lambda bi, c: (bi, 0, o))
    par = lambda o: (lambda bi, c: (0, o))
    in_specs = [
        pl.BlockSpec((BB, L, w), col(0)), pl.BlockSpec((BB, L, w), col(1)), pl.BlockSpec((BB, L, w), col(2)),
        pl.BlockSpec((BB, L, LORA_W), col(lora_blk)),
        pl.BlockSpec((BB, 1, w), sh(0)), pl.BlockSpec((BB, 1, w), sh(1)), pl.BlockSpec((BB, 1, w), sh(2)),
        pl.BlockSpec((BB, 1, LORA_W), sh(lora_blk)),
        pl.BlockSpec((BB, N_HEADS, HEAD_DIM, HEAD_DIM), lambda bi, c: (bi, 0, 0, 0)),
        pl.BlockSpec((1, w), par(0)), pl.BlockSpec((1, w), par(1)), pl.BlockSpec((1, w), par(2)),
        pl.BlockSpec((1, LORA_W), par(lora_blk)),
    ] + [pl.BlockSpec((1, w), par(0))] * 7 + [
        pl.BlockSpec((LANES, w), par(0)), pl.BlockSpec((LORA_A_WIN, w), par(0)),
        pl.BlockSpec((LORA_W - LORA_G_START, w), par(0)),
    ]
    out_specs = [pl.BlockSpec((BB, L, w), col(0)),
                 pl.BlockSpec((BB, N_HEADS, HEAD_DIM, HEAD_DIM), lambda bi, c: (bi, 0, 0, 0))]
    return pl.pallas_call(
        functools.partial(_rwkv_prompt_kernel, L=L, NC=nc, BB=BB, plan=plan),
        grid=(b // BB, nc),
        in_specs=in_specs,
        out_specs=out_specs,
        out_shape=[jax.ShapeDtypeStruct((b, t, D_RWKV), BF16),
                   jax.ShapeDtypeStruct((b, N_HEADS, HEAD_DIM, HEAD_DIM), F32)],
        scratch_shapes=[pltpu.VMEM((BB, D_RWKV // LANES, LANES, LANES), F32),
                        pltpu.VMEM((BB, 8, w), F32), pltpu.VMEM((BB, 8, w), F32),
                        pltpu.VMEM((BB, 8, w), F32), pltpu.VMEM((BB, 8, LORA_W), F32)],
        compiler_params=_cparams(("parallel", "arbitrary")),
        name="rwkv_mix",
    )(proj3, proj3, proj3, proj3, shift3, shift3, shift3, shift3, s0,
      p["mu"], p["mu"], p["mu"], p["mu"], p["w0"], p["a0"], p["k_k"], p["k_a"], p["r_k"],
      p["ln_w"], p["ln_b"], p["w_lora"], p["a_lora"], p["g_lora"])


def _rwkv_sample_prep_kernel(pr_ref, pk_ref, pv_ref, plo_ref, shr_ref, shk_ref, shv_ref, shl_ref,
                             mur_ref, muk_ref, muv_ref, mul_ref, w0_ref, a0_ref, kk_ref, ka_ref, rk_ref,
                             wl_ref, al_ref, gl_ref, *out_refs, nb, nt):
    W = pr_ref.shape[1]
    n_tiles = W // LANES
    rows = nb * nt
    seg_sum = functools.partial(_seg_sum, seg=_head_pair_mask().astype(BF16), rows=rows)

    def shifted(p_ref, s_ref, mu_ref):
        p = p_ref[...]
        prev = jnp.concatenate([s_ref[...], p[0:rows - nb]], axis=0)
        return p + mu_ref[...] * (prev - p)

    ps_l = shifted(plo_ref, shl_ref, mul_ref)
    td = _tanh(ps_l[:, 0:LANES]).astype(BF16)
    da = ps_l[:, 0:LORA_A_WIN].astype(BF16)
    sg = _sigmoid(ps_l[:, LORA_G_START:]).astype(BF16)
    r = shifted(pr_ref, shr_ref, mur_ref)
    k = shifted(pk_ref, shk_ref, muk_ref)
    v = shifted(pv_ref, shv_ref, muv_ref)
    zlog = w0_ref[...] + _dot(td, wl_ref[...])
    w = jnp.exp(-math.exp(-0.5) / (1.0 + jnp.exp(-zlog)))
    a_sig = _sigmoid(a0_ref[...] + _dot(da, al_ref[...]))
    kk = k * kk_ref[...]
    kk = kk * jnp.minimum(lax.rsqrt(jnp.maximum(seg_sum(kk * kk), 0.0)), 1e12)
    k_h = k * (1.0 + (a_sig - 1.0) * ka_ref[...])
    g = _dot(sg, gl_ref[...])
    bonus = seg_sum(r * k_h * rk_ref[...]) * v
    vals = dict(r=r, w=w, k=k_h, v=v, a=-kk, b=kk * a_sig, g=g, bonus=bonus)
    for name, o_ref in zip(PREP_OUT, out_refs):
        x = vals[name]
        for t in range(nt):
            for c in range(n_tiles):
                o_ref[t, c * LANES:(c + 1) * LANES, :] = x[t * nb:(t + 1) * nb, c * LANES:(c + 1) * LANES].T


def _rwkv_sample_prep(proj_r, shift, p, nb, nt):
    w = SAMPLE_PREP_W
    kb = D_RWKV // w
    lora_blk = 3 * D_RWKV // LORA_W
    rows = nt * nb
    col = lambda o: (lambda g: (0, o + g))
    in_specs = (
        [pl.BlockSpec((rows, w), col(0)), pl.BlockSpec((rows, w), col(kb)), pl.BlockSpec((rows, w), col(2 * kb)),
         pl.BlockSpec((rows, LORA_W), lambda g: (0, lora_blk))]
        + [pl.BlockSpec((nb, w), col(0)), pl.BlockSpec((nb, w), col(kb)), pl.BlockSpec((nb, w), col(2 * kb)),
           pl.BlockSpec((nb, LORA_W), lambda g: (0, lora_blk))]
        + [pl.BlockSpec((1, w), col(0)), pl.BlockSpec((1, w), col(kb)), pl.BlockSpec((1, w), col(2 * kb)),
           pl.BlockSpec((1, LORA_W), lambda g: (0, lora_blk))]
        + [pl.BlockSpec((1, w), col(0))] * 5
        + [pl.BlockSpec((LANES, w), col(0)), pl.BlockSpec((LORA_A_WIN, w), col(0)),
           pl.BlockSpec((LORA_W - LORA_G_START, w), col(0))])
    out_spec = pl.BlockSpec((nt, w, nb), lambda g: (0, g, 0))
    return pl.pallas_call(
        functools.partial(_rwkv_sample_prep_kernel, nb=nb, nt=nt),
        grid=(kb,),
        in_specs=in_specs,
        out_specs=[out_spec] * len(PREP_OUT),
        out_shape=[jax.ShapeDtypeStruct((nt, D_RWKV, nb), F32)] * len(PREP_OUT),
        compiler_params=_cparams(("parallel",)),
        name="rwkv_sample_prep",
    )(proj_r, proj_r, proj_r, proj_r, shift, shift, shift, shift, p["mu"], p["mu"], p["mu"], p["mu"],
      p["w0"], p["a0"], p["k_k"], p["k_a"], p["r_k"], p["w_lora"], p["a_lora"], p["g_lora"])


def _wkv_sample_kernel(s_ref, r_ref, w_ref, k_ref, v_ref, a_ref, b_ref, g_ref, bonus_ref, lnw_ref, lnb_ref,
                       so_ref, o_ref, y_ref, *, nt):
    H = HEAD_DIM
    SUB = 8
    PAR = 4
    rowid = lax.broadcasted_iota(jnp.int32, (SUB, LANES), 0)

    def body(i8, carry):
        base = pl.multiple_of(i8 * SUB, SUB)
        v8 = [v_ref[t, pl.ds(base, SUB), :] for t in range(nt)]
        y8 = [jnp.zeros((SUB, LANES), F32) for _ in range(nt)]
        for h0 in range(0, SUB, PAR):
            ids = list(range(h0, h0 + PAR))
            S = [s_ref[0, base + ii] for ii in ids]
            for t in range(nt):
                a, w, b, k, r = a_ref[t], w_ref[t], b_ref[t], k_ref[t], r_ref[t]
                for n, ii in enumerate(ids):
                    sa = jnp.sum(S[n] * a, axis=0, keepdims=True)
                    S[n] = S[n] * w + sa * b + v8[t][ii:ii + 1, :] * k
                    y = jnp.sum(S[n] * r, axis=0, keepdims=True)
                    y8[t] = jnp.where(rowid == ii, y, y8[t])
            for n, ii in enumerate(ids):
                so_ref[0, base + ii] = S[n]
        for t in range(nt):
            y_ref[t, pl.ds(base, SUB), :] = y8[t]
        return carry

    lax.fori_loop(0, H // SUB, body, 0)
    for t in range(nt):
        y = y_ref[t]
        mean = jnp.sum(y, axis=0, keepdims=True) * (1.0 / H)
        d = y - mean
        var = jnp.sum(d * d, axis=0, keepdims=True) * (1.0 / H)
        out = (d * lax.rsqrt(var + GN_EPS) * lnw_ref[...] + lnb_ref[...] + bonus_ref[t]) * g_ref[t]
        o_ref[t] = out.astype(o_ref.dtype)


def _wkv_sample(state_t, prep, lnw_b, lnb_b):
    nh, hd, _, nb = state_t.shape
    nt = prep[0].shape[0]
    st_spec = pl.BlockSpec((1, hd, hd, nb), lambda h: (h, 0, 0, 0))
    ch_spec = pl.BlockSpec((nt, hd, nb), lambda h: (0, h, 0))
    ln_spec = pl.BlockSpec((hd, nb), lambda h: (h, 0))
    return pl.pallas_call(
        functools.partial(_wkv_sample_kernel, nt=nt),
        grid=(nh,),
        in_specs=[st_spec] + [ch_spec] * len(PREP_OUT) + [ln_spec, ln_spec],
        out_specs=[st_spec, ch_spec],
        out_shape=[jax.ShapeDtypeStruct(state_t.shape, F32), jax.ShapeDtypeStruct((nt, nh * hd, nb), BF16)],
        scratch_shapes=[pltpu.VMEM((nt, hd, nb), F32)],
        compiler_params=_cparams(("parallel",)),
        name="wkv_sample",
    )(state_t, *prep, lnw_b, lnb_b)


def _swa_prompt_kernel(slope_ref, sink_ref, q_ref, kc_ref, kp_ref, vc_ref, vp_ref, o_ref, bias_ref):
    n = pl.program_id(1)
    blk = WINDOW
    H = HEAD_DIM

    @pl.when((pl.program_id(0) == 0) & (n == 0))
    def _():
        t = lax.broadcasted_iota(jnp.int32, (blk, 2 * blk), 0)
        j = lax.broadcasted_iota(jnp.int32, (blk, 2 * blk), 1)
        dist = t - j + blk
        band = (dist >= 0) & (dist <= WINDOW)
        first = band & (j >= blk)
        distf = dist.astype(F32)
        for h in range(N_HEADS):
            ab = -slope_ref[h] * distf
            bias_ref[0, h] = jnp.where(first, ab, -jnp.inf)
            bias_ref[1, h] = jnp.where(band, ab, -jnp.inf)

    sel = jnp.where(n == 0, 0, 1)
    low = lax.broadcasted_iota(jnp.int32, (blk, LANES), 1) < H
    scale = H ** -0.5
    tile = lambda i: slice(i * LANES, (i + 1) * LANES)

    def kv_group(KV):
        kslab = {hk: jnp.concatenate([kp_ref[0, :, tile(hk // 2)], kc_ref[0, :, tile(hk // 2)]],
                                     axis=0).astype(BF16) for hk in KV}
        vslab = {hk: jnp.concatenate([vp_ref[0, :, tile(hk // 2)], vc_ref[0, :, tile(hk // 2)]],
                                     axis=0).astype(BF16) for hk in KV}
        lhs = {}
        for hk in KV:
            parts = []
            for s2 in range(2):
                xs = q_ref[0, :, tile(2 * hk + s2)].astype(F32) * scale
                xr = pltpu.roll(xs, H, axis=1)
                if hk % 2 == 0:
                    parts += [jnp.where(low, xs, 0.0), jnp.where(low, xr, 0.0)]
                else:
                    parts += [jnp.where(low, 0.0, xr), jnp.where(low, 0.0, xs)]
            lhs[hk] = jnp.concatenate(parts, axis=0).astype(BF16)
        s = {hk: _dot_nt(lhs[hk], kslab[hk]) for hk in KV}
        p, rden = {}, {}
        for hk in KV:
            ps, rs = [], []
            for g in range(GQA_GROUP):
                h = hk * GQA_GROUP + g
                sg = s[hk][g * blk:(g + 1) * blk] + bias_ref[sel, h]
                m = jnp.maximum(jnp.max(sg, axis=-1, keepdims=True), sink_ref[h])
                e = jnp.exp(sg - m)
                rs.append(1.0 / (jnp.sum(e, axis=-1, keepdims=True) + jnp.exp(sink_ref[h] - m)))
                ps.append(e.astype(BF16))
            p[hk] = jnp.concatenate(ps, axis=0)
            rden[hk] = rs
        o = {hk: _dot(p[hk], vslab[hk]) for hk in KV}
        for hk in KV:
            for s2 in range(2):
                ga, gb = 2 * s2, 2 * s2 + 1
                oa = o[hk][ga * blk:(ga + 1) * blk] * rden[hk][ga]
                ob = o[hk][gb * blk:(gb + 1) * blk] * rden[hk][gb]
                if hk % 2 == 0:
                    out = jnp.where(low, oa, pltpu.roll(ob, H, axis=1))
                else:
                    out = jnp.where(low, pltpu.roll(oa, H, axis=1), ob)
                o_ref[0, :, tile(2 * hk + s2)] = out.astype(o_ref.dtype)

    for g0 in range(0, N_KV_HEADS, KV_PER_GROUP):
        kv_group(range(g0, g0 + KV_PER_GROUP))


def _swa_prompt(q3, kv3, slopes, sinks):
    b, t, _ = q3.shape
    nb = t // WINDOW
    smem = pl.BlockSpec(memory_space=pltpu.SMEM)
    prev = lambda n: jnp.maximum(n - 1, 0)
    return pl.pallas_call(
        _swa_prompt_kernel,
        grid=(b, nb),
        in_specs=[smem, smem,
                  pl.BlockSpec((1, WINDOW, D_MODEL), lambda bi, n: (bi, n, 0)),
                  pl.BlockSpec((1, WINDOW, D_KV), lambda bi, n: (bi, n, 0)),
                  pl.BlockSpec((1, WINDOW, D_KV), lambda bi, n: (bi, prev(n), 0)),
                  pl.BlockSpec((1, WINDOW, D_KV), lambda bi, n: (bi, n, 1)),
                  pl.BlockSpec((1, WINDOW, D_KV), lambda bi, n: (bi, prev(n), 1))],
        out_specs=pl.BlockSpec((1, WINDOW, D_MODEL), lambda bi, n: (bi, n, 0)),
        out_shape=jax.ShapeDtypeStruct((b, t, D_MODEL), BF16),
        scratch_shapes=[pltpu.VMEM((2, N_HEADS, WINDOW, 2 * WINDOW), F32)],
        compiler_params=_cparams(("arbitrary", "arbitrary")),
        name="swa_prompt",
    )(slopes, sinks, q3, kv3, kv3, kv3, kv3)


def _swa_sample_kernel(slope_ref, sink_ref, q_ref, kc_ref, vc_ref, knew_ref, vnew_ref,
                       o_ref, kwin_ref, vwin_ref, nbuf_ref, *, tq, BB, UNR):
    GT = GQA_GROUP * tq
    R = N_KV_HEADS * GT
    NP = 16
    C = D_KV
    row = lax.broadcasted_iota(jnp.int32, (R, WINDOW), 0)
    wcol = lax.broadcasted_iota(jnp.int32, (R, WINDOW), 1)
    t = lax.rem(row, tq)
    slope = slope_ref[...]
    sink = sink_ref[...]
    dist_o = WINDOW + t - wcol
    bias_old = jnp.where(dist_o <= WINDOW, -slope * dist_o.astype(F32), -jnp.inf)
    s_idx = wcol - (WINDOW - tq)
    dist_n = t - s_idx
    bias_new = jnp.where((s_idx >= 0) & (dist_n >= 0), -slope * dist_n.astype(F32), -jnp.inf)
    hkmask = (lax.broadcasted_iota(jnp.int32, (R, C), 0) // GT) == (lax.broadcasted_iota(jnp.int32, (R, C), 1) // HEAD_DIM)
    srow = lax.broadcasted_iota(jnp.int32, (NP, WINDOW), 0)
    scol = lax.broadcasted_iota(jnp.int32, (NP, WINDOW), 1)
    selw = ((scol == srow + (WINDOW - tq)) & (srow < tq)).astype(BF16)
    lane_new = lax.broadcasted_iota(jnp.int32, (C, WINDOW), 1) >= WINDOW - tq
    scale = HEAD_DIM ** -0.5
    for u in range(UNR):
        nbuf_ref[u, :, tq:NP, :] = jnp.zeros((2, NP - tq, C), F32)

    def transposed_new(x):
        return sum(_dot_tn(part, selw) for part in _split3(x))

    def body(i, carry):
        bs = [i * UNR + u for u in range(UNR)]
        U = range(UNR)
        for u in U:
            nbuf_ref[u, 0, 0:tq, :] = knew_ref[bs[u]]
            nbuf_ref[u, 1, 0:tq, :] = vnew_ref[bs[u]]
        kt = [kc_ref[b] for b in bs]
        vt = [vc_ref[b] for b in bs]
        knt = [transposed_new(nbuf_ref[u, 0]) for u in U]
        vnt = [transposed_new(nbuf_ref[u, 1]) for u in U]
        qbd = [jnp.where(hkmask, jnp.concatenate([q_ref[b] * scale] * N_KV_HEADS, axis=0), 0.0).astype(BF16)
               for b in bs]
        s_o = [_dot(qbd[u], kt[u].astype(BF16)) + bias_old for u in U]
        s_n = [_dot(qbd[u], knt[u].astype(BF16)) + bias_new for u in U]
        outs = []
        for u in U:
            m = jnp.maximum(jnp.maximum(jnp.max(s_o[u], axis=-1, keepdims=True),
                                        jnp.max(s_n[u], axis=-1, keepdims=True)), sink)
            p_o = jnp.exp(s_o[u] - m)
            p_n = jnp.exp(s_n[u] - m)
            rden = 1.0 / (jnp.sum(p_o, axis=-1, keepdims=True) + jnp.sum(p_n, axis=-1, keepdims=True)
                          + jnp.exp(sink - m))
            o = _dot_nt(p_o.astype(BF16), vt[u].astype(BF16)) + _dot_nt(p_n.astype(BF16), vnt[u].astype(BF16))
            o = jnp.where(hkmask, o * rden, 0.0)
            acc = o[0:GT]
            for hk in range(1, N_KV_HEADS):
                acc = acc + o[hk * GT:(hk + 1) * GT]
            outs.append(acc)
        for u in U:
            kwin_ref[bs[u]] = jnp.where(lane_new, knt[u], pltpu.roll(kt[u], WINDOW - tq, axis=1))
            vwin_ref[bs[u]] = jnp.where(lane_new, vnt[u], pltpu.roll(vt[u], WINDOW - tq, axis=1))
            o_ref[bs[u]] = outs[u].astype(o_ref.dtype)
        return carry

    lax.fori_loop(0, BB // UNR, body, 0)


def _swa_sample(q16, knew3, vnew3, kct, vct, slope_rows, sink_rows):
    b, gt, c = q16.shape
    tq = gt // GQA_GROUP
    bb = 8 if b % 8 == 0 else 1
    unr = 2 if bb % 2 == 0 else 1
    rows = N_KV_HEADS * gt
    blk3 = lambda shape: pl.BlockSpec(shape, lambda i: (i, 0, 0))
    full2 = pl.BlockSpec((rows, 1), lambda i: (0, 0))
    kern = functools.partial(_swa_sample_kernel, tq=tq, BB=bb, UNR=unr)
    return pl.pallas_call(
        kern,
        grid=(b // bb,),
        in_specs=[full2, full2, blk3((bb, gt, c)), blk3((bb, c, WINDOW)), blk3((bb, c, WINDOW)),
                  blk3((bb, tq, c)), blk3((bb, tq, c))],
        out_specs=[blk3((bb, gt, c)), blk3((bb, c, WINDOW)), blk3((bb, c, WINDOW))],
        out_shape=[jax.ShapeDtypeStruct((b, gt, c), BF16),
                   jax.ShapeDtypeStruct((b, c, WINDOW), F32),
                   jax.ShapeDtypeStruct((b, c, WINDOW), F32)],
        scratch_shapes=[pltpu.VMEM((unr, 2, 16, c), F32)],
        compiler_params=_cparams(("parallel",)),
        name="swa_sample",
    )(slope_rows, sink_rows, q16, kct, vct, knew3, vnew3)


def _merge_out_kernel(x_ref, ya_ref, yb_ref, ga_ref, gb_ref, wo_ref, nw_ref, h_ref, hn_ref):
    f32 = lambda ref: ref[...].astype(F32)
    mixed = _sigmoid(f32(ga_ref)) * f32(ya_ref) + _sigmoid(f32(gb_ref)) * f32(yb_ref)
    h = x_ref[...] + _dot(mixed.astype(BF16), wo_ref[...])
    h_ref[...] = h
    ms = jnp.mean(h * h, axis=-1, keepdims=True)
    hn_ref[...] = (h * lax.rsqrt(ms + RMS_EPS) * nw_ref[...]).astype(BF16)


def _merge_out(x, ya, yb, gates, w_out, nw, tm):
    m, d = x.shape
    row = lambda o: (lambda i: (i, o))
    return pl.pallas_call(
        _merge_out_kernel,
        grid=(m // tm,),
        in_specs=[pl.BlockSpec((tm, d), row(0)), pl.BlockSpec((tm, d), row(0)), pl.BlockSpec((tm, d), row(0)),
                  pl.BlockSpec((tm, d), row(0)), pl.BlockSpec((tm, d), row(1)),
                  pl.BlockSpec((d, d), lambda i: (0, 0)), pl.BlockSpec((1, d), lambda i: (0, 0))],
        out_specs=[pl.BlockSpec((tm, d), row(0)), pl.BlockSpec((tm, d), row(0))],
        out_shape=[jax.ShapeDtypeStruct((m, d), F32), jax.ShapeDtypeStruct((m, d), BF16)],
        compiler_params=_cparams(("parallel",)),
        name="merge_out_proj",
    )(x, ya, yb, gates, gates, w_out, nw)


def _mlp_kernel(hn_ref, h_ref, wu_ref, wd_ref, nw_ref, o_ref, acc_ref):
    j = pl.program_id(1)

    @pl.when(j == 0)
    def _():
        acc_ref[...] = jnp.zeros_like(acc_ref)

    u = jnp.maximum(_dot(hn_ref[...], wu_ref[...]), 0.0)
    acc_ref[...] += _dot((u * u).astype(BF16), wd_ref[...])

    @pl.when(j == pl.num_programs(1) - 1)
    def _():
        h = h_ref[...] + acc_ref[...]
        ms = jnp.mean(h * h, axis=-1, keepdims=True)
        o_ref[...] = h * lax.rsqrt(ms + RMS_EPS) * nw_ref[...]


def _mlp(hn, h, w_up, w_down, nw, tm, tf):
    m, d = h.shape
    f = w_up.shape[1]
    return pl.pallas_call(
        _mlp_kernel,
        grid=(m // tm, f // tf),
        in_specs=[pl.BlockSpec((tm, d), lambda i, j: (i, 0)), pl.BlockSpec((tm, d), lambda i, j: (i, 0)),
                  pl.BlockSpec((d, tf), lambda i, j: (0, j)), pl.BlockSpec((tf, d), lambda i, j: (j, 0)),
                  pl.BlockSpec((1, d), lambda i, j: (0, 0))],
        out_specs=pl.BlockSpec((tm, d), lambda i, j: (i, 0)),
        out_shape=jax.ShapeDtypeStruct((m, d), F32),
        scratch_shapes=[pltpu.VMEM((tm, d), F32)],
        compiler_params=_cparams(("parallel", "arbitrary")),
        name="mlp_final_norm",
    )(hn, h, w_up, w_down, nw)


def _pick(m, prefs):
    for t in prefs:
        if m % t == 0:
            return t
    return m


def _pad_cols(v, n):
    return jnp.concatenate([v, jnp.zeros(v.shape[:-1] + (n - v.shape[-1],), v.dtype)], axis=-1)


def _place_rows(w, start, rows):
    n, d = w.shape
    return jnp.concatenate([jnp.zeros((start, d), w.dtype), w, jnp.zeros((rows - start - n, d), w.dtype)], axis=0)


def _layer(x, shift_prev, wkv0, lw, *, prompt, k_cache_t=None, v_cache_t=None):
    b, t, d = x.shape
    m = b * t
    x2 = x.reshape(m, d) if prompt else x.transpose(1, 0, 2).reshape(m, d)
    proj_r, proj_q, proj_kv, proj_g = _in_proj(
        x2, lw["norm_mix_w"], lw["w_t"], IN_GROUPS, [F32, BF16, F32, BF16], _pick(m, (1024, 512, 256, 128, 8)))
    shift_pad = _pad_cols(shift_prev, R_PAD)

    if prompt:
        proj_r3 = proj_r.reshape(b, t, R_PAD)
        kv3 = proj_kv.reshape(b, t, 2 * D_KV)
        ya, wkv_new = _rwkv_prompt(proj_r3, shift_pad[:, None], wkv0, lw, _pick(b, (4, 2, 1)), RWKV_SKEW_PLAN)
        yb = _swa_prompt(proj_q.reshape(b, t, D_MODEL), kv3, lw["slopes"], lw["sinks"])
        k_win = kv3[:, t - WINDOW:, :D_KV].reshape(b, WINDOW, N_KV_HEADS, HEAD_DIM)
        v_win = kv3[:, t - WINDOW:, D_KV:].reshape(b, WINDOW, N_KV_HEADS, HEAD_DIM)
        shift_new = proj_r3[:, t - 1, :R_COLS]
    else:
        prep = _rwkv_sample_prep(proj_r, shift_pad, lw, b, t)
        wkv_t, ya_t = _wkv_sample(wkv0.transpose(1, 2, 3, 0), prep,
                                  jnp.broadcast_to(lw["ln_w"].reshape(D_RWKV, 1), (D_RWKV, b)),
                                  jnp.broadcast_to(lw["ln_b"].reshape(D_RWKV, 1), (D_RWKV, b)))
        ya = ya_t.transpose(0, 2, 1)
        wkv_new = wkv_t.transpose(3, 0, 1, 2)
        q16 = proj_q.reshape(t, b, N_KV_HEADS, GQA_GROUP, HEAD_DIM).transpose(1, 3, 0, 2, 4)
        q16 = q16.reshape(b, GQA_GROUP * t, D_KV)
        kv_bt = proj_kv.reshape(t, b, 2 * D_KV).transpose(1, 0, 2)
        gt_head = (jnp.arange(N_KV_HEADS)[:, None] * GQA_GROUP + jnp.arange(GQA_GROUP)[None, :])
        row_head = jnp.repeat(gt_head, t, axis=1).reshape(-1)
        o16, kwt, vwt = _swa_sample(q16, kv_bt[:, :, :D_KV], kv_bt[:, :, D_KV:], k_cache_t, v_cache_t,
                                    lw["slopes"][row_head][:, None], lw["sinks"][row_head][:, None])
        yb = o16.reshape(b, GQA_GROUP, t, N_KV_HEADS, HEAD_DIM).transpose(2, 0, 3, 1, 4)
        k_win = kwt.reshape(b, N_KV_HEADS, HEAD_DIM, WINDOW).transpose(0, 3, 1, 2)
        v_win = vwt.reshape(b, N_KV_HEADS, HEAD_DIM, WINDOW).transpose(0, 3, 1, 2)
        shift_new = proj_r[(t - 1) * b:, :R_COLS]

    h, hn = _merge_out(x2, ya.reshape(m, d), yb.reshape(m, d), proj_g, lw["w_out"], lw["norm_mlp_w"],
                       _pick(m, (512, 256, 128, 8)))
    y = _mlp(hn, h, lw["w_up"], lw["w_down"], lw["norm_final_w"], _pick(m, (512, 256, 128, 8)), 1024)
    y = y.reshape(b, t, d) if prompt else y.reshape(t, b, d).transpose(1, 0, 2)
    return y, shift_new, wkv_new, k_win, v_win


def kernel(x_prompt, x_sample, state_shift, state_wkv, cache_k_win, cache_v_win, norm_mix_w, w_in, tshift_mu, w0, w_lora, a0, a_lora, g_lora, k_k, k_a, r_k, ln_x_w, ln_x_b, attn_sinks, w_out, norm_mlp_w, w_up, w_down, norm_final_w):
    depth = w_in.shape[0]
    assert depth == 1
    l = 0
    bp = x_prompt.shape[0]
    db = x_sample.shape[0]
    hh = jnp.arange(N_HEADS, dtype=F32)
    lw = dict(
        norm_mix_w=norm_mix_w[l][None],
        w_t=jnp.swapaxes(w_in[l], 0, 1).astype(BF16),
        mu=_pad_cols(tshift_mu[l][None], R_PAD), w0=w0[l][None], a0=a0[l][None], k_k=k_k[l][None],
        k_a=k_a[l][None], r_k=r_k[l].reshape(1, D_RWKV), ln_w=ln_x_w[l][None], ln_b=ln_x_b[l][None],
        w_lora=_place_rows(w_lora[l], 0, LANES).astype(BF16),
        a_lora=_place_rows(a_lora[l], LORA_DECAY_END, LORA_A_WIN).astype(BF16),
        g_lora=_place_rows(g_lora[l], LORA_A_END - LORA_G_START, LORA_W - LORA_G_START).astype(BF16),
        slopes=jnp.exp2(-8.0 * (hh + 1.0) / N_HEADS), sinks=attn_sinks[l].astype(F32),
        w_out=w_out[l].astype(BF16), norm_mlp_w=norm_mlp_w[l][None],
        w_up=w_up[l].astype(BF16), w_down=w_down[l].astype(BF16), norm_final_w=norm_final_w[None],
    )
    yp, sp, wp, kp, vp = _layer(
        x_prompt, jnp.zeros((bp, R_COLS), F32), jnp.zeros((bp, N_HEADS, HEAD_DIM, HEAD_DIM), F32), lw, prompt=True)
    kct = cache_k_win[l].transpose(0, 2, 3, 1).reshape(db, D_KV, WINDOW)
    vct = cache_v_win[l].transpose(0, 2, 3, 1).reshape(db, D_KV, WINDOW)
    ys, ss, ws, ksm, vsm = _layer(x_sample, state_shift[l], state_wkv[l], lw, prompt=False,
                                  k_cache_t=kct, v_cache_t=vct)
    return (yp, ys, sp[None], wp[None], kp[None], vp[None], ss[None], ws[None], ksm[None], vsm[None])
```

```python
import functools
import math

import jax
import jax.numpy as jnp
from jax import lax
from jax.experimental import pallas as pl
from jax.experimental.pallas import tpu as pltpu

F32 = jnp.float32
BF16 = jnp.bfloat16

D_MODEL = 2048
HEAD_DIM = 64
N_HEADS = D_MODEL // HEAD_DIM
N_KV_HEADS = 8
GQA_GROUP = N_HEADS // N_KV_HEADS
D_KV = N_KV_HEADS * HEAD_DIM
WINDOW = 128
D_FF = 4 * D_MODEL
D_DECAY_LORA = 96
D_A_LORA = 96
D_GATE_LORA = 256
D_RWKV = D_MODEL
R_COLS = 3 * D_RWKV + D_DECAY_LORA + D_A_LORA + D_GATE_LORA
C_IN = R_COLS + D_MODEL + 2 * D_KV + 2 * D_MODEL
RMS_EPS = 1e-5
GN_EPS = 64e-5

LANES = 128
PROJ_TILE = 512
CAST_STEPS = 128
R_PAD = -(-R_COLS // PROJ_TILE) * PROJ_TILE
LORA_W = R_PAD - 3 * D_RWKV
LORA_DECAY_END = D_DECAY_LORA
LORA_A_END = D_DECAY_LORA + D_A_LORA
LORA_A_WIN = -(-LORA_A_END // LANES) * LANES
LORA_G_START = (LORA_A_END // LANES) * LANES
IN_GROUPS = ((0, R_PAD), (R_COLS, D_MODEL), (R_COLS + D_MODEL, 2 * D_KV), (R_COLS + D_MODEL + 2 * D_KV, 2 * D_MODEL))

VMEM_LIMIT = 56 * 1024 * 1024
RWKV_CHUNK = 64
RWKV_SKEW_PLAN = (1, 1, 4, 4, 4, 4, 1, 4, 4, 4, 4)
KV_PER_GROUP = 2
SAMPLE_PREP_W = 512
PREP_OUT = ("r", "w", "k", "v", "a", "b", "g", "bonus")


def _cparams(sem):
    return pltpu.CompilerParams(dimension_semantics=sem, vmem_limit_bytes=VMEM_LIMIT)


def _dot(a, b):
    return jnp.dot(a, b, preferred_element_type=F32)


def _dot_nt(a, b):
    return lax.dot_general(a, b, (((1,), (1,)), ((), ())), preferred_element_type=F32)


def _dot_tn(a, b):
    return lax.dot_general(a, b, (((0,), (0,)), ((), ())), preferred_element_type=F32)


def _sigmoid(x):
    return 1.0 / (1.0 + jnp.exp(-x))


def _tanh(x):
    return 1.0 - 2.0 / (1.0 + jnp.exp(2.0 * x))


def _split2(x):
    hi = x.astype(BF16)
    lo = (x - hi.astype(F32)).astype(BF16)
    return hi, lo


def _split3(x):
    hi = x.astype(BF16)
    r1 = x - hi.astype(F32)
    mid = r1.astype(BF16)
    lo = (r1 - mid.astype(F32)).astype(BF16)
    return hi, mid, lo


def _head_pair_mask():
    ji = lax.broadcasted_iota(jnp.int32, (LANES, LANES), 0)
    jj = lax.broadcasted_iota(jnp.int32, (LANES, LANES), 1)
    return (ji < HEAD_DIM) == (jj < HEAD_DIM)


def _seg_sum(x, seg, rows):
    n = x.shape[1] // LANES
    xs = jnp.concatenate([x[:, i * LANES:(i + 1) * LANES] for i in range(n)], axis=0).astype(BF16)
    ys = _dot(xs, seg)
    return jnp.concatenate([ys[i * rows:(i + 1) * rows] for i in range(n)], axis=1)


def _in_proj_kernel(x_ref, nw_ref, *refs, bounds, n_cast, cast_steps):
    n = len(bounds)
    w_refs, ci_refs = refs[:n], refs[n:n + n_cast]
    o_refs, co_refs = refs[n + n_cast:2 * n + n_cast], refs[2 * n + n_cast:2 * n + 2 * n_cast]
    xn_ref = refs[2 * n + 2 * n_cast]
    i = pl.program_id(0)
    j = pl.program_id(1)

    if n_cast:
        @pl.when(i * pl.num_programs(1) + j < cast_steps)
        def _():
            for ci_ref, co_ref in zip(ci_refs, co_refs):
                co_ref[...] = ci_ref[...].astype(co_ref.dtype)

    @pl.when(j == 0)
    def _():
        x = x_ref[...]
        ms = jnp.mean(x * x, axis=-1, keepdims=True)
        xn_ref[...] = (x * lax.rsqrt(ms + RMS_EPS) * nw_ref[...]).astype(BF16)

    for w_ref, o_ref, (lo, hi) in zip(w_refs, o_refs, bounds):
        @pl.when((j >= lo) & (j < hi))
        def _(w_ref=w_ref, o_ref=o_ref):
            o_ref[...] = _dot_nt(xn_ref[...], w_ref[...]).astype(o_ref.dtype)


def _in_proj(x, nw, w_t, groups, out_dtypes, tm, cast=()):
    m, d = x.shape
    tn = PROJ_TILE
    bounds, lo = [], 0
    for _, width in groups:
        bounds.append((lo, lo + width // tn))
        lo = bounds[-1][1]
    n_j = lo
    ROW_ALIGN = 16
    assert all(start % ROW_ALIGN == 0 and start + width <= w_t.shape[0] for start, width in groups)
    cast_steps = 0
    if cast:
        cast_steps = CAST_STEPS
        while cast_steps > (m // tm) * n_j:
            cast_steps //= 2
        assert all(c.shape[0] % (cast_steps * ROW_ALIGN) == 0 for c in cast)

    def clamp(lo_, hi_):
        return lambda j: jnp.clip(j - lo_, 0, hi_ - lo_ - 1)

    cast_blk = lambda i, j: (jnp.minimum(i * n_j + j, cast_steps - 1), 0)
    in_specs = [pl.BlockSpec((tm, d), lambda i, j: (i, 0)), pl.BlockSpec((1, d), lambda i, j: (0, 0))]
    out_specs, out_shape = [], []
    for (start, width), dt, (lo_, hi_) in zip(groups, out_dtypes, bounds):
        c = clamp(lo_, hi_)
        in_specs.append(pl.BlockSpec((pl.Element(tn), pl.Element(d)),
                                     lambda i, j, c=c, start=start: (pl.multiple_of(start + c(j) * tn, ROW_ALIGN), 0)))
        out_specs.append(pl.BlockSpec((tm, tn), lambda i, j, c=c: (i, c(j))))
        out_shape.append(jax.ShapeDtypeStruct((m, width), dt))
    for c in cast:
        blk = (c.shape[0] // cast_steps, c.shape[1])
        in_specs.append(pl.BlockSpec(blk, cast_blk))
        out_specs.append(pl.BlockSpec(blk, cast_blk))
        out_shape.append(jax.ShapeDtypeStruct(c.shape, BF16))
    outs = pl.pallas_call(
        functools.partial(_in_proj_kernel, bounds=tuple(bounds), n_cast=len(cast), cast_steps=cast_steps),
        grid=(m // tm, n_j),
        in_specs=in_specs,
        out_specs=out_specs,
        out_shape=out_shape,
        scratch_shapes=[pltpu.VMEM((tm, d), BF16)],
        compiler_params=_cparams(("parallel", "arbitrary")),
        name="norm_in_proj",
    )(x, nw, *([w_t] * len(groups)), *cast)
    return outs[:len(groups)], outs[len(groups):]


def _rwkv_prompt_kernel(pr_ref, pk_ref, pv_ref, plo_ref, shr_ref, shk_ref, shv_ref, shl_ref, s0_ref,
                         mur_ref, muk_ref, muv_ref, mul_ref, w0_ref, a0_ref, kk_ref, ka_ref, rk_ref, lnw_ref, lnb_ref,
                         wl_ref, al_ref, gl_ref,
                         y_ref, so_ref,
                         sd_ref, br_ref, bk_ref, bv_ref, bl_ref, *, L, NC, BB, plan):
    c = pl.program_id(1)
    H = HEAD_DIM
    NP = D_RWKV // LANES
    P = range(NP)
    sls = [slice(i * LANES, (i + 1) * LANES) for i in P]
    n_dbl = max(1, math.ceil(math.log2(L)))

    head1 = lax.broadcasted_iota(jnp.int32, (L, LANES), 1) < H
    row_i = lax.broadcasted_iota(jnp.int32, (L, 2 * L), 0)
    col_i = lax.broadcasted_iota(jnp.int32, (L, 2 * L), 1)
    col_t = jnp.where(col_i >= L, col_i - L, col_i)
    strict = col_t < row_i
    incl = col_t <= row_i
    colh1 = col_i < L
    tri = (lax.broadcasted_iota(jnp.int32, (L, L), 1) <= lax.broadcasted_iota(jnp.int32, (L, L), 0)).astype(BF16)
    same_head = _head_pair_mask()
    seg = same_head.astype(BF16)
    seg_sum = functools.partial(_seg_sum, seg=seg, rows=L)
    first_row = lax.broadcasted_iota(jnp.int32, (L, 1), 0) == 0

    def stack_heads(x, m):
        zero = jnp.zeros_like(x)
        return jnp.concatenate([jnp.where(m, x, zero), jnp.where(m, zero, x)], axis=0)

    @pl.when(c == 0)
    def _init():
        z = jnp.zeros((H, H), F32)
        for bi in range(BB):
            br_ref[bi, 0:1, :] = shr_ref[bi]
            bk_ref[bi, 0:1, :] = shk_ref[bi]
            bv_ref[bi, 0:1, :] = shv_ref[bi]
            bl_ref[bi, 0:1, :] = shl_ref[bi]
            for i in P:
                top = jnp.concatenate([s0_ref[bi, 2 * i], z], axis=1)
                bot = jnp.concatenate([z, s0_ref[bi, 2 * i + 1]], axis=1)
                sd_ref[bi, i] = jnp.concatenate([top, bot], axis=0)

    def prologue_items(bi, T):
        S = {}

        def shifted(p_ref, carry_ref, mu_ref):
            p = p_ref[bi]
            prev = jnp.where(first_row, carry_ref[bi, 0:1, :], pltpu.roll(p, 1, axis=0))
            return p + mu_ref[...] * (prev - p)

        def load_shift():
            ps_l = shifted(plo_ref, bl_ref, mul_ref)
            S["td"] = _tanh(ps_l[:, 0:LANES]).astype(BF16)
            S["da"] = ps_l[:, 0:LORA_A_WIN].astype(BF16)
            S["sg"] = _sigmoid(ps_l[:, LORA_G_START:]).astype(BF16)
            S["r"] = shifted(pr_ref, br_ref, mur_ref)
            S["k"] = shifted(pk_ref, bk_ref, muk_ref)
            S["v"] = shifted(pv_ref, bv_ref, muv_ref)
            S["kk"] = S["k"] * kk_ref[...]

        def matmuls_1():
            S["zlog"] = w0_ref[...] + _dot(S["td"], wl_ref[...])
            S["apre"] = a0_ref[...] + _dot(S["da"], al_ref[...])
            S["g"] = _dot(S["sg"], gl_ref[...])
            S["n2"] = seg_sum(S["kk"] * S["kk"])

        def vector_1(i):
            s = sls[i]
            logw = -math.exp(-0.5) / (1.0 + jnp.exp(-S["zlog"][:, s]))
            a_sig = _sigmoid(S["apre"][:, s])
            kk = S["kk"][:, s] * jnp.minimum(lax.rsqrt(jnp.maximum(S["n2"][:, s], 0.0)), 1e12)
            k_h = S["k"][:, s] * (1.0 + (a_sig - 1.0) * ka_ref[:, s])
            hi, lo = _split2(logw)
            S[("v1", i)] = dict(logw=logw, bvec=kk * a_sig, kk=kk, k_h=k_h, hi=hi, lo=lo,
                                prod=(S["r"][:, s] * k_h * rk_ref[:, s]).astype(BF16))

        def matmuls_2():
            cat = lambda n: jnp.concatenate([S[("v1", i)][n] for i in P], axis=1)
            S["cum"] = _dot(tri, cat("hi")) + _dot(tri, cat("lo"))
            S["bsum"] = _dot(jnp.concatenate([S[("v1", i)]["prod"] for i in P], axis=0), seg)

        def vector_2(i):
            s = sls[i]
            t1 = S[("v1", i)]
            cum = S["cum"][:, s]
            cum_l = cum[L - 1:L, :]
            p_inv = jnp.exp(-cum)
            p_end = jnp.exp(cum_l - cum)
            v = S["v"][:, s]
            T[i] = dict(
                a_t=(-t1["kk"] * jnp.exp(cum - t1["logw"])).astype(BF16),
                r_t=(S["r"][:, s] * jnp.exp(cum)).astype(BF16),
                b_t=(t1["bvec"] * p_inv).astype(BF16), k_t=(t1["k_h"] * p_inv).astype(BF16),
                b_e=(t1["bvec"] * p_end).astype(BF16), k_e=(t1["k_h"] * p_end).astype(BF16),
                v_b=v.astype(BF16), g=S["g"][:, s], bonus=S["bsum"][i * L:(i + 1) * L] * v, p_l=jnp.exp(cum_l))

        return ([load_shift, matmuls_1] + [functools.partial(vector_1, i) for i in P] + [matmuls_2]
                + [functools.partial(vector_2, i) for i in P])

    def chain_stages(bi, T, sd, res):
        lhs = [jnp.concatenate([T[i]["a_t"], T[i]["r_t"]], axis=0) for i in P]
        rhs = [jnp.concatenate([stack_heads(T[i]["b_t"], head1), stack_heads(T[i]["k_t"], head1)], axis=0) for i in P]
        aa = [_dot_nt(lhs[i], rhs[i]) for i in P]
        yield
        sa = [_dot_nt(lhs[i], sd[i].astype(BF16)) for i in P]
        yield
        v_st = [stack_heads(T[i]["v_b"], head1) for i in P]
        a_ak = [jnp.where(strict, aa[i][0:L, 2 * L:4 * L], 0.0).astype(BF16) for i in P]
        x = [sa[i][0:L] + _dot(a_ak[i], v_st[i]) for i in P]
        ap = [jnp.where(strict, aa[i][0:L, 0:2 * L], 0.0).astype(BF16) for i in P]
        yield
        for d in range(n_dbl):
            xs = [stack_heads(x[i].astype(BF16), head1) for i in P]
            if d + 1 < n_dbl:
                both = [_dot(ap[i], jnp.concatenate([xs[i], stack_heads(ap[i], colh1)], axis=1)) for i in P]
                x = [x[i] + both[i][:, 0:2 * L] for i in P]
                ap = [both[i][:, 2 * L:].astype(BF16) for i in P]
            else:
                x = [x[i] + _dot(ap[i], xs[i]) for i in P]
            yield
        u_b = [x[i].astype(BF16) for i in P]
        a_rb = [jnp.where(incl, aa[i][L:2 * L, 0:2 * L], 0.0).astype(BF16) for i in P]
        a_rk = [jnp.where(incl, aa[i][L:2 * L, 2 * L:4 * L], 0.0).astype(BF16) for i in P]
        y = [sa[i][L:2 * L] + _dot(jnp.concatenate([a_rb[i], a_rk[i]], axis=1),
                                   jnp.concatenate([stack_heads(u_b[i], head1), v_st[i]], axis=0)) for i in P]
        yield
        ds = [_dot_tn(jnp.concatenate([u_b[i], T[i]["v_b"]], axis=0),
                      jnp.concatenate([T[i]["b_e"], T[i]["k_e"]], axis=0)) for i in P]
        res["sd"] = [sd[i] * T[i]["p_l"] + jnp.where(same_head, ds[i], 0.0) for i in P]
        yield
        y = jnp.concatenate(y, axis=1)
        mean = seg_sum(y) * (1.0 / H)
        yield
        dlt = y - mean
        var = seg_sum(dlt * dlt) * (1.0 / H)
        yield
        yn = dlt * lax.rsqrt(var + GN_EPS) * lnw_ref[...] + lnb_ref[...]
        g = jnp.concatenate([T[i]["g"] for i in P], axis=1)
        bonus = jnp.concatenate([T[i]["bonus"] for i in P], axis=1)
        res["out"] = (yn + bonus) * g

    sds = [[sd_ref[bi, i] for i in P] for bi in range(BB)]
    Ts = [[None] * NP for _ in range(BB)]
    results = [{} for _ in range(BB)]
    for item in prologue_items(0, Ts[0]):
        item()
    for bi in range(BB):
        items = prologue_items(bi + 1, Ts[bi + 1]) if bi + 1 < BB else []
        quota = iter(plan)
        for _ in chain_stages(bi, Ts[bi], sds[bi], results[bi]):
            for _ in range(next(quota, 0)):
                if items:
                    items.pop(0)()
        while items:
            items.pop(0)()

    for bi in range(BB):
        for i in P:
            sd_ref[bi, i] = results[bi]["sd"][i]
        y_ref[bi] = results[bi]["out"].astype(y_ref.dtype)
        br_ref[bi, 0:1, :] = pr_ref[bi, L - 1:L, :]
        bk_ref[bi, 0:1, :] = pk_ref[bi, L - 1:L, :]
        bv_ref[bi, 0:1, :] = pv_ref[bi, L - 1:L, :]
        bl_ref[bi, 0:1, :] = plo_ref[bi, L - 1:L, :]

    @pl.when(c == NC - 1)
    def _finish():
        for bi in range(BB):
            for i in P:
                s = sd_ref[bi, i]
                so_ref[bi, 2 * i] = s[0:H, 0:H]
                so_ref[bi, 2 * i + 1] = s[H:2 * H, H:2 * H]


def _rwkv_prompt(proj3, shift3, s0, p, BB, plan):
    b, t, _ = proj3.shape
    L = RWKV_CHUNK
    assert t % L == 0 and 2 * L == LANES and b % BB == 0
    nc = t // L
    w = D_RWKV
    lora_blk = 3 * D_RWKV // LORA_W
    col = lambda o: (lambda bi, c: (bi, c, o))
    sh = lambda o: (lambda bi, c: (bi, 0, o))
    par = lambda o: (lambda bi, c: (0, o))
    in_specs = [
        pl.BlockSpec((BB, L, w), col(0)), pl.BlockSpec((BB, L, w), col(1)), pl.BlockSpec((BB, L, w), col(2)),
        pl.BlockSpec((BB, L, LORA_W), col(lora_blk)),
        pl.BlockSpec((BB, 1, w), sh(0)), pl.BlockSpec((BB, 1, w), sh(1)), pl.BlockSpec((BB, 1, w), sh(2)),
        pl.BlockSpec((BB, 1, LORA_W), sh(lora_blk)),
        pl.BlockSpec((BB, N_HEADS, HEAD_DIM, HEAD_DIM), lambda bi, c: (bi, 0, 0, 0)),
        pl.BlockSpec((1, w), par(0)), pl.BlockSpec((1, w), par(1)), pl.BlockSpec((1, w), par(2)),
        pl.BlockSpec((1, LORA_W), par(lora_blk)),
    ] + [pl.BlockSpec((1, w), par(0))] * 7 + [
        pl.BlockSpec((LANES, w), par(0)), pl.BlockSpec((LORA_A_WIN, w), par(0)),
        pl.BlockSpec((LORA_W - LORA_G_START, w), par(0)),
    ]
    out_specs = [pl.BlockSpec((BB, L, w), col(0)),
                 pl.BlockSpec((BB, N_HEADS, HEAD_DIM, HEAD_DIM), lambda bi, c: (bi, 0, 0, 0))]
    return pl.pallas_call(
        functools.partial(_rwkv_prompt_kernel, L=L, NC=nc, BB=BB, plan=plan),
        grid=(b // BB, nc),
        in_specs=in_specs,
        out_specs=out_specs,
        out_shape=[jax.ShapeDtypeStruct((b, t, D_RWKV), BF16),
                   jax.ShapeDtypeStruct((b, N_HEADS, HEAD_DIM, HEAD_DIM), F32)],
        scratch_shapes=[pltpu.VMEM((BB, D_RWKV // LANES, LANES, LANES), F32),
                        pltpu.VMEM((BB, 8, w), F32), pltpu.VMEM((BB, 8, w), F32),
                        pltpu.VMEM((BB, 8, w), F32), pltpu.VMEM((BB, 8, LORA_W), F32)],
        compiler_params=_cparams(("parallel", "arbitrary")),
        name="rwkv_mix",
    )(proj3, proj3, proj3, proj3, shift3, shift3, shift3, shift3, s0,
      p["mu"], p["mu"], p["mu"], p["mu"], p["w0"], p["a0"], p["k_k"], p["k_a"], p["r_k"],
      p["ln_w"], p["ln_b"], p["w_lora"], p["a_lora"], p["g_lora"])


def _rwkv_sample_prep_kernel(pr_ref, pk_ref, pv_ref, plo_ref, shr_ref, shk_ref, shv_ref, shl_ref,
                             mur_ref, muk_ref, muv_ref, mul_ref, w0_ref, a0_ref, kk_ref, ka_ref, rk_ref,
                             wl_ref, al_ref, gl_ref, *out_refs, nb, nt):
    W = pr_ref.shape[1]
    n_tiles = W // LANES
    rows = nb * nt
    seg_sum = functools.partial(_seg_sum, seg=_head_pair_mask().astype(BF16), rows=rows)

    def shifted(p_ref, s_ref, mu_ref):
        p = p_ref[...]
        prev = jnp.concatenate([s_ref[...], p[0:rows - nb]], axis=0)
        return p + mu_ref[...] * (prev - p)

    ps_l = shifted(plo_ref, shl_ref, mul_ref)
    td = _tanh(ps_l[:, 0:LANES]).astype(BF16)
    da = ps_l[:, 0:LORA_A_WIN].astype(BF16)
    sg = _sigmoid(ps_l[:, LORA_G_START:]).astype(BF16)
    r = shifted(pr_ref, shr_ref, mur_ref)
    k = shifted(pk_ref, shk_ref, muk_ref)
    v = shifted(pv_ref, shv_ref, muv_ref)
    zlog = w0_ref[...] + _dot(td, wl_ref[...])
    w = jnp.exp(-math.exp(-0.5) / (1.0 + jnp.exp(-zlog)))
    a_sig = _sigmoid(a0_ref[...] + _dot(da, al_ref[...]))
    kk = k * kk_ref[...]
    kk = kk * jnp.minimum(lax.rsqrt(jnp.maximum(seg_sum(kk * kk), 0.0)), 1e12)
    k_h = k * (1.0 + (a_sig - 1.0) * ka_ref[...])
    g = _dot(sg, gl_ref[...])
    bonus = seg_sum(r * k_h * rk_ref[...]) * v
    vals = dict(r=r, w=w, k=k_h, v=v, a=-kk, b=kk * a_sig, g=g, bonus=bonus)
    for name, o_ref in zip(PREP_OUT, out_refs):
        x = vals[name]
        for t in range(nt):
            for c in range(n_tiles):
                o_ref[t, c * LANES:(c + 1) * LANES, :] = x[t * nb:(t + 1) * nb, c * LANES:(c + 1) * LANES].T


def _rwkv_sample_prep(proj_r, shift, p, nb, nt):
    w = SAMPLE_PREP_W
    kb = D_RWKV // w
    lora_blk = 3 * D_RWKV // LORA_W
    rows = nt * nb
    col = lambda o: (lambda g: (0, o + g))
    in_specs = (
        [pl.BlockSpec((rows, w), col(0)), pl.BlockSpec((rows, w), col(kb)), pl.BlockSpec((rows, w), col(2 * kb)),
         pl.BlockSpec((rows, LORA_W), lambda g: (0, lora_blk))]
        + [pl.BlockSpec((nb, w), col(0)), pl.BlockSpec((nb, w), col(kb)), pl.BlockSpec((nb, w), col(2 * kb)),
           pl.BlockSpec((nb, LORA_W), lambda g: (0, lora_blk))]
        + [pl.BlockSpec((1, w), col(0)), pl.BlockSpec((1, w), col(kb)), pl.BlockSpec((1, w), col(2 * kb)),
           pl.BlockSpec((1, LORA_W), lambda g: (0, lora_blk))]
        + [pl.BlockSpec((1, w), col(0))] * 5
        + [pl.BlockSpec((LANES, w), col(0)), pl.BlockSpec((LORA_A_WIN, w), col(0)),
           pl.BlockSpec((LORA_W - LORA_G_START, w), col(0))])
    out_spec = pl.BlockSpec((nt, w, nb), lambda g: (0, g, 0))
    return pl.pallas_call(
        functools.partial(_rwkv_sample_prep_kernel, nb=nb, nt=nt),
        grid=(kb,),
        in_specs=in_specs,
        out_specs=[out_spec] * len(PREP_OUT),
        out_shape=[jax.ShapeDtypeStruct((nt, D_RWKV, nb), F32)] * len(PREP_OUT),
        compiler_params=_cparams(("parallel",)),
        name="rwkv_sample_prep",
    )(proj_r, proj_r, proj_r, proj_r, shift, shift, shift, shift, p["mu"], p["mu"], p["mu"], p["mu"],
      p["w0"], p["a0"], p["k_k"], p["k_a"], p["r_k"], p["w_lora"], p["a_lora"], p["g_lora"])


def _wkv_sample_kernel(s_ref, r_ref, w_ref, k_ref, v_ref, a_ref, b_ref, g_ref, bonus_ref, lnw_ref, lnb_ref,
                       so_ref, o_ref, y_ref, *, nt):
    H = HEAD_DIM
    SUB = 8
    PAR = 4
    rowid = lax.broadcasted_iota(jnp.int32, (SUB, LANES), 0)

    def body(i8, carry):
        base = pl.multiple_of(i8 * SUB, SUB)
        v8 = [v_ref[t, pl.ds(base, SUB), :] for t in range(nt)]
        y8 = [jnp.zeros((SUB, LANES), F32) for _ in range(nt)]
        for h0 in range(0, SUB, PAR):
            ids = list(range(h0, h0 + PAR))
            S = [s_ref[0, base + ii] for ii in ids]
            for t in range(nt):
                a, w, b, k, r = a_ref[t], w_ref[t], b_ref[t], k_ref[t], r_ref[t]
                for n, ii in enumerate(ids):
                    sa = jnp.sum(S[n] * a, axis=0, keepdims=True)
                    S[n] = S[n] * w + sa * b + v8[t][ii:ii + 1, :] * k
                    y = jnp.sum(S[n] * r, axis=0, keepdims=True)
                    y8[t] = jnp.where(rowid == ii, y, y8[t])
            for n, ii in enumerate(ids):
                so_ref[0, base + ii] = S[n]
        for t in range(nt):
            y_ref[t, pl.ds(base, SUB), :] = y8[t]
        return carry

    lax.fori_loop(0, H // SUB, body, 0)
    for t in range(nt):
        y = y_ref[t]
        mean = jnp.sum(y, axis=0, keepdims=True) * (1.0 / H)
        d = y - mean
        var = jnp.sum(d * d, axis=0, keepdims=True) * (1.0 / H)
        out = (d * lax.rsqrt(var + GN_EPS) * lnw_ref[...] + lnb_ref[...] + bonus_ref[t]) * g_ref[t]
        o_ref[t] = out.astype(o_ref.dtype)


def _wkv_sample(state_t, prep, lnw_b, lnb_b):
    nh, hd, _, nb = state_t.shape
    nt = prep[0].shape[0]
    st_spec = pl.BlockSpec((1, hd, hd, nb), lambda h: (h, 0, 0, 0))
    ch_spec = pl.BlockSpec((nt, hd, nb), lambda h: (0, h, 0))
    ln_spec = pl.BlockSpec((hd, nb), lambda h: (h, 0))
    return pl.pallas_call(
        functools.partial(_wkv_sample_kernel, nt=nt),
        grid=(nh,),
        in_specs=[st_spec] + [ch_spec] * len(PREP_OUT) + [ln_spec, ln_spec],
        out_specs=[st_spec, ch_spec],
        out_shape=[jax.ShapeDtypeStruct(state_t.shape, F32), jax.ShapeDtypeStruct((nt, nh * hd, nb), BF16)],
        scratch_shapes=[pltpu.VMEM((nt, hd, nb), F32)],
        compiler_params=_cparams(("parallel",)),
        name="wkv_sample",
    )(state_t, *prep, lnw_b, lnb_b)


def _swa_prompt_kernel(slope_ref, sink_ref, q_ref, kc_ref, kp_ref, vc_ref, vp_ref, o_ref, bias_ref):
    n = pl.program_id(1)
    blk = WINDOW
    H = HEAD_DIM

    @pl.when((pl.program_id(0) == 0) & (n == 0))
    def _():
        t = lax.broadcasted_iota(jnp.int32, (blk, 2 * blk), 0)
        j = lax.broadcasted_iota(jnp.int32, (blk, 2 * blk), 1)
        dist = t - j + blk
        band = (dist >= 0) & (dist <= WINDOW)
        first = band & (j >= blk)
        distf = dist.astype(F32)
        for h in range(N_HEADS):
            ab = -slope_ref[h] * distf
            bias_ref[0, h] = jnp.where(first, ab, -jnp.inf)
            bias_ref[1, h] = jnp.where(band, ab, -jnp.inf)

    sel = jnp.where(n == 0, 0, 1)
    low = lax.broadcasted_iota(jnp.int32, (blk, LANES), 1) < H
    scale = H ** -0.5
    tile = lambda i: slice(i * LANES, (i + 1) * LANES)

    def kv_group(KV):
        kslab = {hk: jnp.concatenate([kp_ref[0, :, tile(hk // 2)], kc_ref[0, :, tile(hk // 2)]],
                                     axis=0).astype(BF16) for hk in KV}
        vslab = {hk: jnp.concatenate([vp_ref[0, :, tile(hk // 2)], vc_ref[0, :, tile(hk // 2)]],
                                     axis=0).astype(BF16) for hk in KV}
        lhs = {}
        for hk in KV:
            parts = []
            for s2 in range(2):
                xs = q_ref[0, :, tile(2 * hk + s2)].astype(F32) * scale
                xr = pltpu.roll(xs, H, axis=1)
                if hk % 2 == 0:
                    parts += [jnp.where(low, xs, 0.0), jnp.where(low, xr, 0.0)]
                else:
                    parts += [jnp.where(low, 0.0, xr), jnp.where(low, 0.0, xs)]
            lhs[hk] = jnp.concatenate(parts, axis=0).astype(BF16)
        s = {hk: _dot_nt(lhs[hk], kslab[hk]) for hk in KV}
        p, rden = {}, {}
        for hk in KV:
            ps, rs = [], []
            for g in range(GQA_GROUP):
                h = hk * GQA_GROUP + g
                sg = s[hk][g * blk:(g + 1) * blk] + bias_ref[sel, h]
                m = jnp.maximum(jnp.max(sg, axis=-1, keepdims=True), sink_ref[h])
                e = jnp.exp(sg - m)
                rs.append(1.0 / (jnp.sum(e, axis=-1, keepdims=True) + jnp.exp(sink_ref[h] - m)))
                ps.append(e.astype(BF16))
            p[hk] = jnp.concatenate(ps, axis=0)
            rden[hk] = rs
        o = {hk: _dot(p[hk], vslab[hk]) for hk in KV}
        for hk in KV:
            for s2 in range(2):
                ga, gb = 2 * s2, 2 * s2 + 1
                oa = o[hk][ga * blk:(ga + 1) * blk] * rden[hk][ga]
                ob = o[hk][gb * blk:(gb + 1) * blk] * rden[hk][gb]
                if hk % 2 == 0:
                    out = jnp.where(low, oa, pltpu.roll(ob, H, axis=1))
                else:
                    out = jnp.where(low, pltpu.roll(oa, H, axis=1), ob)
                o_ref[0, :, tile(2 * hk + s2)] = out.astype(o_ref.dtype)

    for g0 in range(0, N_KV_HEADS, KV_PER_GROUP):
        kv_group(range(g0, g0 + KV_PER_GROUP))


def _swa_prompt(q3, kv3, slopes, sinks):
    b, t, _ = q3.shape
    nb = t // WINDOW
    smem = pl.BlockSpec(memory_space=pltpu.SMEM)
    prev = lambda n: jnp.maximum(n - 1, 0)
    return pl.pallas_call(
        _swa_prompt_kernel,
        grid=(b, nb),
        in_specs=[smem, smem,
                  pl.BlockSpec((1, WINDOW, D_MODEL), lambda bi, n: (bi, n, 0)),
                  pl.BlockSpec((1, WINDOW, D_KV), lambda bi, n: (bi, n, 0)),
                  pl.BlockSpec((1, WINDOW, D_KV), lambda bi, n: (bi, prev(n), 0)),
                  pl.BlockSpec((1, WINDOW, D_KV), lambda bi, n: (bi, n, 1)),
                  pl.BlockSpec((1, WINDOW, D_KV), lambda bi, n: (bi, prev(n), 1))],
        out_specs=pl.BlockSpec((1, WINDOW, D_MODEL), lambda bi, n: (bi, n, 0)),
        out_shape=jax.ShapeDtypeStruct((b, t, D_MODEL), BF16),
        scratch_shapes=[pltpu.VMEM((2, N_HEADS, WINDOW, 2 * WINDOW), F32)],
        compiler_params=_cparams(("arbitrary", "arbitrary")),
        name="swa_prompt",
    )(slopes, sinks, q3, kv3, kv3, kv3, kv3)


def _swa_sample_kernel(slope_ref, sink_ref, q_ref, kc_ref, vc_ref, knew_ref, vnew_ref,
                       o_ref, kwin_ref, vwin_ref, nbuf_ref, *, tq, BB, UNR):
    GT = GQA_GROUP * tq
    R = N_KV_HEADS * GT
    NP = 16
    C = D_KV
    row = lax.broadcasted_iota(jnp.int32, (R, WINDOW), 0)
    wcol = lax.broadcasted_iota(jnp.int32, (R, WINDOW), 1)
    t = lax.rem(row, tq)
    slope = slope_ref[...]
    sink = sink_ref[...]
    dist_o = WINDOW + t - wcol
    bias_old = jnp.where(dist_o <= WINDOW, -slope * dist_o.astype(F32), -jnp.inf)
    s_idx = wcol - (WINDOW - tq)
    dist_n = t - s_idx
    bias_new = jnp.where((s_idx >= 0) & (dist_n >= 0), -slope * dist_n.astype(F32), -jnp.inf)
    hkmask = (lax.broadcasted_iota(jnp.int32, (R, C), 0) // GT) == (lax.broadcasted_iota(jnp.int32, (R, C), 1) // HEAD_DIM)
    srow = lax.broadcasted_iota(jnp.int32, (NP, WINDOW), 0)
    scol = lax.broadcasted_iota(jnp.int32, (NP, WINDOW), 1)
    selw = ((scol == srow + (WINDOW - tq)) & (srow < tq)).astype(BF16)
    lane_new = lax.broadcasted_iota(jnp.int32, (C, WINDOW), 1) >= WINDOW - tq
    scale = HEAD_DIM ** -0.5
    for u in range(UNR):
        nbuf_ref[u, :, tq:NP, :] = jnp.zeros((2, NP - tq, C), F32)

    def transposed_new(x):
        return sum(_dot_tn(part, selw) for part in _split3(x))

    def body(i, carry):
        bs = [i * UNR + u for u in range(UNR)]
        U = range(UNR)
        for u in U:
            nbuf_ref[u, 0, 0:tq, :] = knew_ref[bs[u]]
            nbuf_ref[u, 1, 0:tq, :] = vnew_ref[bs[u]]
        kt = [kc_ref[b] for b in bs]
        vt = [vc_ref[b] for b in bs]
        knt = [transposed_new(nbuf_ref[u, 0]) for u in U]
        vnt = [transposed_new(nbuf_ref[u, 1]) for u in U]
        qbd = [jnp.where(hkmask, jnp.concatenate([q_ref[b] * scale] * N_KV_HEADS, axis=0), 0.0).astype(BF16)
               for b in bs]
        s_o = [_dot(qbd[u], kt[u].astype(BF16)) + bias_old for u in U]
        s_n = [_dot(qbd[u], knt[u].astype(BF16)) + bias_new for u in U]
        outs = []
        for u in U:
            m = jnp.maximum(jnp.maximum(jnp.max(s_o[u], axis=-1, keepdims=True),
                                        jnp.max(s_n[u], axis=-1, keepdims=True)), sink)
            p_o = jnp.exp(s_o[u] - m)
            p_n = jnp.exp(s_n[u] - m)
            rden = 1.0 / (jnp.sum(p_o, axis=-1, keepdims=True) + jnp.sum(p_n, axis=-1, keepdims=True)
                          + jnp.exp(sink - m))
            o = _dot_nt(p_o.astype(BF16), vt[u].astype(BF16)) + _dot_nt(p_n.astype(BF16), vnt[u].astype(BF16))
            o = jnp.where(hkmask, o * rden, 0.0)
            acc = o[0:GT]
            for hk in range(1, N_KV_HEADS):
                acc = acc + o[hk * GT:(hk + 1) * GT]
            outs.append(acc)
        for u in U:
            kwin_ref[bs[u]] = jnp.where(lane_new, knt[u], pltpu.roll(kt[u], WINDOW - tq, axis=1))
            vwin_ref[bs[u]] = jnp.where(lane_new, vnt[u], pltpu.roll(vt[u], WINDOW - tq, axis=1))
            o_ref[bs[u]] = outs[u].astype(o_ref.dtype)
        return carry

    lax.fori_loop(0, BB // UNR, body, 0)


def _swa_sample(q16, knew3, vnew3, kct, vct, slope_rows, sink_rows):
    b, gt, c = q16.shape
    tq = gt // GQA_GROUP
    bb = 8 if b % 8 == 0 else 1
    unr = 2 if bb % 2 == 0 else 1
    rows = N_KV_HEADS * gt
    blk3 = lambda shape: pl.BlockSpec(shape, lambda i: (i, 0, 0))
    full2 = pl.BlockSpec((rows, 1), lambda i: (0, 0))
    kern = functools.partial(_swa_sample_kernel, tq=tq, BB=bb, UNR=unr)
    return pl.pallas_call(
        kern,
        grid=(b // bb,),
        in_specs=[full2, full2, blk3((bb, gt, c)), blk3((bb, c, WINDOW)), blk3((bb, c, WINDOW)),
                  blk3((bb, tq, c)), blk3((bb, tq, c))],
        out_specs=[blk3((bb, gt, c)), blk3((bb, c, WINDOW)), blk3((bb, c, WINDOW))],
        out_shape=[jax.ShapeDtypeStruct((b, gt, c), BF16),
                   jax.ShapeDtypeStruct((b, c, WINDOW), F32),
                   jax.ShapeDtypeStruct((b, c, WINDOW), F32)],
        scratch_shapes=[pltpu.VMEM((unr, 2, 16, c), F32)],
        compiler_params=_cparams(("parallel",)),
        name="swa_sample",
    )(slope_rows, sink_rows, q16, kct, vct, knew3, vnew3)


def _merge_out_kernel(x_ref, ya_ref, yb_ref, ga_ref, gb_ref, wo_ref, nw_ref, h_ref, hn_ref):
    f32 = lambda ref: ref[...].astype(F32)
    mixed = _sigmoid(f32(ga_ref)) * f32(ya_ref) + _sigmoid(f32(gb_ref)) * f32(yb_ref)
    h = x_ref[...] + _dot(mixed.astype(BF16), wo_ref[...])
    h_ref[...] = h
    ms = jnp.mean(h * h, axis=-1, keepdims=True)
    hn_ref[...] = (h * lax.rsqrt(ms + RMS_EPS) * nw_ref[...]).astype(BF16)


def _merge_out(x, ya, yb, gates, w_out, nw, tm):
    m, d = x.shape
    row = lambda o: (lambda i: (i, o))
    return pl.pallas_call(
        _merge_out_kernel,
        grid=(m // tm,),
        in_specs=[pl.BlockSpec((tm, d), row(0)), pl.BlockSpec((tm, d), row(0)), pl.BlockSpec((tm, d), row(0)),
                  pl.BlockSpec((tm, d), row(0)), pl.BlockSpec((tm, d), row(1)),
                  pl.BlockSpec((d, d), lambda i: (0, 0)), pl.BlockSpec((1, d), lambda i: (0, 0))],
        out_specs=[pl.BlockSpec((tm, d), row(0)), pl.BlockSpec((tm, d), row(0))],
        out_shape=[jax.ShapeDtypeStruct((m, d), F32), jax.ShapeDtypeStruct((m, d), BF16)],
        compiler_params=_cparams(("parallel",)),
        name="merge_out_proj",
    )(x, ya, yb, gates, gates, w_out, nw)


def _mlp_kernel(hn_ref, h_ref, wu_ref, wd_ref, nw_ref, o_ref, acc_ref):
    j = pl.program_id(1)

    @pl.when(j == 0)
    def _():
        acc_ref[...] = jnp.zeros_like(acc_ref)

    u = jnp.maximum(_dot(hn_ref[...], wu_ref[...]), 0.0)
    acc_ref[...] += _dot((u * u).astype(BF16), wd_ref[...])

    @pl.when(j == pl.num_programs(1) - 1)
    def _():
        h = h_ref[...] + acc_ref[...]
        ms = jnp.mean(h * h, axis=-1, keepdims=True)
        o_ref[...] = h * lax.rsqrt(ms + RMS_EPS) * nw_ref[...]


def _mlp(hn, h, w_up, w_down, nw, tm, tf):
    m, d = h.shape
    f = w_up.shape[1]
    return pl.pallas_call(
        _mlp_kernel,
        grid=(m // tm, f // tf),
        in_specs=[pl.BlockSpec((tm, d), lambda i, j: (i, 0)), pl.BlockSpec((tm, d), lambda i, j: (i, 0)),
                  pl.BlockSpec((d, tf), lambda i, j: (0, j)), pl.BlockSpec((tf, d), lambda i, j: (j, 0)),
                  pl.BlockSpec((1, d), lambda i, j: (0, 0))],
        out_specs=pl.BlockSpec((tm, d), lambda i, j: (i, 0)),
        out_shape=jax.ShapeDtypeStruct((m, d), F32),
        scratch_shapes=[pltpu.VMEM((tm, d), F32)],
        compiler_params=_cparams(("parallel", "arbitrary")),
        name="mlp_final_norm",
    )(hn, h, w_up, w_down, nw)


def _pick(m, prefs):
    for t in prefs:
        if m % t == 0:
            return t
    return m


def _pad_cols(v, n):
    return jnp.concatenate([v, jnp.zeros(v.shape[:-1] + (n - v.shape[-1],), v.dtype)], axis=-1)


def _place_rows(w, start, rows):
    n, d = w.shape
    return jnp.concatenate([jnp.zeros((start, d), w.dtype), w, jnp.zeros((rows - start - n, d), w.dtype)], axis=0)


def _layer(x, shift_prev, wkv0, lw, *, prompt, k_cache_t=None, v_cache_t=None):
    b, t, d = x.shape
    m = b * t
    x2 = x.reshape(m, d) if prompt else x.transpose(1, 0, 2).reshape(m, d)
    cast_names = [n for n in ("w_out", "w_up", "w_down") if n not in lw]
    (proj_r, proj_q, proj_kv, proj_g), casts = _in_proj(
        x2, lw["norm_mix_w"], lw["w_t"], IN_GROUPS, [F32, BF16, F32, BF16], _pick(m, (1024, 512, 256, 128, 8)),
        cast=tuple(lw[n + "_f32"] for n in cast_names))
    lw.update(zip(cast_names, casts))
    shift_pad = _pad_cols(shift_prev, R_PAD)

    if prompt:
        proj_r3 = proj_r.reshape(b, t, R_PAD)
        kv3 = proj_kv.reshape(b, t, 2 * D_KV)
        ya, wkv_new = _rwkv_prompt(proj_r3, shift_pad[:, None], wkv0, lw, _pick(b, (4, 2, 1)), RWKV_SKEW_PLAN)
        yb = _swa_prompt(proj_q.reshape(b, t, D_MODEL), kv3, lw["slopes"], lw["sinks"])
        k_win = kv3[:, t - WINDOW:, :D_KV].reshape(b, WINDOW, N_KV_HEADS, HEAD_DIM)
        v_win = kv3[:, t - WINDOW:, D_KV:].reshape(b, WINDOW, N_KV_HEADS, HEAD_DIM)
        shift_new = proj_r3[:, t - 1, :R_COLS]
    else:
        prep = _rwkv_sample_prep(proj_r, shift_pad, lw, b, t)
        wkv_t, ya_t = _wkv_sample(wkv0.transpose(1, 2, 3, 0), prep,
                                  jnp.broadcast_to(lw["ln_w"].reshape(D_RWKV, 1), (D_RWKV, b)),
                                  jnp.broadcast_to(lw["ln_b"].reshape(D_RWKV, 1), (D_RWKV, b)))
        ya = ya_t.transpose(0, 2, 1)
        wkv_new = wkv_t.transpose(3, 0, 1, 2)
        q16 = proj_q.reshape(t, b, N_KV_HEADS, GQA_GROUP, HEAD_DIM).transpose(1, 3, 0, 2, 4)
        q16 = q16.reshape(b, GQA_GROUP * t, D_KV)
        kv_bt = proj_kv.reshape(t, b, 2 * D_KV).transpose(1, 0, 2)
        gt_head = (jnp.arange(N_KV_HEADS)[:, None] * GQA_GROUP + jnp.arange(GQA_GROUP)[None, :])
        row_head = jnp.repeat(gt_head, t, axis=1).reshape(-1)
        o16, kwt, vwt = _swa_sample(q16, kv_bt[:, :, :D_KV], kv_bt[:, :, D_KV:], k_cache_t, v_cache_t,
                                    lw["slopes"][row_head][:, None], lw["sinks"][row_head][:, None])
        yb = o16.reshape(b, GQA_GROUP, t, N_KV_HEADS, HEAD_DIM).transpose(2, 0, 3, 1, 4)
        k_win = kwt.reshape(b, N_KV_HEADS, HEAD_DIM, WINDOW).transpose(0, 3, 1, 2)
        v_win = vwt.reshape(b, N_KV_HEADS, HEAD_DIM, WINDOW).transpose(0, 3, 1, 2)
        shift_new = proj_r[(t - 1) * b:, :R_COLS]

    h, hn = _merge_out(x2, ya.reshape(m, d), yb.reshape(m, d), proj_g, lw["w_out"], lw["norm_mlp_w"],
                       _pick(m, (512, 256, 128, 8)))
    y = _mlp(hn, h, lw["w_up"], lw["w_down"], lw["norm_final_w"], _pick(m, (512, 256, 128, 8)), 1024)
    y = y.reshape(b, t, d) if prompt else y.reshape(t, b, d).transpose(1, 0, 2)
    return y, shift_new, wkv_new, k_win, v_win


def kernel(x_prompt, x_sample, state_shift, state_wkv, cache_k_win, cache_v_win, norm_mix_w, w_in, tshift_mu, w0, w_lora, a0, a_lora, g_lora, k_k, k_a, r_k, ln_x_w, ln_x_b, attn_sinks, w_out, norm_mlp_w, w_up, w_down, norm_final_w):
    depth = w_in.shape[0]
    assert depth == 1
    l = 0
    bp = x_prompt.shape[0]
    db = x_sample.shape[0]
    hh = jnp.arange(N_HEADS, dtype=F32)
    lw = dict(
        norm_mix_w=norm_mix_w[l][None],
        w_t=jnp.swapaxes(w_in[l], 0, 1).astype(BF16),
        mu=_pad_cols(tshift_mu[l][None], R_PAD), w0=w0[l][None], a0=a0[l][None], k_k=k_k[l][None],
        k_a=k_a[l][None], r_k=r_k[l].reshape(1, D_RWKV), ln_w=ln_x_w[l][None], ln_b=ln_x_b[l][None],
        w_lora=_place_rows(w_lora[l], 0, LANES).astype(BF16),
        a_lora=_place_rows(a_lora[l], LORA_DECAY_END, LORA_A_WIN).astype(BF16),
        g_lora=_place_rows(g_lora[l], LORA_A_END - LORA_G_START, LORA_W - LORA_G_START).astype(BF16),
        slopes=jnp.exp2(-8.0 * (hh + 1.0) / N_HEADS), sinks=attn_sinks[l].astype(F32),
        w_out_f32=w_out[l], norm_mlp_w=norm_mlp_w[l][None],
        w_up_f32=w_up[l], w_down_f32=w_down[l], norm_final_w=norm_final_w[None],
    )
    yp, sp, wp, kp, vp = _layer(
        x_prompt, jnp.zeros((bp, R_COLS), F32), jnp.zeros((bp, N_HEADS, HEAD_DIM, HEAD_DIM), F32), lw, prompt=True)
    kct = cache_k_win[l].transpose(0, 2, 3, 1).reshape(db, D_KV, WINDOW)
    vct = cache_v_win[l].transpose(0, 2, 3, 1).reshape(db, D_KV, WINDOW)
    ys, ss, ws, ksm, vsm = _layer(x_sample, state_shift[l], state_wkv[l], lw, prompt=False,
                                  k_cache_t=kct, v_cache_t=vct)
    return (yp, ys, sp[None], wp[None], kp[None], vp[None], ss[None], ws[None], ksm[None], vsm[None])
```
